```python
import math
import jax, jax.numpy as jnp
from jax import lax
import numpy as np

D_MODEL = 2048
BATCH = 8
SEQ = 4096
DEPTH = 4

N_MIXERS = 4
PLE_DIM = 256
EPS = 1e-6
BLOCK = 128

SWA_HEADS = 32
SWA_KV_HEADS = 4
SWA_HEAD_DIM = 64
SWA_GROUP = SWA_HEADS // SWA_KV_HEADS
SWA_WIDTH = SWA_HEADS * SWA_HEAD_DIM
SWA_KV_WIDTH = SWA_KV_HEADS * SWA_HEAD_DIM
WINDOW = 128
REL_BUCKETS = 32
REL_MAX_DIST = 128

CONV_WIDTH = D_MODEL
CONV_TAPS = 3

SSM_WIDTH = D_MODEL
SSM_GROUP = 16
SSM_STATE = 64
SSM_GROUPS = SSM_WIDTH // SSM_GROUP
DT_MIN = 1e-3
DT_MAX = 1e-1

FOX_HEADS = 32
FOX_HEAD_DIM = 64
FOX_WIDTH = FOX_HEADS * FOX_HEAD_DIM

kernel_name = 'hybrid_swa_conv_s5_fox_trunk'


def n_layers_of(m):
    return len(range(m, DEPTH, N_MIXERS))


def rmsnorm(x, g):
    x32 = x.astype(jnp.float32)
    y = x32 * lax.rsqrt(jnp.mean(x32 * x32, axis=-1, keepdims=True) + EPS)
    return (y * g.astype(jnp.float32)).astype(x.dtype)


def t5_bucket(dist):
    max_exact = REL_BUCKETS // 2
    d = np.maximum(dist, 1).astype(np.float32)
    large = max_exact + (np.log(d / max_exact) / np.log(REL_MAX_DIST / max_exact) * (REL_BUCKETS - max_exact)).astype(np.int32)
    large = np.minimum(large, REL_BUCKETS - 1)
    return np.where(dist < max_exact, dist, large).astype(np.int32)


def swa_mixer(h, w_in, w_out, sinks, rel_bias):
    bsz, seq, _ = h.shape
    nb = seq // BLOCK
    q, k, v, gate = jnp.split(h @ w_in, [SWA_WIDTH, SWA_WIDTH + SWA_KV_WIDTH, SWA_WIDTH + 2 * SWA_KV_WIDTH], axis=-1)
    q = q.reshape(bsz, nb, BLOCK, SWA_KV_HEADS, SWA_GROUP, SWA_HEAD_DIM)
    k = k.reshape(bsz, nb, BLOCK, SWA_KV_HEADS, SWA_HEAD_DIM)
    v = v.reshape(bsz, nb, BLOCK, SWA_KV_HEADS, SWA_HEAD_DIM)
    prev = lambda t: jnp.concatenate([jnp.zeros_like(t[:, :1]), t[:, :-1]], axis=1)
    kb = jnp.concatenate([prev(k), k], axis=2)
    vb = jnp.concatenate([prev(v), v], axis=2)
    qi = np.arange(BLOCK)[:, None]
    kj = np.arange(2 * BLOCK)[None, :]
    dist = qi + BLOCK - kj
    band = (dist >= 0) & (dist < WINDOW)
    exists = (np.arange(nb)[:, None, None] * BLOCK - BLOCK + kj[None]) >= 0
    mask = jnp.asarray(band[None] & exists)[None, :, None, None]
    bucket = t5_bucket(np.clip(dist, 0, None))
    bias = rel_bias.astype(jnp.float32)[bucket]
    bias = bias.transpose(2, 0, 1).reshape(SWA_KV_HEADS, SWA_GROUP, BLOCK, 2 * BLOCK)
    scores = jnp.einsum('bnqhgd,bnjhd->bnhgqj', q, kb).astype(jnp.float32) * (SWA_HEAD_DIM ** -0.5) + bias
    scores = jnp.where(mask, scores, jnp.finfo(jnp.float32).min)
    sink = sinks.astype(jnp.float32).reshape(SWA_KV_HEADS, SWA_GROUP)[:, :, None, None]
    m = jnp.maximum(scores.max(axis=-1, keepdims=True), sink)
    e = jnp.exp(scores - m)
    probs = e / (e.sum(axis=-1, keepdims=True) + jnp.exp(sink - m))
    out = jnp.einsum('bnhgqj,bnjhd->bnqhgd', probs.astype(vb.dtype), vb).reshape(bsz, seq, SWA_WIDTH)
    return (out * jax.nn.silu(gate)) @ w_out


def conv_mixer(h, w_in, conv_kernel, w_out):
    seq = h.shape[1]
    bg, cg, u, gate = jnp.split(h @ w_in, 4, axis=-1)
    z = cg * u
    zp = jnp.pad(z, ((0, 0), (CONV_TAPS - 1, 0), (0, 0)))
    conv = zp[:, 0:seq] * conv_kernel[0]
    for tap in range(1, CONV_TAPS):
        conv = conv + zp[:, tap:tap + seq] * conv_kernel[tap]
    y = bg * conv
    return (y * jax.nn.silu(gate)) @ w_out


def _complex_linear_combine(left, right):
    a1r, a1i, b1r, b1i = left
    a2r, a2i, b2r, b2i = right
    return (a1r * a2r - a1i * a2i, a1r * a2i + a1i * a2r,
            a2r * b1r - a2i * b1i + b2r, a2r * b1i + a2i * b1r + b2i)


def ssm_mixer(h, w_in, lam_re, lam_im, log_dt, b_re, b_im, c_re, c_im, d_skip, w_glu, b_glu, w_out):
    f32 = jnp.float32
    bsz, seq, _ = h.shape
    u, gate = jnp.split(h @ w_in, 2, axis=-1)
    u32 = u.astype(f32)
    ug = u32.reshape(bsz, seq, SSM_GROUPS, SSM_GROUP)
    dt = jnp.exp(log_dt.astype(f32))[:, None]
    lr = lam_re.astype(f32)
    li = lam_im.astype(f32)
    mag = jnp.exp(lr * dt)
    ab_re = mag * jnp.cos(li * dt)
    ab_im = mag * jnp.sin(li * dt)
    den = lr * lr + li * li
    nr = ab_re - 1.0
    coef_re = ((nr * lr + ab_im * li) / den)[..., None]
    coef_im = ((ab_im * lr - nr * li) / den)[..., None]
    br = b_re.astype(f32)
    bi = b_im.astype(f32)
    bb_re = coef_re * br - coef_im * bi
    bb_im = coef_re * bi + coef_im * br
    bu_re = jnp.einsum('bsgc,gnc->bsgn', ug, bb_re)
    bu_im = jnp.einsum('bsgc,gnc->bsgn', ug, bb_im)
    a_re = jnp.broadcast_to(ab_re[None, None], (1, seq, SSM_GROUPS, SSM_STATE))
    a_im = jnp.broadcast_to(ab_im[None, None], (1, seq, SSM_GROUPS, SSM_STATE))
    _, _, s_re, s_im = lax.associative_scan(_complex_linear_combine, (a_re, a_im, bu_re, bu_im), axis=1)
    y = jnp.einsum('bsgn,gcn->bsgc', s_re, c_re.astype(f32)) - jnp.einsum('bsgn,gcn->bsgc', s_im, c_im.astype(f32))
    y = y.reshape(bsz, seq, SSM_WIDTH) + d_skip.astype(f32) * u32
    y = jax.nn.gelu(y)
    ga, gb = jnp.split(y @ w_glu.astype(f32) + b_glu.astype(f32), 2, axis=-1)
    y = (ga * jax.nn.sigmoid(gb)).astype(h.dtype)
    return (y * jax.nn.silu(gate)) @ w_out


def fox_mixer(h, w_in, w_fg, b_fg, w_out):
    f32 = jnp.float32
    bsz, seq, _ = h.shape
    nb = seq // BLOCK
    q, k, v, gate = jnp.split(h @ w_in, 4, axis=-1)
    q = q.reshape(bsz, nb, BLOCK, FOX_HEADS, FOX_HEAD_DIM).transpose(1, 0, 2, 3, 4)
    k = k.reshape(bsz, seq, FOX_HEADS, FOX_HEAD_DIM)
    v = v.reshape(bsz, seq, FOX_HEADS, FOX_HEAD_DIM)
    log_f = jax.nn.log_sigmoid((h @ w_fg).astype(f32) + b_fg.astype(f32))
    csum = jnp.cumsum(log_f, axis=1)
    c_keys = csum.transpose(0, 2, 1)[:, :, None, :]
    c_q = csum.reshape(bsz, nb, BLOCK, FOX_HEADS).transpose(1, 0, 3, 2)
    kpos = jnp.arange(seq)
    neg = jnp.finfo(f32).min

    def block(args):
        n, qn, cn = args
        s = jnp.einsum('bqhd,bkhd->bhqk', qn, k).astype(f32) * (FOX_HEAD_DIM ** -0.5)
        s = s + cn[..., None] - c_keys
        qpos = n * BLOCK + jnp.arange(BLOCK)
        s = jnp.where(kpos[None, :] <= qpos[:, None], s, neg)
        pr = jax.nn.softmax(s, axis=-1)
        return jnp.einsum('bhqk,bkhd->bqhd', pr.astype(v.dtype), v)

    out = lax.map(block, (jnp.arange(nb), q, c_q))
    out = out.transpose(1, 0, 2, 3, 4).reshape(bsz, seq, FOX_WIDTH)
    return (out * jax.nn.silu(gate)) @ w_out


def _fwd_setup_inputs(seed: int = 0) -> dict:
    key = jax.random.key(seed)
    ks = iter(jax.random.split(key, 40))
    f32 = jnp.float32
    nrm = lambda shape, scale: jax.random.normal(next(ks), shape, f32) * scale
    na, nc, ns, nf = (n_layers_of(m) for m in range(N_MIXERS))
    return {
        'x': nrm((BATCH, SEQ, D_MODEL), 1.0),
        'p': nrm((DEPTH, BATCH, SEQ, PLE_DIM), 1.0),
        'norm_g': 1.0 + nrm((DEPTH, D_MODEL), 0.01),
        'final_g': 1.0 + nrm((D_MODEL,), 0.01),
        'rel_bias': nrm((REL_BUCKETS, SWA_HEADS), 0.5),
        'swa_w_in': nrm((na, D_MODEL, 2 * SWA_WIDTH + 2 * SWA_KV_WIDTH), D_MODEL ** -0.5),
        'swa_w_out': nrm((na, SWA_WIDTH, D_MODEL), SWA_WIDTH ** -0.5),
        'swa_sinks': nrm((na, SWA_HEADS), 1.0),
        'conv_w_in': nrm((nc, D_MODEL, 4 * CONV_WIDTH), D_MODEL ** -0.5),
        'conv_kernel': nrm((nc, CONV_TAPS, CONV_WIDTH), CONV_TAPS ** -0.5),
        'conv_w_out': nrm((nc, CONV_WIDTH, D_MODEL), CONV_WIDTH ** -0.5),
        'ssm_w_in': nrm((ns, D_MODEL, 2 * SSM_WIDTH), D_MODEL ** -0.5),
        'ssm_lam_re': -0.5 + nrm((ns, SSM_GROUPS, SSM_STATE), 0.01),
        'ssm_lam_im': jnp.broadcast_to(math.pi * jnp.arange(SSM_STATE, dtype=f32), (ns, SSM_GROUPS, SSM_STATE)) + nrm((ns, SSM_GROUPS, SSM_STATE), 0.01),
        'ssm_log_dt': jax.random.uniform(next(ks), (ns, SSM_GROUPS), f32, math.log(DT_MIN), math.log(DT_MAX)),
        'ssm_b_re': nrm((ns, SSM_GROUPS, SSM_STATE, SSM_GROUP), (2 * SSM_GROUP) ** -0.5),
        'ssm_b_im': nrm((ns, SSM_GROUPS, SSM_STATE, SSM_GROUP), (2 * SSM_GROUP) ** -0.5),
        'ssm_c_re': nrm((ns, SSM_GROUPS, SSM_GROUP, SSM_STATE), SSM_STATE ** -0.5),
        'ssm_c_im': nrm((ns, SSM_GROUPS, SSM_GROUP, SSM_STATE), SSM_STATE ** -0.5),
        'ssm_d': nrm((ns, SSM_WIDTH), 1.0),
        'ssm_w_glu': nrm((ns, SSM_WIDTH, 2 * SSM_WIDTH), SSM_WIDTH ** -0.5),
        'ssm_b_glu': nrm((ns, 2 * SSM_WIDTH), 0.01),
        'ssm_w_out': nrm((ns, SSM_WIDTH, D_MODEL), SSM_WIDTH ** -0.5),
        'fox_w_in': nrm((nf, D_MODEL, 4 * FOX_WIDTH), D_MODEL ** -0.5),
        'fox_w_fg': nrm((nf, D_MODEL, FOX_HEADS), D_MODEL ** -0.5),
        'fox_b_fg': jax.random.uniform(next(ks), (nf, FOX_HEADS), f32, 1.0, 5.0),
        'fox_w_out': nrm((nf, FOX_WIDTH, D_MODEL), FOX_WIDTH ** -0.5),
        'ple_proj': nrm((DEPTH, PLE_DIM, D_MODEL), PLE_DIM ** -0.5),
        'ple_norm': 1.0 + nrm((DEPTH, D_MODEL), 0.01),
        'ple_gate': nrm((DEPTH, D_MODEL, D_MODEL), D_MODEL ** -0.5),
    }


def _fwd_reference(x, p, norm_g, final_g, rel_bias, swa_w_in, swa_w_out, swa_sinks, conv_w_in, conv_kernel, conv_w_out, ssm_w_in, ssm_lam_re, ssm_lam_im, ssm_log_dt, ssm_b_re, ssm_b_im, ssm_c_re, ssm_c_im, ssm_d, ssm_w_glu, ssm_b_glu, ssm_w_out, fox_w_in, fox_w_fg, fox_b_fg, fox_w_out, ple_proj, ple_norm, ple_gate):
    for i in range(DEPTH):
        mixer, j = i % N_MIXERS, i // N_MIXERS
        hn = rmsnorm(x, norm_g[i])
        if mixer == 0:
            y = swa_mixer(hn, swa_w_in[j], swa_w_out[j], swa_sinks[j], rel_bias)
        elif mixer == 1:
            y = conv_mixer(hn, conv_w_in[j], conv_kernel[j], conv_w_out[j])
        elif mixer == 2:
            y = ssm_mixer(hn, ssm_w_in[j], ssm_lam_re[j], ssm_lam_im[j], ssm_log_dt[j], ssm_b_re[j], ssm_b_im[j], ssm_c_re[j], ssm_c_im[j], ssm_d[j], ssm_w_glu[j], ssm_b_glu[j], ssm_w_out[j])
        else:
            y = fox_mixer(hn, fox_w_in[j], fox_w_fg[j], fox_b_fg[j], fox_w_out[j])
        x = x + y
        emb = p[i] @ ple_proj[i]
        g = jax.nn.sigmoid(rmsnorm(x, ple_norm[i]) @ ple_gate[i])
        x = x + emb * g
    return rmsnorm(x, final_g)


import jax as _jax
import jax.numpy as _jnp

TWIN_FORMAT = 'train_step'
FWD_PARAMS = ['x', 'p', 'norm_g', 'final_g', 'rel_bias', 'swa_w_in', 'swa_w_out', 'swa_sinks', 'conv_w_in', 'conv_kernel', 'conv_w_out', 'ssm_w_in', 'ssm_lam_re', 'ssm_lam_im', 'ssm_log_dt', 'ssm_b_re', 'ssm_b_im', 'ssm_c_re', 'ssm_c_im', 'ssm_d', 'ssm_w_glu', 'ssm_b_glu', 'ssm_w_out', 'fox_w_in', 'fox_w_fg', 'fox_b_fg', 'fox_w_out', 'ple_proj', 'ple_norm', 'ple_gate']
TWIN_WEIGHTS = ['norm_g', 'final_g', 'rel_bias', 'swa_w_in', 'swa_w_out', 'swa_sinks', 'conv_w_in', 'conv_kernel', 'conv_w_out', 'ssm_w_in', 'ssm_lam_re', 'ssm_lam_im', 'ssm_log_dt', 'ssm_b_re', 'ssm_b_im', 'ssm_c_re', 'ssm_c_im', 'ssm_d', 'ssm_w_glu', 'ssm_b_glu', 'ssm_w_out', 'fox_w_in', 'fox_w_fg', 'fox_b_fg', 'fox_w_out', 'ple_proj', 'ple_norm', 'ple_gate']
TWIN_DIFF_INPUT = 'x'
TWIN_INPUTS = ['x', 'p', 'norm_g', 'final_g', 'rel_bias', 'swa_w_in', 'swa_w_out', 'swa_sinks', 'conv_w_in', 'conv_kernel', 'conv_w_out', 'ssm_w_in', 'ssm_lam_re', 'ssm_lam_im', 'ssm_log_dt', 'ssm_b_re', 'ssm_b_im', 'ssm_c_re', 'ssm_c_im', 'ssm_d', 'ssm_w_glu', 'ssm_b_glu', 'ssm_w_out', 'fox_w_in', 'fox_w_fg', 'fox_b_fg', 'fox_w_out', 'ple_proj', 'ple_norm', 'ple_gate', 'loss_target', 'm_norm_g', 'm_final_g', 'm_rel_bias', 'm_swa_w_in', 'm_swa_w_out', 'm_swa_sinks', 'm_conv_w_in', 'm_conv_kernel', 'm_conv_w_out', 'm_ssm_w_in', 'm_ssm_lam_re', 'm_ssm_lam_im', 'm_ssm_log_dt', 'm_ssm_b_re', 'm_ssm_b_im', 'm_ssm_c_re', 'm_ssm_c_im', 'm_ssm_d', 'm_ssm_w_glu', 'm_ssm_b_glu', 'm_ssm_w_out', 'm_fox_w_in', 'm_fox_w_fg', 'm_fox_b_fg', 'm_fox_w_out', 'm_ple_proj', 'm_ple_norm', 'm_ple_gate', 'v_norm_g', 'v_final_g', 'v_rel_bias', 'v_swa_w_in', 'v_swa_w_out', 'v_swa_sinks', 'v_conv_w_in', 'v_conv_kernel', 'v_conv_w_out', 'v_ssm_w_in', 'v_ssm_lam_re', 'v_ssm_lam_im', 'v_ssm_log_dt', 'v_ssm_b_re', 'v_ssm_b_im', 'v_ssm_c_re', 'v_ssm_c_im', 'v_ssm_d', 'v_ssm_w_glu', 'v_ssm_b_glu', 'v_ssm_w_out', 'v_fox_w_in', 'v_fox_w_fg', 'v_fox_b_fg', 'v_fox_w_out', 'v_ple_proj', 'v_ple_norm', 'v_ple_gate']
TWIN_OUTPUTS = ['loss', 'grad_x', 'grad_norm_g', 'grad_final_g', 'grad_rel_bias', 'grad_swa_w_in', 'grad_swa_w_out', 'grad_swa_sinks', 'grad_conv_w_in', 'grad_conv_kernel', 'grad_conv_w_out', 'grad_ssm_w_in', 'grad_ssm_lam_re', 'grad_ssm_lam_im', 'grad_ssm_log_dt', 'grad_ssm_b_re', 'grad_ssm_b_im', 'grad_ssm_c_re', 'grad_ssm_c_im', 'grad_ssm_d', 'grad_ssm_w_glu', 'grad_ssm_b_glu', 'grad_ssm_w_out', 'grad_fox_w_in', 'grad_fox_w_fg', 'grad_fox_b_fg', 'grad_fox_w_out', 'grad_ple_proj', 'grad_ple_norm', 'grad_ple_gate', 'delta_norm_g', 'delta_final_g', 'delta_rel_bias', 'delta_swa_w_in', 'delta_swa_w_out', 'delta_swa_sinks', 'delta_conv_w_in', 'delta_conv_kernel', 'delta_conv_w_out', 'delta_ssm_w_in', 'delta_ssm_lam_re', 'delta_ssm_lam_im', 'delta_ssm_log_dt', 'delta_ssm_b_re', 'delta_ssm_b_im', 'delta_ssm_c_re', 'delta_ssm_c_im', 'delta_ssm_d', 'delta_ssm_w_glu', 'delta_ssm_b_glu', 'delta_ssm_w_out', 'delta_fox_w_in', 'delta_fox_w_fg', 'delta_fox_b_fg', 'delta_fox_w_out', 'delta_ple_proj', 'delta_ple_norm', 'delta_ple_gate', 'new_m_norm_g', 'new_m_final_g', 'new_m_rel_bias', 'new_m_swa_w_in', 'new_m_swa_w_out', 'new_m_swa_sinks', 'new_m_conv_w_in', 'new_m_conv_kernel', 'new_m_conv_w_out', 'new_m_ssm_w_in', 'new_m_ssm_lam_re', 'new_m_ssm_lam_im', 'new_m_ssm_log_dt', 'new_m_ssm_b_re', 'new_m_ssm_b_im', 'new_m_ssm_c_re', 'new_m_ssm_c_im', 'new_m_ssm_d', 'new_m_ssm_w_glu', 'new_m_ssm_b_glu', 'new_m_ssm_w_out', 'new_m_fox_w_in', 'new_m_fox_w_fg', 'new_m_fox_b_fg', 'new_m_fox_w_out', 'new_m_ple_proj', 'new_m_ple_norm', 'new_m_ple_gate', 'new_v_norm_g', 'new_v_final_g', 'new_v_rel_bias', 'new_v_swa_w_in', 'new_v_swa_w_out', 'new_v_swa_sinks', 'new_v_conv_w_in', 'new_v_conv_kernel', 'new_v_conv_w_out', 'new_v_ssm_w_in', 'new_v_ssm_lam_re', 'new_v_ssm_lam_im', 'new_v_ssm_log_dt', 'new_v_ssm_b_re', 'new_v_ssm_b_im', 'new_v_ssm_c_re', 'new_v_ssm_c_im', 'new_v_ssm_d', 'new_v_ssm_w_glu', 'new_v_ssm_b_glu', 'new_v_ssm_w_out', 'new_v_fox_w_in', 'new_v_fox_w_fg', 'new_v_fox_b_fg', 'new_v_fox_w_out', 'new_v_ple_proj', 'new_v_ple_norm', 'new_v_ple_gate']
TWIN_LEAF_KINDS = {'loss': 'loss', 'grad_x': 'grad_x', 'grad_norm_g': 'grad_w', 'grad_final_g': 'grad_w', 'grad_rel_bias': 'grad_w', 'grad_swa_w_in': 'grad_w', 'grad_swa_w_out': 'grad_w', 'grad_swa_sinks': 'grad_w', 'grad_conv_w_in': 'grad_w', 'grad_conv_kernel': 'grad_w', 'grad_conv_w_out': 'grad_w', 'grad_ssm_w_in': 'grad_w', 'grad_ssm_lam_re': 'grad_w', 'grad_ssm_lam_im': 'grad_w', 'grad_ssm_log_dt': 'grad_w', 'grad_ssm_b_re': 'grad_w', 'grad_ssm_b_im': 'grad_w', 'grad_ssm_c_re': 'grad_w', 'grad_ssm_c_im': 'grad_w', 'grad_ssm_d': 'grad_w', 'grad_ssm_w_glu': 'grad_w', 'grad_ssm_b_glu': 'grad_w', 'grad_ssm_w_out': 'grad_w', 'grad_fox_w_in': 'grad_w', 'grad_fox_w_fg': 'grad_w', 'grad_fox_b_fg': 'grad_w', 'grad_fox_w_out': 'grad_w', 'grad_ple_proj': 'grad_w', 'grad_ple_norm': 'grad_w', 'grad_ple_gate': 'grad_w', 'delta_norm_g': 'delta_w', 'delta_final_g': 'delta_w', 'delta_rel_bias': 'delta_w', 'delta_swa_w_in': 'delta_w', 'delta_swa_w_out': 'delta_w', 'delta_swa_sinks': 'delta_w', 'delta_conv_w_in': 'delta_w', 'delta_conv_kernel': 'delta_w', 'delta_conv_w_out': 'delta_w', 'delta_ssm_w_in': 'delta_w', 'delta_ssm_lam_re': 'delta_w', 'delta_ssm_lam_im': 'delta_w', 'delta_ssm_log_dt': 'delta_w', 'delta_ssm_b_re': 'delta_w', 'delta_ssm_b_im': 'delta_w', 'delta_ssm_c_re': 'delta_w', 'delta_ssm_c_im': 'delta_w', 'delta_ssm_d': 'delta_w', 'delta_ssm_w_glu': 'delta_w', 'delta_ssm_b_glu': 'delta_w', 'delta_ssm_w_out': 'delta_w', 'delta_fox_w_in': 'delta_w', 'delta_fox_w_fg': 'delta_w', 'delta_fox_b_fg': 'delta_w', 'delta_fox_w_out': 'delta_w', 'delta_ple_proj': 'delta_w', 'delta_ple_norm': 'delta_w', 'delta_ple_gate': 'delta_w', 'new_m_norm_g': 'new_m', 'new_m_final_g': 'new_m', 'new_m_rel_bias': 'new_m', 'new_m_swa_w_in': 'new_m', 'new_m_swa_w_out': 'new_m', 'new_m_swa_sinks': 'new_m', 'new_m_conv_w_in': 'new_m', 'new_m_conv_kernel': 'new_m', 'new_m_conv_w_out': 'new_m', 'new_m_ssm_w_in': 'new_m', 'new_m_ssm_lam_re': 'new_m', 'new_m_ssm_lam_im': 'new_m', 'new_m_ssm_log_dt': 'new_m', 'new_m_ssm_b_re': 'new_m', 'new_m_ssm_b_im': 'new_m', 'new_m_ssm_c_re': 'new_m', 'new_m_ssm_c_im': 'new_m', 'new_m_ssm_d': 'new_m', 'new_m_ssm_w_glu': 'new_m', 'new_m_ssm_b_glu': 'new_m', 'new_m_ssm_w_out': 'new_m', 'new_m_fox_w_in': 'new_m', 'new_m_fox_w_fg': 'new_m', 'new_m_fox_b_fg': 'new_m', 'new_m_fox_w_out': 'new_m', 'new_m_ple_proj': 'new_m', 'new_m_ple_norm': 'new_m', 'new_m_ple_gate': 'new_m', 'new_v_norm_g': 'new_v', 'new_v_final_g': 'new_v', 'new_v_rel_bias': 'new_v', 'new_v_swa_w_in': 'new_v', 'new_v_swa_w_out': 'new_v', 'new_v_swa_sinks': 'new_v', 'new_v_conv_w_in': 'new_v', 'new_v_conv_kernel': 'new_v', 'new_v_conv_w_out': 'new_v', 'new_v_ssm_w_in': 'new_v', 'new_v_ssm_lam_re': 'new_v', 'new_v_ssm_lam_im': 'new_v', 'new_v_ssm_log_dt': 'new_v', 'new_v_ssm_b_re': 'new_v', 'new_v_ssm_b_im': 'new_v', 'new_v_ssm_c_re': 'new_v', 'new_v_ssm_c_im': 'new_v', 'new_v_ssm_d': 'new_v', 'new_v_ssm_w_glu': 'new_v', 'new_v_ssm_b_glu': 'new_v', 'new_v_ssm_w_out': 'new_v', 'new_v_fox_w_in': 'new_v', 'new_v_fox_w_fg': 'new_v', 'new_v_fox_b_fg': 'new_v', 'new_v_fox_w_out': 'new_v', 'new_v_ple_proj': 'new_v', 'new_v_ple_norm': 'new_v', 'new_v_ple_gate': 'new_v'}


def _forward(args):
    return _fwd_reference(*[args[k] for k in FWD_PARAMS])


def _output_shape():
    def fwd():
        inp = _fwd_setup_inputs(0)
        return _fwd_reference(*[inp[k] for k in FWD_PARAMS])
    out = _jax.eval_shape(fwd)
    return out.shape, out.dtype

N_MICROBATCH = 1
ADAM_LR = 0.001
ADAM_B1 = 0.9
ADAM_B2 = 0.999
ADAM_EPS = 1e-08
ADAM_WD = 0.01
ADAM_STEP = 10
PER_EXAMPLE_BATCH_AXIS = {'x': 0, 'p': 1, 'loss_target': 0}
SHARED_INPUTS = []
_WEIGHT_DTYPES = {'norm_g': _jnp.float32, 'final_g': _jnp.float32, 'rel_bias': _jnp.float32, 'swa_w_in': _jnp.float32, 'swa_w_out': _jnp.float32, 'swa_sinks': _jnp.float32, 'conv_w_in': _jnp.float32, 'conv_kernel': _jnp.float32, 'conv_w_out': _jnp.float32, 'ssm_w_in': _jnp.float32, 'ssm_lam_re': _jnp.float32, 'ssm_lam_im': _jnp.float32, 'ssm_log_dt': _jnp.float32, 'ssm_b_re': _jnp.float32, 'ssm_b_im': _jnp.float32, 'ssm_c_re': _jnp.float32, 'ssm_c_im': _jnp.float32, 'ssm_d': _jnp.float32, 'ssm_w_glu': _jnp.float32, 'ssm_b_glu': _jnp.float32, 'ssm_w_out': _jnp.float32, 'fox_w_in': _jnp.float32, 'fox_w_fg': _jnp.float32, 'fox_b_fg': _jnp.float32, 'fox_w_out': _jnp.float32, 'ple_proj': _jnp.float32, 'ple_norm': _jnp.float32, 'ple_gate': _jnp.float32}
MOMENT_SCALE = {'norm_g': 3.982234e-02, 'final_g': 1.600674e+01, 'rel_bias': 1.410424e-02, 'swa_w_in': 1.446417e-02, 'swa_w_out': 1.151553e-02, 'swa_sinks': 5.656511e-03, 'conv_w_in': 3.567413e-02, 'conv_kernel': 3.474307e-02, 'conv_w_out': 3.529715e-02, 'ssm_w_in': 1.240224e-02, 'ssm_lam_re': 8.810105e-04, 'ssm_lam_im': 9.028784e-04, 'ssm_log_dt': 5.392151e-01, 'ssm_b_re': 5.945921e-04, 'ssm_b_im': 5.857796e-04, 'ssm_c_re': 8.336417e-04, 'ssm_c_im': 8.326984e-04, 'ssm_d': 1.292409e-02, 'ssm_w_glu': 8.695751e-03, 'ssm_b_glu': 1.343202e-02, 'ssm_w_out': 1.182587e-02, 'fox_w_in': 9.809558e-03, 'fox_w_fg': 1.669372e-02, 'fox_b_fg': 3.761027e-02, 'fox_w_out': 1.086618e-02, 'ple_proj': 3.539540e-02, 'ple_norm': 1.382148e-02, 'ple_gate': 1.381359e-02}


def _to_microbatches(a, axis):
    t = _jnp.moveaxis(a, axis, 0)
    t = t.reshape((N_MICROBATCH, t.shape[0] // N_MICROBATCH) + t.shape[1:])
    return _jnp.moveaxis(t, 1, axis + 1)


def setup_inputs(seed: int = 0) -> dict:
    inp = _fwd_setup_inputs(seed)
    key = _jax.random.fold_in(_jax.random.key(seed), 7919)
    shape, _ = _output_shape()
    out = dict(inp)
    out["loss_target"] = _jax.random.normal(_jax.random.fold_in(key, 0), shape, _jnp.float32)
    for i, name in enumerate(TWIN_WEIGHTS):
        w = inp[name].astype(_jnp.float32)
        if MOMENT_SCALE is None:
            s = _jnp.sqrt(_jnp.mean(_jnp.square(w)) + 1e-30)
        else:
            s = MOMENT_SCALE[name]
        km, kv = _jax.random.split(_jax.random.fold_in(key, i + 1))
        out[name] = w
        out["m_" + name] = s * _jax.random.normal(km, w.shape, _jnp.float32)
        out["v_" + name] = (s * s) * _jax.random.uniform(kv, w.shape, _jnp.float32, 0.5, 1.5)
    if N_MICROBATCH > 1:
        for name, axis in PER_EXAMPLE_BATCH_AXIS.items():
            out[name] = _to_microbatches(out[name], axis)
    return {'x': out['x'], 'p': out['p'], 'norm_g': out['norm_g'], 'final_g': out['final_g'], 'rel_bias': out['rel_bias'], 'swa_w_in': out['swa_w_in'], 'swa_w_out': out['swa_w_out'], 'swa_sinks': out['swa_sinks'], 'conv_w_in': out['conv_w_in'], 'conv_kernel': out['conv_kernel'], 'conv_w_out': out['conv_w_out'], 'ssm_w_in': out['ssm_w_in'], 'ssm_lam_re': out['ssm_lam_re'], 'ssm_lam_im': out['ssm_lam_im'], 'ssm_log_dt': out['ssm_log_dt'], 'ssm_b_re': out['ssm_b_re'], 'ssm_b_im': out['ssm_b_im'], 'ssm_c_re': out['ssm_c_re'], 'ssm_c_im': out['ssm_c_im'], 'ssm_d': out['ssm_d'], 'ssm_w_glu': out['ssm_w_glu'], 'ssm_b_glu': out['ssm_b_glu'], 'ssm_w_out': out['ssm_w_out'], 'fox_w_in': out['fox_w_in'], 'fox_w_fg': out['fox_w_fg'], 'fox_b_fg': out['fox_b_fg'], 'fox_w_out': out['fox_w_out'], 'ple_proj': out['ple_proj'], 'ple_norm': out['ple_norm'], 'ple_gate': out['ple_gate'], 'loss_target': out['loss_target'], 'm_norm_g': out['m_norm_g'], 'm_final_g': out['m_final_g'], 'm_rel_bias': out['m_rel_bias'], 'm_swa_w_in': out['m_swa_w_in'], 'm_swa_w_out': out['m_swa_w_out'], 'm_swa_sinks': out['m_swa_sinks'], 'm_conv_w_in': out['m_conv_w_in'], 'm_conv_kernel': out['m_conv_kernel'], 'm_conv_w_out': out['m_conv_w_out'], 'm_ssm_w_in': out['m_ssm_w_in'], 'm_ssm_lam_re': out['m_ssm_lam_re'], 'm_ssm_lam_im': out['m_ssm_lam_im'], 'm_ssm_log_dt': out['m_ssm_log_dt'], 'm_ssm_b_re': out['m_ssm_b_re'], 'm_ssm_b_im': out['m_ssm_b_im'], 'm_ssm_c_re': out['m_ssm_c_re'], 'm_ssm_c_im': out['m_ssm_c_im'], 'm_ssm_d': out['m_ssm_d'], 'm_ssm_w_glu': out['m_ssm_w_glu'], 'm_ssm_b_glu': out['m_ssm_b_glu'], 'm_ssm_w_out': out['m_ssm_w_out'], 'm_fox_w_in': out['m_fox_w_in'], 'm_fox_w_fg': out['m_fox_w_fg'], 'm_fox_b_fg': out['m_fox_b_fg'], 'm_fox_w_out': out['m_fox_w_out'], 'm_ple_proj': out['m_ple_proj'], 'm_ple_norm': out['m_ple_norm'], 'm_ple_gate': out['m_ple_gate'], 'v_norm_g': out['v_norm_g'], 'v_final_g': out['v_final_g'], 'v_rel_bias': out['v_rel_bias'], 'v_swa_w_in': out['v_swa_w_in'], 'v_swa_w_out': out['v_swa_w_out'], 'v_swa_sinks': out['v_swa_sinks'], 'v_conv_w_in': out['v_conv_w_in'], 'v_conv_kernel': out['v_conv_kernel'], 'v_conv_w_out': out['v_conv_w_out'], 'v_ssm_w_in': out['v_ssm_w_in'], 'v_ssm_lam_re': out['v_ssm_lam_re'], 'v_ssm_lam_im': out['v_ssm_lam_im'], 'v_ssm_log_dt': out['v_ssm_log_dt'], 'v_ssm_b_re': out['v_ssm_b_re'], 'v_ssm_b_im': out['v_ssm_b_im'], 'v_ssm_c_re': out['v_ssm_c_re'], 'v_ssm_c_im': out['v_ssm_c_im'], 'v_ssm_d': out['v_ssm_d'], 'v_ssm_w_glu': out['v_ssm_w_glu'], 'v_ssm_b_glu': out['v_ssm_b_glu'], 'v_ssm_w_out': out['v_ssm_w_out'], 'v_fox_w_in': out['v_fox_w_in'], 'v_fox_w_fg': out['v_fox_w_fg'], 'v_fox_b_fg': out['v_fox_b_fg'], 'v_fox_w_out': out['v_fox_w_out'], 'v_ple_proj': out['v_ple_proj'], 'v_ple_norm': out['v_ple_norm'], 'v_ple_gate': out['v_ple_gate']}


def _loss(weights, diff, rest, loss_target):
    with _jax.named_scope("forward"):
        args = {**rest, TWIN_DIFF_INPUT: diff, **{k: w.astype(_WEIGHT_DTYPES[k]) for k, w in weights.items()}}
        y = _forward(args)
    with _jax.named_scope("loss_head"):
        err = _jnp.square(y.astype(_jnp.float32) - loss_target)
        return 0.5 * _jnp.sum(_jnp.mean(err, axis=-1)) if err.ndim else 0.5 * err


def _adamw(w, g, m, v):
    m = ADAM_B1 * m + (1.0 - ADAM_B1) * g
    v = ADAM_B2 * v + (1.0 - ADAM_B2) * _jnp.square(g)
    m_hat = m / (1.0 - ADAM_B1 ** ADAM_STEP)
    v_hat = v / (1.0 - ADAM_B2 ** ADAM_STEP)
    delta = -ADAM_LR * (m_hat / (_jnp.sqrt(v_hat) + ADAM_EPS) + ADAM_WD * w)
    return delta, m, v


def reference(x, p, norm_g, final_g, rel_bias, swa_w_in, swa_w_out, swa_sinks, conv_w_in, conv_kernel, conv_w_out, ssm_w_in, ssm_lam_re, ssm_lam_im, ssm_log_dt, ssm_b_re, ssm_b_im, ssm_c_re, ssm_c_im, ssm_d, ssm_w_glu, ssm_b_glu, ssm_w_out, fox_w_in, fox_w_fg, fox_b_fg, fox_w_out, ple_proj, ple_norm, ple_gate, loss_target, m_norm_g, m_final_g, m_rel_bias, m_swa_w_in, m_swa_w_out, m_swa_sinks, m_conv_w_in, m_conv_kernel, m_conv_w_out, m_ssm_w_in, m_ssm_lam_re, m_ssm_lam_im, m_ssm_log_dt, m_ssm_b_re, m_ssm_b_im, m_ssm_c_re, m_ssm_c_im, m_ssm_d, m_ssm_w_glu, m_ssm_b_glu, m_ssm_w_out, m_fox_w_in, m_fox_w_fg, m_fox_b_fg, m_fox_w_out, m_ple_proj, m_ple_norm, m_ple_gate, v_norm_g, v_final_g, v_rel_bias, v_swa_w_in, v_swa_w_out, v_swa_sinks, v_conv_w_in, v_conv_kernel, v_conv_w_out, v_ssm_w_in, v_ssm_lam_re, v_ssm_lam_im, v_ssm_log_dt, v_ssm_b_re, v_ssm_b_im, v_ssm_c_re, v_ssm_c_im, v_ssm_d, v_ssm_w_glu, v_ssm_b_glu, v_ssm_w_out, v_fox_w_in, v_fox_w_fg, v_fox_b_fg, v_fox_w_out, v_ple_proj, v_ple_norm, v_ple_gate):
    given = dict(x=x, p=p, norm_g=norm_g, final_g=final_g, rel_bias=rel_bias, swa_w_in=swa_w_in, swa_w_out=swa_w_out, swa_sinks=swa_sinks, conv_w_in=conv_w_in, conv_kernel=conv_kernel, conv_w_out=conv_w_out, ssm_w_in=ssm_w_in, ssm_lam_re=ssm_lam_re, ssm_lam_im=ssm_lam_im, ssm_log_dt=ssm_log_dt, ssm_b_re=ssm_b_re, ssm_b_im=ssm_b_im, ssm_c_re=ssm_c_re, ssm_c_im=ssm_c_im, ssm_d=ssm_d, ssm_w_glu=ssm_w_glu, ssm_b_glu=ssm_b_glu, ssm_w_out=ssm_w_out, fox_w_in=fox_w_in, fox_w_fg=fox_w_fg, fox_b_fg=fox_b_fg, fox_w_out=fox_w_out, ple_proj=ple_proj, ple_norm=ple_norm, ple_gate=ple_gate, loss_target=loss_target, m_norm_g=m_norm_g, m_final_g=m_final_g, m_rel_bias=m_rel_bias, m_swa_w_in=m_swa_w_in, m_swa_w_out=m_swa_w_out, m_swa_sinks=m_swa_sinks, m_conv_w_in=m_conv_w_in, m_conv_kernel=m_conv_kernel, m_conv_w_out=m_conv_w_out, m_ssm_w_in=m_ssm_w_in, m_ssm_lam_re=m_ssm_lam_re, m_ssm_lam_im=m_ssm_lam_im, m_ssm_log_dt=m_ssm_log_dt, m_ssm_b_re=m_ssm_b_re, m_ssm_b_im=m_ssm_b_im, m_ssm_c_re=m_ssm_c_re, m_ssm_c_im=m_ssm_c_im, m_ssm_d=m_ssm_d, m_ssm_w_glu=m_ssm_w_glu, m_ssm_b_glu=m_ssm_b_glu, m_ssm_w_out=m_ssm_w_out, m_fox_w_in=m_fox_w_in, m_fox_w_fg=m_fox_w_fg, m_fox_b_fg=m_fox_b_fg, m_fox_w_out=m_fox_w_out, m_ple_proj=m_ple_proj, m_ple_norm=m_ple_norm, m_ple_gate=m_ple_gate, v_norm_g=v_norm_g, v_final_g=v_final_g, v_rel_bias=v_rel_bias, v_swa_w_in=v_swa_w_in, v_swa_w_out=v_swa_w_out, v_swa_sinks=v_swa_sinks, v_conv_w_in=v_conv_w_in, v_conv_kernel=v_conv_kernel, v_conv_w_out=v_conv_w_out, v_ssm_w_in=v_ssm_w_in, v_ssm_lam_re=v_ssm_lam_re, v_ssm_lam_im=v_ssm_lam_im, v_ssm_log_dt=v_ssm_log_dt, v_ssm_b_re=v_ssm_b_re, v_ssm_b_im=v_ssm_b_im, v_ssm_c_re=v_ssm_c_re, v_ssm_c_im=v_ssm_c_im, v_ssm_d=v_ssm_d, v_ssm_w_glu=v_ssm_w_glu, v_ssm_b_glu=v_ssm_b_glu, v_ssm_w_out=v_ssm_w_out, v_fox_w_in=v_fox_w_in, v_fox_w_fg=v_fox_w_fg, v_fox_b_fg=v_fox_b_fg, v_fox_w_out=v_fox_w_out, v_ple_proj=v_ple_proj, v_ple_norm=v_ple_norm, v_ple_gate=v_ple_gate)
    weights = {n: given[n] for n in TWIN_WEIGHTS}
    shared = {n: given[n] for n in SHARED_INPUTS}
    per_example = {n: given[n] for n in ['x', 'p']}
    grad_fn = _jax.value_and_grad(_loss, argnums=(0, 1))

    def one_microbatch(ex, loss_target):
        ex = dict(ex)
        diff = ex.pop(TWIN_DIFF_INPUT)
        return grad_fn(weights, diff, {**shared, **ex}, loss_target)

    if N_MICROBATCH == 1:
        loss, (grad_w, grad_x) = one_microbatch(per_example, given["loss_target"])
    else:
        def body(carry, xs):
            loss_sum, grad_sum = carry
            l_k, (gw_k, gx_k) = one_microbatch(xs[0], xs[1])
            with _jax.named_scope("update"):
                return (loss_sum + l_k, _jax.tree.map(_jnp.add, grad_sum, gw_k)), gx_k

        init = (_jnp.zeros((), _jnp.float32), _jax.tree.map(_jnp.zeros_like, weights))
        (loss, grad_w), grad_x = _jax.lax.scan(body, init, (per_example, given["loss_target"]))
    with _jax.named_scope("update"):
        delta_w, new_m, new_v = {}, {}, {}
        for n in TWIN_WEIGHTS:
            delta_w[n], new_m[n], new_v[n] = _adamw(weights[n], grad_w[n], given["m_" + n], given["v_" + n])
    return (loss, grad_x, *[grad_w[n] for n in TWIN_WEIGHTS], *[delta_w[n] for n in TWIN_WEIGHTS],
            *[new_m[n] for n in TWIN_WEIGHTS], *[new_v[n] for n in TWIN_WEIGHTS])
```

```python
import functools
import math

import numpy as np
import jax
import jax.numpy as jnp
from jax import lax
from jax.experimental import pallas as pl
from jax.experimental.pallas import tpu as pltpu

F32 = jnp.float32
BF16 = jnp.bfloat16

EPS = 1e-6
BLOCK = 128
REL_BUCKETS = 32
REL_MAX_DIST = 128
SWA_HEADS, SWA_KV_HEADS, HEAD_DIM = 32, 4, 64
SWA_GROUP = SWA_HEADS // SWA_KV_HEADS
FOX_HEADS = 32
SSM_GROUP, SSM_STATE = 16, 64
GROUPS_PER_STEP = 8
STATE_W = GROUPS_PER_STEP * SSM_STATE
LANES = 128
NEG = -1e30

ADAM_LR, ADAM_B1, ADAM_B2, ADAM_EPS, ADAM_WD, ADAM_STEP = 0.001, 0.9, 0.999, 1e-08, 0.01, 10

VMEM_LIMIT_V7X = 56 * 1024 * 1024
FLAT_W = 1024
MESH = pl.DeviceIdType.MESH
ANY = pl.BlockSpec(memory_space=pl.ANY)


def _cp(sem):
    return pltpu.CompilerParams(dimension_semantics=sem, vmem_limit_bytes=VMEM_LIMIT_V7X)


def _tile(n, target, mult=LANES):
    if n <= target:
        return n
    t = (target // mult) * mult
    while t >= mult:
        if n % t == 0:
            return t
        t -= mult
    return n


def _mm(a, b, *, ta=False, tb=False, out_dtype=BF16, name, tm=1024, tn=1024, tk=512):
    if ta:
        kdim, m = a.shape
    else:
        m, kdim = a.shape
    n = b.shape[0] if tb else b.shape[1]
    assert (b.shape[1] if tb else b.shape[0]) == kdim
    tm, tn, tk = _tile(m, tm), _tile(n, tn), _tile(kdim, tk)
    nk = kdim // tk
    dn = (((0 if ta else 1,), (1 if tb else 0,)), ((), ()))

    def body(a_ref, b_ref, o_ref, acc_ref):
        k = pl.program_id(2)

        @pl.when(k == 0)
        def _():
            acc_ref[...] = jnp.zeros_like(acc_ref)

        acc_ref[...] += lax.dot_general(a_ref[...].astype(BF16), b_ref[...].astype(BF16), dn,
                                        preferred_element_type=F32)

        @pl.when(k == nk - 1)
        def _():
            o_ref[...] = acc_ref[...].astype(o_ref.dtype)

    a_spec = pl.BlockSpec((tk, tm), lambda i, j, k: (k, i)) if ta else pl.BlockSpec((tm, tk), lambda i, j, k: (i, k))
    b_spec = pl.BlockSpec((tn, tk), lambda i, j, k: (j, k)) if tb else pl.BlockSpec((tk, tn), lambda i, j, k: (k, j))
    return pl.pallas_call(
        body, name=name, grid=(m // tm, n // tn, nk),
        in_specs=[a_spec, b_spec], out_specs=pl.BlockSpec((tm, tn), lambda i, j, k: (i, j)),
        out_shape=jax.ShapeDtypeStruct((m, n), out_dtype),
        scratch_shapes=[pltpu.VMEM((tm, tn), F32)],
        compiler_params=_cp(("parallel", "parallel", "arbitrary")),
    )(a, b)


def _rows(arr, t):
    return (arr, (t, arr.shape[1]), lambda i: (i, 0))


def _cols(arr, cb, off=0):
    return (arr, (arr.shape[0], cb), lambda i: (0, i + off))


def _full(arr):
    nd = arr.ndim
    return (arr, arr.shape, lambda i: (0,) * nd)


def _lead(arr):
    return (arr, (1,) + arr.shape[1:], lambda i: (i, 0, 0))


def _out(shape, dtype, block, imap, acc=False):
    return (jax.ShapeDtypeStruct(shape, dtype), block, imap, acc)


def _orows(s, w, dtype, t):
    return _out((s, w), dtype, (t, w), lambda i: (i, 0))


def _ocols(s, w, dtype, cb):
    return _out((s, w), dtype, (s, cb), lambda i: (0, i))


def _oacc(shape):
    nd = len(shape)
    return _out(shape, F32, shape, lambda i: (0,) * nd, True)


def _olead(n, a, b, dtype=F32):
    return _out((n, a, b), dtype, (1, a, b), lambda i: (i, 0, 0))


def _tiled(fn, ins, outs, n, name):
    has_acc = any(o[3] for o in outs)
    ni = len(ins)

    def body(*refs):
        vals = fn(*[r[...] for r in refs[:ni]])
        i = pl.program_id(0)
        for r, v, o in zip(refs[ni:], vals, outs):
            if o[3]:
                @pl.when(i == 0)
                def _(r=r, v=v):
                    r[...] = v.astype(r.dtype)

                @pl.when(i > 0)
                def _(r=r, v=v):
                    r[...] += v.astype(r.dtype)
            else:
                r[...] = v.astype(r.dtype)

    res = pl.pallas_call(
        body, name=name, grid=(n,),
        in_specs=[pl.BlockSpec(b, m) for _, b, m in ins],
        out_specs=[pl.BlockSpec(b, m) for _, b, m, _ in outs],
        out_shape=[s for s, _, _, _ in outs],
        compiler_params=_cp(("arbitrary",) if has_acc else ("parallel",)),
    )(*[a for a, _, _ in ins])
    return res


def _silu(x):
    return x * jax.nn.sigmoid(x)


def _gelu(x):
    return 0.5 * x * (1.0 + jnp.tanh(math.sqrt(2.0 / math.pi) * (x + 0.044715 * (x * x * x))))


def _rms(x, g):
    r = lax.rsqrt(jnp.mean(x * x, axis=-1, keepdims=True) + EPS)
    return x * r * g


def _shift_rows(x, sh, up=False):
    s = x.shape[0]
    rows = lax.broadcasted_iota(jnp.int32, x.shape, 0)
    if up:
        return jnp.where(rows < s - sh, pltpu.roll(x, s - sh, 0), 0.0)
    return jnp.where(rows >= sh, pltpu.roll(x, sh, 0), 0.0)


def _scan_complex(xr, xi, pr, pi, reverse):
    s = xr.shape[0]
    k = 0
    while (1 << k) < s:
        sh = 1 << k
        sr, si = _shift_rows(xr, sh, reverse), _shift_rows(xi, sh, reverse)
        ar, ai = pr[k:k + 1, :], pi[k:k + 1, :]
        xr, xi = xr + ar * sr - ai * si, xi + ar * si + ai * sr
        k += 1
    return xr, xi


def _cumsum_rows(x, reverse):
    s = x.shape[0]
    k = 0
    while (1 << k) < s:
        x = x + _shift_rows(x, 1 << k, reverse)
        k += 1
    return x


def _dot(a, b, ca=1, cb=0):
    return lax.dot_general(a.astype(BF16), b.astype(BF16), (((ca,), (cb,)), ((), ())), preferred_element_type=F32)


def _t5_bucket(dist):
    max_exact = REL_BUCKETS // 2
    d = np.maximum(dist, 1).astype(np.float32)
    large = max_exact + (np.log(d / max_exact) / np.log(REL_MAX_DIST / max_exact) * (REL_BUCKETS - max_exact)).astype(np.int32)
    large = np.minimum(large, REL_BUCKETS - 1)
    return np.where(dist < max_exact, dist, large).astype(np.int32)


def _swa_bucket_table():
    qi = np.arange(BLOCK)[:, None]
    kj = np.arange(2 * BLOCK)[None, :]
    return _t5_bucket(np.clip(qi + BLOCK - kj, 0, None))


def _swa_scores(qg, kb, bias_g, sk, n):
    s = _dot(qg, kb, 1, 1) * (HEAD_DIM ** -0.5) + bias_g
    row = lax.broadcasted_iota(jnp.int32, s.shape, 0)
    col = lax.broadcasted_iota(jnp.int32, s.shape, 1)
    dist = row + BLOCK - col
    mask = (dist >= 0) & (dist < BLOCK) & ((col >= BLOCK) | (n > 0))
    s = jnp.where(mask, s, NEG)
    m = jnp.maximum(jnp.max(s, axis=1, keepdims=True), sk)
    e = jnp.exp(s - m)
    es = jnp.exp(sk - m)
    den = jnp.sum(e, axis=1, keepdims=True) + es
    return e / den, es / den


def _swa_specs(proj, k4, v4, nb, clamp):
    gw = SWA_GROUP * HEAD_DIM
    gate_off = (SWA_HEADS * HEAD_DIM + 2 * SWA_KV_HEADS * HEAD_DIM) // gw
    cur = (lambda n: jnp.minimum(n, nb - 1)) if clamp else (lambda n: n)
    prev = lambda n: jnp.maximum(cur(n) - 1, 0)
    return [
        pl.BlockSpec((BLOCK, gw), lambda h, n: (cur(n), h)),
        pl.BlockSpec((1, BLOCK, HEAD_DIM), lambda h, n: (h, cur(n), 0)),
        pl.BlockSpec((1, BLOCK, HEAD_DIM), lambda h, n: (h, prev(n), 0)),
        pl.BlockSpec((1, BLOCK, HEAD_DIM), lambda h, n: (h, cur(n), 0)),
        pl.BlockSpec((1, BLOCK, HEAD_DIM), lambda h, n: (h, prev(n), 0)),
        pl.BlockSpec((BLOCK, gw), lambda h, n: (cur(n), gate_off + h)),
    ], [proj, k4, k4, v4, v4, proj]


def _swa_fwd(proj, k4, v4, bias, sinks):
    s = proj.shape[0]
    nb = s // BLOCK
    gw = SWA_GROUP * HEAD_DIM

    def body(q_ref, kc_ref, kp_ref, vc_ref, vp_ref, gate_ref, bias_ref, sink_ref, a_ref):
        n = pl.program_id(1)
        kb = jnp.concatenate([kp_ref[0], kc_ref[0]], axis=0)
        vb = jnp.concatenate([vp_ref[0], vc_ref[0]], axis=0)
        for g in range(SWA_GROUP):
            sl = slice(g * HEAD_DIM, (g + 1) * HEAD_DIM)
            p, _ = _swa_scores(q_ref[:, sl], kb, bias_ref[g], sink_ref[0, g:g + 1, :1], n)
            og = _dot(p, vb)
            a_ref[:, sl] = (og * _silu(gate_ref[:, sl].astype(F32))).astype(a_ref.dtype)

    specs, args = _swa_specs(proj, k4, v4, nb, False)
    return pl.pallas_call(
        body, name="swa_fwd", grid=(SWA_KV_HEADS, nb),
        in_specs=specs + [pl.BlockSpec((SWA_GROUP, BLOCK, 2 * BLOCK), lambda h, n: (h, 0, 0)),
                          pl.BlockSpec((1, SWA_GROUP, LANES), lambda h, n: (h, 0, 0))],
        out_specs=pl.BlockSpec((BLOCK, gw), lambda h, n: (n, h)),
        out_shape=jax.ShapeDtypeStruct((s, SWA_HEADS * HEAD_DIM), BF16),
        compiler_params=_cp(("parallel", "parallel")),
    )(*args, bias, sinks)


def _swa_bwd(proj, k4, v4, bias, sinks, da):
    s = proj.shape[0]
    nb = s // BLOCK
    gw = SWA_GROUP * HEAD_DIM

    def body(q_ref, kc_ref, kp_ref, vc_ref, vp_ref, gate_ref, bias_ref, sink_ref, da_ref,
             dq_ref, dgate_ref, dk_ref, dv_ref, dbias_ref, dsink_ref, dk_own, dv_own):
        n = pl.program_id(1)

        @pl.when(n == 0)
        def _():
            dk_own[...] = jnp.zeros_like(dk_own)
            dv_own[...] = jnp.zeros_like(dv_own)
            dbias_ref[...] = jnp.zeros_like(dbias_ref)
            dsink_ref[...] = jnp.zeros_like(dsink_ref)

        @pl.when(n < nb)
        def _():
            kb = jnp.concatenate([kp_ref[0], kc_ref[0]], axis=0)
            vb = jnp.concatenate([vp_ref[0], vc_ref[0]], axis=0)
            dkb = jnp.zeros((2 * BLOCK, HEAD_DIM), F32)
            dvb = jnp.zeros((2 * BLOCK, HEAD_DIM), F32)
            for g in range(SWA_GROUP):
                sl = slice(g * HEAD_DIM, (g + 1) * HEAD_DIM)
                qg = q_ref[:, sl]
                p, p0 = _swa_scores(qg, kb, bias_ref[g], sink_ref[0, g:g + 1, :1], n)
                og = _dot(p, vb)
                gate = gate_ref[:, sl].astype(F32)
                dag = da_ref[:, sl].astype(F32)
                sg = jax.nn.sigmoid(gate)
                do = dag * gate * sg
                dgate_ref[:, sl] = (dag * og * sg * (1.0 + gate * (1.0 - sg))).astype(dgate_ref.dtype)
                dp = _dot(do, vb, 1, 1)
                delta = jnp.sum(do * og, axis=1, keepdims=True)
                ds = p * (dp - delta)
                dbias_ref[g] += ds
                dsink_ref[0, g:g + 1, :] += jnp.zeros((1, LANES), F32) - jnp.sum(p0 * delta, axis=0, keepdims=True)
                dq_ref[:, sl] = (_dot(ds, kb) * (HEAD_DIM ** -0.5)).astype(dq_ref.dtype)
                dkb += _dot(ds, qg, 0, 0) * (HEAD_DIM ** -0.5)
                dvb += _dot(p, do, 0, 0)
            dk_ref[0] = dk_own[...] + dkb[:BLOCK]
            dv_ref[0] = dv_own[...] + dvb[:BLOCK]
            dk_own[...] = dkb[BLOCK:]
            dv_own[...] = dvb[BLOCK:]

        @pl.when(n == nb)
        def _():
            dk_ref[0] = dk_own[...]
            dv_ref[0] = dv_own[...]

    specs, args = _swa_specs(proj, k4, v4, nb, True)
    cur = lambda n: jnp.minimum(n, nb - 1)
    trail = lambda n: jnp.maximum(n - 1, 0)
    return pl.pallas_call(
        body, name="swa_bwd", grid=(SWA_KV_HEADS, nb + 1),
        in_specs=specs + [pl.BlockSpec((SWA_GROUP, BLOCK, 2 * BLOCK), lambda h, n: (h, 0, 0)),
                          pl.BlockSpec((1, SWA_GROUP, LANES), lambda h, n: (h, 0, 0)),
                          pl.BlockSpec((BLOCK, gw), lambda h, n: (cur(n), h))],
        out_specs=[pl.BlockSpec((BLOCK, gw), lambda h, n: (cur(n), h)),
                   pl.BlockSpec((BLOCK, gw), lambda h, n: (cur(n), h)),
                   pl.BlockSpec((1, BLOCK, HEAD_DIM), lambda h, n: (h, trail(n), 0)),
                   pl.BlockSpec((1, BLOCK, HEAD_DIM), lambda h, n: (h, trail(n), 0)),
                   pl.BlockSpec((SWA_GROUP, BLOCK, 2 * BLOCK), lambda h, n: (h, 0, 0)),
                   pl.BlockSpec((1, SWA_GROUP, LANES), lambda h, n: (h, 0, 0))],
        out_shape=[jax.ShapeDtypeStruct((s, SWA_HEADS * HEAD_DIM), BF16),
                   jax.ShapeDtypeStruct((s, SWA_HEADS * HEAD_DIM), BF16),
                   jax.ShapeDtypeStruct((SWA_KV_HEADS, s, HEAD_DIM), F32),
                   jax.ShapeDtypeStruct((SWA_KV_HEADS, s, HEAD_DIM), F32),
                   jax.ShapeDtypeStruct((SWA_HEADS, BLOCK, 2 * BLOCK), F32),
                   jax.ShapeDtypeStruct((SWA_KV_HEADS, SWA_GROUP, LANES), F32)],
        scratch_shapes=[pltpu.VMEM((BLOCK, HEAD_DIM), F32), pltpu.VMEM((BLOCK, HEAD_DIM), F32)],
        compiler_params=_cp(("arbitrary", "arbitrary")),
    )(*args, bias, sinks, da)


FOX_TILE = 512


def _fox_p(q, k, cq, ck, i, j, t, stat):
    s = _dot(q, k, 1, 1) * (HEAD_DIM ** -0.5) + cq - ck
    rows = i * t + lax.broadcasted_iota(jnp.int32, s.shape, 0)
    cols = j * t + lax.broadcasted_iota(jnp.int32, s.shape, 1)
    return s, cols <= rows


def _fox_fwd(proj, cq, ck):
    s = proj.shape[0]
    t = min(FOX_TILE, s)
    nt = s // t
    nh2 = FOX_HEADS // 2
    hd = HEAD_DIM

    def body(q_ref, k_ref, v_ref, cq_ref, ck_ref, o_ref, lse_ref, m_s, l_s, acc_s):
        i, j = pl.program_id(1), pl.program_id(2)

        @pl.when(j == 0)
        def _():
            m_s[...] = jnp.full_like(m_s, NEG)
            l_s[...] = jnp.zeros_like(l_s)
            acc_s[...] = jnp.zeros_like(acc_s)

        @pl.when(j <= i)
        def _():
            for e in range(2):
                sl = slice(e * hd, (e + 1) * hd)
                sc, mask = _fox_p(q_ref[:, sl], k_ref[:, sl], cq_ref[e], ck_ref[e], i, j, t, None)
                sc = jnp.where(mask, sc, NEG)
                m_old = m_s[e]
                m_new = jnp.maximum(m_old, jnp.max(sc, axis=1, keepdims=True))
                alpha = jnp.exp(m_old - m_new)
                p = jnp.exp(sc - m_new)
                l_s[e] = alpha * l_s[e] + jnp.sum(p, axis=1, keepdims=True)
                acc_s[e] = alpha * acc_s[e] + _dot(p, v_ref[:, sl])
                m_s[e] = m_new

        @pl.when(j == nt - 1)
        def _():
            for e in range(2):
                o_ref[:, e * hd:(e + 1) * hd] = (acc_s[e] / l_s[e]).astype(o_ref.dtype)
                lse_ref[e] = m_s[e] + jnp.log(l_s[e])

    kcol = FOX_HEADS * hd // LANES
    return pl.pallas_call(
        body, name="fox_fwd", grid=(nh2, nt, nt),
        in_specs=[pl.BlockSpec((t, LANES), lambda h, i, j: (i, h)),
                  pl.BlockSpec((t, LANES), lambda h, i, j: (jnp.minimum(j, i), kcol + h)),
                  pl.BlockSpec((t, LANES), lambda h, i, j: (jnp.minimum(j, i), 2 * kcol + h)),
                  pl.BlockSpec((2, t, 1), lambda h, i, j: (h, i, 0)),
                  pl.BlockSpec((2, 1, t), lambda h, i, j: (h, 0, jnp.minimum(j, i)))],
        out_specs=[pl.BlockSpec((t, LANES), lambda h, i, j: (i, h)),
                   pl.BlockSpec((2, t, 1), lambda h, i, j: (h, i, 0))],
        out_shape=[jax.ShapeDtypeStruct((s, FOX_HEADS * hd), BF16),
                   jax.ShapeDtypeStruct((FOX_HEADS, s, 1), F32)],
        scratch_shapes=[pltpu.VMEM((2, t, 1), F32), pltpu.VMEM((2, t, 1), F32), pltpu.VMEM((2, t, hd), F32)],
        compiler_params=_cp(("parallel", "parallel", "arbitrary")),
    )(proj, proj, proj, cq, ck)


def _fox_bwd_dq(proj, cq, ck, o, do, lse):
    s = proj.shape[0]
    t = min(FOX_TILE, s)
    nt = s // t
    nh2 = FOX_HEADS // 2
    hd = HEAD_DIM

    def body(q_ref, k_ref, v_ref, cq_ref, ck_ref, o_ref, do_ref, lse_ref, dq_ref, delta_ref, dcq_ref, acc_s):
        i, j = pl.program_id(1), pl.program_id(2)

        @pl.when(j == 0)
        def _():
            acc_s[...] = jnp.zeros_like(acc_s)
            dcq_ref[...] = jnp.zeros_like(dcq_ref)
            for e in range(2):
                sl = slice(e * hd, (e + 1) * hd)
                delta_ref[e] = jnp.sum(do_ref[:, sl].astype(F32) * o_ref[:, sl].astype(F32), axis=1, keepdims=True)

        @pl.when(j <= i)
        def _():
            for e in range(2):
                sl = slice(e * hd, (e + 1) * hd)
                sc, mask = _fox_p(q_ref[:, sl], k_ref[:, sl], cq_ref[e], ck_ref[e], i, j, t, None)
                p = jnp.where(mask, jnp.exp(sc - lse_ref[e]), 0.0)
                dp = _dot(do_ref[:, sl], v_ref[:, sl], 1, 1)
                ds = p * (dp - delta_ref[e])
                dcq_ref[e] += jnp.sum(ds, axis=1, keepdims=True)
                acc_s[e] += _dot(ds, k_ref[:, sl])

        @pl.when(j == nt - 1)
        def _():
            for e in range(2):
                dq_ref[:, e * hd:(e + 1) * hd] = (acc_s[e] * (hd ** -0.5)).astype(dq_ref.dtype)

    kcol = FOX_HEADS * hd // LANES
    qmap = lambda h, i, j: (i, h)
    return pl.pallas_call(
        body, name="fox_bwd_dq", grid=(nh2, nt, nt),
        in_specs=[pl.BlockSpec((t, LANES), qmap),
                  pl.BlockSpec((t, LANES), lambda h, i, j: (jnp.minimum(j, i), kcol + h)),
                  pl.BlockSpec((t, LANES), lambda h, i, j: (jnp.minimum(j, i), 2 * kcol + h)),
                  pl.BlockSpec((2, t, 1), lambda h, i, j: (h, i, 0)),
                  pl.BlockSpec((2, 1, t), lambda h, i, j: (h, 0, jnp.minimum(j, i))),
                  pl.BlockSpec((t, LANES), qmap), pl.BlockSpec((t, LANES), qmap),
                  pl.BlockSpec((2, t, 1), lambda h, i, j: (h, i, 0))],
        out_specs=[pl.BlockSpec((t, LANES), qmap), pl.BlockSpec((2, t, 1), lambda h, i, j: (h, i, 0)),
                   pl.BlockSpec((2, t, 1), lambda h, i, j: (h, i, 0))],
        out_shape=[jax.ShapeDtypeStruct((s, FOX_HEADS * hd), BF16),
                   jax.ShapeDtypeStruct((FOX_HEADS, s, 1), F32), jax.ShapeDtypeStruct((FOX_HEADS, s, 1), F32)],
        scratch_shapes=[pltpu.VMEM((2, t, hd), F32)],
        compiler_params=_cp(("parallel", "parallel", "arbitrary")),
    )(proj, proj, proj, cq, ck, o, do, lse)


def _fox_bwd_dkv(proj, cq, ck, do, lse, delta):
    s = proj.shape[0]
    t = min(FOX_TILE, s)
    nt = s // t
    nh2 = FOX_HEADS // 2
    hd = HEAD_DIM

    def body(q_ref, k_ref, v_ref, cq_ref, ck_ref, do_ref, lse_ref, delta_ref, dk_ref, dv_ref, dck_ref,
             dk_s, dv_s, dck_s):
        j, i = pl.program_id(1), pl.program_id(2)

        @pl.when(i == 0)
        def _():
            dk_s[...] = jnp.zeros_like(dk_s)
            dv_s[...] = jnp.zeros_like(dv_s)
            dck_s[...] = jnp.zeros_like(dck_s)

        @pl.when(i >= j)
        def _():
            for e in range(2):
                sl = slice(e * hd, (e + 1) * hd)
                sc, mask = _fox_p(q_ref[:, sl], k_ref[:, sl], cq_ref[e], ck_ref[e], i, j, t, None)
                p = jnp.where(mask, jnp.exp(sc - lse_ref[e]), 0.0)
                dp = _dot(do_ref[:, sl], v_ref[:, sl], 1, 1)
                ds = p * (dp - delta_ref[e])
                dv_s[e] += _dot(p, do_ref[:, sl], 0, 0)
                dk_s[e] += _dot(ds, q_ref[:, sl], 0, 0)
                dck_s[e] += -jnp.sum(ds, axis=0, keepdims=True)

        @pl.when(i == nt - 1)
        def _():
            for e in range(2):
                dk_ref[:, e * hd:(e + 1) * hd] = (dk_s[e] * (hd ** -0.5)).astype(dk_ref.dtype)
                dv_ref[:, e * hd:(e + 1) * hd] = dv_s[e].astype(dv_ref.dtype)
                dck_ref[e] = dck_s[e]

    kcol = FOX_HEADS * hd // LANES
    qi = lambda i, j: jnp.maximum(i, j)
    return pl.pallas_call(
        body, name="fox_bwd_dkv", grid=(nh2, nt, nt),
        in_specs=[pl.BlockSpec((t, LANES), lambda h, j, i: (qi(i, j), h)),
                  pl.BlockSpec((t, LANES), lambda h, j, i: (j, kcol + h)),
                  pl.BlockSpec((t, LANES), lambda h, j, i: (j, 2 * kcol + h)),
                  pl.BlockSpec((2, t, 1), lambda h, j, i: (h, qi(i, j), 0)),
                  pl.BlockSpec((2, 1, t), lambda h, j, i: (h, 0, j)),
                  pl.BlockSpec((t, LANES), lambda h, j, i: (qi(i, j), h)),
                  pl.BlockSpec((2, t, 1), lambda h, j, i: (h, qi(i, j), 0)),
                  pl.BlockSpec((2, t, 1), lambda h, j, i: (h, qi(i, j), 0))],
        out_specs=[pl.BlockSpec((t, LANES), lambda h, j, i: (j, h)),
                   pl.BlockSpec((t, LANES), lambda h, j, i: (j, h)),
                   pl.BlockSpec((2, 1, t), lambda h, j, i: (h, 0, j))],
        out_shape=[jax.ShapeDtypeStruct((s, FOX_HEADS * hd), BF16),
                   jax.ShapeDtypeStruct((s, FOX_HEADS * hd), BF16),
                   jax.ShapeDtypeStruct((FOX_HEADS, 1, s), F32)],
        scratch_shapes=[pltpu.VMEM((2, t, hd), F32), pltpu.VMEM((2, t, hd), F32), pltpu.VMEM((2, 1, t), F32)],
        compiler_params=_cp(("parallel", "parallel", "arbitrary")),
    )(proj, proj, proj, cq, ck, do, lse, delta)


def _ssm_prep(lam_re, lam_im, log_dt, b_re, b_im, c_re, c_im):
    g, n = lam_re.shape
    dt = jnp.exp(log_dt)[:, None]
    mag = jnp.exp(lam_re * dt)
    ab_re = mag * jnp.cos(lam_im * dt)
    ab_im = mag * jnp.sin(lam_im * dt)
    den = lam_re * lam_re + lam_im * lam_im
    nr = ab_re - 1.0
    coef_re = ((nr * lam_re + ab_im * lam_im) / den)[..., None]
    coef_im = ((ab_im * lam_re - nr * lam_im) / den)[..., None]
    bb_re = coef_re * b_re - coef_im * b_im
    bb_im = coef_re * b_im + coef_im * b_re
    nblk = g // GROUPS_PER_STEP
    eye = jnp.eye(GROUPS_PER_STEP, dtype=F32)

    def bdiag(bb):
        return jnp.einsum('bgnc,gh->bgchn', bb.reshape(nblk, GROUPS_PER_STEP, n, SSM_GROUP), eye).reshape(
            nblk, GROUPS_PER_STEP * SSM_GROUP, STATE_W)

    def cdiag(cc):
        return jnp.einsum('bgcn,gh->bgnhc', cc.reshape(nblk, GROUPS_PER_STEP, SSM_GROUP, n), eye).reshape(
            nblk, STATE_W, GROUPS_PER_STEP * SSM_GROUP)

    return (ab_re.reshape(nblk, 1, STATE_W), ab_im.reshape(nblk, 1, STATE_W),
            bdiag(bb_re), bdiag(bb_im), cdiag(c_re), cdiag(c_im))


def _powers(a_re, a_im, levels):
    rs, ims = [a_re], [a_im]
    for _ in range(levels - 1):
        r, i = rs[-1], ims[-1]
        rs.append(r * r - i * i)
        ims.append(2.0 * r * i)
    return jnp.concatenate(rs, axis=1), jnp.concatenate(ims, axis=1)


def _ssm_fwd(proj, pw_re, pw_im, bd_re, bd_im, cd_re, cd_im, dskip):
    s = proj.shape[0]
    w = dskip.shape[1]
    nblk = w // LANES
    nch = STATE_W // LANES
    levels = pw_re.shape[1]

    def body(u_ref, pr, pi, br, bi, cr, ci, d_ref, pre_ref, yg_ref, xr_ref, xi_ref):
        ch = pl.program_id(1)
        u = u_ref[...]
        xr, xi = _scan_complex(_dot(u, br[0]), _dot(u, bi[0]), pr[0], pi[0], False)
        xr_ref[...] = xr.astype(BF16)
        xi_ref[...] = xi.astype(BF16)
        yc = _dot(xr, cr[0]) - _dot(xi, ci[0])

        @pl.when(ch == 0)
        def _():
            pre_ref[...] = yc + d_ref[...] * u

        @pl.when(ch > 0)
        def _():
            pre_ref[...] += yc

        @pl.when(ch == nch - 1)
        def _():
            yg_ref[...] = _gelu(pre_ref[...]).astype(BF16)

    blk = lambda b, c: (0, b)
    col = lambda b, c: (0, b * nch + c)
    in_chunk = pl.BlockSpec((1, LANES, LANES), lambda b, c: (b, 0, c))
    out_chunk = pl.BlockSpec((1, LANES, LANES), lambda b, c: (b, c, 0))
    pw_spec = pl.BlockSpec((1, levels, LANES), lambda b, c: (b, 0, c))
    return pl.pallas_call(
        body, name="ssm_fwd", grid=(nblk, nch),
        in_specs=[pl.BlockSpec((s, LANES), blk), pw_spec, pw_spec, in_chunk, in_chunk, out_chunk, out_chunk,
                  pl.BlockSpec((1, LANES), blk)],
        out_specs=[pl.BlockSpec((s, LANES), blk), pl.BlockSpec((s, LANES), blk),
                   pl.BlockSpec((s, LANES), col), pl.BlockSpec((s, LANES), col)],
        out_shape=[jax.ShapeDtypeStruct((s, w), F32), jax.ShapeDtypeStruct((s, w), BF16),
                   jax.ShapeDtypeStruct((s, nblk * STATE_W), BF16), jax.ShapeDtypeStruct((s, nblk * STATE_W), BF16)],
        compiler_params=_cp(("parallel", "arbitrary")),
    )(proj,pw_re, pw_im, bd_re, bd_im, cd_re, cd_im, dskip)


def _ssm_bwd(proj, dyg, pre, xr_all, xi_all, pw_re, pw_im, bd_re, bd_im, cd_re, cd_im, dskip):
    s = proj.shape[0]
    w = dskip.shape[1]
    nblk = w // LANES
    nch = STATE_W // LANES
    levels = pw_re.shape[1]

    def body(u_ref, dy_ref, pre_ref, xr_ref, xi_ref, pr, pi, br, bi, cr, ci, d_ref,
             du_ref, dd_ref, dar_ref, dai_ref, dbr_ref, dbi_ref, dcr_ref, dci_ref):
        ch = pl.program_id(1)
        u = u_ref[...]
        _, vjp = jax.vjp(_gelu, pre_ref[...])
        dpre = vjp(dy_ref[...].astype(F32))[0]
        zr, zi = _scan_complex(_dot(dpre, cr[0], 1, 1), -_dot(dpre, ci[0], 1, 1), pr[0], -pi[0], True)
        xpr = _shift_rows(xr_ref[...].astype(F32), 1)
        xpi = _shift_rows(xi_ref[...].astype(F32), 1)
        dar_ref[0] = jnp.sum(zr * xpr + zi * xpi, axis=0, keepdims=True)
        dai_ref[0] = jnp.sum(zi * xpr - zr * xpi, axis=0, keepdims=True)
        dcr_ref[0] = _dot(xr_ref[...], dpre, 0, 0)
        dci_ref[0] = -_dot(xi_ref[...], dpre, 0, 0)
        dbr_ref[0] = _dot(u, zr, 0, 0)
        dbi_ref[0] = _dot(u, zi, 0, 0)
        duc = _dot(zr, br[0], 1, 1) + _dot(zi, bi[0], 1, 1)

        @pl.when(ch == 0)
        def _():
            du_ref[...] = duc + dpre * d_ref[...]
            dd_ref[...] = jnp.sum(dpre * u, axis=0, keepdims=True)

        @pl.when(ch > 0)
        def _():
            du_ref[...] += duc

    blk = lambda b, c: (0, b)
    col = lambda b, c: (0, b * nch + c)
    in_chunk = pl.BlockSpec((1, LANES, LANES), lambda b, c: (b, 0, c))
    out_chunk = pl.BlockSpec((1, LANES, LANES), lambda b, c: (b, c, 0))
    pw_spec = pl.BlockSpec((1, levels, LANES), lambda b, c: (b, 0, c))
    a_spec = pl.BlockSpec((1, 1, LANES), lambda b, c: (b, 0, c))
    return pl.pallas_call(
        body, name="ssm_bwd", grid=(nblk, nch),
        in_specs=[pl.BlockSpec((s, LANES), blk), pl.BlockSpec((s, LANES), blk), pl.BlockSpec((s, LANES), blk),
                  pl.BlockSpec((s, LANES), col), pl.BlockSpec((s, LANES), col),
                  pw_spec, pw_spec, in_chunk, in_chunk, out_chunk, out_chunk, pl.BlockSpec((1, LANES), blk)],
        out_specs=[pl.BlockSpec((s, LANES), blk), pl.BlockSpec((1, LANES), blk), a_spec, a_spec,
                   in_chunk, in_chunk, out_chunk, out_chunk],
        out_shape=[jax.ShapeDtypeStruct((s, w), F32), jax.ShapeDtypeStruct((1, w), F32),
                   jax.ShapeDtypeStruct((nblk, 1, STATE_W), F32), jax.ShapeDtypeStruct((nblk, 1, STATE_W), F32),
                   jax.ShapeDtypeStruct((nblk, LANES, STATE_W), F32), jax.ShapeDtypeStruct((nblk, LANES, STATE_W), F32),
                   jax.ShapeDtypeStruct((nblk, STATE_W, LANES), F32), jax.ShapeDtypeStruct((nblk, STATE_W, LANES), F32)],
        compiler_params=_cp(("parallel", "arbitrary")),
    )(proj,dyg, pre, xr_all, xi_all, pw_re, pw_im, bd_re, bd_im, cd_re, cd_im, dskip)


ROW_TILE = 256


def _norm_fwd(x, g, name):
    s, d = x.shape
    t = min(ROW_TILE, s)
    return _tiled(lambda xv, gv: (_rms(xv, gv),), [_rows(x, t), _full(g)], [_orows(s, d, BF16, t)], s // t, name)[0]


def _norm_bwd(x, g, dh_list, dx_in, name):
    s, d = x.shape
    t = min(ROW_TILE, s)
    nh = len(dh_list)

    def fn(xv, gv, dxv, *dhs):
        dh = dhs[0].astype(F32)
        for other in dhs[1:]:
            dh = dh + other.astype(F32)
        _, vjp = jax.vjp(_rms, xv, gv)
        dx, dg = vjp(dh)
        dx = dx + dxv
        return dx, dx, dg

    return _tiled(fn, [_rows(x, t), _full(g), _rows(dx_in, t)] + [_rows(a, t) for a in dh_list],
                  [_orows(s, d, F32, t), _orows(s, d, BF16, t), _oacc((1, d))], s // t, name)


def _ple_fwd(x, y, pn, name):
    s, d = x.shape
    t = min(ROW_TILE, s)

    def fn(xv, yv, gv):
        x1 = xv + yv
        return x1, _rms(x1, gv)

    return _tiled(fn, [_rows(x, t), _rows(y, t), _full(pn)], [_orows(s, d, F32, t), _orows(s, d, BF16, t)], s // t, name)


def _ple_mix(x1, emb, gl, name):
    s, d = x1.shape
    t = min(ROW_TILE, s)
    return _tiled(lambda a, e, g: (a + e * jax.nn.sigmoid(g),), [_rows(x1, t), _rows(emb, t), _rows(gl, t)],
                  [_orows(s, d, F32, t)], s // t, name)[0]


def _ple_mix_bwd(dx2, emb, gl, name):
    s, d = dx2.shape
    t = min(ROW_TILE, s)

    def fn(dx, e, g):
        sg = jax.nn.sigmoid(g)
        return dx * sg, dx * e * sg * (1.0 - sg)

    return _tiled(fn, [_rows(dx2, t), _rows(emb, t), _rows(gl, t)],
                  [_orows(s, d, BF16, t), _orows(s, d, BF16, t)], s // t, name)


def _loss_grad(x, target, g):
    s, d = x.shape
    t = min(ROW_TILE, s)

    def fn(xv, tv, gv):
        def f(xx, gg):
            err = _rms(xx, gg) - tv
            return 0.5 * jnp.sum(jnp.mean(err * err, axis=-1, keepdims=True), axis=0, keepdims=True)

        loss, vjp = jax.vjp(f, xv, gv)
        dx, dg = vjp(jnp.ones((1, 1), F32))
        return loss, dx, dg

    return _tiled(fn, [_rows(x, t), _rows(target, t), _full(g)],
                  [_oacc((1, 1)), _orows(s, d, F32, t), _oacc((1, d))], s // t, "loss_grad")


def _conv_fwd(proj, kern):
    s = proj.shape[0]
    w = kern.shape[1]
    nb = w // LANES

    def fn(bg, cg, u, gate, k):
        z = cg.astype(F32) * u.astype(F32)
        conv = k[2:3] * z + k[1:2] * _shift_rows(z, 1) + k[0:1] * _shift_rows(z, 2)
        return (bg.astype(F32) * conv * _silu(gate.astype(F32)),)

    return _tiled(fn, [_cols(proj, LANES, q * nb) for q in range(4)] + [_cols(kern, LANES)],
                  [_ocols(s, w, BF16, LANES)], nb, "conv_fwd")[0]


def _conv_bwd(proj, kern, da):
    s = proj.shape[0]
    w = kern.shape[1]
    nb = w // LANES

    def fn(bg, cg, u, gate, k, dav):
        bg, cg, u, gate, dav = (a.astype(F32) for a in (bg, cg, u, gate, dav))
        z = cg * u
        z1, z2 = _shift_rows(z, 1), _shift_rows(z, 2)
        conv = k[2:3] * z + k[1:2] * z1 + k[0:1] * z2
        sg = jax.nn.sigmoid(gate)
        dy = dav * gate * sg
        dgate = dav * bg * conv * sg * (1.0 + gate * (1.0 - sg))
        dconv = dy * bg
        dk = jnp.concatenate([jnp.sum(dconv * zz, axis=0, keepdims=True) for zz in (z2, z1, z)], axis=0)
        dz = k[2:3] * dconv + k[1:2] * _shift_rows(dconv, 1, True) + k[0:1] * _shift_rows(dconv, 2, True)
        return dy * conv, dz * u, dz * cg, dgate, dk

    return _tiled(fn, [_cols(proj, LANES, q * nb) for q in range(4)] + [_cols(kern, LANES), _cols(da, LANES)],
                  [_ocols(s, w, BF16, LANES)] * 4 + [_ocols(3, w, F32, LANES)], nb, "conv_bwd")


def _glu_fwd(gl, proj, bglu):
    s, w2 = gl.shape
    w = w2 // 2
    t = min(ROW_TILE, s)

    def fn(glv, gate, b):
        v = glv + b
        return (v[:, :w] * jax.nn.sigmoid(v[:, w:]) * _silu(gate),)

    return _tiled(fn, [_rows(gl, t), (proj, (t, w), lambda i: (i, 1)), _full(bglu)],
                  [_orows(s, w, BF16, t)], s // t, "glu_fwd")[0]


def _glu_bwd(gl, proj, bglu, da):
    s, w2 = gl.shape
    w = w2 // 2
    t = min(ROW_TILE, s)

    def fn(glv, gate, b, dav):
        def f(gg, gt, bb):
            v = gg + bb
            return v[:, :w] * jax.nn.sigmoid(v[:, w:]) * _silu(gt)

        _, vjp = jax.vjp(f, glv, gate, b)
        return vjp(dav.astype(F32))

    return _tiled(fn, [_rows(gl, t), (proj, (t, w), lambda i: (i, 1)), _full(bglu), _rows(da, t)],
                  [_orows(s, w2, BF16, t), _orows(s, w, BF16, t), _oacc((1, w2))], s // t, "glu_bwd")


def _fg_fwd(z, b):
    def fn(zv, bv):
        v = zv + bv
        logf = jnp.minimum(v, 0.0) - jnp.log(1.0 + jnp.exp(-jnp.abs(v)))
        return (_cumsum_rows(logf, False),)

    return _tiled(fn, [_full(z), _full(b)], [_out(z.shape, F32, z.shape, lambda i: (0, 0))], 1, "fg_fwd")[0]


def _fg_bwd(z, b, dcs):
    def fn(zv, bv, dc):
        dz = _cumsum_rows(dc, True) * jax.nn.sigmoid(-(zv + bv))
        return dz, jnp.sum(dz, axis=0, keepdims=True)

    return _tiled(fn, [_full(z), _full(b), _full(dcs)],
                  [_out(z.shape, BF16, z.shape, lambda i: (0, 0)), _out((1, z.shape[1]), F32, (1, z.shape[1]), lambda i: (0, 0))],
                  1, "fg_bwd")


def _fox_gate_bwd(da, o, proj):
    s, w = o.shape
    t = min(ROW_TILE, s)

    def fn(dav, ov, gate):
        dav, ov, gate = dav.astype(F32), ov.astype(F32), gate.astype(F32)
        sg = jax.nn.sigmoid(gate)
        return dav * gate * sg, dav * ov * sg * (1.0 + gate * (1.0 - sg))

    return _tiled(fn, [_rows(da, t), _rows(o, t), (proj, (t, w), lambda i: (i, 3))],
                  [_orows(s, w, BF16, t), _orows(s, w, BF16, t)], s // t, "fox_gate_bwd")


def _fox_gate_fwd(o, proj):
    s, w = o.shape
    t = min(ROW_TILE, s)
    return _tiled(lambda ov, gate: (ov.astype(F32) * _silu(gate.astype(F32)),),
                  [_rows(o, t), (proj, (t, w), lambda i: (i, 3))], [_orows(s, w, BF16, t)], s // t, "fox_gate_fwd")[0]


def _local_step(x, p, target, w):
    s, d = x.shape
    depth = p.shape[0]
    pb = p.astype(BF16)
    grads = {}
    saved = []

    bucket = _swa_bucket_table()
    bias = w['rel_bias'][bucket].transpose(2, 0, 1)
    sinks = jnp.broadcast_to(w['swa_sinks'].reshape(SWA_KV_HEADS, SWA_GROUP, 1), (SWA_KV_HEADS, SWA_GROUP, LANES))
    ssm_params = tuple(w[k][0] for k in ('ssm_lam_re', 'ssm_lam_im', 'ssm_log_dt', 'ssm_b_re', 'ssm_b_im', 'ssm_c_re', 'ssm_c_im'))
    (a_re, a_im, bd_re, bd_im, cd_re, cd_im), ssm_vjp = jax.vjp(_ssm_prep, *ssm_params)
    levels = max(1, (s - 1).bit_length())
    pw_re, pw_im = _powers(a_re, a_im, levels)
    wfg = jnp.pad(w['fox_w_fg'][0], ((0, 0), (0, LANES - FOX_HEADS)))
    bfg = jnp.pad(w['fox_b_fg'], ((0, 0), (0, LANES - FOX_HEADS)))

    def qkv4(a):
        return a.reshape(s, SWA_KV_HEADS, HEAD_DIM).transpose(1, 0, 2)

    for i in range(depth):
        mixer = i % 4
        hn = _norm_fwd(x, w['norm_g'][i:i + 1], f"norm_fwd{i}")
        sv = {'x': x, 'hn': hn}
        if mixer == 0:
            proj = _mm(hn, w['swa_w_in'][0], name="swa_in")
            qw = SWA_HEADS * HEAD_DIM
            kvw = SWA_KV_HEADS * HEAD_DIM
            k4, v4 = qkv4(proj[:, qw:qw + kvw]), qkv4(proj[:, qw + kvw:qw + 2 * kvw])
            a = _swa_fwd(proj, k4, v4, bias, sinks)
            sv.update(proj=proj, k4=k4, v4=v4)
            w_out = w['swa_w_out'][0]
        elif mixer == 1:
            proj = _mm(hn, w['conv_w_in'][0], name="conv_in")
            a = _conv_fwd(proj, w['conv_kernel'][0])
            sv.update(proj=proj)
            w_out = w['conv_w_out'][0]
        elif mixer == 2:
            proj = _mm(hn, w['ssm_w_in'][0], out_dtype=F32, name="ssm_in")
            pre, yg, xr_all, xi_all = _ssm_fwd(proj, pw_re, pw_im, bd_re, bd_im, cd_re, cd_im, w['ssm_d'])
            gl = _mm(yg, w['ssm_w_glu'][0], out_dtype=F32, name="ssm_glu")
            a = _glu_fwd(gl, proj, w['ssm_b_glu'])
            sv.update(proj=proj, pre=pre, yg=yg, xr=xr_all, xi=xi_all, gl=gl)
            w_out = w['ssm_w_out'][0]
        else:
            proj = _mm(hn, w['fox_w_in'][0], name="fox_in")
            z = _mm(hn, wfg, out_dtype=F32, name="fox_fg")
            csum = _fg_fwd(z, bfg)
            cq = csum[:, :FOX_HEADS].T.reshape(FOX_HEADS, s, 1)
            ck = cq.reshape(FOX_HEADS, 1, s)
            o, lse = _fox_fwd(proj, cq, ck)
            a = _fox_gate_fwd(o, proj)
            sv.update(proj=proj, z=z, cq=cq, ck=ck, o=o, lse=lse)
            w_out = w['fox_w_out'][0]
        y = _mm(a, w_out, out_dtype=F32, name=f"mixer_out{i}")
        x1, gn = _ple_fwd(x, y, w['ple_norm'][i:i + 1], f"ple_fwd{i}")
        emb = _mm(pb[i], w['ple_proj'][i], out_dtype=F32, name=f"ple_emb{i}")
        gl2 = _mm(gn, w['ple_gate'][i], out_dtype=F32, name=f"ple_gate{i}")
        x = _ple_mix(x1, emb, gl2, f"ple_mix{i}")
        sv.update(a=a, x1=x1, gn=gn, emb=emb, gl2=gl2)
        saved.append(sv)

    loss, dx, grads['final_g'] = _loss_grad(x, target, w['final_g'].reshape(1, d))
    grads['final_g'] = grads['final_g'].reshape(d)

    g_norm, g_ple_norm, g_ple_proj, g_ple_gate = [None] * depth, [None] * depth, [None] * depth, [None] * depth
    for i in reversed(range(depth)):
        sv = saved[i]
        mixer = i % 4
        demb, dgl2 = _ple_mix_bwd(dx, sv['emb'], sv['gl2'], f"ple_mix_bwd{i}")
        g_ple_proj[i] = _mm(pb[i], demb, ta=True, name=f"ple_emb_dw{i}")
        g_ple_gate[i] = _mm(sv['gn'], dgl2, ta=True, name=f"ple_gate_dw{i}")
        dgn = _mm(dgl2, w['ple_gate'][i], tb=True, out_dtype=F32, name=f"ple_gate_dx{i}")
        dx1, dy, g_ple_norm[i] = _norm_bwd(sv['x1'], w['ple_norm'][i:i + 1], [dgn], dx, f"ple_norm_bwd{i}")
        w_out_name = ('swa_w_out', 'conv_w_out', 'ssm_w_out', 'fox_w_out')[mixer]
        grads[w_out_name] = _mm(sv['a'], dy, ta=True, name=f"mixer_out_dw{i}")[None]
        da = _mm(dy, w[w_out_name][0], tb=True, name=f"mixer_out_dx{i}")
        proj = sv['proj']
        dhs = []
        if mixer == 0:
            dq, dgate, dk4, dv4, dbias, dsink = _swa_bwd(proj, sv['k4'], sv['v4'], bias, sinks, da)
            back = lambda t4: t4.transpose(1, 0, 2).reshape(s, SWA_KV_HEADS * HEAD_DIM).astype(BF16)
            dproj = jnp.concatenate([dq, back(dk4), back(dv4), dgate], axis=1)
            onehot = jnp.asarray(np.eye(REL_BUCKETS, dtype=np.float32)[bucket.reshape(-1)])
            grads['rel_bias'] = _bias_grad(dbias.reshape(SWA_HEADS, -1), onehot)
            grads['swa_sinks'] = dsink[:, :, 0].reshape(1, SWA_HEADS)
            w_in_name = 'swa_w_in'
        elif mixer == 1:
            dbg, dcg, du, dgate, dkern = _conv_bwd(proj, w['conv_kernel'][0], da)
            dproj = jnp.concatenate([dbg, dcg, du, dgate], axis=1)
            grads['conv_kernel'] = dkern[None]
            w_in_name = 'conv_w_in'
        elif mixer == 2:
            dgl, dgate, dbglu = _glu_bwd(sv['gl'], proj, w['ssm_b_glu'], da)
            grads['ssm_b_glu'] = dbglu
            grads['ssm_w_glu'] = _mm(sv['yg'], dgl, ta=True, name="ssm_glu_dw")[None]
            dyg = _mm(dgl, w['ssm_w_glu'][0], tb=True, out_dtype=F32, name="ssm_glu_dx")
            du, dd, da_re, da_im, dbd_re, dbd_im, dcd_re, dcd_im = _ssm_bwd(
                proj, dyg, sv['pre'], sv['xr'], sv['xi'], pw_re, pw_im, bd_re, bd_im, cd_re, cd_im, w['ssm_d'])
            grads['ssm_d'] = dd
            dparams = ssm_vjp((da_re, da_im, dbd_re, dbd_im, dcd_re, dcd_im))
            for k, v in zip(('ssm_lam_re', 'ssm_lam_im', 'ssm_log_dt', 'ssm_b_re', 'ssm_b_im', 'ssm_c_re', 'ssm_c_im'), dparams):
                grads[k] = v[None]
            dproj = jnp.concatenate([du.astype(BF16), dgate], axis=1)
            w_in_name = 'ssm_w_in'
        else:
            do, dgate = _fox_gate_bwd(da, sv['o'], proj)
            dq, delta, dcq = _fox_bwd_dq(proj, sv['cq'], sv['ck'], sv['o'], do, sv['lse'])
            dk, dv, dck = _fox_bwd_dkv(proj, sv['cq'], sv['ck'], do, sv['lse'], delta)
            dcs = jnp.pad((dck.reshape(FOX_HEADS, s) + dcq.reshape(FOX_HEADS, s)).T, ((0, 0), (0, LANES - FOX_HEADS)))
            dz, dbfg = _fg_bwd(sv['z'], bfg, dcs)
            grads['fox_b_fg'] = dbfg[:, :FOX_HEADS]
            grads['fox_w_fg'] = _mm(sv['hn'], dz, ta=True, out_dtype=BF16, name="fox_fg_dw")[:, :FOX_HEADS][None]
            dhs.append(_mm(dz, wfg, tb=True, out_dtype=F32, name="fox_fg_dx"))
            dproj = jnp.concatenate([dq, dk, dv, dgate], axis=1)
            w_in_name = 'fox_w_in'
        grads[w_in_name] = _mm(sv['hn'], dproj, ta=True, name=f"mixer_in_dw{i}")[None]
        dhs.append(_mm(dproj, w[w_in_name][0], tb=True, out_dtype=F32, name=f"mixer_in_dx{i}"))
        dx, _, g_norm[i] = _norm_bwd(sv['x'], w['norm_g'][i:i + 1], dhs, dx1, f"norm_bwd{i}")

    grads['norm_g'] = jnp.concatenate(g_norm, axis=0)
    grads['ple_norm'] = jnp.concatenate(g_ple_norm, axis=0)
    grads['ple_proj'] = jnp.stack(g_ple_proj)
    grads['ple_gate'] = jnp.stack(g_ple_gate)
    return loss, dx, grads


def _bias_grad(dbias, onehot):
    hi = dbias.astype(BF16)
    r1 = dbias - hi.astype(F32)
    mid = r1.astype(BF16)
    lo = (r1 - mid.astype(F32)).astype(BF16)
    stacked = jnp.concatenate([hi, mid, lo], axis=0)
    out = _mm(stacked, onehot, out_dtype=F32, name="rel_bias_grad", tk=2048)
    nh = dbias.shape[0]
    return (out[:nh] + out[nh:2 * nh] + out[2 * nh:]).T


WEIGHTS = ['norm_g', 'final_g', 'rel_bias', 'swa_w_in', 'swa_w_out', 'swa_sinks', 'conv_w_in', 'conv_kernel', 'conv_w_out',
           'ssm_w_in', 'ssm_lam_re', 'ssm_lam_im', 'ssm_log_dt', 'ssm_b_re', 'ssm_b_im', 'ssm_c_re', 'ssm_c_im', 'ssm_d',
           'ssm_w_glu', 'ssm_b_glu', 'ssm_w_out', 'fox_w_in', 'fox_w_fg', 'fox_b_fg', 'fox_w_out', 'ple_proj', 'ple_norm',
           'ple_gate']
BIG = {'swa_w_in': 2, 'swa_w_out': 1, 'conv_w_in': 2, 'conv_w_out': 1, 'ssm_w_in': 2, 'ssm_w_glu': 2, 'ssm_w_out': 1,
       'fox_w_in': 2, 'fox_w_fg': 1, 'fox_w_out': 1, 'ple_proj': 2, 'ple_gate': 1}
SMALL = {'conv_kernel': 2, 'ssm_d': 1, 'ssm_b_glu': 1}
REPLICATED = [n for n in WEIGHTS if n not in BIG and n not in SMALL]
N_CHIPS = 4
N_DEV = 8
BIG_ROWS = 256
SMALL_ROWS = 8
REPL_ROWS = 64


def _flat(pieces, dtype, lead, row_mult):
    flat = jnp.concatenate([q.astype(dtype) for q in pieces], axis=-1)
    pad = (-flat.shape[-1]) % (row_mult * FLAT_W)
    flat = jnp.pad(flat, [(0, 0)] * len(lead) + [(0, pad)])
    return flat.reshape(*lead, -1, FLAT_W)


def _unflat(flat, lead_ndim, sizes):
    lead = flat.shape[:lead_ndim]
    flat = flat.reshape(*lead, -1)
    out, off = [], 0
    for n in sizes:
        out.append(flat[..., off:off + n])
        off += n
    return out


def _split_shards(full, axis):
    shp = full.shape
    parts = full.reshape(shp[:axis] + (N_CHIPS, shp[axis] // N_CHIPS) + shp[axis + 1:])
    return jnp.moveaxis(parts, axis, 0)


def _join_shards(parts, axis):
    moved = jnp.moveaxis(parts, 0, axis)
    shp = moved.shape
    return moved.reshape(shp[:axis] + (shp[axis] * shp[axis + 1],) + shp[axis + 2:])


def _coords():
    return lax.axis_index("x"), lax.axis_index("y"), lax.axis_index("c")


def _remote(k, src, dst, to, send_sems, recv_sems):
    return pltpu.make_async_remote_copy(src_ref=src, dst_ref=dst, send_sem=send_sems.at[k], recv_sem=recv_sems.at[k],
                                        device_id=to, device_id_type=MESH)


def _gather_weights(wsh, ssh):
    def body(w_ref, s_ref, wout, sout, send_sems, recv_sems, local_sems):
        x, y, c = _coords()
        me = 2 * x + y
        chips = [(1 - x, y), (x, 1 - y), (1 - x, 1 - y)]
        rc = functools.partial(_remote, send_sems=send_sems, recv_sems=recv_sems)
        local = [pltpu.make_async_copy(w_ref, wout.at[me], local_sems.at[0]),
                 pltpu.make_async_copy(s_ref, sout.at[me], local_sems.at[1])]
        for cp in local:
            cp.start()
        sends = []
        for j, (cx, cy) in enumerate(chips):
            sends.append(rc(j, w_ref.at[c], wout.at[me, c], (cx, cy, c)))
            sends.append(rc(3 + j, s_ref, sout.at[me], (cx, cy, c)))
        for cp in sends:
            cp.start()
        for j, (cx, cy) in enumerate(chips):
            k = 2 * cx + cy
            rc(j, w_ref.at[c], wout.at[k, c], (x, y, c)).wait_recv()
            fwd = rc(6 + j, wout.at[k, c], wout.at[k, c], (x, y, 1 - c))
            fwd.start()
            sends.append(fwd)
        for j, (cx, cy) in enumerate(chips):
            k = 2 * cx + cy
            rc(6 + j, w_ref.at[c], wout.at[k, 1 - c], (x, y, c)).wait_recv()
            rc(3 + j, s_ref, sout.at[k], (x, y, c)).wait_recv()
        for cp in sends:
            cp.wait_send()
        for cp in local:
            cp.wait()

    return pl.pallas_call(
        body, name="gather_weights", in_specs=[ANY, ANY], out_specs=[ANY, ANY],
        out_shape=[jax.ShapeDtypeStruct((N_CHIPS,) + wsh.shape, wsh.dtype),
                   jax.ShapeDtypeStruct((N_CHIPS,) + ssh.shape, ssh.dtype)],
        scratch_shapes=[pltpu.SemaphoreType.DMA((9,)), pltpu.SemaphoreType.DMA((9,)), pltpu.SemaphoreType.DMA((2,))],
    )(wsh, ssh)


def _exchange_grads(gbig, gsmall, grepl):
    def body(g_ref, s_ref, r_ref, og, osm, orp, send_sems, recv_sems, local_sems):
        x, y, c = _coords()
        me = 4 * x + 2 * y + c
        rc = functools.partial(_remote, send_sems=send_sems, recv_sems=recv_sems)
        local = [pltpu.make_async_copy(g_ref.at[me], og.at[me], local_sems.at[0]),
                 pltpu.make_async_copy(s_ref.at[me], osm.at[me], local_sems.at[1]),
                 pltpu.make_async_copy(r_ref, orp.at[me], local_sems.at[2])]
        for cp in local:
            cp.start()
        peers = []
        for d in range(1, N_DEV):
            px = 1 - x if d & 4 else x
            py = 1 - y if d & 2 else y
            pc = 1 - c if d & 1 else c
            peers.append((4 * px + 2 * py + pc, (px, py, pc)))
        sends = []
        for i, (peer, to) in enumerate(peers):
            sends.append(rc(3 * i, g_ref.at[peer], og.at[me], to))
            sends.append(rc(3 * i + 1, s_ref.at[peer], osm.at[me], to))
            sends.append(rc(3 * i + 2, r_ref, orp.at[me], to))
        for cp in sends:
            cp.start()
        for i, (peer, to) in enumerate(peers):
            rc(3 * i, g_ref.at[peer], og.at[peer], to).wait_recv()
            rc(3 * i + 1, s_ref.at[peer], osm.at[peer], to).wait_recv()
            rc(3 * i + 2, r_ref, orp.at[peer], to).wait_recv()
        for cp in sends:
            cp.wait_send()
        for cp in local:
            cp.wait()

    nsem = 3 * (N_DEV - 1)
    return pl.pallas_call(
        body, name="exchange_grads", in_specs=[ANY, ANY, ANY], out_specs=[ANY, ANY, ANY],
        out_shape=[jax.ShapeDtypeStruct(gbig.shape, gbig.dtype), jax.ShapeDtypeStruct(gsmall.shape, gsmall.dtype),
                   jax.ShapeDtypeStruct((N_DEV,) + grepl.shape, grepl.dtype)],
        scratch_shapes=[pltpu.SemaphoreType.DMA((nsem,)), pltpu.SemaphoreType.DMA((nsem,)), pltpu.SemaphoreType.DMA((3,))],
    )(gbig, gsmall, grepl)


def _sibling_exchange(gh, sh):
    def body(g_ref, s_ref, og, osm, send_sems, recv_sems, local_sems):
        x, y, c = _coords()
        rc = functools.partial(_remote, send_sems=send_sems, recv_sems=recv_sems)
        local = [pltpu.make_async_copy(g_ref, og.at[c], local_sems.at[0]),
                 pltpu.make_async_copy(s_ref, osm.at[c], local_sems.at[1])]
        sends = [rc(0, g_ref, og.at[c], (x, y, 1 - c)), rc(1, s_ref, osm.at[c], (x, y, 1 - c))]
        for cp in local + sends:
            cp.start()
        rc(0, g_ref, og.at[1 - c], (x, y, c)).wait_recv()
        rc(1, s_ref, osm.at[1 - c], (x, y, c)).wait_recv()
        for cp in sends:
            cp.wait_send()
        for cp in local:
            cp.wait()

    return pl.pallas_call(
        body, name="sibling_exchange", in_specs=[ANY, ANY], out_specs=[ANY, ANY],
        out_shape=[jax.ShapeDtypeStruct((2,) + gh.shape, gh.dtype), jax.ShapeDtypeStruct((2,) + sh.shape, sh.dtype)],
        scratch_shapes=[pltpu.SemaphoreType.DMA((2,)), pltpu.SemaphoreType.DMA((2,)), pltpu.SemaphoreType.DMA((2,))],
    )(gh, sh)


def _sum_senders(recv, name):
    n, r, w = recv.shape
    t = _tile(r, 256, 8)

    def fn(v):
        acc = v[0].astype(F32)
        for i in range(1, n):
            acc = acc + v[i].astype(F32)
        return (acc,)

    return _tiled(fn, [(recv, (n, t, w), lambda i: (0, i, 0))], [_orows(r, w, F32, t)], r // t, name)[0]


def _adamw(w, g, m, v, name):
    r, wd = w.shape
    t = _tile(r, 256, 8)

    def fn(wv, gv, mv, vv):
        m2 = ADAM_B1 * mv + (1.0 - ADAM_B1) * gv
        v2 = ADAM_B2 * vv + (1.0 - ADAM_B2) * (gv * gv)
        m_hat = m2 / (1.0 - ADAM_B1 ** ADAM_STEP)
        v_hat = v2 / (1.0 - ADAM_B2 ** ADAM_STEP)
        delta = -ADAM_LR * (m_hat / (jnp.sqrt(v_hat) + ADAM_EPS) + ADAM_WD * wv)
        return delta, m2, v2

    return _tiled(fn, [_rows(a, t) for a in (w, g, m, v)], [_orows(r, wd, F32, t)] * 3, r // t, name)


def kernel(x, p, norm_g, final_g, rel_bias, swa_w_in, swa_w_out, swa_sinks, conv_w_in, conv_kernel, conv_w_out, ssm_w_in, ssm_lam_re, ssm_lam_im, ssm_log_dt, ssm_b_re, ssm_b_im, ssm_c_re, ssm_c_im, ssm_d, ssm_w_glu, ssm_b_glu, ssm_w_out, fox_w_in, fox_w_fg, fox_b_fg, fox_w_out, ple_proj, ple_norm, ple_gate, loss_target, m_norm_g, m_final_g, m_rel_bias, m_swa_w_in, m_swa_w_out, m_swa_sinks, m_conv_w_in, m_conv_kernel, m_conv_w_out, m_ssm_w_in, m_ssm_lam_re, m_ssm_lam_im, m_ssm_log_dt, m_ssm_b_re, m_ssm_b_im, m_ssm_c_re, m_ssm_c_im, m_ssm_d, m_ssm_w_glu, m_ssm_b_glu, m_ssm_w_out, m_fox_w_in, m_fox_w_fg, m_fox_b_fg, m_fox_w_out, m_ple_proj, m_ple_norm, m_ple_gate, v_norm_g, v_final_g, v_rel_bias, v_swa_w_in, v_swa_w_out, v_swa_sinks, v_conv_w_in, v_conv_kernel, v_conv_w_out, v_ssm_w_in, v_ssm_lam_re, v_ssm_lam_im, v_ssm_log_dt, v_ssm_b_re, v_ssm_b_im, v_ssm_c_re, v_ssm_c_im, v_ssm_d, v_ssm_w_glu, v_ssm_b_glu, v_ssm_w_out, v_fox_w_in, v_fox_w_fg, v_fox_b_fg, v_fox_w_out, v_ple_proj, v_ple_norm, v_ple_gate):
    given = dict(locals())
    shard_shape = {n: given[n].shape for n in WEIGHTS}
    half = {n: math.prod(shard_shape[n]) // 2 for n in WEIGHTS}

    wsh = _flat([given[n].reshape(-1) for n in BIG], BF16, (), 2 * BIG_ROWS)
    ssh = _flat([given[n].reshape(-1) for n in SMALL], F32, (), SMALL_ROWS)
    wall, sall = _gather_weights(wsh.reshape(2, -1, FLAT_W), ssh)
    full = {n: given[n] for n in REPLICATED}
    for group, gathered in ((BIG, wall.reshape(N_CHIPS, -1, FLAT_W)), (SMALL, sall)):
        pieces = _unflat(gathered, 1, [2 * half[n] for n in group])
        for n, piece in zip(group, pieces):
            full[n] = _join_shards(piece.reshape((N_CHIPS,) + shard_shape[n]), group[n])

    loss, dx, grads = _local_step(x[0], p[:, 0], loss_target[0], full)

    gbig = _flat([_split_shards(grads[n], BIG[n]).reshape(N_DEV, -1) for n in BIG], BF16, (N_DEV,), BIG_ROWS)
    gsmall = _flat([_split_shards(grads[n], SMALL[n]).reshape(N_DEV, -1) for n in SMALL], F32, (N_DEV,), SMALL_ROWS)
    grepl = _flat([grads[n].reshape(-1) for n in REPLICATED], F32, (), REPL_ROWS)
    recv_big, recv_small, recv_repl = _exchange_grads(gbig, gsmall, grepl)
    gfull, sfull = _sibling_exchange(_sum_senders(recv_big, "sum_big"), _sum_senders(recv_small, "sum_small"))
    g_repl = _sum_senders(recv_repl, "sum_repl")

    out_g, out_d, out_m, out_v = {}, {}, {}, {}
    for n, piece in zip(BIG, _unflat(gfull, 1, [half[n] for n in BIG])):
        as_rows = lambda t: t.reshape(-1, FLAT_W)
        d, m2, v2 = _adamw(as_rows(given[n]), as_rows(piece), as_rows(given['m_' + n]), as_rows(given['v_' + n]), f"adamw_{n}")
        out_g[n], out_d[n], out_m[n], out_v[n] = (t.reshape(shard_shape[n]) for t in (piece, d, m2, v2))

    pack_small = lambda pre: _flat([given[pre + n].reshape(2, -1) for n in SMALL], F32, (2,), SMALL_ROWS).reshape(-1, FLAT_W)
    res = _adamw(pack_small(''), sfull.reshape(-1, FLAT_W), pack_small('m_'), pack_small('v_'), "adamw_small")
    for dst, flat in zip((out_g, out_d, out_m, out_v), (sfull,) + tuple(res)):
        for n, piece in zip(SMALL, _unflat(flat.reshape(2, -1, FLAT_W), 1, [half[n] for n in SMALL])):
            dst[n] = piece.reshape(shard_shape[n])

    pack_repl = lambda pre: _flat([given[pre + n].reshape(-1) for n in REPLICATED], F32, (), REPL_ROWS)
    res = _adamw(pack_repl(''), g_repl, pack_repl('m_'), pack_repl('v_'), "adamw_repl")
    for dst, flat in zip((out_g, out_d, out_m, out_v), (g_repl,) + tuple(res)):
        for n, piece in zip(REPLICATED, _unflat(flat, 0, [2 * half[n] for n in REPLICATED])):
            dst[n] = piece.reshape(shard_shape[n])

    total = lax.psum(loss[0, 0], ("x", "y", "c"))
    return (total, dx[None], *[out_g[n] for n in WEIGHTS], *[out_d[n] for n in WEIGHTS],
            *[out_m[n] for n in WEIGHTS], *[out_v[n] for n in WEIGHTS])
```

```python
import functools
import math

import numpy as np
import jax
import jax.numpy as jnp
from jax import lax
from jax.experimental import pallas as pl
from jax.experimental.pallas import tpu as pltpu

F32 = jnp.float32
BF16 = jnp.bfloat16

EPS = 1e-6
BLOCK = 128
REL_BUCKETS = 32
REL_MAX_DIST = 128
SWA_HEADS, SWA_KV_HEADS, HEAD_DIM = 32, 4, 64
SWA_GROUP = SWA_HEADS // SWA_KV_HEADS
FOX_HEADS = 32
SSM_GROUP, SSM_STATE = 16, 64
GROUPS_PER_STEP = 8
STATE_W = GROUPS_PER_STEP * SSM_STATE
LANES = 128
NEG = -1e30

ADAM_LR, ADAM_B1, ADAM_B2, ADAM_EPS, ADAM_WD, ADAM_STEP = 0.001, 0.9, 0.999, 1e-08, 0.01, 10

VMEM_LIMIT_V7X = 56 * 1024 * 1024
FLAT_W = 1024
MESH = pl.DeviceIdType.MESH
ANY = pl.BlockSpec(memory_space=pl.ANY)


def _cp(sem):
    return pltpu.CompilerParams(dimension_semantics=sem, vmem_limit_bytes=VMEM_LIMIT_V7X)


def _tile(n, target, mult=LANES):
    if n <= target:
        return n
    t = (target // mult) * mult
    while t >= mult:
        if n % t == 0:
            return t
        t -= mult
    return n


def _mm(a, b, *, ta=False, tb=False, out_dtype=BF16, name, tm=1024, tn=1024, tk=512):
    if ta:
        kdim, m = a.shape
    else:
        m, kdim = a.shape
    n = b.shape[0] if tb else b.shape[1]
    assert (b.shape[1] if tb else b.shape[0]) == kdim
    tm, tn, tk = _tile(m, tm), _tile(n, tn), _tile(kdim, tk)
    nk = kdim // tk
    dn = (((0 if ta else 1,), (1 if tb else 0,)), ((), ()))

    def body(a_ref, b_ref, o_ref, acc_ref):
        k = pl.program_id(2)

        @pl.when(k == 0)
        def _():
            acc_ref[...] = jnp.zeros_like(acc_ref)

        acc_ref[...] += lax.dot_general(a_ref[...].astype(BF16), b_ref[...].astype(BF16), dn,
                                        preferred_element_type=F32)

        @pl.when(k == nk - 1)
        def _():
            o_ref[...] = acc_ref[...].astype(o_ref.dtype)

    a_spec = pl.BlockSpec((tk, tm), lambda i, j, k: (k, i)) if ta else pl.BlockSpec((tm, tk), lambda i, j, k: (i, k))
    b_spec = pl.BlockSpec((tn, tk), lambda i, j, k: (j, k)) if tb else pl.BlockSpec((tk, tn), lambda i, j, k: (k, j))
    return pl.pallas_call(
        body, name=name, grid=(m // tm, n // tn, nk),
        in_specs=[a_spec, b_spec], out_specs=pl.BlockSpec((tm, tn), lambda i, j, k: (i, j)),
        out_shape=jax.ShapeDtypeStruct((m, n), out_dtype),
        scratch_shapes=[pltpu.VMEM((tm, tn), F32)],
        compiler_params=_cp(("parallel", "parallel", "arbitrary")),
    )(a, b)


def _rows(arr, t):
    return (arr, (t, arr.shape[1]), lambda i: (i, 0))


def _cols(arr, cb, off=0):
    return (arr, (arr.shape[0], cb), lambda i: (0, i + off))


def _full(arr):
    nd = arr.ndim
    return (arr, arr.shape, lambda i: (0,) * nd)


def _lead(arr):
    return (arr, (1,) + arr.shape[1:], lambda i: (i, 0, 0))


def _out(shape, dtype, block, imap, acc=False):
    return (jax.ShapeDtypeStruct(shape, dtype), block, imap, acc)


def _orows(s, w, dtype, t):
    return _out((s, w), dtype, (t, w), lambda i: (i, 0))


def _ocols(s, w, dtype, cb):
    return _out((s, w), dtype, (s, cb), lambda i: (0, i))


def _oacc(shape):
    nd = len(shape)
    return _out(shape, F32, shape, lambda i: (0,) * nd, True)


def _olead(n, a, b, dtype=F32):
    return _out((n, a, b), dtype, (1, a, b), lambda i: (i, 0, 0))


def _tiled(fn, ins, outs, n, name):
    has_acc = any(o[3] for o in outs)
    ni = len(ins)

    def body(*refs):
        vals = fn(*[r[...] for r in refs[:ni]])
        i = pl.program_id(0)
        for r, v, o in zip(refs[ni:], vals, outs):
            if o[3]:
                @pl.when(i == 0)
                def _(r=r, v=v):
                    r[...] = v.astype(r.dtype)

                @pl.when(i > 0)
                def _(r=r, v=v):
                    r[...] += v.astype(r.dtype)
            else:
                r[...] = v.astype(r.dtype)

    res = pl.pallas_call(
        body, name=name, grid=(n,),
        in_specs=[pl.BlockSpec(b, m) for _, b, m in ins],
        out_specs=[pl.BlockSpec(b, m) for _, b, m, _ in outs],
        out_shape=[s for s, _, _, _ in outs],
        compiler_params=_cp(("arbitrary",) if has_acc else ("parallel",)),
    )(*[a for a, _, _ in ins])
    return res


def _silu(x):
    return x * jax.nn.sigmoid(x)


def _gelu(x):
    return 0.5 * x * (1.0 + jnp.tanh(math.sqrt(2.0 / math.pi) * (x + 0.044715 * (x * x * x))))


def _rms(x, g):
    r = lax.rsqrt(jnp.mean(x * x, axis=-1, keepdims=True) + EPS)
    return x * r * g


def _shift_rows(x, sh, up=False):
    s = x.shape[0]
    rows = lax.broadcasted_iota(jnp.int32, x.shape, 0)
    if up:
        return jnp.where(rows < s - sh, pltpu.roll(x, s - sh, 0), 0.0)
    return jnp.where(rows >= sh, pltpu.roll(x, sh, 0), 0.0)


def _scan_complex(xr, xi, pr, pi, reverse):
    s = xr.shape[0]
    k = 0
    while (1 << k) < s:
        sh = 1 << k
        sr, si = _shift_rows(xr, sh, reverse), _shift_rows(xi, sh, reverse)
        ar, ai = pr[k:k + 1, :], pi[k:k + 1, :]
        xr, xi = xr + ar * sr - ai * si, xi + ar * si + ai * sr
        k += 1
    return xr, xi


def _cumsum_rows(x, reverse):
    s = x.shape[0]
    k = 0
    while (1 << k) < s:
        x = x + _shift_rows(x, 1 << k, reverse)
        k += 1
    return x


def _dot(a, b, ca=1, cb=0):
    return lax.dot_general(a.astype(BF16), b.astype(BF16), (((ca,), (cb,)), ((), ())), preferred_element_type=F32)


def _t5_bucket(dist):
    max_exact = REL_BUCKETS // 2
    d = np.maximum(dist, 1).astype(np.float32)
    large = max_exact + (np.log(d / max_exact) / np.log(REL_MAX_DIST / max_exact) * (REL_BUCKETS - max_exact)).astype(np.int32)
    large = np.minimum(large, REL_BUCKETS - 1)
    return np.where(dist < max_exact, dist, large).astype(np.int32)


def _swa_bucket_table():
    qi = np.arange(BLOCK)[:, None]
    kj = np.arange(2 * BLOCK)[None, :]
    return _t5_bucket(np.clip(qi + BLOCK - kj, 0, None))


def _swa_scores(qg, kb, bias_g, sk, n):
    s = _dot(qg, kb, 1, 1) * (HEAD_DIM ** -0.5) + bias_g
    row = lax.broadcasted_iota(jnp.int32, s.shape, 0)
    col = lax.broadcasted_iota(jnp.int32, s.shape, 1)
    dist = row + BLOCK - col
    mask = (dist >= 0) & (dist < BLOCK) & ((col >= BLOCK) | (n > 0))
    s = jnp.where(mask, s, NEG)
    m = jnp.maximum(jnp.max(s, axis=1, keepdims=True), sk)
    e = jnp.exp(s - m)
    es = jnp.exp(sk - m)
    den = jnp.sum(e, axis=1, keepdims=True) + es
    return e / den, es / den


def _swa_specs(proj, k4, v4, nb, clamp):
    gw = SWA_GROUP * HEAD_DIM
    gate_off = (SWA_HEADS * HEAD_DIM + 2 * SWA_KV_HEADS * HEAD_DIM) // gw
    cur = (lambda n: jnp.minimum(n, nb - 1)) if clamp else (lambda n: n)
    prev = lambda n: jnp.maximum(cur(n) - 1, 0)
    return [
        pl.BlockSpec((BLOCK, gw), lambda h, n: (cur(n), h)),
        pl.BlockSpec((1, BLOCK, HEAD_DIM), lambda h, n: (h, cur(n), 0)),
        pl.BlockSpec((1, BLOCK, HEAD_DIM), lambda h, n: (h, prev(n), 0)),
        pl.BlockSpec((1, BLOCK, HEAD_DIM), lambda h, n: (h, cur(n), 0)),
        pl.BlockSpec((1, BLOCK, HEAD_DIM), lambda h, n: (h, prev(n), 0)),
        pl.BlockSpec((BLOCK, gw), lambda h, n: (cur(n), gate_off + h)),
    ], [proj, k4, k4, v4, v4, proj]


def _swa_fwd(proj, k4, v4, bias, sinks):
    s = proj.shape[0]
    nb = s // BLOCK
    gw = SWA_GROUP * HEAD_DIM

    def body(q_ref, kc_ref, kp_ref, vc_ref, vp_ref, gate_ref, bias_ref, sink_ref, a_ref):
        n = pl.program_id(1)
        kb = jnp.concatenate([kp_ref[0], kc_ref[0]], axis=0)
        vb = jnp.concatenate([vp_ref[0], vc_ref[0]], axis=0)
        for g in range(SWA_GROUP):
            sl = slice(g * HEAD_DIM, (g + 1) * HEAD_DIM)
            p, _ = _swa_scores(q_ref[:, sl], kb, bias_ref[g], sink_ref[0, g:g + 1, :1], n)
            og = _dot(p, vb)
            a_ref[:, sl] = (og * _silu(gate_ref[:, sl].astype(F32))).astype(a_ref.dtype)

    specs, args = _swa_specs(proj, k4, v4, nb, False)
    return pl.pallas_call(
        body, name="swa_fwd", grid=(SWA_KV_HEADS, nb),
        in_specs=specs + [pl.BlockSpec((SWA_GROUP, BLOCK, 2 * BLOCK), lambda h, n: (h, 0, 0)),
                          pl.BlockSpec((1, SWA_GROUP, LANES), lambda h, n: (h, 0, 0))],
        out_specs=pl.BlockSpec((BLOCK, gw), lambda h, n: (n, h)),
        out_shape=jax.ShapeDtypeStruct((s, SWA_HEADS * HEAD_DIM), BF16),
        compiler_params=_cp(("parallel", "parallel")),
    )(*args, bias, sinks)


def _swa_bwd(proj, k4, v4, bias, sinks, da):
    s = proj.shape[0]
    nb = s // BLOCK
    gw = SWA_GROUP * HEAD_DIM

    def body(q_ref, kc_ref, kp_ref, vc_ref, vp_ref, gate_ref, bias_ref, sink_ref, da_ref,
             dq_ref, dgate_ref, dk_ref, dv_ref, dbias_ref, dsink_ref, dk_own, dv_own):
        n = pl.program_id(1)

        @pl.when(n == 0)
        def _():
            dk_own[...] = jnp.zeros_like(dk_own)
            dv_own[...] = jnp.zeros_like(dv_own)
            dbias_ref[...] = jnp.zeros_like(dbias_ref)
            dsink_ref[...] = jnp.zeros_like(dsink_ref)

        @pl.when(n < nb)
        def _():
            kb = jnp.concatenate([kp_ref[0], kc_ref[0]], axis=0)
            vb = jnp.concatenate([vp_ref[0], vc_ref[0]], axis=0)
            dkb = jnp.zeros((2 * BLOCK, HEAD_DIM), F32)
            dvb = jnp.zeros((2 * BLOCK, HEAD_DIM), F32)
            for g in range(SWA_GROUP):
                sl = slice(g * HEAD_DIM, (g + 1) * HEAD_DIM)
                qg = q_ref[:, sl]
                p, p0 = _swa_scores(qg, kb, bias_ref[g], sink_ref[0, g:g + 1, :1], n)
                og = _dot(p, vb)
                gate = gate_ref[:, sl].astype(F32)
                dag = da_ref[:, sl].astype(F32)
                sg = jax.nn.sigmoid(gate)
                do = dag * gate * sg
                dgate_ref[:, sl] = (dag * og * sg * (1.0 + gate * (1.0 - sg))).astype(dgate_ref.dtype)
                dp = _dot(do, vb, 1, 1)
                delta = jnp.sum(do * og, axis=1, keepdims=True)
                ds = p * (dp - delta)
                dbias_ref[g] += ds
                dsink_ref[0, g:g + 1, :] += jnp.zeros((1, LANES), F32) - jnp.sum(p0 * delta, axis=0, keepdims=True)
                dq_ref[:, sl] = (_dot(ds, kb) * (HEAD_DIM ** -0.5)).astype(dq_ref.dtype)
                dkb += _dot(ds, qg, 0, 0) * (HEAD_DIM ** -0.5)
                dvb += _dot(p, do, 0, 0)
            dk_ref[0] = dk_own[...] + dkb[:BLOCK]
            dv_ref[0] = dv_own[...] + dvb[:BLOCK]
            dk_own[...] = dkb[BLOCK:]
            dv_own[...] = dvb[BLOCK:]

        @pl.when(n == nb)
        def _():
            dk_ref[0] = dk_own[...]
            dv_ref[0] = dv_own[...]

    specs, args = _swa_specs(proj, k4, v4, nb, True)
    cur = lambda n: jnp.minimum(n, nb - 1)
    trail = lambda n: jnp.maximum(n - 1, 0)
    return pl.pallas_call(
        body, name="swa_bwd", grid=(SWA_KV_HEADS, nb + 1),
        in_specs=specs + [pl.BlockSpec((SWA_GROUP, BLOCK, 2 * BLOCK), lambda h, n: (h, 0, 0)),
                          pl.BlockSpec((1, SWA_GROUP, LANES), lambda h, n: (h, 0, 0)),
                          pl.BlockSpec((BLOCK, gw), lambda h, n: (cur(n), h))],
        out_specs=[pl.BlockSpec((BLOCK, gw), lambda h, n: (cur(n), h)),
                   pl.BlockSpec((BLOCK, gw), lambda h, n: (cur(n), h)),
                   pl.BlockSpec((1, BLOCK, HEAD_DIM), lambda h, n: (h, trail(n), 0)),
                   pl.BlockSpec((1, BLOCK, HEAD_DIM), lambda h, n: (h, trail(n), 0)),
                   pl.BlockSpec((SWA_GROUP, BLOCK, 2 * BLOCK), lambda h, n: (h, 0, 0)),
                   pl.BlockSpec((1, SWA_GROUP, LANES), lambda h, n: (h, 0, 0))],
        out_shape=[jax.ShapeDtypeStruct((s, SWA_HEADS * HEAD_DIM), BF16),
                   jax.ShapeDtypeStruct((s, SWA_HEADS * HEAD_DIM), BF16),
                   jax.ShapeDtypeStruct((SWA_KV_HEADS, s, HEAD_DIM), F32),
                   jax.ShapeDtypeStruct((SWA_KV_HEADS, s, HEAD_DIM), F32),
                   jax.ShapeDtypeStruct((SWA_HEADS, BLOCK, 2 * BLOCK), F32),
                   jax.ShapeDtypeStruct((SWA_KV_HEADS, SWA_GROUP, LANES), F32)],
        scratch_shapes=[pltpu.VMEM((BLOCK, HEAD_DIM), F32), pltpu.VMEM((BLOCK, HEAD_DIM), F32)],
        compiler_params=_cp(("arbitrary", "arbitrary")),
    )(*args, bias, sinks, da)


FOX_TILE = 512


def _fox_scores(q, k, cq, ck, diag):
    s = _dot(q, k, 1, 1) * (HEAD_DIM ** -0.5) + cq - ck
    if diag:
        s = jnp.where(lax.broadcasted_iota(jnp.int32, s.shape, 1) <= lax.broadcasted_iota(jnp.int32, s.shape, 0), s, NEG)
    return s


def _below_and_on_diagonal(i, j, step):
    @pl.when(j < i)
    def _():
        step(False)

    @pl.when(j == i)
    def _():
        step(True)


def _fox_fwd(proj, cq, ck):
    s = proj.shape[0]
    t = min(FOX_TILE, s)
    nt = s // t
    nh2 = FOX_HEADS // 2
    hd = HEAD_DIM

    def body(q_ref, k_ref, v_ref, cq_ref, ck_ref, o_ref, lse_ref, m_s, l_s, acc_s):
        i, j = pl.program_id(1), pl.program_id(2)

        @pl.when(j == 0)
        def _():
            m_s[...] = jnp.full_like(m_s, NEG)
            l_s[...] = jnp.zeros_like(l_s)
            acc_s[...] = jnp.zeros_like(acc_s)

        def step(diag):
            for e in range(2):
                sl = slice(e * hd, (e + 1) * hd)
                sc = _fox_scores(q_ref[:, sl], k_ref[:, sl], cq_ref[e], ck_ref[e], diag)
                m_old = m_s[e]
                m_new = jnp.maximum(m_old, jnp.max(sc, axis=1, keepdims=True))
                alpha = jnp.exp(m_old - m_new)
                p = jnp.exp(sc - m_new)
                l_s[e] = alpha * l_s[e] + jnp.sum(p, axis=1, keepdims=True)
                acc_s[e] = alpha * acc_s[e] + _dot(p, v_ref[:, sl])
                m_s[e] = m_new

        _below_and_on_diagonal(i, j, step)

        @pl.when(j == nt - 1)
        def _():
            for e in range(2):
                o_ref[:, e * hd:(e + 1) * hd] = (acc_s[e] / l_s[e]).astype(o_ref.dtype)
                lse_ref[e] = m_s[e] + jnp.log(l_s[e])

    kcol = FOX_HEADS * hd // LANES
    return pl.pallas_call(
        body, name="fox_fwd", grid=(nh2, nt, nt),
        in_specs=[pl.BlockSpec((t, LANES), lambda h, i, j: (i, h)),
                  pl.BlockSpec((t, LANES), lambda h, i, j: (jnp.minimum(j, i), kcol + h)),
                  pl.BlockSpec((t, LANES), lambda h, i, j: (jnp.minimum(j, i), 2 * kcol + h)),
                  pl.BlockSpec((2, t, 1), lambda h, i, j: (h, i, 0)),
                  pl.BlockSpec((2, 1, t), lambda h, i, j: (h, 0, jnp.minimum(j, i)))],
        out_specs=[pl.BlockSpec((t, LANES), lambda h, i, j: (i, h)),
                   pl.BlockSpec((2, t, 1), lambda h, i, j: (h, i, 0))],
        out_shape=[jax.ShapeDtypeStruct((s, FOX_HEADS * hd), BF16),
                   jax.ShapeDtypeStruct((FOX_HEADS, s, 1), F32)],
        scratch_shapes=[pltpu.VMEM((2, t, 1), F32), pltpu.VMEM((2, t, 1), F32), pltpu.VMEM((2, t, hd), F32)],
        compiler_params=_cp(("parallel", "parallel", "arbitrary")),
    )(proj, proj, proj, cq, ck)


def _fox_bwd_dq(proj, cq, ck, o, do, lse):
    s = proj.shape[0]
    t = min(FOX_TILE, s)
    nt = s // t
    nh2 = FOX_HEADS // 2
    hd = HEAD_DIM

    def body(q_ref, k_ref, v_ref, cq_ref, ck_ref, o_ref, do_ref, lse_ref, dq_ref, delta_ref, dcq_ref, acc_s):
        i, j = pl.program_id(1), pl.program_id(2)

        @pl.when(j == 0)
        def _():
            acc_s[...] = jnp.zeros_like(acc_s)
            dcq_ref[...] = jnp.zeros_like(dcq_ref)
            for e in range(2):
                sl = slice(e * hd, (e + 1) * hd)
                delta_ref[e] = jnp.sum(do_ref[:, sl].astype(F32) * o_ref[:, sl].astype(F32), axis=1, keepdims=True)

        def step(diag):
            for e in range(2):
                sl = slice(e * hd, (e + 1) * hd)
                p = jnp.exp(_fox_scores(q_ref[:, sl], k_ref[:, sl], cq_ref[e], ck_ref[e], diag) - lse_ref[e])
                dp = _dot(do_ref[:, sl], v_ref[:, sl], 1, 1)
                ds = p * (dp - delta_ref[e])
                dcq_ref[e] += jnp.sum(ds, axis=1, keepdims=True)
                acc_s[e] += _dot(ds, k_ref[:, sl])

        _below_and_on_diagonal(i, j, step)

        @pl.when(j == nt - 1)
        def _():
            for e in range(2):
                dq_ref[:, e * hd:(e + 1) * hd] = (acc_s[e] * (hd ** -0.5)).astype(dq_ref.dtype)

    kcol = FOX_HEADS * hd // LANES
    qmap = lambda h, i, j: (i, h)
    return pl.pallas_call(
        body, name="fox_bwd_dq", grid=(nh2, nt, nt),
        in_specs=[pl.BlockSpec((t, LANES), qmap),
                  pl.BlockSpec((t, LANES), lambda h, i, j: (jnp.minimum(j, i), kcol + h)),
                  pl.BlockSpec((t, LANES), lambda h, i, j: (jnp.minimum(j, i), 2 * kcol + h)),
                  pl.BlockSpec((2, t, 1), lambda h, i, j: (h, i, 0)),
                  pl.BlockSpec((2, 1, t), lambda h, i, j: (h, 0, jnp.minimum(j, i))),
                  pl.BlockSpec((t, LANES), qmap), pl.BlockSpec((t, LANES), qmap),
                  pl.BlockSpec((2, t, 1), lambda h, i, j: (h, i, 0))],
        out_specs=[pl.BlockSpec((t, LANES), qmap), pl.BlockSpec((2, t, 1), lambda h, i, j: (h, i, 0)),
                   pl.BlockSpec((2, t, 1), lambda h, i, j: (h, i, 0))],
        out_shape=[jax.ShapeDtypeStruct((s, FOX_HEADS * hd), BF16),
                   jax.ShapeDtypeStruct((FOX_HEADS, s, 1), F32), jax.ShapeDtypeStruct((FOX_HEADS, s, 1), F32)],
        scratch_shapes=[pltpu.VMEM((2, t, hd), F32)],
        compiler_params=_cp(("parallel", "parallel", "arbitrary")),
    )(proj, proj, proj, cq, ck, o, do, lse)


def _fox_bwd_dkv(proj, cq, ck, do, lse, delta):
    s = proj.shape[0]
    t = min(FOX_TILE, s)
    nt = s // t
    nh2 = FOX_HEADS // 2
    hd = HEAD_DIM

    def body(q_ref, k_ref, v_ref, cq_ref, ck_ref, do_ref, lse_ref, delta_ref, dk_ref, dv_ref, dck_ref,
             dk_s, dv_s, dck_s):
        j, i = pl.program_id(1), pl.program_id(2)

        @pl.when(i == 0)
        def _():
            dk_s[...] = jnp.zeros_like(dk_s)
            dv_s[...] = jnp.zeros_like(dv_s)
            dck_s[...] = jnp.zeros_like(dck_s)

        def step(diag):
            for e in range(2):
                sl = slice(e * hd, (e + 1) * hd)
                p = jnp.exp(_fox_scores(q_ref[:, sl], k_ref[:, sl], cq_ref[e], ck_ref[e], diag) - lse_ref[e])
                dp = _dot(do_ref[:, sl], v_ref[:, sl], 1, 1)
                ds = p * (dp - delta_ref[e])
                dv_s[e] += _dot(p, do_ref[:, sl], 0, 0)
                dk_s[e] += _dot(ds, q_ref[:, sl], 0, 0)
                dck_s[e] += -jnp.sum(ds, axis=0, keepdims=True)

        _below_and_on_diagonal(i, j, step)

        @pl.when(i == nt - 1)
        def _():
            for e in range(2):
                dk_ref[:, e * hd:(e + 1) * hd] = (dk_s[e] * (hd ** -0.5)).astype(dk_ref.dtype)
                dv_ref[:, e * hd:(e + 1) * hd] = dv_s[e].astype(dv_ref.dtype)
                dck_ref[e] = dck_s[e]

    kcol = FOX_HEADS * hd // LANES
    qi = lambda i, j: jnp.maximum(i, j)
    return pl.pallas_call(
        body, name="fox_bwd_dkv", grid=(nh2, nt, nt),
        in_specs=[pl.BlockSpec((t, LANES), lambda h, j, i: (qi(i, j), h)),
                  pl.BlockSpec((t, LANES), lambda h, j, i: (j, kcol + h)),
                  pl.BlockSpec((t, LANES), lambda h, j, i: (j, 2 * kcol + h)),
                  pl.BlockSpec((2, t, 1), lambda h, j, i: (h, qi(i, j), 0)),
                  pl.BlockSpec((2, 1, t), lambda h, j, i: (h, 0, j)),
                  pl.BlockSpec((t, LANES), lambda h, j, i: (qi(i, j), h)),
                  pl.BlockSpec((2, t, 1), lambda h, j, i: (h, qi(i, j), 0)),
                  pl.BlockSpec((2, t, 1), lambda h, j, i: (h, qi(i, j), 0))],
        out_specs=[pl.BlockSpec((t, LANES), lambda h, j, i: (j, h)),
                   pl.BlockSpec((t, LANES), lambda h, j, i: (j, h)),
                   pl.BlockSpec((2, 1, t), lambda h, j, i: (h, 0, j))],
        out_shape=[jax.ShapeDtypeStruct((s, FOX_HEADS * hd), BF16),
                   jax.ShapeDtypeStruct((s, FOX_HEADS * hd), BF16),
                   jax.ShapeDtypeStruct((FOX_HEADS, 1, s), F32)],
        scratch_shapes=[pltpu.VMEM((2, t, hd), F32), pltpu.VMEM((2, t, hd), F32), pltpu.VMEM((2, 1, t), F32)],
        compiler_params=_cp(("parallel", "parallel", "arbitrary")),
    )(proj, proj, proj, cq, ck, do, lse, delta)


def _ssm_prep(lam_re, lam_im, log_dt, b_re, b_im, c_re, c_im):
    g, n = lam_re.shape
    dt = jnp.exp(log_dt)[:, None]
    mag = jnp.exp(lam_re * dt)
    ab_re = mag * jnp.cos(lam_im * dt)
    ab_im = mag * jnp.sin(lam_im * dt)
    den = lam_re * lam_re + lam_im * lam_im
    nr = ab_re - 1.0
    coef_re = ((nr * lam_re + ab_im * lam_im) / den)[..., None]
    coef_im = ((ab_im * lam_re - nr * lam_im) / den)[..., None]
    bb_re = coef_re * b_re - coef_im * b_im
    bb_im = coef_re * b_im + coef_im * b_re
    nblk = g // GROUPS_PER_STEP
    eye = jnp.eye(GROUPS_PER_STEP, dtype=F32)

    def bdiag(bb):
        return jnp.einsum('bgnc,gh->bgchn', bb.reshape(nblk, GROUPS_PER_STEP, n, SSM_GROUP), eye).reshape(
            nblk, GROUPS_PER_STEP * SSM_GROUP, STATE_W)

    def cdiag(cc):
        return jnp.einsum('bgcn,gh->bgnhc', cc.reshape(nblk, GROUPS_PER_STEP, SSM_GROUP, n), eye).reshape(
            nblk, STATE_W, GROUPS_PER_STEP * SSM_GROUP)

    return (ab_re.reshape(nblk, 1, STATE_W), ab_im.reshape(nblk, 1, STATE_W),
            bdiag(bb_re), bdiag(bb_im), cdiag(c_re), cdiag(c_im))


def _powers(a_re, a_im, levels):
    rs, ims = [a_re], [a_im]
    for _ in range(levels - 1):
        r, i = rs[-1], ims[-1]
        rs.append(r * r - i * i)
        ims.append(2.0 * r * i)
    return jnp.concatenate(rs, axis=1), jnp.concatenate(ims, axis=1)


def _ssm_fwd(proj, pw_re, pw_im, bd_re, bd_im, cd_re, cd_im, dskip):
    s = proj.shape[0]
    w = dskip.shape[1]
    nblk = w // LANES
    nch = STATE_W // LANES
    levels = pw_re.shape[1]

    def body(u_ref, pr, pi, br, bi, cr, ci, d_ref, pre_ref, yg_ref, xr_ref, xi_ref):
        ch = pl.program_id(1)
        u = u_ref[...]
        xr, xi = _scan_complex(_dot(u, br[0]), _dot(u, bi[0]), pr[0], pi[0], False)
        xr_ref[...] = xr.astype(BF16)
        xi_ref[...] = xi.astype(BF16)
        yc = _dot(xr, cr[0]) - _dot(xi, ci[0])

        @pl.when(ch == 0)
        def _():
            pre_ref[...] = yc + d_ref[...] * u

        @pl.when(ch > 0)
        def _():
            pre_ref[...] += yc

        @pl.when(ch == nch - 1)
        def _():
            yg_ref[...] = _gelu(pre_ref[...]).astype(BF16)

    blk = lambda b, c: (0, b)
    col = lambda b, c: (0, b * nch + c)
    in_chunk = pl.BlockSpec((1, LANES, LANES), lambda b, c: (b, 0, c))
    out_chunk = pl.BlockSpec((1, LANES, LANES), lambda b, c: (b, c, 0))
    pw_spec = pl.BlockSpec((1, levels, LANES), lambda b, c: (b, 0, c))
    return pl.pallas_call(
        body, name="ssm_fwd", grid=(nblk, nch),
        in_specs=[pl.BlockSpec((s, LANES), blk), pw_spec, pw_spec, in_chunk, in_chunk, out_chunk, out_chunk,
                  pl.BlockSpec((1, LANES), blk)],
        out_specs=[pl.BlockSpec((s, LANES), blk), pl.BlockSpec((s, LANES), blk),
                   pl.BlockSpec((s, LANES), col), pl.BlockSpec((s, LANES), col)],
        out_shape=[jax.ShapeDtypeStruct((s, w), F32), jax.ShapeDtypeStruct((s, w), BF16),
                   jax.ShapeDtypeStruct((s, nblk * STATE_W), BF16), jax.ShapeDtypeStruct((s, nblk * STATE_W), BF16)],
        compiler_params=_cp(("parallel", "arbitrary")),
    )(proj,pw_re, pw_im, bd_re, bd_im, cd_re, cd_im, dskip)


def _ssm_bwd(proj, dyg, pre, xr_all, xi_all, pw_re, pw_im, bd_re, bd_im, cd_re, cd_im, dskip):
    s = proj.shape[0]
    w = dskip.shape[1]
    nblk = w // LANES
    nch = STATE_W // LANES
    levels = pw_re.shape[1]

    def body(u_ref, dy_ref, pre_ref, xr_ref, xi_ref, pr, pi, br, bi, cr, ci, d_ref,
             du_ref, dd_ref, dar_ref, dai_ref, dbr_ref, dbi_ref, dcr_ref, dci_ref):
        ch = pl.program_id(1)
        u = u_ref[...]
        _, vjp = jax.vjp(_gelu, pre_ref[...])
        dpre = vjp(dy_ref[...].astype(F32))[0]
        zr, zi = _scan_complex(_dot(dpre, cr[0], 1, 1), -_dot(dpre, ci[0], 1, 1), pr[0], -pi[0], True)
        xpr = _shift_rows(xr_ref[...].astype(F32), 1)
        xpi = _shift_rows(xi_ref[...].astype(F32), 1)
        dar_ref[0] = jnp.sum(zr * xpr + zi * xpi, axis=0, keepdims=True)
        dai_ref[0] = jnp.sum(zi * xpr - zr * xpi, axis=0, keepdims=True)
        dcr_ref[0] = _dot(xr_ref[...], dpre, 0, 0)
        dci_ref[0] = -_dot(xi_ref[...], dpre, 0, 0)
        dbr_ref[0] = _dot(u, zr, 0, 0)
        dbi_ref[0] = _dot(u, zi, 0, 0)
        duc = _dot(zr, br[0], 1, 1) + _dot(zi, bi[0], 1, 1)

        @pl.when(ch == 0)
        def _():
            du_ref[...] = duc + dpre * d_ref[...]
            dd_ref[...] = jnp.sum(dpre * u, axis=0, keepdims=True)

        @pl.when(ch > 0)
        def _():
            du_ref[...] += duc

    blk = lambda b, c: (0, b)
    col = lambda b, c: (0, b * nch + c)
    in_chunk = pl.BlockSpec((1, LANES, LANES), lambda b, c: (b, 0, c))
    out_chunk = pl.BlockSpec((1, LANES, LANES), lambda b, c: (b, c, 0))
    pw_spec = pl.BlockSpec((1, levels, LANES), lambda b, c: (b, 0, c))
    a_spec = pl.BlockSpec((1, 1, LANES), lambda b, c: (b, 0, c))
    return pl.pallas_call(
        body, name="ssm_bwd", grid=(nblk, nch),
        in_specs=[pl.BlockSpec((s, LANES), blk), pl.BlockSpec((s, LANES), blk), pl.BlockSpec((s, LANES), blk),
                  pl.BlockSpec((s, LANES), col), pl.BlockSpec((s, LANES), col),
                  pw_spec, pw_spec, in_chunk, in_chunk, out_chunk, out_chunk, pl.BlockSpec((1, LANES), blk)],
        out_specs=[pl.BlockSpec((s, LANES), blk), pl.BlockSpec((1, LANES), blk), a_spec, a_spec,
                   in_chunk, in_chunk, out_chunk, out_chunk],
        out_shape=[jax.ShapeDtypeStruct((s, w), F32), jax.ShapeDtypeStruct((1, w), F32),
                   jax.ShapeDtypeStruct((nblk, 1, STATE_W), F32), jax.ShapeDtypeStruct((nblk, 1, STATE_W), F32),
                   jax.ShapeDtypeStruct((nblk, LANES, STATE_W), F32), jax.ShapeDtypeStruct((nblk, LANES, STATE_W), F32),
                   jax.ShapeDtypeStruct((nblk, STATE_W, LANES), F32), jax.ShapeDtypeStruct((nblk, STATE_W, LANES), F32)],
        compiler_params=_cp(("parallel", "arbitrary")),
    )(proj,dyg, pre, xr_all, xi_all, pw_re, pw_im, bd_re, bd_im, cd_re, cd_im, dskip)


ROW_TILE = 256


def _norm_fwd(x, g, name):
    s, d = x.shape
    t = min(ROW_TILE, s)
    return _tiled(lambda xv, gv: (_rms(xv, gv),), [_rows(x, t), _full(g)], [_orows(s, d, BF16, t)], s // t, name)[0]


def _norm_bwd(x, g, dh_list, dx_in, name):
    s, d = x.shape
    t = min(ROW_TILE, s)
    nh = len(dh_list)

    def fn(xv, gv, dxv, *dhs):
        dh = dhs[0].astype(F32)
        for other in dhs[1:]:
            dh = dh + other.astype(F32)
        _, vjp = jax.vjp(_rms, xv, gv)
        dx, dg = vjp(dh)
        dx = dx + dxv
        return dx, dx, dg

    return _tiled(fn, [_rows(x, t), _full(g), _rows(dx_in, t)] + [_rows(a, t) for a in dh_list],
                  [_orows(s, d, F32, t), _orows(s, d, BF16, t), _oacc((1, d))], s // t, name)


def _ple_fwd(x, y, pn, name):
    s, d = x.shape
    t = min(ROW_TILE, s)

    def fn(xv, yv, gv):
        x1 = xv + yv
        return x1, _rms(x1, gv)

    return _tiled(fn, [_rows(x, t), _rows(y, t), _full(pn)], [_orows(s, d, F32, t), _orows(s, d, BF16, t)], s // t, name)


def _ple_mix(x1, emb, gl, name):
    s, d = x1.shape
    t = min(ROW_TILE, s)
    return _tiled(lambda a, e, g: (a + e * jax.nn.sigmoid(g),), [_rows(x1, t), _rows(emb, t), _rows(gl, t)],
                  [_orows(s, d, F32, t)], s // t, name)[0]


def _ple_mix_bwd(dx2, emb, gl, name):
    s, d = dx2.shape
    t = min(ROW_TILE, s)

    def fn(dx, e, g):
        sg = jax.nn.sigmoid(g)
        return dx * sg, dx * e * sg * (1.0 - sg)

    return _tiled(fn, [_rows(dx2, t), _rows(emb, t), _rows(gl, t)],
                  [_orows(s, d, BF16, t), _orows(s, d, BF16, t)], s // t, name)


def _loss_grad(x, target, g):
    s, d = x.shape
    t = min(ROW_TILE, s)

    def fn(xv, tv, gv):
        def f(xx, gg):
            err = _rms(xx, gg) - tv
            return 0.5 * jnp.sum(jnp.mean(err * err, axis=-1, keepdims=True), axis=0, keepdims=True)

        loss, vjp = jax.vjp(f, xv, gv)
        dx, dg = vjp(jnp.ones((1, 1), F32))
        return loss, dx, dg

    return _tiled(fn, [_rows(x, t), _rows(target, t), _full(g)],
                  [_oacc((1, 1)), _orows(s, d, F32, t), _oacc((1, d))], s // t, "loss_grad")


def _conv_fwd(proj, kern):
    s = proj.shape[0]
    w = kern.shape[1]
    nb = w // LANES

    def fn(bg, cg, u, gate, k):
        z = cg.astype(F32) * u.astype(F32)
        conv = k[2:3] * z + k[1:2] * _shift_rows(z, 1) + k[0:1] * _shift_rows(z, 2)
        return (bg.astype(F32) * conv * _silu(gate.astype(F32)),)

    return _tiled(fn, [_cols(proj, LANES, q * nb) for q in range(4)] + [_cols(kern, LANES)],
                  [_ocols(s, w, BF16, LANES)], nb, "conv_fwd")[0]


def _conv_bwd(proj, kern, da):
    s = proj.shape[0]
    w = kern.shape[1]
    nb = w // LANES

    def fn(bg, cg, u, gate, k, dav):
        bg, cg, u, gate, dav = (a.astype(F32) for a in (bg, cg, u, gate, dav))
        z = cg * u
        z1, z2 = _shift_rows(z, 1), _shift_rows(z, 2)
        conv = k[2:3] * z + k[1:2] * z1 + k[0:1] * z2
        sg = jax.nn.sigmoid(gate)
        dy = dav * gate * sg
        dgate = dav * bg * conv * sg * (1.0 + gate * (1.0 - sg))
        dconv = dy * bg
        dk = jnp.concatenate([jnp.sum(dconv * zz, axis=0, keepdims=True) for zz in (z2, z1, z)], axis=0)
        dz = k[2:3] * dconv + k[1:2] * _shift_rows(dconv, 1, True) + k[0:1] * _shift_rows(dconv, 2, True)
        return dy * conv, dz * u, dz * cg, dgate, dk

    return _tiled(fn, [_cols(proj, LANES, q * nb) for q in range(4)] + [_cols(kern, LANES), _cols(da, LANES)],
                  [_ocols(s, w, BF16, LANES)] * 4 + [_ocols(3, w, F32, LANES)], nb, "conv_bwd")


def _glu_fwd(gl, proj, bglu):
    s, w2 = gl.shape
    w = w2 // 2
    t = min(ROW_TILE, s)

    def fn(glv, gate, b):
        v = glv + b
        return (v[:, :w] * jax.nn.sigmoid(v[:, w:]) * _silu(gate),)

    return _tiled(fn, [_rows(gl, t), (proj, (t, w), lambda i: (i, 1)), _full(bglu)],
                  [_orows(s, w, BF16, t)], s // t, "glu_fwd")[0]


def _glu_bwd(gl, proj, bglu, da):
    s, w2 = gl.shape
    w = w2 // 2
    t = min(ROW_TILE, s)

    def fn(glv, gate, b, dav):
        def f(gg, gt, bb):
            v = gg + bb
            return v[:, :w] * jax.nn.sigmoid(v[:, w:]) * _silu(gt)

        _, vjp = jax.vjp(f, glv, gate, b)
        return vjp(dav.astype(F32))

    return _tiled(fn, [_rows(gl, t), (proj, (t, w), lambda i: (i, 1)), _full(bglu), _rows(da, t)],
                  [_orows(s, w2, BF16, t), _orows(s, w, BF16, t), _oacc((1, w2))], s // t, "glu_bwd")


def _fg_fwd(z, b):
    def fn(zv, bv):
        v = zv + bv
        logf = jnp.minimum(v, 0.0) - jnp.log(1.0 + jnp.exp(-jnp.abs(v)))
        return (_cumsum_rows(logf, False),)

    return _tiled(fn, [_full(z), _full(b)], [_out(z.shape, F32, z.shape, lambda i: (0, 0))], 1, "fg_fwd")[0]


def _fg_bwd(z, b, dcs):
    def fn(zv, bv, dc):
        dz = _cumsum_rows(dc, True) * jax.nn.sigmoid(-(zv + bv))
        return dz, jnp.sum(dz, axis=0, keepdims=True)

    return _tiled(fn, [_full(z), _full(b), _full(dcs)],
                  [_out(z.shape, BF16, z.shape, lambda i: (0, 0)), _out((1, z.shape[1]), F32, (1, z.shape[1]), lambda i: (0, 0))],
                  1, "fg_bwd")


def _fox_gate_bwd(da, o, proj):
    s, w = o.shape
    t = min(ROW_TILE, s)

    def fn(dav, ov, gate):
        dav, ov, gate = dav.astype(F32), ov.astype(F32), gate.astype(F32)
        sg = jax.nn.sigmoid(gate)
        return dav * gate * sg, dav * ov * sg * (1.0 + gate * (1.0 - sg))

    return _tiled(fn, [_rows(da, t), _rows(o, t), (proj, (t, w), lambda i: (i, 3))],
                  [_orows(s, w, BF16, t), _orows(s, w, BF16, t)], s // t, "fox_gate_bwd")


def _fox_gate_fwd(o, proj):
    s, w = o.shape
    t = min(ROW_TILE, s)
    return _tiled(lambda ov, gate: (ov.astype(F32) * _silu(gate.astype(F32)),),
                  [_rows(o, t), (proj, (t, w), lambda i: (i, 3))], [_orows(s, w, BF16, t)], s // t, "fox_gate_fwd")[0]


def _local_step(x, p, target, w):
    s, d = x.shape
    depth = p.shape[0]
    pb = p.astype(BF16)
    grads = {}
    saved = []

    bucket = _swa_bucket_table()
    onehot = np.eye(REL_BUCKETS, dtype=np.float32)[bucket.reshape(-1)]
    bias = _bias_table(w['rel_bias'], jnp.asarray(onehot.T, BF16))
    sinks = jnp.broadcast_to(w['swa_sinks'].reshape(SWA_KV_HEADS, SWA_GROUP, 1), (SWA_KV_HEADS, SWA_GROUP, LANES))
    ssm_params = tuple(w[k][0] for k in ('ssm_lam_re', 'ssm_lam_im', 'ssm_log_dt', 'ssm_b_re', 'ssm_b_im', 'ssm_c_re', 'ssm_c_im'))
    (a_re, a_im, bd_re, bd_im, cd_re, cd_im), ssm_vjp = jax.vjp(_ssm_prep, *ssm_params)
    levels = max(1, (s - 1).bit_length())
    pw_re, pw_im = _powers(a_re, a_im, levels)
    wfg = jnp.pad(w['fox_w_fg'][0], ((0, 0), (0, LANES - FOX_HEADS)))
    bfg = jnp.pad(w['fox_b_fg'], ((0, 0), (0, LANES - FOX_HEADS)))

    def qkv4(a):
        return a.reshape(s, SWA_KV_HEADS, HEAD_DIM).transpose(1, 0, 2)

    for i in range(depth):
        mixer = i % 4
        hn = _norm_fwd(x, w['norm_g'][i:i + 1], f"norm_fwd{i}")
        sv = {'x': x, 'hn': hn}
        if mixer == 0:
            proj = _mm(hn, w['swa_w_in'][0], name="swa_in")
            qw = SWA_HEADS * HEAD_DIM
            kvw = SWA_KV_HEADS * HEAD_DIM
            k4, v4 = qkv4(proj[:, qw:qw + kvw]), qkv4(proj[:, qw + kvw:qw + 2 * kvw])
            a = _swa_fwd(proj, k4, v4, bias, sinks)
            sv.update(proj=proj, k4=k4, v4=v4)
            w_out = w['swa_w_out'][0]
        elif mixer == 1:
            proj = _mm(hn, w['conv_w_in'][0], name="conv_in")
            a = _conv_fwd(proj, w['conv_kernel'][0])
            sv.update(proj=proj)
            w_out = w['conv_w_out'][0]
        elif mixer == 2:
            proj = _mm(hn, w['ssm_w_in'][0], out_dtype=F32, name="ssm_in")
            pre, yg, xr_all, xi_all = _ssm_fwd(proj, pw_re, pw_im, bd_re, bd_im, cd_re, cd_im, w['ssm_d'])
            gl = _mm(yg, w['ssm_w_glu'][0], out_dtype=F32, name="ssm_glu")
            a = _glu_fwd(gl, proj, w['ssm_b_glu'])
            sv.update(proj=proj, pre=pre, yg=yg, xr=xr_all, xi=xi_all, gl=gl)
            w_out = w['ssm_w_out'][0]
        else:
            proj = _mm(hn, w['fox_w_in'][0], name="fox_in")
            z = _mm(hn, wfg, out_dtype=F32, name="fox_fg")
            csum = _fg_fwd(z, bfg)
            cq = csum[:, :FOX_HEADS].T.reshape(FOX_HEADS, s, 1)
            ck = cq.reshape(FOX_HEADS, 1, s)
            o, lse = _fox_fwd(proj, cq, ck)
            a = _fox_gate_fwd(o, proj)
            sv.update(proj=proj, z=z, cq=cq, ck=ck, o=o, lse=lse)
            w_out = w['fox_w_out'][0]
        y = _mm(a, w_out, out_dtype=F32, name=f"mixer_out{i}")
        x1, gn = _ple_fwd(x, y, w['ple_norm'][i:i + 1], f"ple_fwd{i}")
        emb = _mm(pb[i], w['ple_proj'][i], out_dtype=F32, name=f"ple_emb{i}")
        gl2 = _mm(gn, w['ple_gate'][i], out_dtype=F32, name=f"ple_gate{i}")
        x = _ple_mix(x1, emb, gl2, f"ple_mix{i}")
        sv.update(a=a, x1=x1, gn=gn, emb=emb, gl2=gl2)
        saved.append(sv)

    loss, dx, grads['final_g'] = _loss_grad(x, target, w['final_g'].reshape(1, d))
    grads['final_g'] = grads['final_g'].reshape(d)

    g_norm, g_ple_norm, g_ple_proj, g_ple_gate = [None] * depth, [None] * depth, [None] * depth, [None] * depth
    for i in reversed(range(depth)):
        sv = saved[i]
        mixer = i % 4
        demb, dgl2 = _ple_mix_bwd(dx, sv['emb'], sv['gl2'], f"ple_mix_bwd{i}")
        g_ple_proj[i] = _mm(pb[i], demb, ta=True, name=f"ple_emb_dw{i}")
        g_ple_gate[i] = _mm(sv['gn'], dgl2, ta=True, name=f"ple_gate_dw{i}")
        dgn = _mm(dgl2, w['ple_gate'][i], tb=True, out_dtype=F32, name=f"ple_gate_dx{i}")
        dx1, dy, g_ple_norm[i] = _norm_bwd(sv['x1'], w['ple_norm'][i:i + 1], [dgn], dx, f"ple_norm_bwd{i}")
        w_out_name = ('swa_w_out', 'conv_w_out', 'ssm_w_out', 'fox_w_out')[mixer]
        grads[w_out_name] = _mm(sv['a'], dy, ta=True, name=f"mixer_out_dw{i}")[None]
        da = _mm(dy, w[w_out_name][0], tb=True, name=f"mixer_out_dx{i}")
        proj = sv['proj']
        dhs = []
        if mixer == 0:
            dq, dgate, dk4, dv4, dbias, dsink = _swa_bwd(proj, sv['k4'], sv['v4'], bias, sinks, da)
            back = lambda t4: t4.transpose(1, 0, 2).reshape(s, SWA_KV_HEADS * HEAD_DIM).astype(BF16)
            dproj = jnp.concatenate([dq, back(dk4), back(dv4), dgate], axis=1)
            grads['rel_bias'] = _bias_grad(dbias.reshape(SWA_HEADS, -1), jnp.asarray(onehot, BF16))
            grads['swa_sinks'] = dsink[:, :, 0].reshape(1, SWA_HEADS)
            w_in_name = 'swa_w_in'
        elif mixer == 1:
            dbg, dcg, du, dgate, dkern = _conv_bwd(proj, w['conv_kernel'][0], da)
            dproj = jnp.concatenate([dbg, dcg, du, dgate], axis=1)
            grads['conv_kernel'] = dkern[None]
            w_in_name = 'conv_w_in'
        elif mixer == 2:
            dgl, dgate, dbglu = _glu_bwd(sv['gl'], proj, w['ssm_b_glu'], da)
            grads['ssm_b_glu'] = dbglu
            grads['ssm_w_glu'] = _mm(sv['yg'], dgl, ta=True, name="ssm_glu_dw")[None]
            dyg = _mm(dgl, w['ssm_w_glu'][0], tb=True, out_dtype=F32, name="ssm_glu_dx")
            du, dd, da_re, da_im, dbd_re, dbd_im, dcd_re, dcd_im = _ssm_bwd(
                proj, dyg, sv['pre'], sv['xr'], sv['xi'], pw_re, pw_im, bd_re, bd_im, cd_re, cd_im, w['ssm_d'])
            grads['ssm_d'] = dd
            dparams = ssm_vjp((da_re, da_im, dbd_re, dbd_im, dcd_re, dcd_im))
            for k, v in zip(('ssm_lam_re', 'ssm_lam_im', 'ssm_log_dt', 'ssm_b_re', 'ssm_b_im', 'ssm_c_re', 'ssm_c_im'), dparams):
                grads[k] = v[None]
            dproj = jnp.concatenate([du.astype(BF16), dgate], axis=1)
            w_in_name = 'ssm_w_in'
        else:
            do, dgate = _fox_gate_bwd(da, sv['o'], proj)
            dq, delta, dcq = _fox_bwd_dq(proj, sv['cq'], sv['ck'], sv['o'], do, sv['lse'])
            dk, dv, dck = _fox_bwd_dkv(proj, sv['cq'], sv['ck'], do, sv['lse'], delta)
            dcs = jnp.pad((dck.reshape(FOX_HEADS, s) + dcq.reshape(FOX_HEADS, s)).T, ((0, 0), (0, LANES - FOX_HEADS)))
            dz, dbfg = _fg_bwd(sv['z'], bfg, dcs)
            grads['fox_b_fg'] = dbfg[:, :FOX_HEADS]
            grads['fox_w_fg'] = _mm(sv['hn'], dz, ta=True, out_dtype=BF16, name="fox_fg_dw")[:, :FOX_HEADS][None]
            dhs.append(_mm(dz, wfg, tb=True, out_dtype=F32, name="fox_fg_dx"))
            dproj = jnp.concatenate([dq, dk, dv, dgate], axis=1)
            w_in_name = 'fox_w_in'
        grads[w_in_name] = _mm(sv['hn'], dproj, ta=True, name=f"mixer_in_dw{i}")[None]
        dhs.append(_mm(dproj, w[w_in_name][0], tb=True, out_dtype=F32, name=f"mixer_in_dx{i}"))
        dx, _, g_norm[i] = _norm_bwd(sv['x'], w['norm_g'][i:i + 1], dhs, dx1, f"norm_bwd{i}")

    grads['norm_g'] = jnp.concatenate(g_norm, axis=0)
    grads['ple_norm'] = jnp.concatenate(g_ple_norm, axis=0)
    grads['ple_proj'] = jnp.stack(g_ple_proj)
    grads['ple_gate'] = jnp.stack(g_ple_gate)
    return loss, dx, grads


def _split3(a):
    hi = a.astype(BF16)
    r1 = a - hi.astype(F32)
    mid = r1.astype(BF16)
    lo = (r1 - mid.astype(F32)).astype(BF16)
    return jnp.concatenate([hi, mid, lo], axis=0)


def _bias_grad(dbias, onehot):
    out = _mm(_split3(dbias), onehot, out_dtype=F32, name="rel_bias_grad", tk=2048)
    nh = dbias.shape[0]
    return (out[:nh] + out[nh:2 * nh] + out[2 * nh:]).T


def _bias_table(rel_bias, onehot_t):
    nh = rel_bias.shape[1]
    out = _mm(_split3(rel_bias.T), onehot_t, out_dtype=F32, name="rel_bias_table")
    return (out[:nh] + out[nh:2 * nh] + out[2 * nh:]).reshape(nh, BLOCK, 2 * BLOCK)


WEIGHTS = ['norm_g', 'final_g', 'rel_bias', 'swa_w_in', 'swa_w_out', 'swa_sinks', 'conv_w_in', 'conv_kernel', 'conv_w_out',
           'ssm_w_in', 'ssm_lam_re', 'ssm_lam_im', 'ssm_log_dt', 'ssm_b_re', 'ssm_b_im', 'ssm_c_re', 'ssm_c_im', 'ssm_d',
           'ssm_w_glu', 'ssm_b_glu', 'ssm_w_out', 'fox_w_in', 'fox_w_fg', 'fox_b_fg', 'fox_w_out', 'ple_proj', 'ple_norm',
           'ple_gate']
BIG = {'swa_w_in': 2, 'swa_w_out': 1, 'conv_w_in': 2, 'conv_w_out': 1, 'ssm_w_in': 2, 'ssm_w_glu': 2, 'ssm_w_out': 1,
       'fox_w_in': 2, 'fox_w_fg': 1, 'fox_w_out': 1, 'ple_proj': 2, 'ple_gate': 1}
SMALL = {'conv_kernel': 2, 'ssm_d': 1, 'ssm_b_glu': 1}
REPLICATED = [n for n in WEIGHTS if n not in BIG and n not in SMALL]
N_CHIPS = 4
N_DEV = 8
BIG_ROWS = 256
SMALL_ROWS = 8
REPL_ROWS = 64


def _flat(pieces, dtype, lead, row_mult):
    flat = jnp.concatenate([q.astype(dtype) for q in pieces], axis=-1)
    pad = (-flat.shape[-1]) % (row_mult * FLAT_W)
    flat = jnp.pad(flat, [(0, 0)] * len(lead) + [(0, pad)])
    return flat.reshape(*lead, -1, FLAT_W)


def _unflat(flat, lead_ndim, sizes):
    lead = flat.shape[:lead_ndim]
    flat = flat.reshape(*lead, -1)
    out, off = [], 0
    for n in sizes:
        out.append(flat[..., off:off + n])
        off += n
    return out


def _split_shards(full, axis):
    shp = full.shape
    parts = full.reshape(shp[:axis] + (N_CHIPS, shp[axis] // N_CHIPS) + shp[axis + 1:])
    return jnp.moveaxis(parts, axis, 0)


def _join_shards(parts, axis):
    moved = jnp.moveaxis(parts, 0, axis)
    shp = moved.shape
    return moved.reshape(shp[:axis] + (shp[axis] * shp[axis + 1],) + shp[axis + 2:])


def _coords():
    return lax.axis_index("x"), lax.axis_index("y"), lax.axis_index("c")


def _remote(k, src, dst, to, send_sems, recv_sems):
    return pltpu.make_async_remote_copy(src_ref=src, dst_ref=dst, send_sem=send_sems.at[k], recv_sem=recv_sems.at[k],
                                        device_id=to, device_id_type=MESH)


def _gather_weights(wsh, ssh):
    def body(w_ref, s_ref, wout, sout, send_sems, recv_sems):
        x, y, c = _coords()
        me = 2 * x + y
        chips = [(1 - x, y), (x, 1 - y), (1 - x, 1 - y)]
        rc = functools.partial(_remote, send_sems=send_sems, recv_sems=recv_sems)
        sends = []
        for j, (cx, cy) in enumerate(chips):
            sends.append(rc(j, w_ref.at[c], wout.at[me, c], (cx, cy, c)))
            sends.append(rc(3 + j, s_ref, sout.at[me], (cx, cy, c)))
        for cp in sends:
            cp.start()
        for j, (cx, cy) in enumerate(chips):
            k = 2 * cx + cy
            rc(j, w_ref.at[c], wout.at[k, c], (x, y, c)).wait_recv()
            fwd = rc(6 + j, wout.at[k, c], wout.at[k, c], (x, y, 1 - c))
            fwd.start()
            sends.append(fwd)
        for j, (cx, cy) in enumerate(chips):
            k = 2 * cx + cy
            rc(6 + j, w_ref.at[c], wout.at[k, 1 - c], (x, y, c)).wait_recv()
            rc(3 + j, s_ref, sout.at[k], (x, y, c)).wait_recv()
        for cp in sends:
            cp.wait_send()

    return pl.pallas_call(
        body, name="gather_weights", in_specs=[ANY, ANY], out_specs=[ANY, ANY],
        out_shape=[jax.ShapeDtypeStruct((N_CHIPS,) + wsh.shape, wsh.dtype),
                   jax.ShapeDtypeStruct((N_CHIPS,) + ssh.shape, ssh.dtype)],
        scratch_shapes=[pltpu.SemaphoreType.DMA((9,)), pltpu.SemaphoreType.DMA((9,))],
    )(wsh, ssh)


def _pair_exchange(g4):
    def body(g_ref, out, send_sems, recv_sems):
        x, y, c = _coords()
        rc = functools.partial(_remote, send_sems=send_sems, recv_sems=recv_sems)
        sends = [rc(j, g_ref.at[j, 1 - c], out.at[j], (x, y, 1 - c)) for j in range(N_CHIPS)]
        for cp in sends:
            cp.start()
        for j in range(N_CHIPS):
            rc(j, g_ref.at[j, c], out.at[j], (x, y, c)).wait_recv()
        for cp in sends:
            cp.wait_send()

    return pl.pallas_call(
        body, name="pair_exchange", in_specs=[ANY], out_specs=ANY,
        out_shape=jax.ShapeDtypeStruct((N_CHIPS,) + g4.shape[2:], g4.dtype),
        scratch_shapes=[pltpu.SemaphoreType.DMA((N_CHIPS,)), pltpu.SemaphoreType.DMA((N_CHIPS,))],
    )(g4)


def _pair_sum(mine, theirs):
    n, r, w = mine.shape
    t = _tile(r, 256, 16)
    spec = lambda a: (a, (n, t, w), lambda i: (0, i, 0))
    return _tiled(lambda a, b: (a.astype(F32) + b.astype(F32),), [spec(mine), spec(theirs)],
                  [_out((n, r, w), BF16, (n, t, w), lambda i: (0, i, 0))], r // t, "pair_sum")[0]


def _exchange_grads(pbig, gsmall, grepl):
    def body(g_ref, s_ref, r_ref, og, osm, orp, send_sems, recv_sems, local_sems):
        x, y, c = _coords()
        me = 4 * x + 2 * y + c
        my_chip = 2 * x + y
        rc = functools.partial(_remote, send_sems=send_sems, recv_sems=recv_sems)
        local = [pltpu.make_async_copy(s_ref.at[me], osm.at[me], local_sems.at[0]),
                 pltpu.make_async_copy(r_ref, orp.at[me], local_sems.at[1])]
        for cp in local:
            cp.start()
        peers = []
        for d in range(1, N_DEV):
            px = 1 - x if d & 4 else x
            py = 1 - y if d & 2 else y
            pc = 1 - c if d & 1 else c
            peers.append((d, 4 * px + 2 * py + pc, 2 * px + py, (px, py, pc)))
        sends = []
        for i, (d, peer, chip, to) in enumerate(peers):
            sends.append(rc(3 * i + 1, s_ref.at[peer], osm.at[me], to))
            sends.append(rc(3 * i + 2, r_ref, orp.at[me], to))
            if d & 1 == 0:
                sends.append(rc(3 * i, g_ref.at[chip], og.at[my_chip], to))
        for cp in sends:
            cp.start()
        for i, (d, peer, chip, to) in enumerate(peers):
            rc(3 * i + 1, s_ref.at[peer], osm.at[peer], to).wait_recv()
            rc(3 * i + 2, r_ref, orp.at[peer], to).wait_recv()
            if d & 1 == 0:
                rc(3 * i, g_ref.at[chip], og.at[chip], to).wait_recv()
        for cp in sends:
            cp.wait_send()
        for cp in local:
            cp.wait()

    nsem = 3 * (N_DEV - 1)
    return pl.pallas_call(
        body, name="exchange_grads", in_specs=[ANY, ANY, ANY], out_specs=[ANY, ANY, ANY],
        out_shape=[jax.ShapeDtypeStruct(pbig.shape, pbig.dtype), jax.ShapeDtypeStruct(gsmall.shape, gsmall.dtype),
                   jax.ShapeDtypeStruct((N_DEV,) + grepl.shape, grepl.dtype)],
        scratch_shapes=[pltpu.SemaphoreType.DMA((nsem,)), pltpu.SemaphoreType.DMA((nsem,)), pltpu.SemaphoreType.DMA((2,))],
    )(pbig, gsmall, grepl)


SIBLING_CHUNKS = 4


def _sibling_exchange(gh, sh):
    rows = gh.shape[0] // SIBLING_CHUNKS

    def body(g_ref, s_ref, og, osm, send_sems, recv_sems):
        x, y, c = _coords()
        rc = functools.partial(_remote, send_sems=send_sems, recv_sems=recv_sems)
        chunk = lambda ref, k: ref.at[pl.ds(k * rows, rows)]
        sends = [rc(k, chunk(g_ref, k), chunk(og, k), (x, y, 1 - c)) for k in range(SIBLING_CHUNKS)]
        sends.append(rc(SIBLING_CHUNKS, s_ref, osm, (x, y, 1 - c)))
        for cp in sends:
            cp.start()
        for k in range(SIBLING_CHUNKS):
            rc(k, chunk(g_ref, k), chunk(og, k), (x, y, c)).wait_recv()
        rc(SIBLING_CHUNKS, s_ref, osm, (x, y, c)).wait_recv()
        for cp in sends:
            cp.wait_send()

    nsem = SIBLING_CHUNKS + 1
    return pl.pallas_call(
        body, name="sibling_exchange", in_specs=[ANY, ANY], out_specs=[ANY, ANY],
        out_shape=[jax.ShapeDtypeStruct(gh.shape, gh.dtype), jax.ShapeDtypeStruct(sh.shape, sh.dtype)],
        scratch_shapes=[pltpu.SemaphoreType.DMA((nsem,)), pltpu.SemaphoreType.DMA((nsem,))],
    )(gh, sh)


def _sum_senders(recv, name):
    n, r, w = recv.shape
    t = _tile(r, 256, 8)

    def fn(v):
        acc = v[0].astype(F32)
        for i in range(1, n):
            acc = acc + v[i].astype(F32)
        return (acc,)

    return _tiled(fn, [(recv, (n, t, w), lambda i: (0, i, 0))], [_orows(r, w, F32, t)], r // t, name)[0]


def _adamw(w, g, m, v, name):
    r, wd = w.shape
    t = _tile(r, 256, 8)

    def fn(wv, gv, mv, vv):
        m2 = ADAM_B1 * mv + (1.0 - ADAM_B1) * gv
        v2 = ADAM_B2 * vv + (1.0 - ADAM_B2) * (gv * gv)
        m_hat = m2 / (1.0 - ADAM_B1 ** ADAM_STEP)
        v_hat = v2 / (1.0 - ADAM_B2 ** ADAM_STEP)
        delta = -ADAM_LR * (m_hat / (jnp.sqrt(v_hat) + ADAM_EPS) + ADAM_WD * wv)
        return delta, m2, v2

    return _tiled(fn, [_rows(a, t) for a in (w, g, m, v)], [_orows(r, wd, F32, t)] * 3, r // t, name)


def kernel(x, p, norm_g, final_g, rel_bias, swa_w_in, swa_w_out, swa_sinks, conv_w_in, conv_kernel, conv_w_out, ssm_w_in, ssm_lam_re, ssm_lam_im, ssm_log_dt, ssm_b_re, ssm_b_im, ssm_c_re, ssm_c_im, ssm_d, ssm_w_glu, ssm_b_glu, ssm_w_out, fox_w_in, fox_w_fg, fox_b_fg, fox_w_out, ple_proj, ple_norm, ple_gate, loss_target, m_norm_g, m_final_g, m_rel_bias, m_swa_w_in, m_swa_w_out, m_swa_sinks, m_conv_w_in, m_conv_kernel, m_conv_w_out, m_ssm_w_in, m_ssm_lam_re, m_ssm_lam_im, m_ssm_log_dt, m_ssm_b_re, m_ssm_b_im, m_ssm_c_re, m_ssm_c_im, m_ssm_d, m_ssm_w_glu, m_ssm_b_glu, m_ssm_w_out, m_fox_w_in, m_fox_w_fg, m_fox_b_fg, m_fox_w_out, m_ple_proj, m_ple_norm, m_ple_gate, v_norm_g, v_final_g, v_rel_bias, v_swa_w_in, v_swa_w_out, v_swa_sinks, v_conv_w_in, v_conv_kernel, v_conv_w_out, v_ssm_w_in, v_ssm_lam_re, v_ssm_lam_im, v_ssm_log_dt, v_ssm_b_re, v_ssm_b_im, v_ssm_c_re, v_ssm_c_im, v_ssm_d, v_ssm_w_glu, v_ssm_b_glu, v_ssm_w_out, v_fox_w_in, v_fox_w_fg, v_fox_b_fg, v_fox_w_out, v_ple_proj, v_ple_norm, v_ple_gate):
    given = dict(locals())
    shard_shape = {n: given[n].shape for n in WEIGHTS}
    half = {n: math.prod(shard_shape[n]) // 2 for n in WEIGHTS}

    wsh = _flat([given[n].reshape(-1) for n in BIG], BF16, (), 2 * BIG_ROWS)
    ssh = _flat([given[n].reshape(-1) for n in SMALL], F32, (), SMALL_ROWS)
    core = lax.axis_index("c")
    my_chip = 2 * lax.axis_index("x") + lax.axis_index("y")
    wsh = wsh.reshape(2, -1, FLAT_W)
    wall, sall = _gather_weights(wsh, ssh)
    wall = lax.dynamic_update_slice(wall, wsh[None], (my_chip, 0, 0, 0))
    sall = lax.dynamic_update_slice(sall, ssh[None], (my_chip, 0, 0))
    full = {n: given[n] for n in REPLICATED}
    for group, gathered in ((BIG, wall.reshape(N_CHIPS, -1, FLAT_W)), (SMALL, sall)):
        pieces = _unflat(gathered, 1, [2 * half[n] for n in group])
        for n, piece in zip(group, pieces):
            full[n] = _join_shards(piece.reshape((N_CHIPS,) + shard_shape[n]), group[n])

    loss, dx, grads = _local_step(x[0], p[:, 0], loss_target[0], full)

    gbig = _flat([_split_shards(grads[n], BIG[n]).reshape(N_DEV, -1) for n in BIG], BF16, (N_DEV,), BIG_ROWS)
    gsmall = _flat([_split_shards(grads[n], SMALL[n]).reshape(N_DEV, -1) for n in SMALL], F32, (N_DEV,), SMALL_ROWS)
    grepl = _flat([grads[n].reshape(-1) for n in REPLICATED], F32, (), REPL_ROWS)
    g4 = gbig.reshape(N_CHIPS, 2, -1, FLAT_W)
    pbig = _pair_sum(lax.dynamic_index_in_dim(g4, core, axis=1, keepdims=False), _pair_exchange(g4))
    recv_big, recv_small, recv_repl = _exchange_grads(pbig, gsmall, grepl)
    recv_big = lax.dynamic_update_slice(recv_big, lax.dynamic_index_in_dim(pbig, my_chip, axis=0), (my_chip, 0, 0))
    gh, sh = _sum_senders(recv_big, "sum_big"), _sum_senders(recv_small, "sum_small")
    og, osm = _sibling_exchange(gh, sh)
    both = lambda a, b: jnp.stack([jnp.where(core == 0, a, b), jnp.where(core == 0, b, a)])
    gfull, sfull = both(gh, og), both(sh, osm)
    g_repl = _sum_senders(recv_repl, "sum_repl")

    out_g, out_d, out_m, out_v = {}, {}, {}, {}
    for n, piece in zip(BIG, _unflat(gfull, 1, [half[n] for n in BIG])):
        as_rows = lambda t, n=n: t.reshape(-1, shard_shape[n][-1])
        d, m2, v2 = _adamw(as_rows(given[n]), as_rows(piece), as_rows(given['m_' + n]), as_rows(given['v_' + n]), f"adamw_{n}")
        out_g[n], out_d[n], out_m[n], out_v[n] = (t.reshape(shard_shape[n]) for t in (piece, d, m2, v2))

    pack_small = lambda pre: _flat([given[pre + n].reshape(2, -1) for n in SMALL], F32, (2,), SMALL_ROWS).reshape(-1, FLAT_W)
    res = _adamw(pack_small(''), sfull.reshape(-1, FLAT_W), pack_small('m_'), pack_small('v_'), "adamw_small")
    for dst, flat in zip((out_g, out_d, out_m, out_v), (sfull,) + tuple(res)):
        for n, piece in zip(SMALL, _unflat(flat.reshape(2, -1, FLAT_W), 1, [half[n] for n in SMALL])):
            dst[n] = piece.reshape(shard_shape[n])

    pack_repl = lambda pre: _flat([given[pre + n].reshape(-1) for n in REPLICATED], F32, (), REPL_ROWS)
    res = _adamw(pack_repl(''), g_repl, pack_repl('m_'), pack_repl('v_'), "adamw_repl")
    for dst, flat in zip((out_g, out_d, out_m, out_v), (g_repl,) + tuple(res)):
        for n, piece in zip(REPLICATED, _unflat(flat, 0, [2 * half[n] for n in REPLICATED])):
            dst[n] = piece.reshape(shard_shape[n])

    total = lax.psum(loss[0, 0], ("x", "y", "c"))
    return (total, dx[None], *[out_g[n] for n in WEIGHTS], *[out_d[n] for n in WEIGHTS],
            *[out_m[n] for n in WEIGHTS], *[out_v[n] for n in WEIGHTS])
```

```python
import functools
import math

import numpy as np
import jax
import jax.numpy as jnp
from jax import lax
from jax.experimental import pallas as pl
from jax.experimental.pallas import tpu as pltpu

F32 = jnp.float32
BF16 = jnp.bfloat16

EPS = 1e-6
BLOCK = 128
REL_BUCKETS = 32
REL_MAX_DIST = 128
SWA_HEADS, SWA_KV_HEADS, HEAD_DIM = 32, 4, 64
SWA_GROUP = SWA_HEADS // SWA_KV_HEADS
FOX_HEADS = 32
SSM_GROUP, SSM_STATE = 16, 64
GROUPS_PER_STEP = 8
STATE_W = GROUPS_PER_STEP * SSM_STATE
LANES = 128
NEG = -1e30

ADAM_LR, ADAM_B1, ADAM_B2, ADAM_EPS, ADAM_WD, ADAM_STEP = 0.001, 0.9, 0.999, 1e-08, 0.01, 10

VMEM_LIMIT_V7X = 56 * 1024 * 1024
FLAT_W = 1024
MESH = pl.DeviceIdType.MESH
ANY = pl.BlockSpec(memory_space=pl.ANY)


def _cp(sem):
    return pltpu.CompilerParams(dimension_semantics=sem, vmem_limit_bytes=VMEM_LIMIT_V7X)


def _tile(n, target, mult=LANES):
    if n <= target:
        return n
    t = (target // mult) * mult
    while t >= mult:
        if n % t == 0:
            return t
        t -= mult
    return n


def _mm(a, b, *, ta=False, tb=False, out_dtype=BF16, name, tm=1024, tn=1024, tk=1024):
    if ta:
        kdim, m = a.shape
    else:
        m, kdim = a.shape
    n = b.shape[0] if tb else b.shape[1]
    assert (b.shape[1] if tb else b.shape[0]) == kdim
    tm, tn, tk = _tile(m, tm), _tile(n, tn), _tile(kdim, tk)
    nk = kdim // tk
    dn = (((0 if ta else 1,), (1 if tb else 0,)), ((), ()))

    def body(a_ref, b_ref, o_ref, acc_ref):
        k = pl.program_id(2)

        @pl.when(k == 0)
        def _():
            acc_ref[...] = jnp.zeros_like(acc_ref)

        acc_ref[...] += lax.dot_general(a_ref[...].astype(BF16), b_ref[...].astype(BF16), dn,
                                        preferred_element_type=F32)

        @pl.when(k == nk - 1)
        def _():
            o_ref[...] = acc_ref[...].astype(o_ref.dtype)

    a_spec = pl.BlockSpec((tk, tm), lambda i, j, k: (k, i)) if ta else pl.BlockSpec((tm, tk), lambda i, j, k: (i, k))
    b_spec = pl.BlockSpec((tn, tk), lambda i, j, k: (j, k)) if tb else pl.BlockSpec((tk, tn), lambda i, j, k: (k, j))
    return pl.pallas_call(
        body, name=name, grid=(m // tm, n // tn, nk),
        in_specs=[a_spec, b_spec], out_specs=pl.BlockSpec((tm, tn), lambda i, j, k: (i, j)),
        out_shape=jax.ShapeDtypeStruct((m, n), out_dtype),
        scratch_shapes=[pltpu.VMEM((tm, tn), F32)],
        compiler_params=_cp(("parallel", "parallel", "arbitrary")),
    )(a, b)


def _rows(arr, t):
    return (arr, (t, arr.shape[1]), lambda i: (i, 0))


def _cols(arr, cb, off=0):
    return (arr, (arr.shape[0], cb), lambda i: (0, i + off))


def _full(arr):
    nd = arr.ndim
    return (arr, arr.shape, lambda i: (0,) * nd)


def _lead(arr):
    return (arr, (1,) + arr.shape[1:], lambda i: (i, 0, 0))


def _out(shape, dtype, block, imap, acc=False):
    return (jax.ShapeDtypeStruct(shape, dtype), block, imap, acc)


def _orows(s, w, dtype, t):
    return _out((s, w), dtype, (t, w), lambda i: (i, 0))


def _ocols(s, w, dtype, cb):
    return _out((s, w), dtype, (s, cb), lambda i: (0, i))


def _oacc(shape):
    nd = len(shape)
    return _out(shape, F32, shape, lambda i: (0,) * nd, True)


def _olead(n, a, b, dtype=F32):
    return _out((n, a, b), dtype, (1, a, b), lambda i: (i, 0, 0))


def _tiled(fn, ins, outs, n, name):
    has_acc = any(o[3] for o in outs)
    ni = len(ins)

    def body(*refs):
        vals = fn(*[r[...] for r in refs[:ni]])
        i = pl.program_id(0)
        for r, v, o in zip(refs[ni:], vals, outs):
            if o[3]:
                @pl.when(i == 0)
                def _(r=r, v=v):
                    r[...] = v.astype(r.dtype)

                @pl.when(i > 0)
                def _(r=r, v=v):
                    r[...] += v.astype(r.dtype)
            else:
                r[...] = v.astype(r.dtype)

    res = pl.pallas_call(
        body, name=name, grid=(n,),
        in_specs=[pl.BlockSpec(b, m) for _, b, m in ins],
        out_specs=[pl.BlockSpec(b, m) for _, b, m, _ in outs],
        out_shape=[s for s, _, _, _ in outs],
        compiler_params=_cp(("arbitrary",) if has_acc else ("parallel",)),
    )(*[a for a, _, _ in ins])
    return res


def _silu(x):
    return x * jax.nn.sigmoid(x)


def _gelu(x):
    return 0.5 * x * (1.0 + jnp.tanh(math.sqrt(2.0 / math.pi) * (x + 0.044715 * (x * x * x))))


def _rms(x, g):
    r = lax.rsqrt(jnp.mean(x * x, axis=-1, keepdims=True) + EPS)
    return x * r * g


def _shift_rows(x, sh, up=False):
    s = x.shape[0]
    rows = lax.broadcasted_iota(jnp.int32, x.shape, 0)
    if up:
        return jnp.where(rows < s - sh, pltpu.roll(x, s - sh, 0), 0.0)
    return jnp.where(rows >= sh, pltpu.roll(x, sh, 0), 0.0)


def _scan_complex(xr, xi, pr, pi, reverse):
    s = xr.shape[0]
    k = 0
    while (1 << k) < s:
        sh = 1 << k
        sr, si = _shift_rows(xr, sh, reverse), _shift_rows(xi, sh, reverse)
        ar, ai = pr[k:k + 1, :], pi[k:k + 1, :]
        xr, xi = xr + ar * sr - ai * si, xi + ar * si + ai * sr
        k += 1
    return xr, xi


def _cumsum_rows(x, reverse):
    s = x.shape[0]
    k = 0
    while (1 << k) < s:
        x = x + _shift_rows(x, 1 << k, reverse)
        k += 1
    return x


def _dot(a, b, ca=1, cb=0):
    return lax.dot_general(a.astype(BF16), b.astype(BF16), (((ca,), (cb,)), ((), ())), preferred_element_type=F32)


def _t5_bucket(dist):
    max_exact = REL_BUCKETS // 2
    d = np.maximum(dist, 1).astype(np.float32)
    large = max_exact + (np.log(d / max_exact) / np.log(REL_MAX_DIST / max_exact) * (REL_BUCKETS - max_exact)).astype(np.int32)
    large = np.minimum(large, REL_BUCKETS - 1)
    return np.where(dist < max_exact, dist, large).astype(np.int32)


def _swa_bucket_table():
    qi = np.arange(BLOCK)[:, None]
    kj = np.arange(2 * BLOCK)[None, :]
    return _t5_bucket(np.clip(qi + BLOCK - kj, 0, None))


def _swa_scores(qg, kb, bias_g, sk, n):
    s = _dot(qg, kb, 1, 1) * (HEAD_DIM ** -0.5) + bias_g
    row = lax.broadcasted_iota(jnp.int32, s.shape, 0)
    col = lax.broadcasted_iota(jnp.int32, s.shape, 1)
    dist = row + BLOCK - col
    mask = (dist >= 0) & (dist < BLOCK) & ((col >= BLOCK) | (n > 0))
    s = jnp.where(mask, s, NEG)
    m = jnp.maximum(jnp.max(s, axis=1, keepdims=True), sk)
    e = jnp.exp(s - m)
    es = jnp.exp(sk - m)
    den = jnp.sum(e, axis=1, keepdims=True) + es
    return e / den, es / den


def _swa_specs(proj, k4, v4, nb, clamp):
    gw = SWA_GROUP * HEAD_DIM
    gate_off = (SWA_HEADS * HEAD_DIM + 2 * SWA_KV_HEADS * HEAD_DIM) // gw
    cur = (lambda n: jnp.minimum(n, nb - 1)) if clamp else (lambda n: n)
    prev = lambda n: jnp.maximum(cur(n) - 1, 0)
    return [
        pl.BlockSpec((BLOCK, gw), lambda h, n: (cur(n), h)),
        pl.BlockSpec((1, BLOCK, HEAD_DIM), lambda h, n: (h, cur(n), 0)),
        pl.BlockSpec((1, BLOCK, HEAD_DIM), lambda h, n: (h, prev(n), 0)),
        pl.BlockSpec((1, BLOCK, HEAD_DIM), lambda h, n: (h, cur(n), 0)),
        pl.BlockSpec((1, BLOCK, HEAD_DIM), lambda h, n: (h, prev(n), 0)),
        pl.BlockSpec((BLOCK, gw), lambda h, n: (cur(n), gate_off + h)),
    ], [proj, k4, k4, v4, v4, proj]


def _swa_fwd(proj, k4, v4, bias, sinks):
    s = proj.shape[0]
    nb = s // BLOCK
    gw = SWA_GROUP * HEAD_DIM

    def body(q_ref, kc_ref, kp_ref, vc_ref, vp_ref, gate_ref, bias_ref, sink_ref, a_ref):
        n = pl.program_id(1)
        kb = jnp.concatenate([kp_ref[0], kc_ref[0]], axis=0)
        vb = jnp.concatenate([vp_ref[0], vc_ref[0]], axis=0)
        for g in range(SWA_GROUP):
            sl = slice(g * HEAD_DIM, (g + 1) * HEAD_DIM)
            p, _ = _swa_scores(q_ref[:, sl], kb, bias_ref[g], sink_ref[0, g:g + 1, :1], n)
            og = _dot(p, vb)
            a_ref[:, sl] = (og * _silu(gate_ref[:, sl].astype(F32))).astype(a_ref.dtype)

    specs, args = _swa_specs(proj, k4, v4, nb, False)
    return pl.pallas_call(
        body, name="swa_fwd", grid=(SWA_KV_HEADS, nb),
        in_specs=specs + [pl.BlockSpec((SWA_GROUP, BLOCK, 2 * BLOCK), lambda h, n: (h, 0, 0)),
                          pl.BlockSpec((1, SWA_GROUP, LANES), lambda h, n: (h, 0, 0))],
        out_specs=pl.BlockSpec((BLOCK, gw), lambda h, n: (n, h)),
        out_shape=jax.ShapeDtypeStruct((s, SWA_HEADS * HEAD_DIM), BF16),
        compiler_params=_cp(("parallel", "parallel")),
    )(*args, bias, sinks)


def _swa_bwd(proj, k4, v4, bias, sinks, da):
    s = proj.shape[0]
    nb = s // BLOCK
    gw = SWA_GROUP * HEAD_DIM

    def body(q_ref, kc_ref, kp_ref, vc_ref, vp_ref, gate_ref, bias_ref, sink_ref, da_ref,
             dq_ref, dgate_ref, dk_ref, dv_ref, dbias_ref, dsink_ref, dk_own, dv_own):
        n = pl.program_id(1)

        @pl.when(n == 0)
        def _():
            dk_own[...] = jnp.zeros_like(dk_own)
            dv_own[...] = jnp.zeros_like(dv_own)
            dbias_ref[...] = jnp.zeros_like(dbias_ref)
            dsink_ref[...] = jnp.zeros_like(dsink_ref)

        @pl.when(n < nb)
        def _():
            kb = jnp.concatenate([kp_ref[0], kc_ref[0]], axis=0)
            vb = jnp.concatenate([vp_ref[0], vc_ref[0]], axis=0)
            dkb = jnp.zeros((2 * BLOCK, HEAD_DIM), F32)
            dvb = jnp.zeros((2 * BLOCK, HEAD_DIM), F32)
            for g in range(SWA_GROUP):
                sl = slice(g * HEAD_DIM, (g + 1) * HEAD_DIM)
                qg = q_ref[:, sl]
                p, p0 = _swa_scores(qg, kb, bias_ref[g], sink_ref[0, g:g + 1, :1], n)
                og = _dot(p, vb)
                gate = gate_ref[:, sl].astype(F32)
                dag = da_ref[:, sl].astype(F32)
                sg = jax.nn.sigmoid(gate)
                do = dag * gate * sg
                dgate_ref[:, sl] = (dag * og * sg * (1.0 + gate * (1.0 - sg))).astype(dgate_ref.dtype)
                dp = _dot(do, vb, 1, 1)
                delta = jnp.sum(do * og, axis=1, keepdims=True)
                ds = p * (dp - delta)
                dbias_ref[g] += ds
                dsink_ref[0, g:g + 1, :] += jnp.zeros((1, LANES), F32) - jnp.sum(p0 * delta, axis=0, keepdims=True)
                dq_ref[:, sl] = (_dot(ds, kb) * (HEAD_DIM ** -0.5)).astype(dq_ref.dtype)
                dkb += _dot(ds, qg, 0, 0) * (HEAD_DIM ** -0.5)
                dvb += _dot(p, do, 0, 0)
            dk_ref[0] = dk_own[...] + dkb[:BLOCK]
            dv_ref[0] = dv_own[...] + dvb[:BLOCK]
            dk_own[...] = dkb[BLOCK:]
            dv_own[...] = dvb[BLOCK:]

        @pl.when(n == nb)
        def _():
            dk_ref[0] = dk_own[...]
            dv_ref[0] = dv_own[...]

    specs, args = _swa_specs(proj, k4, v4, nb, True)
    cur = lambda n: jnp.minimum(n, nb - 1)
    trail = lambda n: jnp.maximum(n - 1, 0)
    return pl.pallas_call(
        body, name="swa_bwd", grid=(SWA_KV_HEADS, nb + 1),
        in_specs=specs + [pl.BlockSpec((SWA_GROUP, BLOCK, 2 * BLOCK), lambda h, n: (h, 0, 0)),
                          pl.BlockSpec((1, SWA_GROUP, LANES), lambda h, n: (h, 0, 0)),
                          pl.BlockSpec((BLOCK, gw), lambda h, n: (cur(n), h))],
        out_specs=[pl.BlockSpec((BLOCK, gw), lambda h, n: (cur(n), h)),
                   pl.BlockSpec((BLOCK, gw), lambda h, n: (cur(n), h)),
                   pl.BlockSpec((1, BLOCK, HEAD_DIM), lambda h, n: (h, trail(n), 0)),
                   pl.BlockSpec((1, BLOCK, HEAD_DIM), lambda h, n: (h, trail(n), 0)),
                   pl.BlockSpec((SWA_GROUP, BLOCK, 2 * BLOCK), lambda h, n: (h, 0, 0)),
                   pl.BlockSpec((1, SWA_GROUP, LANES), lambda h, n: (h, 0, 0))],
        out_shape=[jax.ShapeDtypeStruct((s, SWA_HEADS * HEAD_DIM), BF16),
                   jax.ShapeDtypeStruct((s, SWA_HEADS * HEAD_DIM), BF16),
                   jax.ShapeDtypeStruct((SWA_KV_HEADS, s, HEAD_DIM), F32),
                   jax.ShapeDtypeStruct((SWA_KV_HEADS, s, HEAD_DIM), F32),
                   jax.ShapeDtypeStruct((SWA_HEADS, BLOCK, 2 * BLOCK), F32),
                   jax.ShapeDtypeStruct((SWA_KV_HEADS, SWA_GROUP, LANES), F32)],
        scratch_shapes=[pltpu.VMEM((BLOCK, HEAD_DIM), F32), pltpu.VMEM((BLOCK, HEAD_DIM), F32)],
        compiler_params=_cp(("arbitrary", "arbitrary")),
    )(*args, bias, sinks, da)


FOX_TILE = 512


def _below_and_on_diagonal(i, j, step):
    @pl.when(j < i)
    def _():
        step(False)

    @pl.when(j == i)
    def _():
        step(True)


FOX_AUG = 128
FOX_CQ, FOX_CK = HEAD_DIM, HEAD_DIM + 1


def _fox_operands(proj, csum):
    s = proj.shape[0]
    hw = FOX_HEADS * HEAD_DIM
    heads = lambda a: a.reshape(s, FOX_HEADS, HEAD_DIM).transpose(1, 0, 2)
    q = heads(proj[:, :hw]) * jnp.asarray(HEAD_DIM ** -0.5, BF16)
    k, v = heads(proj[:, hw:2 * hw]), heads(proj[:, 2 * hw:3 * hw])
    one = jnp.ones((FOX_HEADS, s, 1), BF16)
    zero = jnp.zeros((FOX_HEADS, s, 1), BF16)
    pad = jnp.zeros((FOX_HEADS, s, FOX_AUG - HEAD_DIM - 2), BF16)
    qa = jnp.concatenate([q, zero, -one, pad], axis=-1)
    ka = jnp.concatenate([k, one, zero, pad], axis=-1)
    tr = lambda a: a.transpose(0, 2, 1)
    c = csum[:, :FOX_HEADS].T
    return (qa, ka, tr(qa), tr(ka), v, tr(v), c[:, None, :], jnp.broadcast_to(c[:, :, None], (FOX_HEADS, s, LANES)))


def _fox_scores_t(ka, qa, cq, ck, diag):
    st = _dot(ka, qa, 1, 1) + cq - jnp.concatenate([ck] * (qa.shape[0] // LANES), axis=1)
    if diag:
        st = jnp.where(lax.broadcasted_iota(jnp.int32, st.shape, 0) <= lax.broadcasted_iota(jnp.int32, st.shape, 1), st, NEG)
    return st


def _fox_attention(qa, ka, vt, c_row, c_lanes):
    nh, s, aw = qa.shape
    t = min(FOX_TILE, s)
    nt = s // t
    hd = HEAD_DIM

    def body(qa_ref, ka_ref, vt_ref, cq_ref, ck_ref, o_ref, lse_ref, m_s, l_s, acc_s):
        i, j = pl.program_id(1), pl.program_id(2)

        @pl.when(j == 0)
        def _():
            m_s[...] = jnp.full_like(m_s, NEG)
            l_s[...] = jnp.zeros_like(l_s)
            acc_s[...] = jnp.zeros_like(acc_s)

        def step(diag):
            st = _fox_scores_t(ka_ref[0], qa_ref[0], cq_ref[0], ck_ref[0], diag)
            m_old = m_s[...]
            m_new = jnp.maximum(m_old, jnp.max(st, axis=0, keepdims=True))
            alpha = jnp.exp(m_old - m_new)
            p = jnp.exp(st - m_new)
            l_s[...] = alpha * l_s[...] + jnp.sum(p, axis=0, keepdims=True)
            acc_s[...] = alpha * acc_s[...] + _dot(vt_ref[0], p)
            m_s[...] = m_new

        _below_and_on_diagonal(i, j, step)

        @pl.when(j == nt - 1)
        def _():
            o_ref[0] = (acc_s[...] / l_s[...]).astype(o_ref.dtype)
            lse_ref[0] = m_s[...] + jnp.log(l_s[...])

    kj = lambda i, j: jnp.minimum(j, i)
    return pl.pallas_call(
        body, name="fox_fwd", grid=(nh, nt, nt),
        in_specs=[pl.BlockSpec((1, t, aw), lambda h, i, j: (h, i, 0)),
                  pl.BlockSpec((1, t, aw), lambda h, i, j: (h, kj(i, j), 0)),
                  pl.BlockSpec((1, hd, t), lambda h, i, j: (h, 0, kj(i, j))),
                  pl.BlockSpec((1, 1, t), lambda h, i, j: (h, 0, i)),
                  pl.BlockSpec((1, t, LANES), lambda h, i, j: (h, kj(i, j), 0))],
        out_specs=[pl.BlockSpec((1, hd, t), lambda h, i, j: (h, 0, i)),
                   pl.BlockSpec((1, 1, t), lambda h, i, j: (h, 0, i))],
        out_shape=[jax.ShapeDtypeStruct((nh, hd, s), BF16), jax.ShapeDtypeStruct((nh, 1, s), F32)],
        scratch_shapes=[pltpu.VMEM((1, t), F32), pltpu.VMEM((1, t), F32), pltpu.VMEM((hd, t), F32)],
        compiler_params=_cp(("parallel", "parallel", "arbitrary")),
    )(qa, ka, vt, c_row, c_lanes)


def _fox_attention_bwd(qa, ka, qat, kat, v, c_row, c_lanes, ot, dot, lse):
    nh, s, aw = qa.shape
    t = min(FOX_TILE, s)
    nt = s // t
    hd = HEAD_DIM

    def body(qa_ref, ka_ref, qat_ref, kat_ref, v_ref, cq_ref, ck_ref, ot_ref, dot_ref, lse_ref,
             dqa_ref, dka_ref, dv_ref, dk_s, dv_s):
        j, i = pl.program_id(1), pl.program_id(2)

        @pl.when((j == 0) & (i == 0))
        def _():
            dqa_ref[...] = jnp.zeros_like(dqa_ref)

        @pl.when(i == 0)
        def _():
            dk_s[...] = jnp.zeros_like(dk_s)
            dv_s[...] = jnp.zeros_like(dv_s)

        def step(diag):
            p = jnp.exp(_fox_scores_t(ka_ref[0], qa_ref[0], cq_ref[0], ck_ref[0], diag) - lse_ref[0])
            do_t = dot_ref[0]
            delta = jnp.sum(do_t.astype(F32) * ot_ref[0].astype(F32), axis=0, keepdims=True)
            ds = (p * (_dot(v_ref[0], do_t) - delta)).astype(BF16)
            dv_s[...] += _dot(do_t, p, 1, 1)
            dk_s[...] += _dot(qat_ref[0], ds, 1, 1)
            cols = pl.ds(pl.multiple_of(i * t, t), t)
            dqa_ref[0, :, cols] += _dot(kat_ref[0], ds)

        _below_and_on_diagonal(i, j, step)

        @pl.when(i == nt - 1)
        def _():
            dka_ref[0] = dk_s[...]
            dv_ref[0] = dv_s[...].astype(dv_ref.dtype)

    qi = lambda i, j: jnp.maximum(i, j)
    return pl.pallas_call(
        body, name="fox_bwd", grid=(nh, nt, nt),
        in_specs=[pl.BlockSpec((1, t, aw), lambda h, j, i: (h, qi(i, j), 0)),
                  pl.BlockSpec((1, t, aw), lambda h, j, i: (h, j, 0)),
                  pl.BlockSpec((1, aw, t), lambda h, j, i: (h, 0, qi(i, j))),
                  pl.BlockSpec((1, aw, t), lambda h, j, i: (h, 0, j)),
                  pl.BlockSpec((1, t, hd), lambda h, j, i: (h, j, 0)),
                  pl.BlockSpec((1, 1, t), lambda h, j, i: (h, 0, qi(i, j))),
                  pl.BlockSpec((1, t, LANES), lambda h, j, i: (h, j, 0)),
                  pl.BlockSpec((1, hd, t), lambda h, j, i: (h, 0, qi(i, j))),
                  pl.BlockSpec((1, hd, t), lambda h, j, i: (h, 0, qi(i, j))),
                  pl.BlockSpec((1, 1, t), lambda h, j, i: (h, 0, qi(i, j)))],
        out_specs=[pl.BlockSpec((1, aw, s), lambda h, j, i: (h, 0, 0)),
                   pl.BlockSpec((1, aw, t), lambda h, j, i: (h, 0, j)),
                   pl.BlockSpec((1, hd, t), lambda h, j, i: (h, 0, j))],
        out_shape=[jax.ShapeDtypeStruct((nh, aw, s), F32), jax.ShapeDtypeStruct((nh, aw, s), F32),
                   jax.ShapeDtypeStruct((nh, hd, s), BF16)],
        scratch_shapes=[pltpu.VMEM((aw, t), F32), pltpu.VMEM((hd, t), F32)],
        compiler_params=_cp(("parallel", "arbitrary", "arbitrary")),
    )(qa, ka, qat, kat, v, c_row, c_lanes, ot, dot, lse)


def _ssm_prep(lam_re, lam_im, log_dt, b_re, b_im, c_re, c_im):
    g, n = lam_re.shape
    dt = jnp.exp(log_dt)[:, None]
    mag = jnp.exp(lam_re * dt)
    ab_re = mag * jnp.cos(lam_im * dt)
    ab_im = mag * jnp.sin(lam_im * dt)
    den = lam_re * lam_re + lam_im * lam_im
    nr = ab_re - 1.0
    coef_re = ((nr * lam_re + ab_im * lam_im) / den)[..., None]
    coef_im = ((ab_im * lam_re - nr * lam_im) / den)[..., None]
    bb_re = coef_re * b_re - coef_im * b_im
    bb_im = coef_re * b_im + coef_im * b_re
    nblk = g // GROUPS_PER_STEP
    eye = jnp.eye(GROUPS_PER_STEP, dtype=F32)

    def bdiag(bb):
        return jnp.einsum('bgnc,gh->bgchn', bb.reshape(nblk, GROUPS_PER_STEP, n, SSM_GROUP), eye).reshape(
            nblk, GROUPS_PER_STEP * SSM_GROUP, STATE_W)

    def cdiag(cc):
        return jnp.einsum('bgcn,gh->bgnhc', cc.reshape(nblk, GROUPS_PER_STEP, SSM_GROUP, n), eye).reshape(
            nblk, STATE_W, GROUPS_PER_STEP * SSM_GROUP)

    return (ab_re.reshape(nblk, 1, STATE_W), ab_im.reshape(nblk, 1, STATE_W),
            bdiag(bb_re), bdiag(bb_im), cdiag(c_re), cdiag(c_im))


def _powers(a_re, a_im, levels):
    rs, ims = [a_re], [a_im]
    for _ in range(levels - 1):
        r, i = rs[-1], ims[-1]
        rs.append(r * r - i * i)
        ims.append(2.0 * r * i)
    return jnp.concatenate(rs, axis=1), jnp.concatenate(ims, axis=1)


def _ssm_fwd(proj, pw_re, pw_im, bd_re, bd_im, cd_re, cd_im, dskip):
    s = proj.shape[0]
    w = dskip.shape[1]
    nblk = w // LANES
    nch = STATE_W // LANES
    levels = pw_re.shape[1]

    def body(u_ref, pr, pi, br, bi, cr, ci, d_ref, pre_ref, yg_ref, xr_ref, xi_ref):
        ch = pl.program_id(1)
        u = u_ref[...]
        xr, xi = _scan_complex(_dot(u, br[0]), _dot(u, bi[0]), pr[0], pi[0], False)
        xr_ref[...] = xr.astype(BF16)
        xi_ref[...] = xi.astype(BF16)
        yc = _dot(xr, cr[0]) - _dot(xi, ci[0])

        @pl.when(ch == 0)
        def _():
            pre_ref[...] = yc + d_ref[...] * u

        @pl.when(ch > 0)
        def _():
            pre_ref[...] += yc

        @pl.when(ch == nch - 1)
        def _():
            yg_ref[...] = _gelu(pre_ref[...]).astype(BF16)

    blk = lambda b, c: (0, b)
    col = lambda b, c: (0, b * nch + c)
    in_chunk = pl.BlockSpec((1, LANES, LANES), lambda b, c: (b, 0, c))
    out_chunk = pl.BlockSpec((1, LANES, LANES), lambda b, c: (b, c, 0))
    pw_spec = pl.BlockSpec((1, levels, LANES), lambda b, c: (b, 0, c))
    return pl.pallas_call(
        body, name="ssm_fwd", grid=(nblk, nch),
        in_specs=[pl.BlockSpec((s, LANES), blk), pw_spec, pw_spec, in_chunk, in_chunk, out_chunk, out_chunk,
                  pl.BlockSpec((1, LANES), blk)],
        out_specs=[pl.BlockSpec((s, LANES), blk), pl.BlockSpec((s, LANES), blk),
                   pl.BlockSpec((s, LANES), col), pl.BlockSpec((s, LANES), col)],
        out_shape=[jax.ShapeDtypeStruct((s, w), F32), jax.ShapeDtypeStruct((s, w), BF16),
                   jax.ShapeDtypeStruct((s, nblk * STATE_W), BF16), jax.ShapeDtypeStruct((s, nblk * STATE_W), BF16)],
        compiler_params=_cp(("parallel", "arbitrary")),
    )(proj,pw_re, pw_im, bd_re, bd_im, cd_re, cd_im, dskip)


def _ssm_bwd(proj, dyg, pre, xr_all, xi_all, pw_re, pw_im, bd_re, bd_im, cd_re, cd_im, dskip):
    s = proj.shape[0]
    w = dskip.shape[1]
    nblk = w // LANES
    nch = STATE_W // LANES
    levels = pw_re.shape[1]

    def body(u_ref, dy_ref, pre_ref, xr_ref, xi_ref, pr, pi, br, bi, cr, ci, d_ref,
             du_ref, dd_ref, dar_ref, dai_ref, dbr_ref, dbi_ref, dcr_ref, dci_ref):
        ch = pl.program_id(1)
        u = u_ref[...]
        _, vjp = jax.vjp(_gelu, pre_ref[...])
        dpre = vjp(dy_ref[...].astype(F32))[0]
        zr, zi = _scan_complex(_dot(dpre, cr[0], 1, 1), -_dot(dpre, ci[0], 1, 1), pr[0], -pi[0], True)
        xpr = _shift_rows(xr_ref[...].astype(F32), 1)
        xpi = _shift_rows(xi_ref[...].astype(F32), 1)
        dar_ref[0] = jnp.sum(zr * xpr + zi * xpi, axis=0, keepdims=True)
        dai_ref[0] = jnp.sum(zi * xpr - zr * xpi, axis=0, keepdims=True)
        dcr_ref[0] = _dot(xr_ref[...], dpre, 0, 0)
        dci_ref[0] = -_dot(xi_ref[...], dpre, 0, 0)
        dbr_ref[0] = _dot(u, zr, 0, 0)
        dbi_ref[0] = _dot(u, zi, 0, 0)
        duc = _dot(zr, br[0], 1, 1) + _dot(zi, bi[0], 1, 1)

        @pl.when(ch == 0)
        def _():
            du_ref[...] = duc + dpre * d_ref[...]
            dd_ref[...] = jnp.sum(dpre * u, axis=0, keepdims=True)

        @pl.when(ch > 0)
        def _():
            du_ref[...] += duc

    blk = lambda b, c: (0, b)
    col = lambda b, c: (0, b * nch + c)
    in_chunk = pl.BlockSpec((1, LANES, LANES), lambda b, c: (b, 0, c))
    out_chunk = pl.BlockSpec((1, LANES, LANES), lambda b, c: (b, c, 0))
    pw_spec = pl.BlockSpec((1, levels, LANES), lambda b, c: (b, 0, c))
    a_spec = pl.BlockSpec((1, 1, LANES), lambda b, c: (b, 0, c))
    return pl.pallas_call(
        body, name="ssm_bwd", grid=(nblk, nch),
        in_specs=[pl.BlockSpec((s, LANES), blk), pl.BlockSpec((s, LANES), blk), pl.BlockSpec((s, LANES), blk),
                  pl.BlockSpec((s, LANES), col), pl.BlockSpec((s, LANES), col),
                  pw_spec, pw_spec, in_chunk, in_chunk, out_chunk, out_chunk, pl.BlockSpec((1, LANES), blk)],
        out_specs=[pl.BlockSpec((s, LANES), blk), pl.BlockSpec((1, LANES), blk), a_spec, a_spec,
                   in_chunk, in_chunk, out_chunk, out_chunk],
        out_shape=[jax.ShapeDtypeStruct((s, w), F32), jax.ShapeDtypeStruct((1, w), F32),
                   jax.ShapeDtypeStruct((nblk, 1, STATE_W), F32), jax.ShapeDtypeStruct((nblk, 1, STATE_W), F32),
                   jax.ShapeDtypeStruct((nblk, LANES, STATE_W), F32), jax.ShapeDtypeStruct((nblk, LANES, STATE_W), F32),
                   jax.ShapeDtypeStruct((nblk, STATE_W, LANES), F32), jax.ShapeDtypeStruct((nblk, STATE_W, LANES), F32)],
        compiler_params=_cp(("parallel", "arbitrary")),
    )(proj,dyg, pre, xr_all, xi_all, pw_re, pw_im, bd_re, bd_im, cd_re, cd_im, dskip)


ROW_TILE = 256


def _norm_fwd(x, g, name):
    s, d = x.shape
    t = min(ROW_TILE, s)
    return _tiled(lambda xv, gv: (_rms(xv, gv),), [_rows(x, t), _full(g)], [_orows(s, d, BF16, t)], s // t, name)[0]


def _norm_bwd(x, g, dh_list, dx_in, name):
    s, d = x.shape
    t = min(ROW_TILE, s)
    nh = len(dh_list)

    def fn(xv, gv, dxv, *dhs):
        dh = dhs[0].astype(F32)
        for other in dhs[1:]:
            dh = dh + other.astype(F32)
        _, vjp = jax.vjp(_rms, xv, gv)
        dx, dg = vjp(dh)
        dx = dx + dxv
        return dx, dx, dg

    return _tiled(fn, [_rows(x, t), _full(g), _rows(dx_in, t)] + [_rows(a, t) for a in dh_list],
                  [_orows(s, d, F32, t), _orows(s, d, BF16, t), _oacc((1, d))], s // t, name)


def _ple_fwd(x, y, pn, name):
    s, d = x.shape
    t = min(ROW_TILE, s)

    def fn(xv, yv, gv):
        x1 = xv + yv
        return x1, _rms(x1, gv)

    return _tiled(fn, [_rows(x, t), _rows(y, t), _full(pn)], [_orows(s, d, F32, t), _orows(s, d, BF16, t)], s // t, name)


def _ple_mix(x1, emb, gl, name):
    s, d = x1.shape
    t = min(ROW_TILE, s)
    return _tiled(lambda a, e, g: (a + e * jax.nn.sigmoid(g),), [_rows(x1, t), _rows(emb, t), _rows(gl, t)],
                  [_orows(s, d, F32, t)], s // t, name)[0]


def _ple_mix_bwd(dx2, emb, gl, name):
    s, d = dx2.shape
    t = min(ROW_TILE, s)

    def fn(dx, e, g):
        sg = jax.nn.sigmoid(g)
        return dx * sg, dx * e * sg * (1.0 - sg)

    return _tiled(fn, [_rows(dx2, t), _rows(emb, t), _rows(gl, t)],
                  [_orows(s, d, BF16, t), _orows(s, d, BF16, t)], s // t, name)


def _loss_grad(x, target, g):
    s, d = x.shape
    t = min(ROW_TILE, s)

    def fn(xv, tv, gv):
        def f(xx, gg):
            err = _rms(xx, gg) - tv
            return 0.5 * jnp.sum(jnp.mean(err * err, axis=-1, keepdims=True), axis=0, keepdims=True)

        loss, vjp = jax.vjp(f, xv, gv)
        dx, dg = vjp(jnp.ones((1, 1), F32))
        return loss, dx, dg

    return _tiled(fn, [_rows(x, t), _rows(target, t), _full(g)],
                  [_oacc((1, 1)), _orows(s, d, F32, t), _oacc((1, d))], s // t, "loss_grad")


def _conv_fwd(proj, kern):
    s = proj.shape[0]
    w = kern.shape[1]
    nb = w // LANES

    def fn(bg, cg, u, gate, k):
        z = cg.astype(F32) * u.astype(F32)
        conv = k[2:3] * z + k[1:2] * _shift_rows(z, 1) + k[0:1] * _shift_rows(z, 2)
        return (bg.astype(F32) * conv * _silu(gate.astype(F32)),)

    return _tiled(fn, [_cols(proj, LANES, q * nb) for q in range(4)] + [_cols(kern, LANES)],
                  [_ocols(s, w, BF16, LANES)], nb, "conv_fwd")[0]


def _conv_bwd(proj, kern, da):
    s = proj.shape[0]
    w = kern.shape[1]
    nb = w // LANES

    def fn(bg, cg, u, gate, k, dav):
        bg, cg, u, gate, dav = (a.astype(F32) for a in (bg, cg, u, gate, dav))
        z = cg * u
        z1, z2 = _shift_rows(z, 1), _shift_rows(z, 2)
        conv = k[2:3] * z + k[1:2] * z1 + k[0:1] * z2
        sg = jax.nn.sigmoid(gate)
        dy = dav * gate * sg
        dgate = dav * bg * conv * sg * (1.0 + gate * (1.0 - sg))
        dconv = dy * bg
        dk = jnp.concatenate([jnp.sum(dconv * zz, axis=0, keepdims=True) for zz in (z2, z1, z)], axis=0)
        dz = k[2:3] * dconv + k[1:2] * _shift_rows(dconv, 1, True) + k[0:1] * _shift_rows(dconv, 2, True)
        return dy * conv, dz * u, dz * cg, dgate, dk

    return _tiled(fn, [_cols(proj, LANES, q * nb) for q in range(4)] + [_cols(kern, LANES), _cols(da, LANES)],
                  [_ocols(s, w, BF16, LANES)] * 4 + [_ocols(3, w, F32, LANES)], nb, "conv_bwd")


def _glu_fwd(gl, proj, bglu):
    s, w2 = gl.shape
    w = w2 // 2
    t = min(ROW_TILE, s)

    def fn(glv, gate, b):
        v = glv + b
        return (v[:, :w] * jax.nn.sigmoid(v[:, w:]) * _silu(gate),)

    return _tiled(fn, [_rows(gl, t), (proj, (t, w), lambda i: (i, 1)), _full(bglu)],
                  [_orows(s, w, BF16, t)], s // t, "glu_fwd")[0]


def _glu_bwd(gl, proj, bglu, da):
    s, w2 = gl.shape
    w = w2 // 2
    t = min(ROW_TILE, s)

    def fn(glv, gate, b, dav):
        def f(gg, gt, bb):
            v = gg + bb
            return v[:, :w] * jax.nn.sigmoid(v[:, w:]) * _silu(gt)

        _, vjp = jax.vjp(f, glv, gate, b)
        return vjp(dav.astype(F32))

    return _tiled(fn, [_rows(gl, t), (proj, (t, w), lambda i: (i, 1)), _full(bglu), _rows(da, t)],
                  [_orows(s, w2, BF16, t), _orows(s, w, BF16, t), _oacc((1, w2))], s // t, "glu_bwd")


def _fg_fwd(z, b):
    def fn(zv, bv):
        v = zv + bv
        logf = jnp.minimum(v, 0.0) - jnp.log(1.0 + jnp.exp(-jnp.abs(v)))
        return (_cumsum_rows(logf, False),)

    return _tiled(fn, [_full(z), _full(b)], [_out(z.shape, F32, z.shape, lambda i: (0, 0))], 1, "fg_fwd")[0]


def _fg_bwd(z, b, dcs):
    def fn(zv, bv, dc):
        dz = _cumsum_rows(dc, True) * jax.nn.sigmoid(-(zv + bv))
        return dz, jnp.sum(dz, axis=0, keepdims=True)

    return _tiled(fn, [_full(z), _full(b), _full(dcs)],
                  [_out(z.shape, BF16, z.shape, lambda i: (0, 0)), _out((1, z.shape[1]), F32, (1, z.shape[1]), lambda i: (0, 0))],
                  1, "fg_bwd")


def _fox_gate_bwd(da, o, proj):
    s, w = o.shape
    t = min(ROW_TILE, s)

    def fn(dav, ov, gate):
        dav, ov, gate = dav.astype(F32), ov.astype(F32), gate.astype(F32)
        sg = jax.nn.sigmoid(gate)
        return dav * gate * sg, dav * ov * sg * (1.0 + gate * (1.0 - sg))

    return _tiled(fn, [_rows(da, t), _rows(o, t), (proj, (t, w), lambda i: (i, 3))],
                  [_orows(s, w, BF16, t), _orows(s, w, BF16, t)], s // t, "fox_gate_bwd")


def _fox_gate_fwd(o, proj):
    s, w = o.shape
    t = min(ROW_TILE, s)
    return _tiled(lambda ov, gate: (ov.astype(F32) * _silu(gate.astype(F32)),),
                  [_rows(o, t), (proj, (t, w), lambda i: (i, 3))], [_orows(s, w, BF16, t)], s // t, "fox_gate_fwd")[0]


def _local_step(x, p, target, w):
    s, d = x.shape
    depth = p.shape[0]
    pb = p.astype(BF16)
    grads = {}
    saved = []

    bucket = _swa_bucket_table()
    onehot = np.eye(REL_BUCKETS, dtype=np.float32)[bucket.reshape(-1)]
    bias = _bias_table(w['rel_bias'], jnp.asarray(onehot.T, BF16))
    sinks = jnp.broadcast_to(w['swa_sinks'].reshape(SWA_KV_HEADS, SWA_GROUP, 1), (SWA_KV_HEADS, SWA_GROUP, LANES))
    ssm_params = tuple(w[k][0] for k in ('ssm_lam_re', 'ssm_lam_im', 'ssm_log_dt', 'ssm_b_re', 'ssm_b_im', 'ssm_c_re', 'ssm_c_im'))
    (a_re, a_im, bd_re, bd_im, cd_re, cd_im), ssm_vjp = jax.vjp(_ssm_prep, *ssm_params)
    levels = max(1, (s - 1).bit_length())
    pw_re, pw_im = _powers(a_re, a_im, levels)
    wfg = jnp.pad(w['fox_w_fg'][0], ((0, 0), (0, LANES - FOX_HEADS)))
    bfg = jnp.pad(w['fox_b_fg'], ((0, 0), (0, LANES - FOX_HEADS)))

    def qkv4(a):
        return a.reshape(s, SWA_KV_HEADS, HEAD_DIM).transpose(1, 0, 2)

    def unheads(a):
        return a.transpose(2, 0, 1).reshape(s, FOX_HEADS * HEAD_DIM)

    for i in range(depth):
        mixer = i % 4
        hn = _norm_fwd(x, w['norm_g'][i:i + 1], f"norm_fwd{i}")
        sv = {'x': x, 'hn': hn}
        if mixer == 0:
            proj = _mm(hn, w['swa_w_in'][0], name="swa_in")
            qw = SWA_HEADS * HEAD_DIM
            kvw = SWA_KV_HEADS * HEAD_DIM
            k4, v4 = qkv4(proj[:, qw:qw + kvw]), qkv4(proj[:, qw + kvw:qw + 2 * kvw])
            a = _swa_fwd(proj, k4, v4, bias, sinks)
            sv.update(proj=proj, k4=k4, v4=v4)
            w_out = w['swa_w_out'][0]
        elif mixer == 1:
            proj = _mm(hn, w['conv_w_in'][0], name="conv_in")
            a = _conv_fwd(proj, w['conv_kernel'][0])
            sv.update(proj=proj)
            w_out = w['conv_w_out'][0]
        elif mixer == 2:
            proj = _mm(hn, w['ssm_w_in'][0], out_dtype=F32, name="ssm_in")
            pre, yg, xr_all, xi_all = _ssm_fwd(proj, pw_re, pw_im, bd_re, bd_im, cd_re, cd_im, w['ssm_d'])
            gl = _mm(yg, w['ssm_w_glu'][0], out_dtype=F32, name="ssm_glu")
            a = _glu_fwd(gl, proj, w['ssm_b_glu'])
            sv.update(proj=proj, pre=pre, yg=yg, xr=xr_all, xi=xi_all, gl=gl)
            w_out = w['ssm_w_out'][0]
        else:
            proj = _mm(hn, w['fox_w_in'][0], name="fox_in")
            z = _mm(hn, wfg, out_dtype=F32, name="fox_fg")
            fox_ops = _fox_operands(proj, _fg_fwd(z, bfg))
            ot, lse = _fox_attention(fox_ops[0], fox_ops[1], fox_ops[5], fox_ops[6], fox_ops[7])
            o = unheads(ot)
            a = _fox_gate_fwd(o, proj)
            sv.update(proj=proj, z=z, fox_ops=fox_ops, ot=ot, o=o, lse=lse)
            w_out = w['fox_w_out'][0]
        y = _mm(a, w_out, out_dtype=F32, name=f"mixer_out{i}")
        x1, gn = _ple_fwd(x, y, w['ple_norm'][i:i + 1], f"ple_fwd{i}")
        emb = _mm(pb[i], w['ple_proj'][i], out_dtype=F32, name=f"ple_emb{i}")
        gl2 = _mm(gn, w['ple_gate'][i], out_dtype=F32, name=f"ple_gate{i}")
        x = _ple_mix(x1, emb, gl2, f"ple_mix{i}")
        sv.update(a=a, x1=x1, gn=gn, emb=emb, gl2=gl2)
        saved.append(sv)

    loss, dx, grads['final_g'] = _loss_grad(x, target, w['final_g'].reshape(1, d))
    grads['final_g'] = grads['final_g'].reshape(d)

    g_norm, g_ple_norm, g_ple_proj, g_ple_gate = [None] * depth, [None] * depth, [None] * depth, [None] * depth
    for i in reversed(range(depth)):
        sv = saved[i]
        mixer = i % 4
        demb, dgl2 = _ple_mix_bwd(dx, sv['emb'], sv['gl2'], f"ple_mix_bwd{i}")
        g_ple_proj[i] = _mm(pb[i], demb, ta=True, name=f"ple_emb_dw{i}")
        g_ple_gate[i] = _mm(sv['gn'], dgl2, ta=True, name=f"ple_gate_dw{i}")
        dgn = _mm(dgl2, w['ple_gate'][i], tb=True, out_dtype=F32, name=f"ple_gate_dx{i}")
        dx1, dy, g_ple_norm[i] = _norm_bwd(sv['x1'], w['ple_norm'][i:i + 1], [dgn], dx, f"ple_norm_bwd{i}")
        w_out_name = ('swa_w_out', 'conv_w_out', 'ssm_w_out', 'fox_w_out')[mixer]
        grads[w_out_name] = _mm(sv['a'], dy, ta=True, name=f"mixer_out_dw{i}")[None]
        da = _mm(dy, w[w_out_name][0], tb=True, name=f"mixer_out_dx{i}")
        proj = sv['proj']
        dhs = []
        if mixer == 0:
            dq, dgate, dk4, dv4, dbias, dsink = _swa_bwd(proj, sv['k4'], sv['v4'], bias, sinks, da)
            back = lambda t4: t4.transpose(1, 0, 2).reshape(s, SWA_KV_HEADS * HEAD_DIM).astype(BF16)
            dproj = jnp.concatenate([dq, back(dk4), back(dv4), dgate], axis=1)
            grads['rel_bias'] = _bias_grad(dbias.reshape(SWA_HEADS, -1), jnp.asarray(onehot, BF16))
            grads['swa_sinks'] = dsink[:, :, 0].reshape(1, SWA_HEADS)
            w_in_name = 'swa_w_in'
        elif mixer == 1:
            dbg, dcg, du, dgate, dkern = _conv_bwd(proj, w['conv_kernel'][0], da)
            dproj = jnp.concatenate([dbg, dcg, du, dgate], axis=1)
            grads['conv_kernel'] = dkern[None]
            w_in_name = 'conv_w_in'
        elif mixer == 2:
            dgl, dgate, dbglu = _glu_bwd(sv['gl'], proj, w['ssm_b_glu'], da)
            grads['ssm_b_glu'] = dbglu
            grads['ssm_w_glu'] = _mm(sv['yg'], dgl, ta=True, name="ssm_glu_dw")[None]
            dyg = _mm(dgl, w['ssm_w_glu'][0], tb=True, out_dtype=F32, name="ssm_glu_dx")
            du, dd, da_re, da_im, dbd_re, dbd_im, dcd_re, dcd_im = _ssm_bwd(
                proj, dyg, sv['pre'], sv['xr'], sv['xi'], pw_re, pw_im, bd_re, bd_im, cd_re, cd_im, w['ssm_d'])
            grads['ssm_d'] = dd
            dparams = ssm_vjp((da_re, da_im, dbd_re, dbd_im, dcd_re, dcd_im))
            for k, v in zip(('ssm_lam_re', 'ssm_lam_im', 'ssm_log_dt', 'ssm_b_re', 'ssm_b_im', 'ssm_c_re', 'ssm_c_im'), dparams):
                grads[k] = v[None]
            dproj = jnp.concatenate([du.astype(BF16), dgate], axis=1)
            w_in_name = 'ssm_w_in'
        else:
            do, dgate = _fox_gate_bwd(da, sv['o'], proj)
            qa, ka, qat, kat, v, _, c_row, c_lanes = sv['fox_ops']
            dot = do.reshape(s, FOX_HEADS, HEAD_DIM).transpose(1, 2, 0)
            dqa, dka, dvt = _fox_attention_bwd(qa, ka, qat, kat, v, c_row, c_lanes, sv['ot'], dot, sv['lse'])
            dq = (unheads(dqa[:, :HEAD_DIM]) * (HEAD_DIM ** -0.5)).astype(BF16)
            dk, dv = unheads(dka[:, :HEAD_DIM]).astype(BF16), unheads(dvt)
            dcs = jnp.pad((dqa[:, FOX_CQ] + dka[:, FOX_CK]).T, ((0, 0), (0, LANES - FOX_HEADS)))
            dz, dbfg = _fg_bwd(sv['z'], bfg, dcs)
            grads['fox_b_fg'] = dbfg[:, :FOX_HEADS]
            grads['fox_w_fg'] = _mm(sv['hn'], dz, ta=True, out_dtype=BF16, name="fox_fg_dw")[:, :FOX_HEADS][None]
            dhs.append(_mm(dz, wfg, tb=True, out_dtype=F32, name="fox_fg_dx"))
            dproj = jnp.concatenate([dq, dk, dv, dgate], axis=1)
            w_in_name = 'fox_w_in'
        grads[w_in_name] = _mm(sv['hn'], dproj, ta=True, name=f"mixer_in_dw{i}")[None]
        dhs.append(_mm(dproj, w[w_in_name][0], tb=True, out_dtype=F32, name=f"mixer_in_dx{i}"))
        dx, _, g_norm[i] = _norm_bwd(sv['x'], w['norm_g'][i:i + 1], dhs, dx1, f"norm_bwd{i}")

    grads['norm_g'] = jnp.concatenate(g_norm, axis=0)
    grads['ple_norm'] = jnp.concatenate(g_ple_norm, axis=0)
    grads['ple_proj'] = jnp.stack(g_ple_proj)
    grads['ple_gate'] = jnp.stack(g_ple_gate)
    return loss, dx, grads


def _bf16_terms(a):
    hi = lax.reduce_precision(a, 8, 7)
    r1 = a - hi
    mid = lax.reduce_precision(r1, 8, 7)
    lo = lax.reduce_precision(r1 - mid, 8, 7)
    return hi.astype(BF16), mid.astype(BF16), lo.astype(BF16)


def _split3(a):
    return jnp.concatenate(_bf16_terms(a), axis=0)


def _bias_grad(dbias, onehot):
    out = _mm(_split3(dbias), onehot, out_dtype=F32, name="rel_bias_grad", tk=2048)
    nh = dbias.shape[0]
    return (out[:nh] + out[nh:2 * nh] + out[2 * nh:]).T


def _bias_table(rel_bias, onehot_t):
    nh = rel_bias.shape[1]
    out = _mm(_split3(rel_bias.T), onehot_t, out_dtype=F32, name="rel_bias_table")
    return (out[:nh] + out[nh:2 * nh] + out[2 * nh:]).reshape(nh, BLOCK, 2 * BLOCK)


WEIGHTS = ['norm_g', 'final_g', 'rel_bias', 'swa_w_in', 'swa_w_out', 'swa_sinks', 'conv_w_in', 'conv_kernel', 'conv_w_out',
           'ssm_w_in', 'ssm_lam_re', 'ssm_lam_im', 'ssm_log_dt', 'ssm_b_re', 'ssm_b_im', 'ssm_c_re', 'ssm_c_im', 'ssm_d',
           'ssm_w_glu', 'ssm_b_glu', 'ssm_w_out', 'fox_w_in', 'fox_w_fg', 'fox_b_fg', 'fox_w_out', 'ple_proj', 'ple_norm',
           'ple_gate']
BIG = {'swa_w_in': 2, 'swa_w_out': 1, 'conv_w_in': 2, 'conv_w_out': 1, 'ssm_w_in': 2, 'ssm_w_glu': 2, 'ssm_w_out': 1,
       'fox_w_in': 2, 'fox_w_fg': 1, 'fox_w_out': 1, 'ple_proj': 2, 'ple_gate': 1}
SMALL = {'conv_kernel': 2, 'ssm_d': 1, 'ssm_b_glu': 1}
REPLICATED = [n for n in WEIGHTS if n not in BIG and n not in SMALL]
N_CHIPS = 4
N_DEV = 8
BIG_ROWS = 256
SMALL_ROWS = 8
REPL_ROWS = 64


def _flat(pieces, dtype, lead, row_mult):
    flat = jnp.concatenate([q.astype(dtype) for q in pieces], axis=-1)
    pad = (-flat.shape[-1]) % (row_mult * FLAT_W)
    flat = jnp.pad(flat, [(0, 0)] * len(lead) + [(0, pad)])
    return flat.reshape(*lead, -1, FLAT_W)


def _unflat(flat, lead_ndim, sizes):
    lead = flat.shape[:lead_ndim]
    flat = flat.reshape(*lead, -1)
    out, off = [], 0
    for n in sizes:
        out.append(flat[..., off:off + n])
        off += n
    return out


def _split_shards(full, axis):
    shp = full.shape
    parts = full.reshape(shp[:axis] + (N_CHIPS, shp[axis] // N_CHIPS) + shp[axis + 1:])
    return jnp.moveaxis(parts, axis, 0)


def _join_shards(parts, axis):
    moved = jnp.moveaxis(parts, 0, axis)
    shp = moved.shape
    return moved.reshape(shp[:axis] + (shp[axis] * shp[axis + 1],) + shp[axis + 2:])


def _coords():
    return lax.axis_index("x"), lax.axis_index("y"), lax.axis_index("c")


def _remote(k, src, dst, to, send_sems, recv_sems):
    return pltpu.make_async_remote_copy(src_ref=src, dst_ref=dst, send_sem=send_sems.at[k], recv_sem=recv_sems.at[k],
                                        device_id=to, device_id_type=MESH)


def _gather_weights(wsh, ssh):
    def body(w_ref, s_ref, wout, sout, send_sems, recv_sems):
        x, y, c = _coords()
        me = 2 * x + y
        chips = [(1 - x, y), (x, 1 - y), (1 - x, 1 - y)]
        rc = functools.partial(_remote, send_sems=send_sems, recv_sems=recv_sems)
        sends = []
        for j, (cx, cy) in enumerate(chips):
            sends.append(rc(j, w_ref.at[c], wout.at[me, c], (cx, cy, c)))
            sends.append(rc(3 + j, s_ref, sout.at[me], (cx, cy, c)))
        for cp in sends:
            cp.start()
        for j, (cx, cy) in enumerate(chips):
            k = 2 * cx + cy
            rc(j, w_ref.at[c], wout.at[k, c], (x, y, c)).wait_recv()
            fwd = rc(6 + j, wout.at[k, c], wout.at[k, c], (x, y, 1 - c))
            fwd.start()
            sends.append(fwd)
        for j, (cx, cy) in enumerate(chips):
            k = 2 * cx + cy
            rc(6 + j, w_ref.at[c], wout.at[k, 1 - c], (x, y, c)).wait_recv()
            rc(3 + j, s_ref, sout.at[k], (x, y, c)).wait_recv()
        for cp in sends:
            cp.wait_send()

    return pl.pallas_call(
        body, name="gather_weights", in_specs=[ANY, ANY], out_specs=[ANY, ANY],
        out_shape=[jax.ShapeDtypeStruct((N_CHIPS,) + wsh.shape, wsh.dtype),
                   jax.ShapeDtypeStruct((N_CHIPS,) + ssh.shape, ssh.dtype)],
        scratch_shapes=[pltpu.SemaphoreType.DMA((9,)), pltpu.SemaphoreType.DMA((9,))],
    )(wsh, ssh)


def _pair_exchange(gbig):
    def body(g_ref, out, send_sems, recv_sems):
        x, y, c = _coords()
        rc = functools.partial(_remote, send_sems=send_sems, recv_sems=recv_sems)
        sends = [rc(j, g_ref.at[2 * j + 1 - c], out.at[j], (x, y, 1 - c)) for j in range(N_CHIPS)]
        for cp in sends:
            cp.start()
        for j in range(N_CHIPS):
            rc(j, g_ref.at[2 * j + c], out.at[j], (x, y, c)).wait_recv()
        for cp in sends:
            cp.wait_send()

    return pl.pallas_call(
        body, name="pair_exchange", in_specs=[ANY], out_specs=ANY,
        out_shape=jax.ShapeDtypeStruct((N_CHIPS,) + gbig.shape[1:], gbig.dtype),
        scratch_shapes=[pltpu.SemaphoreType.DMA((N_CHIPS,)), pltpu.SemaphoreType.DMA((N_CHIPS,))],
    )(gbig)


def _pair_sum(mine, theirs):
    n, r, w = mine.shape
    t = _tile(r, 256, 16)
    spec = lambda a: (a, (n, t, w), lambda i: (0, i, 0))
    return _tiled(lambda a, b: (a.astype(F32) + b.astype(F32),), [spec(mine), spec(theirs)],
                  [_out((n, r, w), BF16, (n, t, w), lambda i: (0, i, 0))], r // t, "pair_sum")[0]


def _exchange_grads(pbig, gsmall, grepl):
    def body(g_ref, s_ref, r_ref, og, osm, orp, send_sems, recv_sems, local_sems):
        x, y, c = _coords()
        me = 4 * x + 2 * y + c
        my_chip = 2 * x + y
        rc = functools.partial(_remote, send_sems=send_sems, recv_sems=recv_sems)
        local = [pltpu.make_async_copy(s_ref.at[me], osm.at[me], local_sems.at[0]),
                 pltpu.make_async_copy(r_ref, orp.at[me], local_sems.at[1])]
        for cp in local:
            cp.start()
        peers = []
        for d in range(1, N_DEV):
            px = 1 - x if d & 4 else x
            py = 1 - y if d & 2 else y
            pc = 1 - c if d & 1 else c
            peers.append((d, 4 * px + 2 * py + pc, 2 * px + py, (px, py, pc)))
        sends = []
        for i, (d, peer, chip, to) in enumerate(peers):
            sends.append(rc(3 * i + 1, s_ref.at[peer], osm.at[me], to))
            sends.append(rc(3 * i + 2, r_ref, orp.at[me], to))
            if d & 1 == 0:
                sends.append(rc(3 * i, g_ref.at[chip], og.at[my_chip], to))
        for cp in sends:
            cp.start()
        for i, (d, peer, chip, to) in enumerate(peers):
            rc(3 * i + 1, s_ref.at[peer], osm.at[peer], to).wait_recv()
            rc(3 * i + 2, r_ref, orp.at[peer], to).wait_recv()
            if d & 1 == 0:
                rc(3 * i, g_ref.at[chip], og.at[chip], to).wait_recv()
        for cp in sends:
            cp.wait_send()
        for cp in local:
            cp.wait()

    nsem = 3 * (N_DEV - 1)
    return pl.pallas_call(
        body, name="exchange_grads", in_specs=[ANY, ANY, ANY], out_specs=[ANY, ANY, ANY],
        out_shape=[jax.ShapeDtypeStruct(pbig.shape, pbig.dtype), jax.ShapeDtypeStruct(gsmall.shape, gsmall.dtype),
                   jax.ShapeDtypeStruct((N_DEV,) + grepl.shape, grepl.dtype)],
        scratch_shapes=[pltpu.SemaphoreType.DMA((nsem,)), pltpu.SemaphoreType.DMA((nsem,)), pltpu.SemaphoreType.DMA((2,))],
    )(pbig, gsmall, grepl)


SIBLING_CHUNKS = 4


def _sibling_exchange(gh, sh):
    rows = gh.shape[0] // SIBLING_CHUNKS

    def body(g_ref, s_ref, og, osm, send_sems, recv_sems):
        x, y, c = _coords()
        rc = functools.partial(_remote, send_sems=send_sems, recv_sems=recv_sems)
        chunk = lambda ref, k: ref.at[pl.ds(k * rows, rows)]
        sends = [rc(k, chunk(g_ref, k), chunk(og, k), (x, y, 1 - c)) for k in range(SIBLING_CHUNKS)]
        sends.append(rc(SIBLING_CHUNKS, s_ref, osm, (x, y, 1 - c)))
        for cp in sends:
            cp.start()
        for k in range(SIBLING_CHUNKS):
            rc(k, chunk(g_ref, k), chunk(og, k), (x, y, c)).wait_recv()
        rc(SIBLING_CHUNKS, s_ref, osm, (x, y, c)).wait_recv()
        for cp in sends:
            cp.wait_send()

    nsem = SIBLING_CHUNKS + 1
    return pl.pallas_call(
        body, name="sibling_exchange", in_specs=[ANY, ANY], out_specs=[ANY, ANY],
        out_shape=[jax.ShapeDtypeStruct(gh.shape, gh.dtype), jax.ShapeDtypeStruct(sh.shape, sh.dtype)],
        scratch_shapes=[pltpu.SemaphoreType.DMA((nsem,)), pltpu.SemaphoreType.DMA((nsem,))],
    )(gh, sh)


def _sum_senders(recv, name):
    n, r, w = recv.shape
    t = _tile(r, 256, 8)

    def fn(v):
        acc = v[0].astype(F32)
        for i in range(1, n):
            acc = acc + v[i].astype(F32)
        return (acc,)

    return _tiled(fn, [(recv, (n, t, w), lambda i: (0, i, 0))], [_orows(r, w, F32, t)], r // t, name)[0]


def _adamw(w, g, m, v, name):
    r, wd = w.shape
    t = _tile(r, 256, 8)

    def fn(wv, gv, mv, vv):
        m2 = ADAM_B1 * mv + (1.0 - ADAM_B1) * gv
        v2 = ADAM_B2 * vv + (1.0 - ADAM_B2) * (gv * gv)
        m_hat = m2 / (1.0 - ADAM_B1 ** ADAM_STEP)
        v_hat = v2 / (1.0 - ADAM_B2 ** ADAM_STEP)
        delta = -ADAM_LR * (m_hat / (jnp.sqrt(v_hat) + ADAM_EPS) + ADAM_WD * wv)
        return delta, m2, v2

    return _tiled(fn, [_rows(a, t) for a in (w, g, m, v)], [_orows(r, wd, F32, t)] * 3, r // t, name)


def kernel(x, p, norm_g, final_g, rel_bias, swa_w_in, swa_w_out, swa_sinks, conv_w_in, conv_kernel, conv_w_out, ssm_w_in, ssm_lam_re, ssm_lam_im, ssm_log_dt, ssm_b_re, ssm_b_im, ssm_c_re, ssm_c_im, ssm_d, ssm_w_glu, ssm_b_glu, ssm_w_out, fox_w_in, fox_w_fg, fox_b_fg, fox_w_out, ple_proj, ple_norm, ple_gate, loss_target, m_norm_g, m_final_g, m_rel_bias, m_swa_w_in, m_swa_w_out, m_swa_sinks, m_conv_w_in, m_conv_kernel, m_conv_w_out, m_ssm_w_in, m_ssm_lam_re, m_ssm_lam_im, m_ssm_log_dt, m_ssm_b_re, m_ssm_b_im, m_ssm_c_re, m_ssm_c_im, m_ssm_d, m_ssm_w_glu, m_ssm_b_glu, m_ssm_w_out, m_fox_w_in, m_fox_w_fg, m_fox_b_fg, m_fox_w_out, m_ple_proj, m_ple_norm, m_ple_gate, v_norm_g, v_final_g, v_rel_bias, v_swa_w_in, v_swa_w_out, v_swa_sinks, v_conv_w_in, v_conv_kernel, v_conv_w_out, v_ssm_w_in, v_ssm_lam_re, v_ssm_lam_im, v_ssm_log_dt, v_ssm_b_re, v_ssm_b_im, v_ssm_c_re, v_ssm_c_im, v_ssm_d, v_ssm_w_glu, v_ssm_b_glu, v_ssm_w_out, v_fox_w_in, v_fox_w_fg, v_fox_b_fg, v_fox_w_out, v_ple_proj, v_ple_norm, v_ple_gate):
    given = dict(locals())
    shard_shape = {n: given[n].shape for n in WEIGHTS}
    half = {n: math.prod(shard_shape[n]) // 2 for n in WEIGHTS}

    wsh = _flat([given[n].reshape(-1) for n in BIG], BF16, (), 2 * BIG_ROWS)
    ssh = _flat([given[n].reshape(-1) for n in SMALL], F32, (), SMALL_ROWS)
    core = lax.axis_index("c")
    my_chip = 2 * lax.axis_index("x") + lax.axis_index("y")
    wsh = wsh.reshape(2, -1, FLAT_W)
    wall, sall = _gather_weights(wsh, ssh)
    wall = lax.dynamic_update_slice(wall, wsh[None], (my_chip, 0, 0, 0))
    sall = lax.dynamic_update_slice(sall, ssh[None], (my_chip, 0, 0))
    full = {n: given[n] for n in REPLICATED}
    for group, gathered in ((BIG, wall.reshape(N_CHIPS, -1, FLAT_W)), (SMALL, sall)):
        pieces = _unflat(gathered, 1, [2 * half[n] for n in group])
        for n, piece in zip(group, pieces):
            full[n] = _join_shards(piece.reshape((N_CHIPS,) + shard_shape[n]), group[n])

    loss, dx, grads = _local_step(x[0], p[:, 0], loss_target[0], full)

    gbig = _flat([_split_shards(grads[n], BIG[n]).reshape(N_DEV, -1) for n in BIG], BF16, (N_DEV,), BIG_ROWS)
    gsmall = _flat([_split_shards(grads[n], SMALL[n]).reshape(N_DEV, -1) for n in SMALL], F32, (N_DEV,), SMALL_ROWS)
    grepl = _flat([grads[n].reshape(-1) for n in REPLICATED], F32, (), REPL_ROWS)
    pbig = _pair_sum(jnp.where(core == 0, gbig[0::2], gbig[1::2]), _pair_exchange(gbig))
    recv_big, recv_small, recv_repl = _exchange_grads(pbig, gsmall, grepl)
    recv_big = lax.dynamic_update_slice(recv_big, lax.dynamic_index_in_dim(pbig, my_chip, axis=0), (my_chip, 0, 0))
    gh, sh = _sum_senders(recv_big, "sum_big"), _sum_senders(recv_small, "sum_small")
    og, osm = _sibling_exchange(gh, sh)
    both = lambda a, b: jnp.stack([jnp.where(core == 0, a, b), jnp.where(core == 0, b, a)])
    gfull, sfull = both(gh, og), both(sh, osm)
    g_repl = _sum_senders(recv_repl, "sum_repl")

    out_g, out_d, out_m, out_v = {}, {}, {}, {}
    for n, piece in zip(BIG, _unflat(gfull, 1, [half[n] for n in BIG])):
        as_rows = lambda t, n=n: t.reshape(-1, shard_shape[n][-1])
        d, m2, v2 = _adamw(as_rows(given[n]), as_rows(piece), as_rows(given['m_' + n]), as_rows(given['v_' + n]), f"adamw_{n}")
        out_g[n], out_d[n], out_m[n], out_v[n] = (t.reshape(shard_shape[n]) for t in (piece, d, m2, v2))

    pack_small = lambda pre: _flat([given[pre + n].reshape(2, -1) for n in SMALL], F32, (2,), SMALL_ROWS).reshape(-1, FLAT_W)
    res = _adamw(pack_small(''), sfull.reshape(-1, FLAT_W), pack_small('m_'), pack_small('v_'), "adamw_small")
    for dst, flat in zip((out_g, out_d, out_m, out_v), (sfull,) + tuple(res)):
        for n, piece in zip(SMALL, _unflat(flat.reshape(2, -1, FLAT_W), 1, [half[n] for n in SMALL])):
            dst[n] = piece.reshape(shard_shape[n])

    pack_repl = lambda pre: _flat([given[pre + n].reshape(-1) for n in REPLICATED], F32, (), REPL_ROWS)
    res = _adamw(pack_repl(''), g_repl, pack_repl('m_'), pack_repl('v_'), "adamw_repl")
    for dst, flat in zip((out_g, out_d, out_m, out_v), (g_repl,) + tuple(res)):
        for n, piece in zip(REPLICATED, _unflat(flat, 0, [2 * half[n] for n in REPLICATED])):
            dst[n] = piece.reshape(shard_shape[n])

    total = lax.psum(loss[0, 0], ("x", "y", "c"))
    return (total, dx[None], *[out_g[n] for n in WEIGHTS], *[out_d[n] for n in WEIGHTS],
            *[out_m[n] for n in WEIGHTS], *[out_v[n] for n in WEIGHTS])
```

```python
import functools
import math

import numpy as np
import jax
import jax.numpy as jnp
from jax import lax
from jax.experimental import pallas as pl
from jax.experimental.pallas import tpu as pltpu

F32 = jnp.float32
BF16 = jnp.bfloat16

EPS = 1e-6
BLOCK = 128
REL_BUCKETS = 32
REL_MAX_DIST = 128
SWA_HEADS, SWA_KV_HEADS, HEAD_DIM = 32, 4, 64
SWA_GROUP = SWA_HEADS // SWA_KV_HEADS
FOX_HEADS = 32
SSM_GROUP, SSM_STATE = 16, 64
GROUPS_PER_STEP = 8
STATE_W = GROUPS_PER_STEP * SSM_STATE
LANES = 128
NEG = -1e30

ADAM_LR, ADAM_B1, ADAM_B2, ADAM_EPS, ADAM_WD, ADAM_STEP = 0.001, 0.9, 0.999, 1e-08, 0.01, 10

VMEM_LIMIT_V7X = 56 * 1024 * 1024
FLAT_W = 1024
MESH = pl.DeviceIdType.MESH
ANY = pl.BlockSpec(memory_space=pl.ANY)


def _cp(sem):
    return pltpu.CompilerParams(dimension_semantics=sem, vmem_limit_bytes=VMEM_LIMIT_V7X)


def _tile(n, target, mult=LANES):
    if n <= target:
        return n
    t = (target // mult) * mult
    while t >= mult:
        if n % t == 0:
            return t
        t -= mult
    return n


def _mm(a, b, *, ta=False, tb=False, out_dtype=BF16, name, tm=1024, tn=1024, tk=1024):
    if ta:
        kdim, m = a.shape
    else:
        m, kdim = a.shape
    n = b.shape[0] if tb else b.shape[1]
    assert (b.shape[1] if tb else b.shape[0]) == kdim
    tm, tn, tk = _tile(m, tm), _tile(n, tn), _tile(kdim, tk)
    nk = kdim // tk
    dn = (((0 if ta else 1,), (1 if tb else 0,)), ((), ()))

    def body(a_ref, b_ref, o_ref, acc_ref):
        k = pl.program_id(2)

        @pl.when(k == 0)
        def _():
            acc_ref[...] = jnp.zeros_like(acc_ref)

        acc_ref[...] += lax.dot_general(a_ref[...].astype(BF16), b_ref[...].astype(BF16), dn,
                                        preferred_element_type=F32)

        @pl.when(k == nk - 1)
        def _():
            o_ref[...] = acc_ref[...].astype(o_ref.dtype)

    a_spec = pl.BlockSpec((tk, tm), lambda i, j, k: (k, i)) if ta else pl.BlockSpec((tm, tk), lambda i, j, k: (i, k))
    b_spec = pl.BlockSpec((tn, tk), lambda i, j, k: (j, k)) if tb else pl.BlockSpec((tk, tn), lambda i, j, k: (k, j))
    return pl.pallas_call(
        body, name=name, grid=(m // tm, n // tn, nk),
        in_specs=[a_spec, b_spec], out_specs=pl.BlockSpec((tm, tn), lambda i, j, k: (i, j)),
        out_shape=jax.ShapeDtypeStruct((m, n), out_dtype),
        scratch_shapes=[pltpu.VMEM((tm, tn), F32)],
        compiler_params=_cp(("parallel", "parallel", "arbitrary")),
    )(a, b)


def _rows(arr, t):
    return (arr, (t, arr.shape[1]), lambda i: (i, 0))


def _cols(arr, cb, off=0):
    return (arr, (arr.shape[0], cb), lambda i: (0, i + off))


def _full(arr):
    nd = arr.ndim
    return (arr, arr.shape, lambda i: (0,) * nd)


def _lead(arr):
    return (arr, (1,) + arr.shape[1:], lambda i: (i, 0, 0))


def _out(shape, dtype, block, imap, acc=False):
    return (jax.ShapeDtypeStruct(shape, dtype), block, imap, acc)


def _orows(s, w, dtype, t):
    return _out((s, w), dtype, (t, w), lambda i: (i, 0))


def _ocols(s, w, dtype, cb):
    return _out((s, w), dtype, (s, cb), lambda i: (0, i))


def _oacc(shape):
    nd = len(shape)
    return _out(shape, F32, shape, lambda i: (0,) * nd, True)


def _olead(n, a, b, dtype=F32):
    return _out((n, a, b), dtype, (1, a, b), lambda i: (i, 0, 0))


def _tiled(fn, ins, outs, n, name):
    has_acc = any(o[3] for o in outs)
    ni = len(ins)

    def body(*refs):
        vals = fn(*[r[...] for r in refs[:ni]])
        i = pl.program_id(0)
        for r, v, o in zip(refs[ni:], vals, outs):
            if o[3]:
                @pl.when(i == 0)
                def _(r=r, v=v):
                    r[...] = v.astype(r.dtype)

                @pl.when(i > 0)
                def _(r=r, v=v):
                    r[...] += v.astype(r.dtype)
            else:
                r[...] = v.astype(r.dtype)

    res = pl.pallas_call(
        body, name=name, grid=(n,),
        in_specs=[pl.BlockSpec(b, m) for _, b, m in ins],
        out_specs=[pl.BlockSpec(b, m) for _, b, m, _ in outs],
        out_shape=[s for s, _, _, _ in outs],
        compiler_params=_cp(("arbitrary",) if has_acc else ("parallel",)),
    )(*[a for a, _, _ in ins])
    return res


def _silu(x):
    return x * jax.nn.sigmoid(x)


def _gelu(x):
    return 0.5 * x * (1.0 + jnp.tanh(math.sqrt(2.0 / math.pi) * (x + 0.044715 * (x * x * x))))


def _rms(x, g):
    r = lax.rsqrt(jnp.mean(x * x, axis=-1, keepdims=True) + EPS)
    return x * r * g


def _shift_rows(x, sh, up=False):
    s = x.shape[0]
    rows = lax.broadcasted_iota(jnp.int32, x.shape, 0)
    if up:
        return jnp.where(rows < s - sh, pltpu.roll(x, s - sh, 0), 0.0)
    return jnp.where(rows >= sh, pltpu.roll(x, sh, 0), 0.0)


def _scan_complex(xr, xi, pr, pi, reverse):
    s = xr.shape[0]
    k = 0
    while (1 << k) < s:
        sh = 1 << k
        sr, si = _shift_rows(xr, sh, reverse), _shift_rows(xi, sh, reverse)
        ar, ai = pr[k:k + 1, :], pi[k:k + 1, :]
        xr, xi = xr + ar * sr - ai * si, xi + ar * si + ai * sr
        k += 1
    return xr, xi


def _cumsum_rows(x, reverse):
    s = x.shape[0]
    k = 0
    while (1 << k) < s:
        x = x + _shift_rows(x, 1 << k, reverse)
        k += 1
    return x


def _dot(a, b, ca=1, cb=0):
    return lax.dot_general(a.astype(BF16), b.astype(BF16), (((ca,), (cb,)), ((), ())), preferred_element_type=F32)


def _t5_bucket(dist):
    max_exact = REL_BUCKETS // 2
    d = np.maximum(dist, 1).astype(np.float32)
    large = max_exact + (np.log(d / max_exact) / np.log(REL_MAX_DIST / max_exact) * (REL_BUCKETS - max_exact)).astype(np.int32)
    large = np.minimum(large, REL_BUCKETS - 1)
    return np.where(dist < max_exact, dist, large).astype(np.int32)


def _swa_bucket_table():
    qi = np.arange(BLOCK)[:, None]
    kj = np.arange(2 * BLOCK)[None, :]
    return _t5_bucket(np.clip(qi + BLOCK - kj, 0, None))


def _swa_scores(qg, kb, bias_g, sk, n):
    s = _dot(qg, kb, 1, 1) * (HEAD_DIM ** -0.5) + bias_g
    row = lax.broadcasted_iota(jnp.int32, s.shape, 0)
    col = lax.broadcasted_iota(jnp.int32, s.shape, 1)
    dist = row + BLOCK - col
    mask = (dist >= 0) & (dist < BLOCK) & ((col >= BLOCK) | (n > 0))
    s = jnp.where(mask, s, NEG)
    m = jnp.maximum(jnp.max(s, axis=1, keepdims=True), sk)
    e = jnp.exp(s - m)
    es = jnp.exp(sk - m)
    den = jnp.sum(e, axis=1, keepdims=True) + es
    return e / den, es / den


def _swa_specs(proj, k4, v4, nb, clamp):
    gw = SWA_GROUP * HEAD_DIM
    gate_off = (SWA_HEADS * HEAD_DIM + 2 * SWA_KV_HEADS * HEAD_DIM) // gw
    cur = (lambda n: jnp.minimum(n, nb - 1)) if clamp else (lambda n: n)
    prev = lambda n: jnp.maximum(cur(n) - 1, 0)
    return [
        pl.BlockSpec((BLOCK, gw), lambda h, n: (cur(n), h)),
        pl.BlockSpec((1, BLOCK, HEAD_DIM), lambda h, n: (h, cur(n), 0)),
        pl.BlockSpec((1, BLOCK, HEAD_DIM), lambda h, n: (h, prev(n), 0)),
        pl.BlockSpec((1, BLOCK, HEAD_DIM), lambda h, n: (h, cur(n), 0)),
        pl.BlockSpec((1, BLOCK, HEAD_DIM), lambda h, n: (h, prev(n), 0)),
        pl.BlockSpec((BLOCK, gw), lambda h, n: (cur(n), gate_off + h)),
    ], [proj, k4, k4, v4, v4, proj]


def _swa_fwd(proj, k4, v4, bias, sinks):
    s = proj.shape[0]
    nb = s // BLOCK
    gw = SWA_GROUP * HEAD_DIM

    def body(q_ref, kc_ref, kp_ref, vc_ref, vp_ref, gate_ref, bias_ref, sink_ref, a_ref):
        n = pl.program_id(1)
        kb = jnp.concatenate([kp_ref[0], kc_ref[0]], axis=0)
        vb = jnp.concatenate([vp_ref[0], vc_ref[0]], axis=0)
        for g in range(SWA_GROUP):
            sl = slice(g * HEAD_DIM, (g + 1) * HEAD_DIM)
            p, _ = _swa_scores(q_ref[:, sl], kb, bias_ref[g], sink_ref[0, g:g + 1, :1], n)
            og = _dot(p, vb)
            a_ref[:, sl] = (og * _silu(gate_ref[:, sl].astype(F32))).astype(a_ref.dtype)

    specs, args = _swa_specs(proj, k4, v4, nb, False)
    return pl.pallas_call(
        body, name="swa_fwd", grid=(SWA_KV_HEADS, nb),
        in_specs=specs + [pl.BlockSpec((SWA_GROUP, BLOCK, 2 * BLOCK), lambda h, n: (h, 0, 0)),
                          pl.BlockSpec((1, SWA_GROUP, LANES), lambda h, n: (h, 0, 0))],
        out_specs=pl.BlockSpec((BLOCK, gw), lambda h, n: (n, h)),
        out_shape=jax.ShapeDtypeStruct((s, SWA_HEADS * HEAD_DIM), BF16),
        compiler_params=_cp(("parallel", "parallel")),
    )(*args, bias, sinks)


def _swa_bwd(proj, k4, v4, bias, sinks, da):
    s = proj.shape[0]
    nb = s // BLOCK
    gw = SWA_GROUP * HEAD_DIM

    def body(q_ref, kc_ref, kp_ref, vc_ref, vp_ref, gate_ref, bias_ref, sink_ref, da_ref,
             dq_ref, dgate_ref, dk_ref, dv_ref, dbias_ref, dsink_ref, dk_own, dv_own):
        n = pl.program_id(1)

        @pl.when(n == 0)
        def _():
            dk_own[...] = jnp.zeros_like(dk_own)
            dv_own[...] = jnp.zeros_like(dv_own)
            dbias_ref[...] = jnp.zeros_like(dbias_ref)
            dsink_ref[...] = jnp.zeros_like(dsink_ref)

        @pl.when(n < nb)
        def _():
            kb = jnp.concatenate([kp_ref[0], kc_ref[0]], axis=0)
            vb = jnp.concatenate([vp_ref[0], vc_ref[0]], axis=0)
            dkb = jnp.zeros((2 * BLOCK, HEAD_DIM), F32)
            dvb = jnp.zeros((2 * BLOCK, HEAD_DIM), F32)
            for g in range(SWA_GROUP):
                sl = slice(g * HEAD_DIM, (g + 1) * HEAD_DIM)
                qg = q_ref[:, sl]
                p, p0 = _swa_scores(qg, kb, bias_ref[g], sink_ref[0, g:g + 1, :1], n)
                og = _dot(p, vb)
                gate = gate_ref[:, sl].astype(F32)
                dag = da_ref[:, sl].astype(F32)
                sg = jax.nn.sigmoid(gate)
                do = dag * gate * sg
                dgate_ref[:, sl] = (dag * og * sg * (1.0 + gate * (1.0 - sg))).astype(dgate_ref.dtype)
                dp = _dot(do, vb, 1, 1)
                delta = jnp.sum(do * og, axis=1, keepdims=True)
                ds = p * (dp - delta)
                dbias_ref[g] += ds
                dsink_ref[0, g:g + 1, :] += jnp.zeros((1, LANES), F32) - jnp.sum(p0 * delta, axis=0, keepdims=True)
                dq_ref[:, sl] = (_dot(ds, kb) * (HEAD_DIM ** -0.5)).astype(dq_ref.dtype)
                dkb += _dot(ds, qg, 0, 0) * (HEAD_DIM ** -0.5)
                dvb += _dot(p, do, 0, 0)
            dk_ref[0] = dk_own[...] + dkb[:BLOCK]
            dv_ref[0] = dv_own[...] + dvb[:BLOCK]
            dk_own[...] = dkb[BLOCK:]
            dv_own[...] = dvb[BLOCK:]

        @pl.when(n == nb)
        def _():
            dk_ref[0] = dk_own[...]
            dv_ref[0] = dv_own[...]

    specs, args = _swa_specs(proj, k4, v4, nb, True)
    cur = lambda n: jnp.minimum(n, nb - 1)
    trail = lambda n: jnp.maximum(n - 1, 0)
    return pl.pallas_call(
        body, name="swa_bwd", grid=(SWA_KV_HEADS, nb + 1),
        in_specs=specs + [pl.BlockSpec((SWA_GROUP, BLOCK, 2 * BLOCK), lambda h, n: (h, 0, 0)),
                          pl.BlockSpec((1, SWA_GROUP, LANES), lambda h, n: (h, 0, 0)),
                          pl.BlockSpec((BLOCK, gw), lambda h, n: (cur(n), h))],
        out_specs=[pl.BlockSpec((BLOCK, gw), lambda h, n: (cur(n), h)),
                   pl.BlockSpec((BLOCK, gw), lambda h, n: (cur(n), h)),
                   pl.BlockSpec((1, BLOCK, HEAD_DIM), lambda h, n: (h, trail(n), 0)),
                   pl.BlockSpec((1, BLOCK, HEAD_DIM), lambda h, n: (h, trail(n), 0)),
                   pl.BlockSpec((SWA_GROUP, BLOCK, 2 * BLOCK), lambda h, n: (h, 0, 0)),
                   pl.BlockSpec((1, SWA_GROUP, LANES), lambda h, n: (h, 0, 0))],
        out_shape=[jax.ShapeDtypeStruct((s, SWA_HEADS * HEAD_DIM), BF16),
                   jax.ShapeDtypeStruct((s, SWA_HEADS * HEAD_DIM), BF16),
                   jax.ShapeDtypeStruct((SWA_KV_HEADS, s, HEAD_DIM), F32),
                   jax.ShapeDtypeStruct((SWA_KV_HEADS, s, HEAD_DIM), F32),
                   jax.ShapeDtypeStruct((SWA_HEADS, BLOCK, 2 * BLOCK), F32),
                   jax.ShapeDtypeStruct((SWA_KV_HEADS, SWA_GROUP, LANES), F32)],
        scratch_shapes=[pltpu.VMEM((BLOCK, HEAD_DIM), F32), pltpu.VMEM((BLOCK, HEAD_DIM), F32)],
        compiler_params=_cp(("arbitrary", "arbitrary")),
    )(*args, bias, sinks, da)


FOX_TILE = 512


def _below_and_on_diagonal(i, j, step):
    @pl.when(j < i)
    def _():
        step(False)

    @pl.when(j == i)
    def _():
        step(True)


FOX_AUG = 128
FOX_CQ, FOX_CK = HEAD_DIM, HEAD_DIM + 1


def _fox_operands(proj, csum):
    s = proj.shape[0]
    hw = FOX_HEADS * HEAD_DIM
    heads = lambda a: a.reshape(s, FOX_HEADS, HEAD_DIM).transpose(1, 0, 2)
    q = heads(proj[:, :hw]) * jnp.asarray(HEAD_DIM ** -0.5, BF16)
    k, v = heads(proj[:, hw:2 * hw]), heads(proj[:, 2 * hw:3 * hw])
    one = jnp.ones((FOX_HEADS, s, 1), BF16)
    zero = jnp.zeros((FOX_HEADS, s, 1), BF16)
    pad = jnp.zeros((FOX_HEADS, s, FOX_AUG - HEAD_DIM - 2), BF16)
    qa = jnp.concatenate([q, zero, -one, pad], axis=-1)
    ka = jnp.concatenate([k, one, zero, pad], axis=-1)
    tr = lambda a: a.transpose(0, 2, 1)
    c = csum[:, :FOX_HEADS].T
    return (qa, ka, tr(qa), tr(ka), v, tr(v), c[:, None, :], jnp.broadcast_to(c[:, :, None], (FOX_HEADS, s, LANES)))


def _fox_scores_t(ka, qa, cq, ck, diag):
    st = _dot(ka, qa, 1, 1) + cq - jnp.concatenate([ck] * (qa.shape[0] // LANES), axis=1)
    if diag:
        st = jnp.where(lax.broadcasted_iota(jnp.int32, st.shape, 0) <= lax.broadcasted_iota(jnp.int32, st.shape, 1), st, NEG)
    return st


def _fox_attention(qa, ka, vt, c_row, c_lanes):
    nh, s, aw = qa.shape
    t = min(FOX_TILE, s)
    nt = s // t
    hd = HEAD_DIM

    def body(qa_ref, ka_ref, vt_ref, cq_ref, ck_ref, o_ref, lse_ref, m_s, l_s, acc_s):
        i, j = pl.program_id(1), pl.program_id(2)

        @pl.when(j == 0)
        def _():
            m_s[...] = jnp.full_like(m_s, NEG)
            l_s[...] = jnp.zeros_like(l_s)
            acc_s[...] = jnp.zeros_like(acc_s)

        def step(diag):
            st = _fox_scores_t(ka_ref[0], qa_ref[0], cq_ref[0], ck_ref[0], diag)
            m_old = m_s[...]
            m_new = jnp.maximum(m_old, jnp.max(st, axis=0, keepdims=True))
            alpha = jnp.exp(m_old - m_new)
            p = jnp.exp(st - m_new)
            l_s[...] = alpha * l_s[...] + jnp.sum(p, axis=0, keepdims=True)
            acc_s[...] = alpha * acc_s[...] + _dot(vt_ref[0], p)
            m_s[...] = m_new

        _below_and_on_diagonal(i, j, step)

        @pl.when(j == nt - 1)
        def _():
            o_ref[0] = (acc_s[...] / l_s[...]).astype(o_ref.dtype)
            lse_ref[0] = m_s[...] + jnp.log(l_s[...])

    kj = lambda i, j: jnp.minimum(j, i)
    return pl.pallas_call(
        body, name="fox_fwd", grid=(nh, nt, nt),
        in_specs=[pl.BlockSpec((1, t, aw), lambda h, i, j: (h, i, 0)),
                  pl.BlockSpec((1, t, aw), lambda h, i, j: (h, kj(i, j), 0)),
                  pl.BlockSpec((1, hd, t), lambda h, i, j: (h, 0, kj(i, j))),
                  pl.BlockSpec((1, 1, t), lambda h, i, j: (h, 0, i)),
                  pl.BlockSpec((1, t, LANES), lambda h, i, j: (h, kj(i, j), 0))],
        out_specs=[pl.BlockSpec((1, hd, t), lambda h, i, j: (h, 0, i)),
                   pl.BlockSpec((1, 1, t), lambda h, i, j: (h, 0, i))],
        out_shape=[jax.ShapeDtypeStruct((nh, hd, s), BF16), jax.ShapeDtypeStruct((nh, 1, s), F32)],
        scratch_shapes=[pltpu.VMEM((1, t), F32), pltpu.VMEM((1, t), F32), pltpu.VMEM((hd, t), F32)],
        compiler_params=_cp(("parallel", "parallel", "arbitrary")),
    )(qa, ka, vt, c_row, c_lanes)


def _fox_attention_bwd(qa, ka, qat, kat, v, c_row, c_lanes, ot, dot, lse):
    nh, s, aw = qa.shape
    t = min(FOX_TILE, s)
    nt = s // t
    hd = HEAD_DIM

    def body(qa_ref, ka_ref, qat_ref, kat_ref, v_ref, cq_ref, ck_ref, ot_ref, dot_ref, lse_ref,
             dqa_ref, dka_ref, dv_ref, dk_s, dv_s):
        j, i = pl.program_id(1), pl.program_id(2)

        @pl.when((j == 0) & (i == 0))
        def _():
            dqa_ref[...] = jnp.zeros_like(dqa_ref)

        @pl.when(i == 0)
        def _():
            dk_s[...] = jnp.zeros_like(dk_s)
            dv_s[...] = jnp.zeros_like(dv_s)

        def step(diag):
            p = jnp.exp(_fox_scores_t(ka_ref[0], qa_ref[0], cq_ref[0], ck_ref[0], diag) - lse_ref[0])
            do_t = dot_ref[0]
            delta = jnp.sum(do_t.astype(F32) * ot_ref[0].astype(F32), axis=0, keepdims=True)
            ds = (p * (_dot(v_ref[0], do_t) - delta)).astype(BF16)
            dv_s[...] += _dot(do_t, p, 1, 1)
            dk_s[...] += _dot(qat_ref[0], ds, 1, 1)
            cols = pl.ds(pl.multiple_of(i * t, t), t)
            dqa_ref[0, :, cols] += _dot(kat_ref[0], ds)

        _below_and_on_diagonal(i, j, step)

        @pl.when(i == nt - 1)
        def _():
            dka_ref[0] = dk_s[...]
            dv_ref[0] = dv_s[...].astype(dv_ref.dtype)

    qi = lambda i, j: jnp.maximum(i, j)
    return pl.pallas_call(
        body, name="fox_bwd", grid=(nh, nt, nt),
        in_specs=[pl.BlockSpec((1, t, aw), lambda h, j, i: (h, qi(i, j), 0)),
                  pl.BlockSpec((1, t, aw), lambda h, j, i: (h, j, 0)),
                  pl.BlockSpec((1, aw, t), lambda h, j, i: (h, 0, qi(i, j))),
                  pl.BlockSpec((1, aw, t), lambda h, j, i: (h, 0, j)),
                  pl.BlockSpec((1, t, hd), lambda h, j, i: (h, j, 0)),
                  pl.BlockSpec((1, 1, t), lambda h, j, i: (h, 0, qi(i, j))),
                  pl.BlockSpec((1, t, LANES), lambda h, j, i: (h, j, 0)),
                  pl.BlockSpec((1, hd, t), lambda h, j, i: (h, 0, qi(i, j))),
                  pl.BlockSpec((1, hd, t), lambda h, j, i: (h, 0, qi(i, j))),
                  pl.BlockSpec((1, 1, t), lambda h, j, i: (h, 0, qi(i, j)))],
        out_specs=[pl.BlockSpec((1, aw, s), lambda h, j, i: (h, 0, 0)),
                   pl.BlockSpec((1, aw, t), lambda h, j, i: (h, 0, j)),
                   pl.BlockSpec((1, hd, t), lambda h, j, i: (h, 0, j))],
        out_shape=[jax.ShapeDtypeStruct((nh, aw, s), F32), jax.ShapeDtypeStruct((nh, aw, s), F32),
                   jax.ShapeDtypeStruct((nh, hd, s), BF16)],
        scratch_shapes=[pltpu.VMEM((aw, t), F32), pltpu.VMEM((hd, t), F32)],
        compiler_params=_cp(("parallel", "arbitrary", "arbitrary")),
    )(qa, ka, qat, kat, v, c_row, c_lanes, ot, dot, lse)


def _ssm_prep(lam_re, lam_im, log_dt, b_re, b_im, c_re, c_im):
    g, n = lam_re.shape
    dt = jnp.exp(log_dt)[:, None]
    mag = jnp.exp(lam_re * dt)
    ab_re = mag * jnp.cos(lam_im * dt)
    ab_im = mag * jnp.sin(lam_im * dt)
    den = lam_re * lam_re + lam_im * lam_im
    nr = ab_re - 1.0
    coef_re = ((nr * lam_re + ab_im * lam_im) / den)[..., None]
    coef_im = ((ab_im * lam_re - nr * lam_im) / den)[..., None]
    bb_re = coef_re * b_re - coef_im * b_im
    bb_im = coef_re * b_im + coef_im * b_re
    nblk = g // GROUPS_PER_STEP
    eye = jnp.eye(GROUPS_PER_STEP, dtype=F32)

    def bdiag(bb):
        return jnp.einsum('bgnc,gh->bgchn', bb.reshape(nblk, GROUPS_PER_STEP, n, SSM_GROUP), eye).reshape(
            nblk, GROUPS_PER_STEP * SSM_GROUP, STATE_W)

    def cdiag(cc):
        return jnp.einsum('bgcn,gh->bgnhc', cc.reshape(nblk, GROUPS_PER_STEP, SSM_GROUP, n), eye).reshape(
            nblk, STATE_W, GROUPS_PER_STEP * SSM_GROUP)

    return (ab_re.reshape(nblk, 1, STATE_W), ab_im.reshape(nblk, 1, STATE_W),
            bdiag(bb_re), bdiag(bb_im), cdiag(c_re), cdiag(c_im))


def _powers(a_re, a_im, levels):
    rs, ims = [a_re], [a_im]
    for _ in range(levels - 1):
        r, i = rs[-1], ims[-1]
        rs.append(r * r - i * i)
        ims.append(2.0 * r * i)
    return jnp.concatenate(rs, axis=1), jnp.concatenate(ims, axis=1)


def _ssm_fwd(proj, pw_re, pw_im, bd_re, bd_im, cd_re, cd_im, dskip):
    s = proj.shape[0]
    w = dskip.shape[1]
    nblk = w // LANES
    nch = STATE_W // LANES
    levels = pw_re.shape[1]

    def body(u_ref, pr, pi, br, bi, cr, ci, d_ref, pre_ref, yg_ref, xr_ref, xi_ref):
        ch = pl.program_id(1)
        u = u_ref[...]
        xr, xi = _scan_complex(_dot(u, br[0]), _dot(u, bi[0]), pr[0], pi[0], False)
        xr_ref[...] = xr.astype(BF16)
        xi_ref[...] = xi.astype(BF16)
        yc = _dot(xr, cr[0]) - _dot(xi, ci[0])

        @pl.when(ch == 0)
        def _():
            pre_ref[...] = yc + d_ref[...] * u

        @pl.when(ch > 0)
        def _():
            pre_ref[...] += yc

        @pl.when(ch == nch - 1)
        def _():
            yg_ref[...] = _gelu(pre_ref[...]).astype(BF16)

    blk = lambda b, c: (0, b)
    col = lambda b, c: (0, b * nch + c)
    in_chunk = pl.BlockSpec((1, LANES, LANES), lambda b, c: (b, 0, c))
    out_chunk = pl.BlockSpec((1, LANES, LANES), lambda b, c: (b, c, 0))
    pw_spec = pl.BlockSpec((1, levels, LANES), lambda b, c: (b, 0, c))
    return pl.pallas_call(
        body, name="ssm_fwd", grid=(nblk, nch),
        in_specs=[pl.BlockSpec((s, LANES), blk), pw_spec, pw_spec, in_chunk, in_chunk, out_chunk, out_chunk,
                  pl.BlockSpec((1, LANES), blk)],
        out_specs=[pl.BlockSpec((s, LANES), blk), pl.BlockSpec((s, LANES), blk),
                   pl.BlockSpec((s, LANES), col), pl.BlockSpec((s, LANES), col)],
        out_shape=[jax.ShapeDtypeStruct((s, w), F32), jax.ShapeDtypeStruct((s, w), BF16),
                   jax.ShapeDtypeStruct((s, nblk * STATE_W), BF16), jax.ShapeDtypeStruct((s, nblk * STATE_W), BF16)],
        compiler_params=_cp(("parallel", "arbitrary")),
    )(proj,pw_re, pw_im, bd_re, bd_im, cd_re, cd_im, dskip)


def _ssm_bwd(proj, dyg, pre, xr_all, xi_all, pw_re, pw_im, bd_re, bd_im, cd_re, cd_im, dskip):
    s = proj.shape[0]
    w = dskip.shape[1]
    nblk = w // LANES
    nch = STATE_W // LANES
    levels = pw_re.shape[1]

    def body(u_ref, dy_ref, pre_ref, xr_ref, xi_ref, pr, pi, br, bi, cr, ci, d_ref,
             du_ref, dd_ref, dar_ref, dai_ref, dbr_ref, dbi_ref, dcr_ref, dci_ref):
        ch = pl.program_id(1)
        u = u_ref[...]
        _, vjp = jax.vjp(_gelu, pre_ref[...])
        dpre = vjp(dy_ref[...].astype(F32))[0]
        zr, zi = _scan_complex(_dot(dpre, cr[0], 1, 1), -_dot(dpre, ci[0], 1, 1), pr[0], -pi[0], True)
        xpr = _shift_rows(xr_ref[...].astype(F32), 1)
        xpi = _shift_rows(xi_ref[...].astype(F32), 1)
        dar_ref[0] = jnp.sum(zr * xpr + zi * xpi, axis=0, keepdims=True)
        dai_ref[0] = jnp.sum(zi * xpr - zr * xpi, axis=0, keepdims=True)
        dcr_ref[0] = _dot(xr_ref[...], dpre, 0, 0)
        dci_ref[0] = -_dot(xi_ref[...], dpre, 0, 0)
        dbr_ref[0] = _dot(u, zr, 0, 0)
        dbi_ref[0] = _dot(u, zi, 0, 0)
        duc = _dot(zr, br[0], 1, 1) + _dot(zi, bi[0], 1, 1)

        @pl.when(ch == 0)
        def _():
            du_ref[...] = duc + dpre * d_ref[...]
            dd_ref[...] = jnp.sum(dpre * u, axis=0, keepdims=True)

        @pl.when(ch > 0)
        def _():
            du_ref[...] += duc

    blk = lambda b, c: (0, b)
    col = lambda b, c: (0, b * nch + c)
    in_chunk = pl.BlockSpec((1, LANES, LANES), lambda b, c: (b, 0, c))
    out_chunk = pl.BlockSpec((1, LANES, LANES), lambda b, c: (b, c, 0))
    pw_spec = pl.BlockSpec((1, levels, LANES), lambda b, c: (b, 0, c))
    a_spec = pl.BlockSpec((1, 1, LANES), lambda b, c: (b, 0, c))
    return pl.pallas_call(
        body, name="ssm_bwd", grid=(nblk, nch),
        in_specs=[pl.BlockSpec((s, LANES), blk), pl.BlockSpec((s, LANES), blk), pl.BlockSpec((s, LANES), blk),
                  pl.BlockSpec((s, LANES), col), pl.BlockSpec((s, LANES), col),
                  pw_spec, pw_spec, in_chunk, in_chunk, out_chunk, out_chunk, pl.BlockSpec((1, LANES), blk)],
        out_specs=[pl.BlockSpec((s, LANES), blk), pl.BlockSpec((1, LANES), blk), a_spec, a_spec,
                   in_chunk, in_chunk, out_chunk, out_chunk],
        out_shape=[jax.ShapeDtypeStruct((s, w), F32), jax.ShapeDtypeStruct((1, w), F32),
                   jax.ShapeDtypeStruct((nblk, 1, STATE_W), F32), jax.ShapeDtypeStruct((nblk, 1, STATE_W), F32),
                   jax.ShapeDtypeStruct((nblk, LANES, STATE_W), F32), jax.ShapeDtypeStruct((nblk, LANES, STATE_W), F32),
                   jax.ShapeDtypeStruct((nblk, STATE_W, LANES), F32), jax.ShapeDtypeStruct((nblk, STATE_W, LANES), F32)],
        compiler_params=_cp(("parallel", "arbitrary")),
    )(proj,dyg, pre, xr_all, xi_all, pw_re, pw_im, bd_re, bd_im, cd_re, cd_im, dskip)


ROW_TILE = 256


def _norm_fwd(x, g, name):
    s, d = x.shape
    t = min(ROW_TILE, s)
    return _tiled(lambda xv, gv: (_rms(xv, gv),), [_rows(x, t), _full(g)], [_orows(s, d, BF16, t)], s // t, name)[0]


def _norm_bwd(x, g, dh_list, dx_in, name):
    s, d = x.shape
    t = min(ROW_TILE, s)
    nh = len(dh_list)

    def fn(xv, gv, dxv, *dhs):
        dh = dhs[0].astype(F32)
        for other in dhs[1:]:
            dh = dh + other.astype(F32)
        _, vjp = jax.vjp(_rms, xv, gv)
        dx, dg = vjp(dh)
        dx = dx + dxv
        return dx, dx, dg

    return _tiled(fn, [_rows(x, t), _full(g), _rows(dx_in, t)] + [_rows(a, t) for a in dh_list],
                  [_orows(s, d, F32, t), _orows(s, d, BF16, t), _oacc((1, d))], s // t, name)


def _ple_fwd(x, y, pn, name):
    s, d = x.shape
    t = min(ROW_TILE, s)

    def fn(xv, yv, gv):
        x1 = xv + yv
        return x1, _rms(x1, gv)

    return _tiled(fn, [_rows(x, t), _rows(y, t), _full(pn)], [_orows(s, d, F32, t), _orows(s, d, BF16, t)], s // t, name)


def _ple_mix(x1, emb, gl, name):
    s, d = x1.shape
    t = min(ROW_TILE, s)
    return _tiled(lambda a, e, g: (a + e * jax.nn.sigmoid(g),), [_rows(x1, t), _rows(emb, t), _rows(gl, t)],
                  [_orows(s, d, F32, t)], s // t, name)[0]


def _ple_mix_bwd(dx2, emb, gl, name):
    s, d = dx2.shape
    t = min(ROW_TILE, s)

    def fn(dx, e, g):
        sg = jax.nn.sigmoid(g)
        return dx * sg, dx * e * sg * (1.0 - sg)

    return _tiled(fn, [_rows(dx2, t), _rows(emb, t), _rows(gl, t)],
                  [_orows(s, d, BF16, t), _orows(s, d, BF16, t)], s // t, name)


def _loss_grad(x, target, g):
    s, d = x.shape
    t = min(ROW_TILE, s)

    def fn(xv, tv, gv):
        def f(xx, gg):
            err = _rms(xx, gg) - tv
            return 0.5 * jnp.sum(jnp.mean(err * err, axis=-1, keepdims=True), axis=0, keepdims=True)

        loss, vjp = jax.vjp(f, xv, gv)
        dx, dg = vjp(jnp.ones((1, 1), F32))
        return loss, dx, dg

    return _tiled(fn, [_rows(x, t), _rows(target, t), _full(g)],
                  [_oacc((1, 1)), _orows(s, d, F32, t), _oacc((1, d))], s // t, "loss_grad")


def _conv_fwd(proj, kern):
    s = proj.shape[0]
    w = kern.shape[1]
    nb = w // LANES

    def fn(bg, cg, u, gate, k):
        z = cg.astype(F32) * u.astype(F32)
        conv = k[2:3] * z + k[1:2] * _shift_rows(z, 1) + k[0:1] * _shift_rows(z, 2)
        return (bg.astype(F32) * conv * _silu(gate.astype(F32)),)

    return _tiled(fn, [_cols(proj, LANES, q * nb) for q in range(4)] + [_cols(kern, LANES)],
                  [_ocols(s, w, BF16, LANES)], nb, "conv_fwd")[0]


def _conv_bwd(proj, kern, da):
    s = proj.shape[0]
    w = kern.shape[1]
    nb = w // LANES

    def fn(bg, cg, u, gate, k, dav):
        bg, cg, u, gate, dav = (a.astype(F32) for a in (bg, cg, u, gate, dav))
        z = cg * u
        z1, z2 = _shift_rows(z, 1), _shift_rows(z, 2)
        conv = k[2:3] * z + k[1:2] * z1 + k[0:1] * z2
        sg = jax.nn.sigmoid(gate)
        dy = dav * gate * sg
        dgate = dav * bg * conv * sg * (1.0 + gate * (1.0 - sg))
        dconv = dy * bg
        dk = jnp.concatenate([jnp.sum(dconv * zz, axis=0, keepdims=True) for zz in (z2, z1, z)], axis=0)
        dz = k[2:3] * dconv + k[1:2] * _shift_rows(dconv, 1, True) + k[0:1] * _shift_rows(dconv, 2, True)
        return dy * conv, dz * u, dz * cg, dgate, dk

    return _tiled(fn, [_cols(proj, LANES, q * nb) for q in range(4)] + [_cols(kern, LANES), _cols(da, LANES)],
                  [_ocols(s, w, BF16, LANES)] * 4 + [_ocols(3, w, F32, LANES)], nb, "conv_bwd")


def _glu_fwd(gl, proj, bglu):
    s, w2 = gl.shape
    w = w2 // 2
    t = min(ROW_TILE, s)

    def fn(glv, gate, b):
        v = glv + b
        return (v[:, :w] * jax.nn.sigmoid(v[:, w:]) * _silu(gate),)

    return _tiled(fn, [_rows(gl, t), (proj, (t, w), lambda i: (i, 1)), _full(bglu)],
                  [_orows(s, w, BF16, t)], s // t, "glu_fwd")[0]


def _glu_bwd(gl, proj, bglu, da):
    s, w2 = gl.shape
    w = w2 // 2
    t = min(ROW_TILE, s)

    def fn(glv, gate, b, dav):
        def f(gg, gt, bb):
            v = gg + bb
            return v[:, :w] * jax.nn.sigmoid(v[:, w:]) * _silu(gt)

        _, vjp = jax.vjp(f, glv, gate, b)
        return vjp(dav.astype(F32))

    return _tiled(fn, [_rows(gl, t), (proj, (t, w), lambda i: (i, 1)), _full(bglu), _rows(da, t)],
                  [_orows(s, w2, BF16, t), _orows(s, w, BF16, t), _oacc((1, w2))], s // t, "glu_bwd")


def _fg_fwd(z, b):
    def fn(zv, bv):
        v = zv + bv
        logf = jnp.minimum(v, 0.0) - jnp.log(1.0 + jnp.exp(-jnp.abs(v)))
        return (_cumsum_rows(logf, False),)

    return _tiled(fn, [_full(z), _full(b)], [_out(z.shape, F32, z.shape, lambda i: (0, 0))], 1, "fg_fwd")[0]


def _fg_bwd(z, b, dcs):
    def fn(zv, bv, dc):
        dz = _cumsum_rows(dc, True) * jax.nn.sigmoid(-(zv + bv))
        return dz, jnp.sum(dz, axis=0, keepdims=True)

    return _tiled(fn, [_full(z), _full(b), _full(dcs)],
                  [_out(z.shape, BF16, z.shape, lambda i: (0, 0)), _out((1, z.shape[1]), F32, (1, z.shape[1]), lambda i: (0, 0))],
                  1, "fg_bwd")


def _fox_gate_bwd(da, o, proj):
    s, w = o.shape
    t = min(ROW_TILE, s)

    def fn(dav, ov, gate):
        dav, ov, gate = dav.astype(F32), ov.astype(F32), gate.astype(F32)
        sg = jax.nn.sigmoid(gate)
        return dav * gate * sg, dav * ov * sg * (1.0 + gate * (1.0 - sg))

    return _tiled(fn, [_rows(da, t), _rows(o, t), (proj, (t, w), lambda i: (i, 3))],
                  [_orows(s, w, BF16, t), _orows(s, w, BF16, t)], s // t, "fox_gate_bwd")


def _fox_gate_fwd(o, proj):
    s, w = o.shape
    t = min(ROW_TILE, s)
    return _tiled(lambda ov, gate: (ov.astype(F32) * _silu(gate.astype(F32)),),
                  [_rows(o, t), (proj, (t, w), lambda i: (i, 3))], [_orows(s, w, BF16, t)], s // t, "fox_gate_fwd")[0]


def _local_step(x, p, target, w):
    s, d = x.shape
    depth = p.shape[0]
    pb = p.astype(BF16)
    grads = {}
    saved = []

    bucket = _swa_bucket_table()
    onehot = np.eye(REL_BUCKETS, dtype=np.float32)[bucket.reshape(-1)]
    bias = _bias_table(w['rel_bias'], jnp.asarray(onehot.T, BF16))
    sinks = jnp.broadcast_to(w['swa_sinks'].reshape(SWA_KV_HEADS, SWA_GROUP, 1), (SWA_KV_HEADS, SWA_GROUP, LANES))
    ssm_params = tuple(w[k][0] for k in ('ssm_lam_re', 'ssm_lam_im', 'ssm_log_dt', 'ssm_b_re', 'ssm_b_im', 'ssm_c_re', 'ssm_c_im'))
    (a_re, a_im, bd_re, bd_im, cd_re, cd_im), ssm_vjp = jax.vjp(_ssm_prep, *ssm_params)
    levels = max(1, (s - 1).bit_length())
    pw_re, pw_im = _powers(a_re, a_im, levels)
    wfg = jnp.pad(w['fox_w_fg'][0], ((0, 0), (0, LANES - FOX_HEADS)))
    bfg = jnp.pad(w['fox_b_fg'], ((0, 0), (0, LANES - FOX_HEADS)))

    def qkv4(a):
        return a.reshape(s, SWA_KV_HEADS, HEAD_DIM).transpose(1, 0, 2)

    def unheads(a):
        return a.transpose(2, 0, 1).reshape(s, FOX_HEADS * HEAD_DIM)

    for i in range(depth):
        mixer = i % 4
        hn = _norm_fwd(x, w['norm_g'][i:i + 1], f"norm_fwd{i}")
        sv = {'x': x, 'hn': hn}
        if mixer == 0:
            proj = _mm(hn, w['swa_w_in'][0], name="swa_in")
            qw = SWA_HEADS * HEAD_DIM
            kvw = SWA_KV_HEADS * HEAD_DIM
            k4, v4 = qkv4(proj[:, qw:qw + kvw]), qkv4(proj[:, qw + kvw:qw + 2 * kvw])
            a = _swa_fwd(proj, k4, v4, bias, sinks)
            sv.update(proj=proj, k4=k4, v4=v4)
            w_out = w['swa_w_out'][0]
        elif mixer == 1:
            proj = _mm(hn, w['conv_w_in'][0], name="conv_in")
            a = _conv_fwd(proj, w['conv_kernel'][0])
            sv.update(proj=proj)
            w_out = w['conv_w_out'][0]
        elif mixer == 2:
            proj = _mm(hn, w['ssm_w_in'][0], out_dtype=F32, name="ssm_in")
            pre, yg, xr_all, xi_all = _ssm_fwd(proj, pw_re, pw_im, bd_re, bd_im, cd_re, cd_im, w['ssm_d'])
            gl = _mm(yg, w['ssm_w_glu'][0], out_dtype=F32, name="ssm_glu")
            a = _glu_fwd(gl, proj, w['ssm_b_glu'])
            sv.update(proj=proj, pre=pre, yg=yg, xr=xr_all, xi=xi_all, gl=gl)
            w_out = w['ssm_w_out'][0]
        else:
            proj = _mm(hn, w['fox_w_in'][0], name="fox_in")
            z = _mm(hn, wfg, out_dtype=F32, name="fox_fg")
            fox_ops = _fox_operands(proj, _fg_fwd(z, bfg))
            ot, lse = _fox_attention(fox_ops[0], fox_ops[1], fox_ops[5], fox_ops[6], fox_ops[7])
            o = unheads(ot)
            a = _fox_gate_fwd(o, proj)
            sv.update(proj=proj, z=z, fox_ops=fox_ops, ot=ot, o=o, lse=lse)
            w_out = w['fox_w_out'][0]
        y = _mm(a, w_out, out_dtype=F32, name=f"mixer_out{i}")
        x1, gn = _ple_fwd(x, y, w['ple_norm'][i:i + 1], f"ple_fwd{i}")
        emb = _mm(pb[i], w['ple_proj'][i], out_dtype=F32, name=f"ple_emb{i}")
        gl2 = _mm(gn, w['ple_gate'][i], out_dtype=F32, name=f"ple_gate{i}")
        x = _ple_mix(x1, emb, gl2, f"ple_mix{i}")
        sv.update(a=a, x1=x1, gn=gn, emb=emb, gl2=gl2)
        saved.append(sv)

    loss, dx, grads['final_g'] = _loss_grad(x, target, w['final_g'].reshape(1, d))
    grads['final_g'] = grads['final_g'].reshape(d)

    g_norm, g_ple_norm, g_ple_proj, g_ple_gate = [None] * depth, [None] * depth, [None] * depth, [None] * depth
    for i in reversed(range(depth)):
        sv = saved[i]
        mixer = i % 4
        demb, dgl2 = _ple_mix_bwd(dx, sv['emb'], sv['gl2'], f"ple_mix_bwd{i}")
        g_ple_proj[i] = _mm(pb[i], demb, ta=True, name=f"ple_emb_dw{i}")
        g_ple_gate[i] = _mm(sv['gn'], dgl2, ta=True, name=f"ple_gate_dw{i}")
        dgn = _mm(dgl2, w['ple_gate'][i], tb=True, out_dtype=F32, name=f"ple_gate_dx{i}")
        dx1, dy, g_ple_norm[i] = _norm_bwd(sv['x1'], w['ple_norm'][i:i + 1], [dgn], dx, f"ple_norm_bwd{i}")
        w_out_name = ('swa_w_out', 'conv_w_out', 'ssm_w_out', 'fox_w_out')[mixer]
        grads[w_out_name] = _mm(sv['a'], dy, ta=True, name=f"mixer_out_dw{i}")[None]
        da = _mm(dy, w[w_out_name][0], tb=True, name=f"mixer_out_dx{i}")
        proj = sv['proj']
        dhs = []
        if mixer == 0:
            dq, dgate, dk4, dv4, dbias, dsink = _swa_bwd(proj, sv['k4'], sv['v4'], bias, sinks, da)
            back = lambda t4: t4.transpose(1, 0, 2).reshape(s, SWA_KV_HEADS * HEAD_DIM).astype(BF16)
            dproj = jnp.concatenate([dq, back(dk4), back(dv4), dgate], axis=1)
            grads['rel_bias'] = _bias_grad(dbias.reshape(SWA_HEADS, -1), jnp.asarray(onehot, BF16))
            grads['swa_sinks'] = dsink[:, :, 0].reshape(1, SWA_HEADS)
            w_in_name = 'swa_w_in'
        elif mixer == 1:
            dbg, dcg, du, dgate, dkern = _conv_bwd(proj, w['conv_kernel'][0], da)
            dproj = jnp.concatenate([dbg, dcg, du, dgate], axis=1)
            grads['conv_kernel'] = dkern[None]
            w_in_name = 'conv_w_in'
        elif mixer == 2:
            dgl, dgate, dbglu = _glu_bwd(sv['gl'], proj, w['ssm_b_glu'], da)
            grads['ssm_b_glu'] = dbglu
            grads['ssm_w_glu'] = _mm(sv['yg'], dgl, ta=True, name="ssm_glu_dw")[None]
            dyg = _mm(dgl, w['ssm_w_glu'][0], tb=True, out_dtype=F32, name="ssm_glu_dx")
            du, dd, da_re, da_im, dbd_re, dbd_im, dcd_re, dcd_im = _ssm_bwd(
                proj, dyg, sv['pre'], sv['xr'], sv['xi'], pw_re, pw_im, bd_re, bd_im, cd_re, cd_im, w['ssm_d'])
            grads['ssm_d'] = dd
            dparams = ssm_vjp((da_re, da_im, dbd_re, dbd_im, dcd_re, dcd_im))
            for k, v in zip(('ssm_lam_re', 'ssm_lam_im', 'ssm_log_dt', 'ssm_b_re', 'ssm_b_im', 'ssm_c_re', 'ssm_c_im'), dparams):
                grads[k] = v[None]
            dproj = jnp.concatenate([du.astype(BF16), dgate], axis=1)
            w_in_name = 'ssm_w_in'
        else:
            do, dgate = _fox_gate_bwd(da, sv['o'], proj)
            qa, ka, qat, kat, v, _, c_row, c_lanes = sv['fox_ops']
            dot = do.reshape(s, FOX_HEADS, HEAD_DIM).transpose(1, 2, 0)
            dqa, dka, dvt = _fox_attention_bwd(qa, ka, qat, kat, v, c_row, c_lanes, sv['ot'], dot, sv['lse'])
            dq = (unheads(dqa[:, :HEAD_DIM]) * (HEAD_DIM ** -0.5)).astype(BF16)
            dk, dv = unheads(dka[:, :HEAD_DIM]).astype(BF16), unheads(dvt)
            dcs = jnp.pad((dqa[:, FOX_CQ] + dka[:, FOX_CK]).T, ((0, 0), (0, LANES - FOX_HEADS)))
            dz, dbfg = _fg_bwd(sv['z'], bfg, dcs)
            grads['fox_b_fg'] = dbfg[:, :FOX_HEADS]
            grads['fox_w_fg'] = _mm(sv['hn'], dz, ta=True, out_dtype=BF16, name="fox_fg_dw")[:, :FOX_HEADS][None]
            dhs.append(_mm(dz, wfg, tb=True, out_dtype=F32, name="fox_fg_dx"))
            dproj = jnp.concatenate([dq, dk, dv, dgate], axis=1)
            w_in_name = 'fox_w_in'
        grads[w_in_name] = _mm(sv['hn'], dproj, ta=True, name=f"mixer_in_dw{i}")[None]
        dhs.append(_mm(dproj, w[w_in_name][0], tb=True, out_dtype=F32, name=f"mixer_in_dx{i}"))
        dx, _, g_norm[i] = _norm_bwd(sv['x'], w['norm_g'][i:i + 1], dhs, dx1, f"norm_bwd{i}")

    grads['norm_g'] = jnp.concatenate(g_norm, axis=0)
    grads['ple_norm'] = jnp.concatenate(g_ple_norm, axis=0)
    grads['ple_proj'] = jnp.stack(g_ple_proj)
    grads['ple_gate'] = jnp.stack(g_ple_gate)
    return loss, dx, grads


def _bf16_terms(a):
    hi = lax.reduce_precision(a, 8, 7)
    r1 = a - hi
    mid = lax.reduce_precision(r1, 8, 7)
    lo = lax.reduce_precision(r1 - mid, 8, 7)
    return hi.astype(BF16), mid.astype(BF16), lo.astype(BF16)


def _split3(a):
    return jnp.concatenate(_bf16_terms(a), axis=0)


def _bias_grad(dbias, onehot):
    out = _mm(_split3(dbias), onehot, out_dtype=F32, name="rel_bias_grad", tk=2048)
    nh = dbias.shape[0]
    return (out[:nh] + out[nh:2 * nh] + out[2 * nh:]).T


def _bias_table(rel_bias, onehot_t):
    nh = rel_bias.shape[1]
    out = _mm(_split3(rel_bias.T), onehot_t, out_dtype=F32, name="rel_bias_table")
    return (out[:nh] + out[nh:2 * nh] + out[2 * nh:]).reshape(nh, BLOCK, 2 * BLOCK)


WEIGHTS = ['norm_g', 'final_g', 'rel_bias', 'swa_w_in', 'swa_w_out', 'swa_sinks', 'conv_w_in', 'conv_kernel', 'conv_w_out',
           'ssm_w_in', 'ssm_lam_re', 'ssm_lam_im', 'ssm_log_dt', 'ssm_b_re', 'ssm_b_im', 'ssm_c_re', 'ssm_c_im', 'ssm_d',
           'ssm_w_glu', 'ssm_b_glu', 'ssm_w_out', 'fox_w_in', 'fox_w_fg', 'fox_b_fg', 'fox_w_out', 'ple_proj', 'ple_norm',
           'ple_gate']
BIG = {'swa_w_in': 2, 'swa_w_out': 1, 'conv_w_in': 2, 'conv_w_out': 1, 'ssm_w_in': 2, 'ssm_w_glu': 2, 'ssm_w_out': 1,
       'fox_w_in': 2, 'fox_w_fg': 1, 'fox_w_out': 1, 'ple_proj': 2, 'ple_gate': 1}
SMALL = {'conv_kernel': 2, 'ssm_d': 1, 'ssm_b_glu': 1}
REPLICATED = [n for n in WEIGHTS if n not in BIG and n not in SMALL]
N_CHIPS = 4
N_DEV = 8
BIG_ROWS = 256
SMALL_ROWS = 8
REPL_ROWS = 64


def _flat(pieces, dtype, lead, row_mult):
    flat = jnp.concatenate([q.astype(dtype) for q in pieces], axis=-1)
    pad = (-flat.shape[-1]) % (row_mult * FLAT_W)
    flat = jnp.pad(flat, [(0, 0)] * len(lead) + [(0, pad)])
    return flat.reshape(*lead, -1, FLAT_W)


def _unflat(flat, lead_ndim, sizes):
    lead = flat.shape[:lead_ndim]
    flat = flat.reshape(*lead, -1)
    out, off = [], 0
    for n in sizes:
        out.append(flat[..., off:off + n])
        off += n
    return out


def _split_shards(full, axis):
    shp = full.shape
    parts = full.reshape(shp[:axis] + (N_CHIPS, shp[axis] // N_CHIPS) + shp[axis + 1:])
    return jnp.moveaxis(parts, axis, 0)


def _join_shards(parts, axis):
    moved = jnp.moveaxis(parts, 0, axis)
    shp = moved.shape
    return moved.reshape(shp[:axis] + (shp[axis] * shp[axis + 1],) + shp[axis + 2:])


def _coords():
    return lax.axis_index("x"), lax.axis_index("y"), lax.axis_index("c")


def _remote(k, src, dst, to, send_sems, recv_sems):
    return pltpu.make_async_remote_copy(src_ref=src, dst_ref=dst, send_sem=send_sems.at[k], recv_sem=recv_sems.at[k],
                                        device_id=to, device_id_type=MESH)


def _gather_weights(bufs, ssh):
    nb = len(bufs)

    def body(*refs):
        w_refs, s_ref = refs[:nb], refs[nb]
        wouts, sout = refs[nb + 1:2 * nb + 1], refs[2 * nb + 1]
        send_sems, recv_sems = refs[2 * nb + 2:]
        x, y, c = _coords()
        me = 2 * x + y
        chips = [(1 - x, y), (x, 1 - y), (1 - x, 1 - y)]
        rc = functools.partial(_remote, send_sems=send_sems, recv_sems=recv_sems)
        sends = []
        for j, (cx, cy) in enumerate(chips):
            for b in range(nb):
                sends.append(rc(6 * b + j, w_refs[b].at[c], wouts[b].at[me, c], (cx, cy, c)))
            sends.append(rc(6 * nb + j, s_ref, sout.at[me], (cx, cy, c)))
        for cp in sends:
            cp.start()
        for j, (cx, cy) in enumerate(chips):
            k = 2 * cx + cy
            for b in range(nb):
                rc(6 * b + j, w_refs[b].at[c], wouts[b].at[k, c], (x, y, c)).wait_recv()
                fwd = rc(6 * b + 3 + j, wouts[b].at[k, c], wouts[b].at[k, c], (x, y, 1 - c))
                fwd.start()
                sends.append(fwd)
        for j, (cx, cy) in enumerate(chips):
            k = 2 * cx + cy
            for b in range(nb):
                rc(6 * b + 3 + j, w_refs[b].at[c], wouts[b].at[k, 1 - c], (x, y, c)).wait_recv()
            rc(6 * nb + j, s_ref, sout.at[k], (x, y, c)).wait_recv()
        for cp in sends:
            cp.wait_send()

    nsem = 6 * nb + 3
    res = pl.pallas_call(
        body, name="gather_weights", in_specs=[ANY] * (nb + 1), out_specs=[ANY] * (nb + 1),
        out_shape=[jax.ShapeDtypeStruct((N_CHIPS,) + a.shape, a.dtype) for a in (*bufs, ssh)],
        scratch_shapes=[pltpu.SemaphoreType.DMA((nsem,)), pltpu.SemaphoreType.DMA((nsem,))],
    )(*bufs, ssh)
    return res[:nb], res[nb]


def _pair_exchange(gbufs):
    nb = len(gbufs)

    def body(*refs):
        g_refs, outs = refs[:nb], refs[nb:2 * nb]
        send_sems, recv_sems = refs[2 * nb:]
        x, y, c = _coords()
        rc = functools.partial(_remote, send_sems=send_sems, recv_sems=recv_sems)
        sends = [rc(N_CHIPS * b + j, g_refs[b].at[2 * j + 1 - c], outs[b].at[j], (x, y, 1 - c))
                 for b in range(nb) for j in range(N_CHIPS)]
        for cp in sends:
            cp.start()
        for b in range(nb):
            for j in range(N_CHIPS):
                rc(N_CHIPS * b + j, g_refs[b].at[2 * j + c], outs[b].at[j], (x, y, c)).wait_recv()
        for cp in sends:
            cp.wait_send()

    nsem = N_CHIPS * nb
    return pl.pallas_call(
        body, name="pair_exchange", in_specs=[ANY] * nb, out_specs=[ANY] * nb,
        out_shape=[jax.ShapeDtypeStruct((N_CHIPS,) + g.shape[1:], g.dtype) for g in gbufs],
        scratch_shapes=[pltpu.SemaphoreType.DMA((nsem,)), pltpu.SemaphoreType.DMA((nsem,))],
    )(*gbufs)


def _pair_sum(mine, theirs, name):
    n, r, w = mine.shape
    t = _tile(r, 256, 16)
    spec = lambda a: (a, (n, t, w), lambda i: (0, i, 0))
    return _tiled(lambda a, b: (a.astype(F32) + b.astype(F32),), [spec(mine), spec(theirs)],
                  [_out((n, r, w), BF16, (n, t, w), lambda i: (0, i, 0))], r // t, name)[0]


def _exchange_grads(pbufs, gsmall, grepl):
    nb = len(pbufs)

    def body(*refs):
        g_refs, s_ref, r_ref = refs[:nb], refs[nb], refs[nb + 1]
        ogs, osm, orp = refs[nb + 2:2 * nb + 2], refs[2 * nb + 2], refs[2 * nb + 3]
        send_sems, recv_sems, local_sems = refs[2 * nb + 4:]
        x, y, c = _coords()
        me = 4 * x + 2 * y + c
        my_chip = 2 * x + y
        rc = functools.partial(_remote, send_sems=send_sems, recv_sems=recv_sems)
        local = [pltpu.make_async_copy(s_ref.at[me], osm.at[me], local_sems.at[0]),
                 pltpu.make_async_copy(r_ref, orp.at[me], local_sems.at[1])]
        for cp in local:
            cp.start()
        peers = []
        for d in range(1, N_DEV):
            px = 1 - x if d & 4 else x
            py = 1 - y if d & 2 else y
            pc = 1 - c if d & 1 else c
            peers.append((d, 4 * px + 2 * py + pc, 2 * px + py, (px, py, pc)))
        per_peer = nb + 2
        sends = []
        for i, (d, peer, chip, to) in enumerate(peers):
            sends.append(rc(per_peer * i + nb, s_ref.at[peer], osm.at[me], to))
            sends.append(rc(per_peer * i + nb + 1, r_ref, orp.at[me], to))
            if d & 1 == 0:
                for b in range(nb):
                    sends.append(rc(per_peer * i + b, g_refs[b].at[chip], ogs[b].at[my_chip], to))
        for cp in sends:
            cp.start()
        for i, (d, peer, chip, to) in enumerate(peers):
            rc(per_peer * i + nb, s_ref.at[peer], osm.at[peer], to).wait_recv()
            rc(per_peer * i + nb + 1, r_ref, orp.at[peer], to).wait_recv()
            if d & 1 == 0:
                for b in range(nb):
                    rc(per_peer * i + b, g_refs[b].at[chip], ogs[b].at[chip], to).wait_recv()
        for cp in sends:
            cp.wait_send()
        for cp in local:
            cp.wait()

    nsem = (nb + 2) * (N_DEV - 1)
    res = pl.pallas_call(
        body, name="exchange_grads", in_specs=[ANY] * (nb + 2), out_specs=[ANY] * (nb + 2),
        out_shape=[jax.ShapeDtypeStruct(a.shape, a.dtype) for a in (*pbufs, gsmall)]
        + [jax.ShapeDtypeStruct((N_DEV,) + grepl.shape, grepl.dtype)],
        scratch_shapes=[pltpu.SemaphoreType.DMA((nsem,)), pltpu.SemaphoreType.DMA((nsem,)), pltpu.SemaphoreType.DMA((2,))],
    )(*pbufs, gsmall, grepl)
    return res[:nb], res[nb], res[nb + 1]


def _sibling_exchange(halves):
    nb = len(halves)

    def body(*refs):
        ins, outs = refs[:nb], refs[nb:2 * nb]
        send_sems, recv_sems = refs[2 * nb:]
        x, y, c = _coords()
        rc = functools.partial(_remote, send_sems=send_sems, recv_sems=recv_sems)
        sends = [rc(b, ins[b], outs[b], (x, y, 1 - c)) for b in range(nb)]
        for cp in sends:
            cp.start()
        for b in range(nb):
            rc(b, ins[b], outs[b], (x, y, c)).wait_recv()
        for cp in sends:
            cp.wait_send()

    return pl.pallas_call(
        body, name="sibling_exchange", in_specs=[ANY] * nb, out_specs=[ANY] * nb,
        out_shape=[jax.ShapeDtypeStruct(a.shape, a.dtype) for a in halves],
        scratch_shapes=[pltpu.SemaphoreType.DMA((nb,)), pltpu.SemaphoreType.DMA((nb,))],
    )(*halves)


def _sum_senders(recv, name):
    n, r, w = recv.shape
    t = _tile(r, 256, 8)

    def fn(v):
        acc = v[0].astype(F32)
        for i in range(1, n):
            acc = acc + v[i].astype(F32)
        return (acc,)

    return _tiled(fn, [(recv, (n, t, w), lambda i: (0, i, 0))], [_orows(r, w, F32, t)], r // t, name)[0]


def _adamw(w, g, m, v, name):
    r, wd = w.shape
    t = _tile(r, 256, 8)

    def fn(wv, gv, mv, vv):
        m2 = ADAM_B1 * mv + (1.0 - ADAM_B1) * gv
        v2 = ADAM_B2 * vv + (1.0 - ADAM_B2) * (gv * gv)
        m_hat = m2 / (1.0 - ADAM_B1 ** ADAM_STEP)
        v_hat = v2 / (1.0 - ADAM_B2 ** ADAM_STEP)
        delta = -ADAM_LR * (m_hat / (jnp.sqrt(v_hat) + ADAM_EPS) + ADAM_WD * wv)
        return delta, m2, v2

    return _tiled(fn, [_rows(a, t) for a in (w, g, m, v)], [_orows(r, wd, F32, t)] * 3, r // t, name)


def kernel(x, p, norm_g, final_g, rel_bias, swa_w_in, swa_w_out, swa_sinks, conv_w_in, conv_kernel, conv_w_out, ssm_w_in, ssm_lam_re, ssm_lam_im, ssm_log_dt, ssm_b_re, ssm_b_im, ssm_c_re, ssm_c_im, ssm_d, ssm_w_glu, ssm_b_glu, ssm_w_out, fox_w_in, fox_w_fg, fox_b_fg, fox_w_out, ple_proj, ple_norm, ple_gate, loss_target, m_norm_g, m_final_g, m_rel_bias, m_swa_w_in, m_swa_w_out, m_swa_sinks, m_conv_w_in, m_conv_kernel, m_conv_w_out, m_ssm_w_in, m_ssm_lam_re, m_ssm_lam_im, m_ssm_log_dt, m_ssm_b_re, m_ssm_b_im, m_ssm_c_re, m_ssm_c_im, m_ssm_d, m_ssm_w_glu, m_ssm_b_glu, m_ssm_w_out, m_fox_w_in, m_fox_w_fg, m_fox_b_fg, m_fox_w_out, m_ple_proj, m_ple_norm, m_ple_gate, v_norm_g, v_final_g, v_rel_bias, v_swa_w_in, v_swa_w_out, v_swa_sinks, v_conv_w_in, v_conv_kernel, v_conv_w_out, v_ssm_w_in, v_ssm_lam_re, v_ssm_lam_im, v_ssm_log_dt, v_ssm_b_re, v_ssm_b_im, v_ssm_c_re, v_ssm_c_im, v_ssm_d, v_ssm_w_glu, v_ssm_b_glu, v_ssm_w_out, v_fox_w_in, v_fox_w_fg, v_fox_b_fg, v_fox_w_out, v_ple_proj, v_ple_norm, v_ple_gate):
    given = dict(locals())
    shard_shape = {n: given[n].shape for n in WEIGHTS}
    half = {n: math.prod(shard_shape[n]) // 2 for n in WEIGHTS}

    widths = sorted({shard_shape[n][-1] for n in BIG}, reverse=True)
    classes = [[n for n in BIG if shard_shape[n][-1] == w] for w in widths]
    rows = {n: math.prod(shard_shape[n][:-1]) for n in BIG}
    core = lax.axis_index("c")
    my_chip = 2 * lax.axis_index("x") + lax.axis_index("y")

    bufs = [jnp.concatenate([given[n].astype(BF16).reshape(rows[n], w) for n in names], axis=0).reshape(2, -1, w)
            for w, names in zip(widths, classes)]
    ssh = _flat([given[n].reshape(-1) for n in SMALL], F32, (), SMALL_ROWS)
    walls, sall = _gather_weights(bufs, ssh)
    walls = [lax.dynamic_update_slice(a, b[None], (my_chip, 0, 0, 0)) for a, b in zip(walls, bufs)]
    sall = lax.dynamic_update_slice(sall, ssh[None], (my_chip, 0, 0))
    full = {n: given[n] for n in REPLICATED}
    for w, names, wall in zip(widths, classes, walls):
        wall, off = wall.reshape(N_CHIPS, -1, w), 0
        for n in names:
            full[n] = _join_shards(wall[:, off:off + rows[n]].reshape((N_CHIPS,) + shard_shape[n]), BIG[n])
            off += rows[n]
    for n, piece in zip(SMALL, _unflat(sall, 1, [2 * half[n] for n in SMALL])):
        full[n] = _join_shards(piece.reshape((N_CHIPS,) + shard_shape[n]), SMALL[n])

    loss, dx, grads = _local_step(x[0], p[:, 0], loss_target[0], full)

    gbufs = [jnp.concatenate([_split_shards(grads[n], BIG[n]).reshape(N_CHIPS, rows[n], w) for n in names], axis=1)
             .reshape(N_DEV, -1, w) for w, names in zip(widths, classes)]
    gsmall = _flat([_split_shards(grads[n], SMALL[n]).reshape(N_DEV, -1) for n in SMALL], F32, (N_DEV,), SMALL_ROWS)
    grepl = _flat([grads[n].reshape(-1) for n in REPLICATED], F32, (), REPL_ROWS)
    pbufs = [_pair_sum(jnp.where(core == 0, g[0::2], g[1::2]), t, f"pair_sum_w{w}")
             for w, g, t in zip(widths, gbufs, _pair_exchange(gbufs))]
    recv_bufs, recv_small, recv_repl = _exchange_grads(pbufs, gsmall, grepl)
    recv_bufs = [lax.dynamic_update_slice(r, lax.dynamic_index_in_dim(pb, my_chip, axis=0), (my_chip, 0, 0))
                 for r, pb in zip(recv_bufs, pbufs)]
    halves = [_sum_senders(r, f"sum_w{w}") for w, r in zip(widths, recv_bufs)] + [_sum_senders(recv_small, "sum_small")]
    others = _sibling_exchange(halves)
    g_repl = _sum_senders(recv_repl, "sum_repl")
    both = [(jnp.where(core == 0, a, b), jnp.where(core == 0, b, a)) for a, b in zip(halves, others)]
    sfull = jnp.stack(both[-1])

    out_g, out_d, out_m, out_v = {}, {}, {}, {}
    for w, names, (lower, upper) in zip(widths, classes, both):
        gfull, off = jnp.concatenate([lower, upper], axis=0), 0
        for n in names:
            as_rows = lambda t, w=w: t.reshape(-1, w)
            piece = gfull[off:off + rows[n]]
            off += rows[n]
            d, m2, v2 = _adamw(as_rows(given[n]), piece, as_rows(given['m_' + n]), as_rows(given['v_' + n]), f"adamw_{n}")
            out_g[n], out_d[n], out_m[n], out_v[n] = (t.reshape(shard_shape[n]) for t in (piece, d, m2, v2))

    pack_small = lambda pre: _flat([given[pre + n].reshape(2, -1) for n in SMALL], F32, (2,), SMALL_ROWS).reshape(-1, FLAT_W)
    res = _adamw(pack_small(''), sfull.reshape(-1, FLAT_W), pack_small('m_'), pack_small('v_'), "adamw_small")
    for dst, flat in zip((out_g, out_d, out_m, out_v), (sfull,) + tuple(res)):
        for n, piece in zip(SMALL, _unflat(flat.reshape(2, -1, FLAT_W), 1, [half[n] for n in SMALL])):
            dst[n] = piece.reshape(shard_shape[n])

    pack_repl = lambda pre: _flat([given[pre + n].reshape(-1) for n in REPLICATED], F32, (), REPL_ROWS)
    res = _adamw(pack_repl(''), g_repl, pack_repl('m_'), pack_repl('v_'), "adamw_repl")
    for dst, flat in zip((out_g, out_d, out_m, out_v), (g_repl,) + tuple(res)):
        for n, piece in zip(REPLICATED, _unflat(flat, 0, [2 * half[n] for n in REPLICATED])):
            dst[n] = piece.reshape(shard_shape[n])

    total = lax.psum(loss[0, 0], ("x", "y", "c"))
    return (total, dx[None], *[out_g[n] for n in WEIGHTS], *[out_d[n] for n in WEIGHTS],
            *[out_m[n] for n in WEIGHTS], *[out_v[n] for n in WEIGHTS])
```

```python
import functools
import math

import numpy as np
import jax
import jax.numpy as jnp
from jax import lax
from jax.experimental import pallas as pl
from jax.experimental.pallas import tpu as pltpu

F32 = jnp.float32
BF16 = jnp.bfloat16

EPS = 1e-6
BLOCK = 128
REL_BUCKETS = 32
REL_MAX_DIST = 128
SWA_HEADS, SWA_KV_HEADS, HEAD_DIM = 32, 4, 64
SWA_GROUP = SWA_HEADS // SWA_KV_HEADS
FOX_HEADS = 32
SSM_GROUP, SSM_STATE = 16, 64
GROUPS_PER_STEP = 8
STATE_W = GROUPS_PER_STEP * SSM_STATE
LANES = 128
NEG = -1e30

ADAM_LR, ADAM_B1, ADAM_B2, ADAM_EPS, ADAM_WD, ADAM_STEP = 0.001, 0.9, 0.999, 1e-08, 0.01, 10

VMEM_LIMIT_V7X = 56 * 1024 * 1024
FLAT_W = 1024
MESH = pl.DeviceIdType.MESH
ANY = pl.BlockSpec(memory_space=pl.ANY)


def _cp(sem):
    return pltpu.CompilerParams(dimension_semantics=sem, vmem_limit_bytes=VMEM_LIMIT_V7X)


def _tile(n, target, mult=LANES):
    if n <= target:
        return n
    t = (target // mult) * mult
    while t >= mult:
        if n % t == 0:
            return t
        t -= mult
    return n


MAX_WHOLE_TILE = 1152


def _div(a, b):
    return lax.div(a, jnp.int32(b))


def _rem(a, b):
    return lax.rem(a, jnp.int32(b))


def _mm(a, b, *, ta=False, tb=False, out_dtype=BF16, name, tm=1024, tn=1024, tk=1024, b_split=False, out_split=0):
    if ta:
        kdim, m = a.shape
    else:
        m, kdim = a.shape
    parts = b.shape[0] if b_split else 1
    b_rows, b_cols = (b.shape[1], b.shape[2] * parts) if b_split else b.shape
    n = b_rows if tb else b_cols
    assert (b_cols if tb else b_rows) == kdim
    n_range = n // (out_split or (1 if tb else parts))
    k_range = kdim // (parts if tb else 1)
    tm, tk = _tile(m, tm), _tile(k_range, tk)
    tn = n_range if n_range <= MAX_WHOLE_TILE else _tile(n_range, tn)
    nk = kdim // tk
    dn = (((0 if ta else 1,), (1 if tb else 0,)), ((), ()))

    def body(a_ref, b_ref, o_ref, acc_ref):
        k = pl.program_id(2)

        @pl.when(k == 0)
        def _():
            acc_ref[...] = jnp.zeros_like(acc_ref)

        acc_ref[...] += lax.dot_general(a_ref[...].astype(BF16), b_ref[...].astype(BF16), dn,
                                        preferred_element_type=F32)

        @pl.when(k == nk - 1)
        def _():
            o_ref[...] = acc_ref[...].astype(o_ref.dtype)

    a_spec = pl.BlockSpec((tk, tm), lambda i, j, k: (k, i)) if ta else pl.BlockSpec((tm, tk), lambda i, j, k: (i, k))
    nj, nkr = n_range // tn, k_range // tk
    if not b_split:
        b_spec = pl.BlockSpec((tn, tk), lambda i, j, k: (j, k)) if tb else pl.BlockSpec((tk, tn), lambda i, j, k: (k, j))
    elif tb:
        b_spec = pl.BlockSpec((None, tn, tk), lambda i, j, k: (_div(k, nkr), j, _rem(k, nkr)))
    else:
        b_spec = pl.BlockSpec((None, tk, tn), lambda i, j, k: (_div(j, nj), k, _rem(j, nj)))
    if out_split:
        out_spec = pl.BlockSpec((None, tm, tn), lambda i, j, k: (_div(j, nj), i, _rem(j, nj)))
        out_shape = jax.ShapeDtypeStruct((out_split, m, n_range), out_dtype)
    else:
        out_spec = pl.BlockSpec((tm, tn), lambda i, j, k: (i, j))
        out_shape = jax.ShapeDtypeStruct((m, n), out_dtype)
    return pl.pallas_call(
        body, name=name, grid=(m // tm, n // tn, nk),
        in_specs=[a_spec, b_spec], out_specs=out_spec, out_shape=out_shape,
        scratch_shapes=[pltpu.VMEM((tm, tn), F32)],
        compiler_params=_cp(("parallel", "parallel", "arbitrary")),
    )(a, b)


def _rows(arr, t):
    return (arr, (t, arr.shape[1]), lambda i: (i, 0))


def _cols(arr, cb, off=0):
    return (arr, (arr.shape[0], cb), lambda i: (0, i + off))


def _full(arr):
    nd = arr.ndim
    return (arr, arr.shape, lambda i: (0,) * nd)


def _lead(arr):
    return (arr, (1,) + arr.shape[1:], lambda i: (i, 0, 0))


def _out(shape, dtype, block, imap, acc=False):
    return (jax.ShapeDtypeStruct(shape, dtype), block, imap, acc)


def _orows(s, w, dtype, t):
    return _out((s, w), dtype, (t, w), lambda i: (i, 0))


def _ocols(s, w, dtype, cb):
    return _out((s, w), dtype, (s, cb), lambda i: (0, i))


def _oacc(shape):
    nd = len(shape)
    return _out(shape, F32, shape, lambda i: (0,) * nd, True)


def _olead(n, a, b, dtype=F32):
    return _out((n, a, b), dtype, (1, a, b), lambda i: (i, 0, 0))


def _tiled(fn, ins, outs, n, name):
    has_acc = any(o[3] for o in outs)
    ni = len(ins)

    def body(*refs):
        vals = fn(*[r[...] for r in refs[:ni]])
        i = pl.program_id(0)
        for r, v, o in zip(refs[ni:], vals, outs):
            if o[3]:
                @pl.when(i == 0)
                def _(r=r, v=v):
                    r[...] = v.astype(r.dtype)

                @pl.when(i > 0)
                def _(r=r, v=v):
                    r[...] += v.astype(r.dtype)
            else:
                r[...] = v.astype(r.dtype)

    res = pl.pallas_call(
        body, name=name, grid=(n,),
        in_specs=[pl.BlockSpec(b, m) for _, b, m in ins],
        out_specs=[pl.BlockSpec(b, m) for _, b, m, _ in outs],
        out_shape=[s for s, _, _, _ in outs],
        compiler_params=_cp(("arbitrary",) if has_acc else ("parallel",)),
    )(*[a for a, _, _ in ins])
    return res


def _silu(x):
    return x * jax.nn.sigmoid(x)


def _gelu(x):
    return 0.5 * x * (1.0 + jnp.tanh(math.sqrt(2.0 / math.pi) * (x + 0.044715 * (x * x * x))))


def _rms(x, g):
    r = lax.rsqrt(jnp.mean(x * x, axis=-1, keepdims=True) + EPS)
    return x * r * g


def _shift_rows(x, sh, up=False):
    s = x.shape[0]
    rows = lax.broadcasted_iota(jnp.int32, x.shape, 0)
    if up:
        return jnp.where(rows < s - sh, pltpu.roll(x, s - sh, 0), 0.0)
    return jnp.where(rows >= sh, pltpu.roll(x, sh, 0), 0.0)


def _scan_complex(xr, xi, pr, pi, reverse):
    s = xr.shape[0]
    k = 0
    while (1 << k) < s:
        sh = 1 << k
        sr, si = _shift_rows(xr, sh, reverse), _shift_rows(xi, sh, reverse)
        ar, ai = pr[k:k + 1, :], pi[k:k + 1, :]
        xr, xi = xr + ar * sr - ai * si, xi + ar * si + ai * sr
        k += 1
    return xr, xi


def _cumsum_rows(x, reverse):
    s = x.shape[0]
    k = 0
    while (1 << k) < s:
        x = x + _shift_rows(x, 1 << k, reverse)
        k += 1
    return x


def _dot(a, b, ca=1, cb=0):
    return lax.dot_general(a.astype(BF16), b.astype(BF16), (((ca,), (cb,)), ((), ())), preferred_element_type=F32)


def _t5_bucket(dist):
    max_exact = REL_BUCKETS // 2
    d = np.maximum(dist, 1).astype(np.float32)
    large = max_exact + (np.log(d / max_exact) / np.log(REL_MAX_DIST / max_exact) * (REL_BUCKETS - max_exact)).astype(np.int32)
    large = np.minimum(large, REL_BUCKETS - 1)
    return np.where(dist < max_exact, dist, large).astype(np.int32)


def _swa_bucket_table():
    qi = np.arange(BLOCK)[:, None]
    kj = np.arange(2 * BLOCK)[None, :]
    return _t5_bucket(np.clip(qi + BLOCK - kj, 0, None))


def _swa_scores(qg, kb, bias_g, sk, n):
    s = _dot(qg, kb, 1, 1) * (HEAD_DIM ** -0.5) + bias_g
    row = lax.broadcasted_iota(jnp.int32, s.shape, 0)
    col = lax.broadcasted_iota(jnp.int32, s.shape, 1)
    dist = row + BLOCK - col
    mask = (dist >= 0) & (dist < BLOCK) & ((col >= BLOCK) | (n > 0))
    s = jnp.where(mask, s, NEG)
    m = jnp.maximum(jnp.max(s, axis=1, keepdims=True), sk)
    e = jnp.exp(s - m)
    es = jnp.exp(sk - m)
    den = jnp.sum(e, axis=1, keepdims=True) + es
    return e / den, es / den


def _swa_specs(proj, k4, v4, nb, clamp):
    gw = SWA_GROUP * HEAD_DIM
    gate_off = (SWA_HEADS * HEAD_DIM + 2 * SWA_KV_HEADS * HEAD_DIM) // gw
    cur = (lambda n: jnp.minimum(n, nb - 1)) if clamp else (lambda n: n)
    prev = lambda n: jnp.maximum(cur(n) - 1, 0)
    return [
        pl.BlockSpec((BLOCK, gw), lambda h, n: (cur(n), h)),
        pl.BlockSpec((1, BLOCK, HEAD_DIM), lambda h, n: (h, cur(n), 0)),
        pl.BlockSpec((1, BLOCK, HEAD_DIM), lambda h, n: (h, prev(n), 0)),
        pl.BlockSpec((1, BLOCK, HEAD_DIM), lambda h, n: (h, cur(n), 0)),
        pl.BlockSpec((1, BLOCK, HEAD_DIM), lambda h, n: (h, prev(n), 0)),
        pl.BlockSpec((BLOCK, gw), lambda h, n: (cur(n), gate_off + h)),
    ], [proj, k4, k4, v4, v4, proj]


def _swa_fwd(proj, k4, v4, bias, sinks):
    s = proj.shape[0]
    nb = s // BLOCK
    gw = SWA_GROUP * HEAD_DIM

    def body(q_ref, kc_ref, kp_ref, vc_ref, vp_ref, gate_ref, bias_ref, sink_ref, a_ref):
        n = pl.program_id(1)
        kb = jnp.concatenate([kp_ref[0], kc_ref[0]], axis=0)
        vb = jnp.concatenate([vp_ref[0], vc_ref[0]], axis=0)
        for g in range(SWA_GROUP):
            sl = slice(g * HEAD_DIM, (g + 1) * HEAD_DIM)
            p, _ = _swa_scores(q_ref[:, sl], kb, bias_ref[g], sink_ref[0, g:g + 1, :1], n)
            og = _dot(p, vb)
            a_ref[:, sl] = (og * _silu(gate_ref[:, sl].astype(F32))).astype(a_ref.dtype)

    specs, args = _swa_specs(proj, k4, v4, nb, False)
    return pl.pallas_call(
        body, name="swa_fwd", grid=(SWA_KV_HEADS, nb),
        in_specs=specs + [pl.BlockSpec((SWA_GROUP, BLOCK, 2 * BLOCK), lambda h, n: (h, 0, 0)),
                          pl.BlockSpec((1, SWA_GROUP, LANES), lambda h, n: (h, 0, 0))],
        out_specs=pl.BlockSpec((BLOCK, gw), lambda h, n: (n, h)),
        out_shape=jax.ShapeDtypeStruct((s, SWA_HEADS * HEAD_DIM), BF16),
        compiler_params=_cp(("parallel", "parallel")),
    )(*args, bias, sinks)


def _swa_bwd(proj, k4, v4, bias, sinks, da):
    s = proj.shape[0]
    nb = s // BLOCK
    gw = SWA_GROUP * HEAD_DIM

    def body(q_ref, kc_ref, kp_ref, vc_ref, vp_ref, gate_ref, bias_ref, sink_ref, da_ref,
             dq_ref, dgate_ref, dk_ref, dv_ref, dbias_ref, dsink_ref, dk_own, dv_own):
        n = pl.program_id(1)

        @pl.when(n == 0)
        def _():
            dk_own[...] = jnp.zeros_like(dk_own)
            dv_own[...] = jnp.zeros_like(dv_own)
            dbias_ref[...] = jnp.zeros_like(dbias_ref)
            dsink_ref[...] = jnp.zeros_like(dsink_ref)

        @pl.when(n < nb)
        def _():
            kb = jnp.concatenate([kp_ref[0], kc_ref[0]], axis=0)
            vb = jnp.concatenate([vp_ref[0], vc_ref[0]], axis=0)
            dkb = jnp.zeros((2 * BLOCK, HEAD_DIM), F32)
            dvb = jnp.zeros((2 * BLOCK, HEAD_DIM), F32)
            for g in range(SWA_GROUP):
                sl = slice(g * HEAD_DIM, (g + 1) * HEAD_DIM)
                qg = q_ref[:, sl]
                p, p0 = _swa_scores(qg, kb, bias_ref[g], sink_ref[0, g:g + 1, :1], n)
                og = _dot(p, vb)
                gate = gate_ref[:, sl].astype(F32)
                dag = da_ref[:, sl].astype(F32)
                sg = jax.nn.sigmoid(gate)
                do = dag * gate * sg
                dgate_ref[:, sl] = (dag * og * sg * (1.0 + gate * (1.0 - sg))).astype(dgate_ref.dtype)
                dp = _dot(do, vb, 1, 1)
                delta = jnp.sum(do * og, axis=1, keepdims=True)
                ds = p * (dp - delta)
                dbias_ref[g] += ds
                dsink_ref[0, g:g + 1, :] += jnp.zeros((1, LANES), F32) - jnp.sum(p0 * delta, axis=0, keepdims=True)
                dq_ref[:, sl] = (_dot(ds, kb) * (HEAD_DIM ** -0.5)).astype(dq_ref.dtype)
                dkb += _dot(ds, qg, 0, 0) * (HEAD_DIM ** -0.5)
                dvb += _dot(p, do, 0, 0)
            dk_ref[0] = dk_own[...] + dkb[:BLOCK]
            dv_ref[0] = dv_own[...] + dvb[:BLOCK]
            dk_own[...] = dkb[BLOCK:]
            dv_own[...] = dvb[BLOCK:]

        @pl.when(n == nb)
        def _():
            dk_ref[0] = dk_own[...]
            dv_ref[0] = dv_own[...]

    specs, args = _swa_specs(proj, k4, v4, nb, True)
    cur = lambda n: jnp.minimum(n, nb - 1)
    trail = lambda n: jnp.maximum(n - 1, 0)
    return pl.pallas_call(
        body, name="swa_bwd", grid=(SWA_KV_HEADS, nb + 1),
        in_specs=specs + [pl.BlockSpec((SWA_GROUP, BLOCK, 2 * BLOCK), lambda h, n: (h, 0, 0)),
                          pl.BlockSpec((1, SWA_GROUP, LANES), lambda h, n: (h, 0, 0)),
                          pl.BlockSpec((BLOCK, gw), lambda h, n: (cur(n), h))],
        out_specs=[pl.BlockSpec((BLOCK, gw), lambda h, n: (cur(n), h)),
                   pl.BlockSpec((BLOCK, gw), lambda h, n: (cur(n), h)),
                   pl.BlockSpec((1, BLOCK, HEAD_DIM), lambda h, n: (h, trail(n), 0)),
                   pl.BlockSpec((1, BLOCK, HEAD_DIM), lambda h, n: (h, trail(n), 0)),
                   pl.BlockSpec((SWA_GROUP, BLOCK, 2 * BLOCK), lambda h, n: (h, 0, 0)),
                   pl.BlockSpec((1, SWA_GROUP, LANES), lambda h, n: (h, 0, 0))],
        out_shape=[jax.ShapeDtypeStruct((s, SWA_HEADS * HEAD_DIM), BF16),
                   jax.ShapeDtypeStruct((s, SWA_HEADS * HEAD_DIM), BF16),
                   jax.ShapeDtypeStruct((SWA_KV_HEADS, s, HEAD_DIM), F32),
                   jax.ShapeDtypeStruct((SWA_KV_HEADS, s, HEAD_DIM), F32),
                   jax.ShapeDtypeStruct((SWA_HEADS, BLOCK, 2 * BLOCK), F32),
                   jax.ShapeDtypeStruct((SWA_KV_HEADS, SWA_GROUP, LANES), F32)],
        scratch_shapes=[pltpu.VMEM((BLOCK, HEAD_DIM), F32), pltpu.VMEM((BLOCK, HEAD_DIM), F32)],
        compiler_params=_cp(("arbitrary", "arbitrary")),
    )(*args, bias, sinks, da)


FOX_TILE = 1024


def _below_and_on_diagonal(i, j, step):
    @pl.when(j < i)
    def _():
        step(False)

    @pl.when(j == i)
    def _():
        step(True)


FOX_AUG = 128
FOX_CQ, FOX_CK = HEAD_DIM, HEAD_DIM + 1


def _fox_operands(proj, csum):
    s = proj.shape[0]
    hw = FOX_HEADS * HEAD_DIM
    heads = lambda a: a.reshape(s, FOX_HEADS, HEAD_DIM).transpose(1, 0, 2)
    q = heads(proj[:, :hw]) * jnp.asarray(HEAD_DIM ** -0.5, BF16)
    k, v = heads(proj[:, hw:2 * hw]), heads(proj[:, 2 * hw:3 * hw])
    one = jnp.ones((FOX_HEADS, s, 1), BF16)
    zero = jnp.zeros((FOX_HEADS, s, 1), BF16)
    pad = jnp.zeros((FOX_HEADS, s, FOX_AUG - HEAD_DIM - 2), BF16)
    qa = jnp.concatenate([q, zero, -one, pad], axis=-1)
    ka = jnp.concatenate([k, one, zero, pad], axis=-1)
    tr = lambda a: a.transpose(0, 2, 1)
    c = csum[:, :FOX_HEADS].T
    return (qa, ka, tr(qa), tr(ka), v, tr(v), c[:, None, :], jnp.broadcast_to(c[:, :, None], (FOX_HEADS, s, LANES)))


def _fox_scores_t(ka, qa, cq, ck, diag):
    st = _dot(ka, qa, 1, 1) + cq - jnp.concatenate([ck] * (qa.shape[0] // LANES), axis=1)
    if diag:
        st = jnp.where(lax.broadcasted_iota(jnp.int32, st.shape, 0) <= lax.broadcasted_iota(jnp.int32, st.shape, 1), st, NEG)
    return st


def _fox_attention(qa, ka, vt, c_row, c_lanes):
    nh, s, aw = qa.shape
    t = min(FOX_TILE, s)
    nt = s // t
    hd = HEAD_DIM

    def body(qa_ref, ka_ref, vt_ref, cq_ref, ck_ref, o_ref, lse_ref, m_s, l_s, acc_s):
        i, j = pl.program_id(1), pl.program_id(2)

        @pl.when(j == 0)
        def _():
            m_s[...] = jnp.full_like(m_s, NEG)
            l_s[...] = jnp.zeros_like(l_s)
            acc_s[...] = jnp.zeros_like(acc_s)

        def step(diag):
            st = _fox_scores_t(ka_ref[0], qa_ref[0], cq_ref[0], ck_ref[0], diag)
            m_old = m_s[...]
            m_new = jnp.maximum(m_old, jnp.max(st, axis=0, keepdims=True))
            alpha = jnp.exp(m_old - m_new)
            p = jnp.exp(st - m_new)
            l_s[...] = alpha * l_s[...] + jnp.sum(p, axis=0, keepdims=True)
            acc_s[...] = alpha * acc_s[...] + _dot(vt_ref[0], p)
            m_s[...] = m_new

        _below_and_on_diagonal(i, j, step)

        @pl.when(j == nt - 1)
        def _():
            o_ref[0] = (acc_s[...] / l_s[...]).astype(o_ref.dtype)
            lse_ref[0] = m_s[...] + jnp.log(l_s[...])

    kj = lambda i, j: jnp.minimum(j, i)
    return pl.pallas_call(
        body, name="fox_fwd", grid=(nh, nt, nt),
        in_specs=[pl.BlockSpec((1, t, aw), lambda h, i, j: (h, i, 0)),
                  pl.BlockSpec((1, t, aw), lambda h, i, j: (h, kj(i, j), 0)),
                  pl.BlockSpec((1, hd, t), lambda h, i, j: (h, 0, kj(i, j))),
                  pl.BlockSpec((1, 1, t), lambda h, i, j: (h, 0, i)),
                  pl.BlockSpec((1, t, LANES), lambda h, i, j: (h, kj(i, j), 0))],
        out_specs=[pl.BlockSpec((1, hd, t), lambda h, i, j: (h, 0, i)),
                   pl.BlockSpec((1, 1, t), lambda h, i, j: (h, 0, i))],
        out_shape=[jax.ShapeDtypeStruct((nh, hd, s), BF16), jax.ShapeDtypeStruct((nh, 1, s), F32)],
        scratch_shapes=[pltpu.VMEM((1, t), F32), pltpu.VMEM((1, t), F32), pltpu.VMEM((hd, t), F32)],
        compiler_params=_cp(("parallel", "parallel", "arbitrary")),
    )(qa, ka, vt, c_row, c_lanes)


def _fox_attention_bwd(qa, ka, qat, kat, v, c_row, c_lanes, ot, dot, lse):
    nh, s, aw = qa.shape
    t = min(FOX_TILE, s)
    nt = s // t
    hd = HEAD_DIM

    def body(qa_ref, ka_ref, qat_ref, kat_ref, v_ref, cq_ref, ck_ref, ot_ref, dot_ref, lse_ref,
             dq_ref, dcq_ref, dk_ref, dck_ref, dv_ref, dqa_s, dk_s, dv_s):
        j, i = pl.program_id(1), pl.program_id(2)

        @pl.when((j == 0) & (i == 0))
        def _():
            dqa_s[...] = jnp.zeros_like(dqa_s)

        @pl.when(i == 0)
        def _():
            dk_s[...] = jnp.zeros_like(dk_s)
            dv_s[...] = jnp.zeros_like(dv_s)

        def step(diag):
            p = jnp.exp(_fox_scores_t(ka_ref[0], qa_ref[0], cq_ref[0], ck_ref[0], diag) - lse_ref[0])
            do_t = dot_ref[0]
            delta = jnp.sum(do_t.astype(F32) * ot_ref[0].astype(F32), axis=0, keepdims=True)
            ds = (p * (_dot(v_ref[0], do_t) - delta)).astype(BF16)
            dv_s[...] += _dot(do_t, p, 1, 1)
            dk_s[...] += _dot(qat_ref[0], ds, 1, 1)
            cols = pl.ds(pl.multiple_of(i * t, t), t)
            dqa_s[:, cols] += _dot(kat_ref[0], ds)

        _below_and_on_diagonal(i, j, step)

        @pl.when(i == nt - 1)
        def _():
            dk_ref[0] = dk_s[:hd].astype(dk_ref.dtype)
            dck_ref[0] = dk_s[FOX_CK:FOX_CK + 1]
            dv_ref[0] = dv_s[...].astype(dv_ref.dtype)

        @pl.when((j == nt - 1) & (i == nt - 1))
        def _():
            dq_ref[0] = (dqa_s[:hd] * (hd ** -0.5)).astype(dq_ref.dtype)
            dcq_ref[0] = dqa_s[FOX_CQ:FOX_CQ + 1]

    qi = lambda i, j: jnp.maximum(i, j)
    return pl.pallas_call(
        body, name="fox_bwd", grid=(nh, nt, nt),
        in_specs=[pl.BlockSpec((1, t, aw), lambda h, j, i: (h, qi(i, j), 0)),
                  pl.BlockSpec((1, t, aw), lambda h, j, i: (h, j, 0)),
                  pl.BlockSpec((1, aw, t), lambda h, j, i: (h, 0, qi(i, j))),
                  pl.BlockSpec((1, aw, t), lambda h, j, i: (h, 0, j)),
                  pl.BlockSpec((1, t, hd), lambda h, j, i: (h, j, 0)),
                  pl.BlockSpec((1, 1, t), lambda h, j, i: (h, 0, qi(i, j))),
                  pl.BlockSpec((1, t, LANES), lambda h, j, i: (h, j, 0)),
                  pl.BlockSpec((1, hd, t), lambda h, j, i: (h, 0, qi(i, j))),
                  pl.BlockSpec((1, hd, t), lambda h, j, i: (h, 0, qi(i, j))),
                  pl.BlockSpec((1, 1, t), lambda h, j, i: (h, 0, qi(i, j)))],
        out_specs=[pl.BlockSpec((1, hd, s), lambda h, j, i: (h, 0, 0)),
                   pl.BlockSpec((1, 1, s), lambda h, j, i: (h, 0, 0)),
                   pl.BlockSpec((1, hd, t), lambda h, j, i: (h, 0, j)),
                   pl.BlockSpec((1, 1, t), lambda h, j, i: (h, 0, j)),
                   pl.BlockSpec((1, hd, t), lambda h, j, i: (h, 0, j))],
        out_shape=[jax.ShapeDtypeStruct((nh, hd, s), BF16), jax.ShapeDtypeStruct((nh, 1, s), F32),
                   jax.ShapeDtypeStruct((nh, hd, s), BF16), jax.ShapeDtypeStruct((nh, 1, s), F32),
                   jax.ShapeDtypeStruct((nh, hd, s), BF16)],
        scratch_shapes=[pltpu.VMEM((aw, s), F32), pltpu.VMEM((aw, t), F32), pltpu.VMEM((hd, t), F32)],
        compiler_params=_cp(("parallel", "arbitrary", "arbitrary")),
    )(qa, ka, qat, kat, v, c_row, c_lanes, ot, dot, lse)


def _ssm_prep(lam_re, lam_im, log_dt, b_re, b_im, c_re, c_im):
    g, n = lam_re.shape
    dt = jnp.exp(log_dt)[:, None]
    mag = jnp.exp(lam_re * dt)
    ab_re = mag * jnp.cos(lam_im * dt)
    ab_im = mag * jnp.sin(lam_im * dt)
    den = lam_re * lam_re + lam_im * lam_im
    nr = ab_re - 1.0
    coef_re = ((nr * lam_re + ab_im * lam_im) / den)[..., None]
    coef_im = ((ab_im * lam_re - nr * lam_im) / den)[..., None]
    bb_re = coef_re * b_re - coef_im * b_im
    bb_im = coef_re * b_im + coef_im * b_re
    nblk = g // GROUPS_PER_STEP
    eye = jnp.eye(GROUPS_PER_STEP, dtype=F32)

    def bdiag(bb):
        return jnp.einsum('bgnc,gh->bgchn', bb.reshape(nblk, GROUPS_PER_STEP, n, SSM_GROUP), eye).reshape(
            nblk, GROUPS_PER_STEP * SSM_GROUP, STATE_W)

    def cdiag(cc):
        return jnp.einsum('bgcn,gh->bgnhc', cc.reshape(nblk, GROUPS_PER_STEP, SSM_GROUP, n), eye).reshape(
            nblk, STATE_W, GROUPS_PER_STEP * SSM_GROUP)

    return (ab_re.reshape(nblk, 1, STATE_W), ab_im.reshape(nblk, 1, STATE_W),
            bdiag(bb_re), bdiag(bb_im), cdiag(c_re), cdiag(c_im))


def _powers(a_re, a_im, levels):
    rs, ims = [a_re], [a_im]
    for _ in range(levels - 1):
        r, i = rs[-1], ims[-1]
        rs.append(r * r - i * i)
        ims.append(2.0 * r * i)
    return jnp.concatenate(rs, axis=1), jnp.concatenate(ims, axis=1)


def _ssm_fwd(proj, pw_re, pw_im, bd_re, bd_im, cd_re, cd_im, dskip):
    s = proj.shape[0]
    w = dskip.shape[1]
    nblk = w // LANES
    nch = STATE_W // LANES
    levels = pw_re.shape[1]

    def body(u_ref, pr, pi, br, bi, cr, ci, d_ref, pre_ref, yg_ref, xr_ref, xi_ref):
        ch = pl.program_id(1)
        u = u_ref[...]
        xr, xi = _scan_complex(_dot(u, br[0]), _dot(u, bi[0]), pr[0], pi[0], False)
        xr_ref[...] = xr.astype(BF16)
        xi_ref[...] = xi.astype(BF16)
        yc = _dot(xr, cr[0]) - _dot(xi, ci[0])

        @pl.when(ch == 0)
        def _():
            pre_ref[...] = yc + d_ref[...] * u

        @pl.when(ch > 0)
        def _():
            pre_ref[...] += yc

        @pl.when(ch == nch - 1)
        def _():
            yg_ref[...] = _gelu(pre_ref[...]).astype(BF16)

    blk = lambda b, c: (0, b)
    col = lambda b, c: (0, b * nch + c)
    in_chunk = pl.BlockSpec((1, LANES, LANES), lambda b, c: (b, 0, c))
    out_chunk = pl.BlockSpec((1, LANES, LANES), lambda b, c: (b, c, 0))
    pw_spec = pl.BlockSpec((1, levels, LANES), lambda b, c: (b, 0, c))
    return pl.pallas_call(
        body, name="ssm_fwd", grid=(nblk, nch),
        in_specs=[pl.BlockSpec((s, LANES), blk), pw_spec, pw_spec, in_chunk, in_chunk, out_chunk, out_chunk,
                  pl.BlockSpec((1, LANES), blk)],
        out_specs=[pl.BlockSpec((s, LANES), blk), pl.BlockSpec((s, LANES), blk),
                   pl.BlockSpec((s, LANES), col), pl.BlockSpec((s, LANES), col)],
        out_shape=[jax.ShapeDtypeStruct((s, w), F32), jax.ShapeDtypeStruct((s, w), BF16),
                   jax.ShapeDtypeStruct((s, nblk * STATE_W), BF16), jax.ShapeDtypeStruct((s, nblk * STATE_W), BF16)],
        compiler_params=_cp(("parallel", "arbitrary")),
    )(proj,pw_re, pw_im, bd_re, bd_im, cd_re, cd_im, dskip)


def _ssm_bwd(proj, dyg, pre, xr_all, xi_all, pw_re, pw_im, bd_re, bd_im, cd_re, cd_im, dskip):
    s = proj.shape[0]
    w = dskip.shape[1]
    nblk = w // LANES
    nch = STATE_W // LANES
    levels = pw_re.shape[1]

    def body(u_ref, dy_ref, pre_ref, xr_ref, xi_ref, pr, pi, br, bi, cr, ci, d_ref,
             du_ref, dd_ref, dar_ref, dai_ref, dbr_ref, dbi_ref, dcr_ref, dci_ref):
        ch = pl.program_id(1)
        u = u_ref[...]
        _, vjp = jax.vjp(_gelu, pre_ref[...])
        dpre = vjp(dy_ref[...].astype(F32))[0]
        zr, zi = _scan_complex(_dot(dpre, cr[0], 1, 1), -_dot(dpre, ci[0], 1, 1), pr[0], -pi[0], True)
        xpr = _shift_rows(xr_ref[...].astype(F32), 1)
        xpi = _shift_rows(xi_ref[...].astype(F32), 1)
        dar_ref[0] = jnp.sum(zr * xpr + zi * xpi, axis=0, keepdims=True)
        dai_ref[0] = jnp.sum(zi * xpr - zr * xpi, axis=0, keepdims=True)
        dcr_ref[0] = _dot(xr_ref[...], dpre, 0, 0)
        dci_ref[0] = -_dot(xi_ref[...], dpre, 0, 0)
        dbr_ref[0] = _dot(u, zr, 0, 0)
        dbi_ref[0] = _dot(u, zi, 0, 0)
        duc = _dot(zr, br[0], 1, 1) + _dot(zi, bi[0], 1, 1)

        @pl.when(ch == 0)
        def _():
            du_ref[...] = duc + dpre * d_ref[...]
            dd_ref[...] = jnp.sum(dpre * u, axis=0, keepdims=True)

        @pl.when(ch > 0)
        def _():
            du_ref[...] += duc

    blk = lambda b, c: (0, b)
    col = lambda b, c: (0, b * nch + c)
    in_chunk = pl.BlockSpec((1, LANES, LANES), lambda b, c: (b, 0, c))
    out_chunk = pl.BlockSpec((1, LANES, LANES), lambda b, c: (b, c, 0))
    pw_spec = pl.BlockSpec((1, levels, LANES), lambda b, c: (b, 0, c))
    a_spec = pl.BlockSpec((1, 1, LANES), lambda b, c: (b, 0, c))
    return pl.pallas_call(
        body, name="ssm_bwd", grid=(nblk, nch),
        in_specs=[pl.BlockSpec((s, LANES), blk), pl.BlockSpec((s, LANES), blk), pl.BlockSpec((s, LANES), blk),
                  pl.BlockSpec((s, LANES), col), pl.BlockSpec((s, LANES), col),
                  pw_spec, pw_spec, in_chunk, in_chunk, out_chunk, out_chunk, pl.BlockSpec((1, LANES), blk)],
        out_specs=[pl.BlockSpec((s, LANES), blk), pl.BlockSpec((1, LANES), blk), a_spec, a_spec,
                   in_chunk, in_chunk, out_chunk, out_chunk],
        out_shape=[jax.ShapeDtypeStruct((s, w), F32), jax.ShapeDtypeStruct((1, w), F32),
                   jax.ShapeDtypeStruct((nblk, 1, STATE_W), F32), jax.ShapeDtypeStruct((nblk, 1, STATE_W), F32),
                   jax.ShapeDtypeStruct((nblk, LANES, STATE_W), F32), jax.ShapeDtypeStruct((nblk, LANES, STATE_W), F32),
                   jax.ShapeDtypeStruct((nblk, STATE_W, LANES), F32), jax.ShapeDtypeStruct((nblk, STATE_W, LANES), F32)],
        compiler_params=_cp(("parallel", "arbitrary")),
    )(proj,dyg, pre, xr_all, xi_all, pw_re, pw_im, bd_re, bd_im, cd_re, cd_im, dskip)


ROW_TILE = 256


def _norm_fwd(x, g, name):
    s, d = x.shape
    t = min(ROW_TILE, s)
    return _tiled(lambda xv, gv: (_rms(xv, gv),), [_rows(x, t), _full(g)], [_orows(s, d, BF16, t)], s // t, name)[0]


def _norm_bwd(x, g, dh_list, dx_in, name):
    s, d = x.shape
    t = min(ROW_TILE, s)
    nh = len(dh_list)

    def fn(xv, gv, dxv, *dhs):
        dh = dhs[0].astype(F32)
        for other in dhs[1:]:
            dh = dh + other.astype(F32)
        _, vjp = jax.vjp(_rms, xv, gv)
        dx, dg = vjp(dh)
        dx = dx + dxv
        return dx, dx, dg

    return _tiled(fn, [_rows(x, t), _full(g), _rows(dx_in, t)] + [_rows(a, t) for a in dh_list],
                  [_orows(s, d, F32, t), _orows(s, d, BF16, t), _oacc((1, d))], s // t, name)


def _ple_fwd(x, y, pn, name):
    s, d = x.shape
    t = min(ROW_TILE, s)

    def fn(xv, yv, gv):
        x1 = xv + yv
        return x1, _rms(x1, gv)

    return _tiled(fn, [_rows(x, t), _rows(y, t), _full(pn)], [_orows(s, d, F32, t), _orows(s, d, BF16, t)], s // t, name)


def _ple_mix(x1, emb, gl, name):
    s, d = x1.shape
    t = min(ROW_TILE, s)
    return _tiled(lambda a, e, g: (a + e * jax.nn.sigmoid(g),), [_rows(x1, t), _rows(emb, t), _rows(gl, t)],
                  [_orows(s, d, F32, t)], s // t, name)[0]


def _ple_mix_bwd(dx2, emb, gl, name):
    s, d = dx2.shape
    t = min(ROW_TILE, s)

    def fn(dx, e, g):
        sg = jax.nn.sigmoid(g)
        return dx * sg, dx * e * sg * (1.0 - sg)

    return _tiled(fn, [_rows(dx2, t), _rows(emb, t), _rows(gl, t)],
                  [_orows(s, d, BF16, t), _orows(s, d, BF16, t)], s // t, name)


def _loss_grad(x, target, g):
    s, d = x.shape
    t = min(ROW_TILE, s)

    def fn(xv, tv, gv):
        def f(xx, gg):
            err = _rms(xx, gg) - tv
            return 0.5 * jnp.sum(jnp.mean(err * err, axis=-1, keepdims=True), axis=0, keepdims=True)

        loss, vjp = jax.vjp(f, xv, gv)
        dx, dg = vjp(jnp.ones((1, 1), F32))
        return loss, dx, dg

    return _tiled(fn, [_rows(x, t), _rows(target, t), _full(g)],
                  [_oacc((1, 1)), _orows(s, d, F32, t), _oacc((1, d))], s // t, "loss_grad")


def _conv_fwd(proj, kern):
    s = proj.shape[0]
    w = kern.shape[1]
    nb = w // LANES

    def fn(bg, cg, u, gate, k):
        z = cg.astype(F32) * u.astype(F32)
        conv = k[2:3] * z + k[1:2] * _shift_rows(z, 1) + k[0:1] * _shift_rows(z, 2)
        return (bg.astype(F32) * conv * _silu(gate.astype(F32)),)

    return _tiled(fn, [_cols(proj, LANES, q * nb) for q in range(4)] + [_cols(kern, LANES)],
                  [_ocols(s, w, BF16, LANES)], nb, "conv_fwd")[0]


def _conv_bwd(proj, kern, da):
    s = proj.shape[0]
    w = kern.shape[1]
    nb = w // LANES

    def fn(bg, cg, u, gate, k, dav):
        bg, cg, u, gate, dav = (a.astype(F32) for a in (bg, cg, u, gate, dav))
        z = cg * u
        z1, z2 = _shift_rows(z, 1), _shift_rows(z, 2)
        conv = k[2:3] * z + k[1:2] * z1 + k[0:1] * z2
        sg = jax.nn.sigmoid(gate)
        dy = dav * gate * sg
        dgate = dav * bg * conv * sg * (1.0 + gate * (1.0 - sg))
        dconv = dy * bg
        dk = jnp.concatenate([jnp.sum(dconv * zz, axis=0, keepdims=True) for zz in (z2, z1, z)], axis=0)
        dz = k[2:3] * dconv + k[1:2] * _shift_rows(dconv, 1, True) + k[0:1] * _shift_rows(dconv, 2, True)
        return dy * conv, dz * u, dz * cg, dgate, dk

    return _tiled(fn, [_cols(proj, LANES, q * nb) for q in range(4)] + [_cols(kern, LANES), _cols(da, LANES)],
                  [_ocols(s, w, BF16, LANES)] * 4 + [_ocols(3, w, F32, LANES)], nb, "conv_bwd")


def _glu_fwd(gl, proj, bglu):
    s, w2 = gl.shape
    w = w2 // 2
    t = min(ROW_TILE, s)

    def fn(glv, gate, b):
        v = glv + b
        return (v[:, :w] * jax.nn.sigmoid(v[:, w:]) * _silu(gate),)

    return _tiled(fn, [_rows(gl, t), (proj, (t, w), lambda i: (i, 1)), _full(bglu)],
                  [_orows(s, w, BF16, t)], s // t, "glu_fwd")[0]


def _glu_bwd(gl, proj, bglu, da):
    s, w2 = gl.shape
    w = w2 // 2
    t = min(ROW_TILE, s)

    def fn(glv, gate, b, dav):
        def f(gg, gt, bb):
            v = gg + bb
            return v[:, :w] * jax.nn.sigmoid(v[:, w:]) * _silu(gt)

        _, vjp = jax.vjp(f, glv, gate, b)
        return vjp(dav.astype(F32))

    return _tiled(fn, [_rows(gl, t), (proj, (t, w), lambda i: (i, 1)), _full(bglu), _rows(da, t)],
                  [_orows(s, w2, BF16, t), _orows(s, w, BF16, t), _oacc((1, w2))], s // t, "glu_bwd")


def _fg_fwd(z, b):
    def fn(zv, bv):
        v = zv + bv
        logf = jnp.minimum(v, 0.0) - jnp.log(1.0 + jnp.exp(-jnp.abs(v)))
        return (_cumsum_rows(logf, False),)

    return _tiled(fn, [_full(z), _full(b)], [_out(z.shape, F32, z.shape, lambda i: (0, 0))], 1, "fg_fwd")[0]


def _fg_bwd(z, b, dcs):
    def fn(zv, bv, dc):
        dz = _cumsum_rows(dc, True) * jax.nn.sigmoid(-(zv + bv))
        return dz, jnp.sum(dz, axis=0, keepdims=True)

    return _tiled(fn, [_full(z), _full(b), _full(dcs)],
                  [_out(z.shape, BF16, z.shape, lambda i: (0, 0)), _out((1, z.shape[1]), F32, (1, z.shape[1]), lambda i: (0, 0))],
                  1, "fg_bwd")


def _fox_gate_bwd(da, o, proj):
    s, w = o.shape
    t = min(ROW_TILE, s)

    def fn(dav, ov, gate):
        dav, ov, gate = dav.astype(F32), ov.astype(F32), gate.astype(F32)
        sg = jax.nn.sigmoid(gate)
        return dav * gate * sg, dav * ov * sg * (1.0 + gate * (1.0 - sg))

    return _tiled(fn, [_rows(da, t), _rows(o, t), (proj, (t, w), lambda i: (i, 3))],
                  [_orows(s, w, BF16, t), _orows(s, w, BF16, t)], s // t, "fox_gate_bwd")


def _fox_gate_fwd(o, proj):
    s, w = o.shape
    t = min(ROW_TILE, s)
    return _tiled(lambda ov, gate: (ov.astype(F32) * _silu(gate.astype(F32)),),
                  [_rows(o, t), (proj, (t, w), lambda i: (i, 3))], [_orows(s, w, BF16, t)], s // t, "fox_gate_fwd")[0]


def _local_step(x, p, target, w):
    s, d = x.shape
    depth = p.shape[0]
    pb = p.astype(BF16)
    grads = {}
    saved = []

    bucket = _swa_bucket_table()
    onehot = np.eye(REL_BUCKETS, dtype=np.float32)[bucket.reshape(-1)]
    bias = _bias_table(w['rel_bias'], jnp.asarray(onehot.T, BF16))
    sinks = jnp.broadcast_to(w['swa_sinks'].reshape(SWA_KV_HEADS, SWA_GROUP, 1), (SWA_KV_HEADS, SWA_GROUP, LANES))
    ssm_params = tuple(w[k][0] for k in ('ssm_lam_re', 'ssm_lam_im', 'ssm_log_dt', 'ssm_b_re', 'ssm_b_im', 'ssm_c_re', 'ssm_c_im'))
    (a_re, a_im, bd_re, bd_im, cd_re, cd_im), ssm_vjp = jax.vjp(_ssm_prep, *ssm_params)
    levels = max(1, (s - 1).bit_length())
    pw_re, pw_im = _powers(a_re, a_im, levels)
    wfg = jnp.pad(w['fox_w_fg'][0], ((0, 0), (0, LANES - FOX_HEADS)))
    bfg = jnp.pad(w['fox_b_fg'], ((0, 0), (0, LANES - FOX_HEADS)))

    def qkv4(a):
        return a.reshape(s, SWA_KV_HEADS, HEAD_DIM).transpose(1, 0, 2)

    def unheads(a):
        return a.transpose(2, 0, 1).reshape(s, FOX_HEADS * HEAD_DIM)

    for i in range(depth):
        mixer = i % 4
        hn = _norm_fwd(x, w['norm_g'][i:i + 1], f"norm_fwd{i}")
        sv = {'x': x, 'hn': hn}
        if mixer == 0:
            proj = _mm(hn, w['swa_w_in'][:, 0], b_split=True, name="swa_in")
            qw = SWA_HEADS * HEAD_DIM
            kvw = SWA_KV_HEADS * HEAD_DIM
            k4, v4 = qkv4(proj[:, qw:qw + kvw]), qkv4(proj[:, qw + kvw:qw + 2 * kvw])
            a = _swa_fwd(proj, k4, v4, bias, sinks)
            sv.update(proj=proj, k4=k4, v4=v4)
            w_out = w['swa_w_out'][0]
        elif mixer == 1:
            proj = _mm(hn, w['conv_w_in'][:, 0], b_split=True, name="conv_in")
            a = _conv_fwd(proj, w['conv_kernel'][0])
            sv.update(proj=proj)
            w_out = w['conv_w_out'][0]
        elif mixer == 2:
            proj = _mm(hn, w['ssm_w_in'][:, 0], b_split=True, out_dtype=F32, name="ssm_in")
            pre, yg, xr_all, xi_all = _ssm_fwd(proj, pw_re, pw_im, bd_re, bd_im, cd_re, cd_im, w['ssm_d'])
            gl = _mm(yg, w['ssm_w_glu'][:, 0], b_split=True, out_dtype=F32, name="ssm_glu")
            a = _glu_fwd(gl, proj, w['ssm_b_glu'])
            sv.update(proj=proj, pre=pre, yg=yg, xr=xr_all, xi=xi_all, gl=gl)
            w_out = w['ssm_w_out'][0]
        else:
            proj = _mm(hn, w['fox_w_in'][:, 0], b_split=True, name="fox_in")
            z = _mm(hn, wfg, out_dtype=F32, name="fox_fg")
            fox_ops = _fox_operands(proj, _fg_fwd(z, bfg))
            ot, lse = _fox_attention(fox_ops[0], fox_ops[1], fox_ops[5], fox_ops[6], fox_ops[7])
            o = unheads(ot)
            a = _fox_gate_fwd(o, proj)
            sv.update(proj=proj, z=z, fox_ops=fox_ops, ot=ot, o=o, lse=lse)
            w_out = w['fox_w_out'][0]
        y = _mm(a, w_out, out_dtype=F32, name=f"mixer_out{i}")
        x1, gn = _ple_fwd(x, y, w['ple_norm'][i:i + 1], f"ple_fwd{i}")
        emb = _mm(pb[i], w['ple_proj'][:, i], b_split=True, out_dtype=F32, name=f"ple_emb{i}")
        gl2 = _mm(gn, w['ple_gate'][i], out_dtype=F32, name=f"ple_gate{i}")
        x = _ple_mix(x1, emb, gl2, f"ple_mix{i}")
        sv.update(a=a, x1=x1, gn=gn, emb=emb, gl2=gl2)
        saved.append(sv)

    loss, dx, grads['final_g'] = _loss_grad(x, target, w['final_g'].reshape(1, d))
    grads['final_g'] = grads['final_g'].reshape(d)

    g_norm, g_ple_norm, g_ple_proj, g_ple_gate = [None] * depth, [None] * depth, [None] * depth, [None] * depth
    for i in reversed(range(depth)):
        sv = saved[i]
        mixer = i % 4
        demb, dgl2 = _ple_mix_bwd(dx, sv['emb'], sv['gl2'], f"ple_mix_bwd{i}")
        g_ple_proj[i] = _mm(pb[i], demb, ta=True, out_split=N_CHIPS, name=f"ple_emb_dw{i}")
        g_ple_gate[i] = _mm(sv['gn'], dgl2, ta=True, name=f"ple_gate_dw{i}")
        dgn = _mm(dgl2, w['ple_gate'][i], tb=True, out_dtype=F32, name=f"ple_gate_dx{i}")
        dx1, dy, g_ple_norm[i] = _norm_bwd(sv['x1'], w['ple_norm'][i:i + 1], [dgn], dx, f"ple_norm_bwd{i}")
        w_out_name = ('swa_w_out', 'conv_w_out', 'ssm_w_out', 'fox_w_out')[mixer]
        grads[w_out_name] = _mm(sv['a'], dy, ta=True, name=f"mixer_out_dw{i}")[None]
        da = _mm(dy, w[w_out_name][0], tb=True, name=f"mixer_out_dx{i}")
        proj = sv['proj']
        dhs = []
        if mixer == 0:
            dq, dgate, dk4, dv4, dbias, dsink = _swa_bwd(proj, sv['k4'], sv['v4'], bias, sinks, da)
            back = lambda t4: t4.transpose(1, 0, 2).reshape(s, SWA_KV_HEADS * HEAD_DIM).astype(BF16)
            dproj = jnp.concatenate([dq, back(dk4), back(dv4), dgate], axis=1)
            grads['rel_bias'] = _bias_grad(dbias.reshape(SWA_HEADS, -1), jnp.asarray(onehot, BF16))
            grads['swa_sinks'] = dsink[:, :, 0].reshape(1, SWA_HEADS)
            w_in_name = 'swa_w_in'
        elif mixer == 1:
            dbg, dcg, du, dgate, dkern = _conv_bwd(proj, w['conv_kernel'][0], da)
            dproj = jnp.concatenate([dbg, dcg, du, dgate], axis=1)
            grads['conv_kernel'] = dkern[None]
            w_in_name = 'conv_w_in'
        elif mixer == 2:
            dgl, dgate, dbglu = _glu_bwd(sv['gl'], proj, w['ssm_b_glu'], da)
            grads['ssm_b_glu'] = dbglu
            grads['ssm_w_glu'] = _mm(sv['yg'], dgl, ta=True, out_split=N_CHIPS, name="ssm_glu_dw")[:, None]
            dyg = _mm(dgl, w['ssm_w_glu'][:, 0], tb=True, b_split=True, out_dtype=F32, name="ssm_glu_dx")
            du, dd, da_re, da_im, dbd_re, dbd_im, dcd_re, dcd_im = _ssm_bwd(
                proj, dyg, sv['pre'], sv['xr'], sv['xi'], pw_re, pw_im, bd_re, bd_im, cd_re, cd_im, w['ssm_d'])
            grads['ssm_d'] = dd
            dparams = ssm_vjp((da_re, da_im, dbd_re, dbd_im, dcd_re, dcd_im))
            for k, v in zip(('ssm_lam_re', 'ssm_lam_im', 'ssm_log_dt', 'ssm_b_re', 'ssm_b_im', 'ssm_c_re', 'ssm_c_im'), dparams):
                grads[k] = v[None]
            dproj = jnp.concatenate([du.astype(BF16), dgate], axis=1)
            w_in_name = 'ssm_w_in'
        else:
            do, dgate = _fox_gate_bwd(da, sv['o'], proj)
            qa, ka, qat, kat, v, _, c_row, c_lanes = sv['fox_ops']
            dot = do.reshape(s, FOX_HEADS, HEAD_DIM).transpose(1, 2, 0)
            dqt, dcq, dkt, dck, dvt = _fox_attention_bwd(qa, ka, qat, kat, v, c_row, c_lanes, sv['ot'], dot, sv['lse'])
            dq, dk, dv = unheads(dqt), unheads(dkt), unheads(dvt)
            dcs = jnp.pad((dcq[:, 0] + dck[:, 0]).T, ((0, 0), (0, LANES - FOX_HEADS)))
            dz, dbfg = _fg_bwd(sv['z'], bfg, dcs)
            grads['fox_b_fg'] = dbfg[:, :FOX_HEADS]
            grads['fox_w_fg'] = _mm(sv['hn'], dz, ta=True, out_dtype=BF16, name="fox_fg_dw")[:, :FOX_HEADS][None]
            dhs.append(_mm(dz, wfg, tb=True, out_dtype=F32, name="fox_fg_dx"))
            dproj = jnp.concatenate([dq, dk, dv, dgate], axis=1)
            w_in_name = 'fox_w_in'
        grads[w_in_name] = _mm(sv['hn'], dproj, ta=True, out_split=N_CHIPS, name=f"mixer_in_dw{i}")[:, None]
        dhs.append(_mm(dproj, w[w_in_name][:, 0], tb=True, b_split=True, out_dtype=F32, name=f"mixer_in_dx{i}"))
        dx, _, g_norm[i] = _norm_bwd(sv['x'], w['norm_g'][i:i + 1], dhs, dx1, f"norm_bwd{i}")

    grads['norm_g'] = jnp.concatenate(g_norm, axis=0)
    grads['ple_norm'] = jnp.concatenate(g_ple_norm, axis=0)
    grads['ple_proj'] = jnp.stack(g_ple_proj, axis=1)
    grads['ple_gate'] = jnp.stack(g_ple_gate)
    return loss, dx, grads


def _bf16_terms(a):
    hi = lax.reduce_precision(a, 8, 7)
    r1 = a - hi
    mid = lax.reduce_precision(r1, 8, 7)
    lo = lax.reduce_precision(r1 - mid, 8, 7)
    return hi.astype(BF16), mid.astype(BF16), lo.astype(BF16)


def _split3(a):
    return jnp.concatenate(_bf16_terms(a), axis=0)


def _bias_grad(dbias, onehot):
    out = _mm(_split3(dbias), onehot, out_dtype=F32, name="rel_bias_grad", tk=2048)
    nh = dbias.shape[0]
    return (out[:nh] + out[nh:2 * nh] + out[2 * nh:]).T


def _bias_table(rel_bias, onehot_t):
    nh = rel_bias.shape[1]
    out = _mm(_split3(rel_bias.T), onehot_t, out_dtype=F32, name="rel_bias_table")
    return (out[:nh] + out[nh:2 * nh] + out[2 * nh:]).reshape(nh, BLOCK, 2 * BLOCK)


WEIGHTS = ['norm_g', 'final_g', 'rel_bias', 'swa_w_in', 'swa_w_out', 'swa_sinks', 'conv_w_in', 'conv_kernel', 'conv_w_out',
           'ssm_w_in', 'ssm_lam_re', 'ssm_lam_im', 'ssm_log_dt', 'ssm_b_re', 'ssm_b_im', 'ssm_c_re', 'ssm_c_im', 'ssm_d',
           'ssm_w_glu', 'ssm_b_glu', 'ssm_w_out', 'fox_w_in', 'fox_w_fg', 'fox_b_fg', 'fox_w_out', 'ple_proj', 'ple_norm',
           'ple_gate']
BIG = {'swa_w_in': 2, 'swa_w_out': 1, 'conv_w_in': 2, 'conv_w_out': 1, 'ssm_w_in': 2, 'ssm_w_glu': 2, 'ssm_w_out': 1,
       'fox_w_in': 2, 'fox_w_fg': 1, 'fox_w_out': 1, 'ple_proj': 2, 'ple_gate': 1}
SMALL = {'conv_kernel': 2, 'ssm_d': 1, 'ssm_b_glu': 1}
REPLICATED = [n for n in WEIGHTS if n not in BIG and n not in SMALL]
N_CHIPS = 4
N_DEV = 8
BIG_ROWS = 256
SMALL_ROWS = 8
REPL_ROWS = 64


def _flat(pieces, dtype, lead, row_mult):
    flat = jnp.concatenate([q.astype(dtype) for q in pieces], axis=-1)
    pad = (-flat.shape[-1]) % (row_mult * FLAT_W)
    flat = jnp.pad(flat, [(0, 0)] * len(lead) + [(0, pad)])
    return flat.reshape(*lead, -1, FLAT_W)


def _unflat(flat, lead_ndim, sizes):
    lead = flat.shape[:lead_ndim]
    flat = flat.reshape(*lead, -1)
    out, off = [], 0
    for n in sizes:
        out.append(flat[..., off:off + n])
        off += n
    return out


def _split_shards(full, axis):
    shp = full.shape
    parts = full.reshape(shp[:axis] + (N_CHIPS, shp[axis] // N_CHIPS) + shp[axis + 1:])
    return jnp.moveaxis(parts, axis, 0)


def _join_shards(parts, axis):
    moved = jnp.moveaxis(parts, 0, axis)
    shp = moved.shape
    return moved.reshape(shp[:axis] + (shp[axis] * shp[axis + 1],) + shp[axis + 2:])


def _coords():
    return lax.axis_index("x"), lax.axis_index("y"), lax.axis_index("c")


def _remote(k, src, dst, to, send_sems, recv_sems):
    return pltpu.make_async_remote_copy(src_ref=src, dst_ref=dst, send_sem=send_sems.at[k], recv_sem=recv_sems.at[k],
                                        device_id=to, device_id_type=MESH)


def _gather_weights(bufs, ssh):
    nb = len(bufs)

    def body(*refs):
        w_refs, s_ref = refs[:nb], refs[nb]
        wouts, sout = refs[nb + 1:2 * nb + 1], refs[2 * nb + 1]
        send_sems, recv_sems = refs[2 * nb + 2:]
        x, y, c = _coords()
        me = 2 * x + y
        chips = [(1 - x, y), (x, 1 - y), (1 - x, 1 - y)]
        rc = functools.partial(_remote, send_sems=send_sems, recv_sems=recv_sems)
        sends = []
        for j, (cx, cy) in enumerate(chips):
            for b in range(nb):
                sends.append(rc(6 * b + j, w_refs[b].at[c], wouts[b].at[me, c], (cx, cy, c)))
            sends.append(rc(6 * nb + j, s_ref, sout.at[me], (cx, cy, c)))
        for cp in sends:
            cp.start()
        for j, (cx, cy) in enumerate(chips):
            k = 2 * cx + cy
            for b in range(nb):
                rc(6 * b + j, w_refs[b].at[c], wouts[b].at[k, c], (x, y, c)).wait_recv()
                fwd = rc(6 * b + 3 + j, wouts[b].at[k, c], wouts[b].at[k, c], (x, y, 1 - c))
                fwd.start()
                sends.append(fwd)
        for j, (cx, cy) in enumerate(chips):
            k = 2 * cx + cy
            for b in range(nb):
                rc(6 * b + 3 + j, w_refs[b].at[c], wouts[b].at[k, 1 - c], (x, y, c)).wait_recv()
            rc(6 * nb + j, s_ref, sout.at[k], (x, y, c)).wait_recv()
        for cp in sends:
            cp.wait_send()

    nsem = 6 * nb + 3
    res = pl.pallas_call(
        body, name="gather_weights", in_specs=[ANY] * (nb + 1), out_specs=[ANY] * (nb + 1),
        out_shape=[jax.ShapeDtypeStruct((N_CHIPS,) + a.shape, a.dtype) for a in (*bufs, ssh)],
        scratch_shapes=[pltpu.SemaphoreType.DMA((nsem,)), pltpu.SemaphoreType.DMA((nsem,))],
    )(*bufs, ssh)
    return res[:nb], res[nb]


def _pair_exchange(gbufs):
    nb = len(gbufs)

    def body(*refs):
        g_refs, outs = refs[:nb], refs[nb:2 * nb]
        send_sems, recv_sems = refs[2 * nb:]
        x, y, c = _coords()
        rc = functools.partial(_remote, send_sems=send_sems, recv_sems=recv_sems)
        sends = [rc(N_CHIPS * b + j, g_refs[b].at[2 * j + 1 - c], outs[b].at[j], (x, y, 1 - c))
                 for b in range(nb) for j in range(N_CHIPS)]
        for cp in sends:
            cp.start()
        for b in range(nb):
            for j in range(N_CHIPS):
                rc(N_CHIPS * b + j, g_refs[b].at[2 * j + c], outs[b].at[j], (x, y, c)).wait_recv()
        for cp in sends:
            cp.wait_send()

    nsem = N_CHIPS * nb
    return pl.pallas_call(
        body, name="pair_exchange", in_specs=[ANY] * nb, out_specs=[ANY] * nb,
        out_shape=[jax.ShapeDtypeStruct((N_CHIPS,) + g.shape[1:], g.dtype) for g in gbufs],
        scratch_shapes=[pltpu.SemaphoreType.DMA((nsem,)), pltpu.SemaphoreType.DMA((nsem,))],
    )(*gbufs)


def _pair_sum(mine, theirs, name):
    n, r, w = mine.shape
    t = _tile(r, 256, 16)
    spec = lambda a: (a, (n, t, w), lambda i: (0, i, 0))
    return _tiled(lambda a, b: (a.astype(F32) + b.astype(F32),), [spec(mine), spec(theirs)],
                  [_out((n, r, w), BF16, (n, t, w), lambda i: (0, i, 0))], r // t, name)[0]


def _exchange_grads(pbufs, gsmall, grepl):
    nb = len(pbufs)

    def body(*refs):
        g_refs, s_ref, r_ref = refs[:nb], refs[nb], refs[nb + 1]
        ogs, osm, orp = refs[nb + 2:2 * nb + 2], refs[2 * nb + 2], refs[2 * nb + 3]
        send_sems, recv_sems, local_sems = refs[2 * nb + 4:]
        x, y, c = _coords()
        me = 4 * x + 2 * y + c
        my_chip = 2 * x + y
        rc = functools.partial(_remote, send_sems=send_sems, recv_sems=recv_sems)
        local = [pltpu.make_async_copy(s_ref.at[me], osm.at[me], local_sems.at[0]),
                 pltpu.make_async_copy(r_ref, orp.at[me], local_sems.at[1])]
        for cp in local:
            cp.start()
        peers = []
        for d in range(1, N_DEV):
            px = 1 - x if d & 4 else x
            py = 1 - y if d & 2 else y
            pc = 1 - c if d & 1 else c
            peers.append((d, 4 * px + 2 * py + pc, 2 * px + py, (px, py, pc)))
        per_peer = nb + 2
        sends = []
        for i, (d, peer, chip, to) in enumerate(peers):
            sends.append(rc(per_peer * i + nb, s_ref.at[peer], osm.at[me], to))
            sends.append(rc(per_peer * i + nb + 1, r_ref, orp.at[me], to))
            if d & 1 == 0:
                for b in range(nb):
                    sends.append(rc(per_peer * i + b, g_refs[b].at[chip], ogs[b].at[my_chip], to))
        for cp in sends:
            cp.start()
        for i, (d, peer, chip, to) in enumerate(peers):
            rc(per_peer * i + nb, s_ref.at[peer], osm.at[peer], to).wait_recv()
            rc(per_peer * i + nb + 1, r_ref, orp.at[peer], to).wait_recv()
            if d & 1 == 0:
                for b in range(nb):
                    rc(per_peer * i + b, g_refs[b].at[chip], ogs[b].at[chip], to).wait_recv()
        for cp in sends:
            cp.wait_send()
        for cp in local:
            cp.wait()

    nsem = (nb + 2) * (N_DEV - 1)
    res = pl.pallas_call(
        body, name="exchange_grads", in_specs=[ANY] * (nb + 2), out_specs=[ANY] * (nb + 2),
        out_shape=[jax.ShapeDtypeStruct(a.shape, a.dtype) for a in (*pbufs, gsmall)]
        + [jax.ShapeDtypeStruct((N_DEV,) + grepl.shape, grepl.dtype)],
        scratch_shapes=[pltpu.SemaphoreType.DMA((nsem,)), pltpu.SemaphoreType.DMA((nsem,)), pltpu.SemaphoreType.DMA((2,))],
    )(*pbufs, gsmall, grepl)
    return res[:nb], res[nb], res[nb + 1]


def _sibling_exchange(halves):
    nb = len(halves)

    def body(*refs):
        ins, outs = refs[:nb], refs[nb:2 * nb]
        send_sems, recv_sems = refs[2 * nb:]
        x, y, c = _coords()
        rc = functools.partial(_remote, send_sems=send_sems, recv_sems=recv_sems)
        sends = [rc(b, ins[b], outs[b], (x, y, 1 - c)) for b in range(nb)]
        for cp in sends:
            cp.start()
        for b in range(nb):
            rc(b, ins[b], outs[b], (x, y, c)).wait_recv()
        for cp in sends:
            cp.wait_send()

    return pl.pallas_call(
        body, name="sibling_exchange", in_specs=[ANY] * nb, out_specs=[ANY] * nb,
        out_shape=[jax.ShapeDtypeStruct(a.shape, a.dtype) for a in halves],
        scratch_shapes=[pltpu.SemaphoreType.DMA((nb,)), pltpu.SemaphoreType.DMA((nb,))],
    )(*halves)


def _sum_senders(recv, name):
    n, r, w = recv.shape
    t = _tile(r, 256, 8)

    def fn(v):
        acc = v[0].astype(F32)
        for i in range(1, n):
            acc = acc + v[i].astype(F32)
        return (acc,)

    return _tiled(fn, [(recv, (n, t, w), lambda i: (0, i, 0))], [_orows(r, w, F32, t)], r // t, name)[0]


def _adamw(w, g, m, v, name):
    r, wd = w.shape
    t = _tile(r, 256, 8)

    def fn(wv, gv, mv, vv):
        m2 = ADAM_B1 * mv + (1.0 - ADAM_B1) * gv
        v2 = ADAM_B2 * vv + (1.0 - ADAM_B2) * (gv * gv)
        m_hat = m2 / (1.0 - ADAM_B1 ** ADAM_STEP)
        v_hat = v2 / (1.0 - ADAM_B2 ** ADAM_STEP)
        delta = -ADAM_LR * (m_hat / (jnp.sqrt(v_hat) + ADAM_EPS) + ADAM_WD * wv)
        return delta, m2, v2

    return _tiled(fn, [_rows(a, t) for a in (w, g, m, v)], [_orows(r, wd, F32, t)] * 3, r // t, name)


def kernel(x, p, norm_g, final_g, rel_bias, swa_w_in, swa_w_out, swa_sinks, conv_w_in, conv_kernel, conv_w_out, ssm_w_in, ssm_lam_re, ssm_lam_im, ssm_log_dt, ssm_b_re, ssm_b_im, ssm_c_re, ssm_c_im, ssm_d, ssm_w_glu, ssm_b_glu, ssm_w_out, fox_w_in, fox_w_fg, fox_b_fg, fox_w_out, ple_proj, ple_norm, ple_gate, loss_target, m_norm_g, m_final_g, m_rel_bias, m_swa_w_in, m_swa_w_out, m_swa_sinks, m_conv_w_in, m_conv_kernel, m_conv_w_out, m_ssm_w_in, m_ssm_lam_re, m_ssm_lam_im, m_ssm_log_dt, m_ssm_b_re, m_ssm_b_im, m_ssm_c_re, m_ssm_c_im, m_ssm_d, m_ssm_w_glu, m_ssm_b_glu, m_ssm_w_out, m_fox_w_in, m_fox_w_fg, m_fox_b_fg, m_fox_w_out, m_ple_proj, m_ple_norm, m_ple_gate, v_norm_g, v_final_g, v_rel_bias, v_swa_w_in, v_swa_w_out, v_swa_sinks, v_conv_w_in, v_conv_kernel, v_conv_w_out, v_ssm_w_in, v_ssm_lam_re, v_ssm_lam_im, v_ssm_log_dt, v_ssm_b_re, v_ssm_b_im, v_ssm_c_re, v_ssm_c_im, v_ssm_d, v_ssm_w_glu, v_ssm_b_glu, v_ssm_w_out, v_fox_w_in, v_fox_w_fg, v_fox_b_fg, v_fox_w_out, v_ple_proj, v_ple_norm, v_ple_gate):
    given = dict(locals())
    shard_shape = {n: given[n].shape for n in WEIGHTS}
    half = {n: math.prod(shard_shape[n]) // 2 for n in WEIGHTS}

    widths = sorted({shard_shape[n][-1] for n in BIG}, reverse=True)
    classes = [[n for n in BIG if shard_shape[n][-1] == w] for w in widths]
    rows = {n: math.prod(shard_shape[n][:-1]) for n in BIG}
    core = lax.axis_index("c")
    my_chip = 2 * lax.axis_index("x") + lax.axis_index("y")

    bufs = [jnp.concatenate([given[n].astype(BF16).reshape(rows[n], w) for n in names], axis=0).reshape(2, -1, w)
            for w, names in zip(widths, classes)]
    ssh = _flat([given[n].reshape(-1) for n in SMALL], F32, (), SMALL_ROWS)
    walls, sall = _gather_weights(bufs, ssh)
    walls = [lax.dynamic_update_slice(a, b[None], (my_chip, 0, 0, 0)) for a, b in zip(walls, bufs)]
    sall = lax.dynamic_update_slice(sall, ssh[None], (my_chip, 0, 0))
    full = {n: given[n] for n in REPLICATED}
    for w, names, wall in zip(widths, classes, walls):
        wall, off = wall.reshape(N_CHIPS, -1, w), 0
        for n in names:
            stacked = wall[:, off:off + rows[n]].reshape((N_CHIPS,) + shard_shape[n])
            full[n] = stacked if BIG[n] == 2 else _join_shards(stacked, BIG[n])
            off += rows[n]
    for n, piece in zip(SMALL, _unflat(sall, 1, [2 * half[n] for n in SMALL])):
        full[n] = _join_shards(piece.reshape((N_CHIPS,) + shard_shape[n]), SMALL[n])

    loss, dx, grads = _local_step(x[0], p[:, 0], loss_target[0], full)

    by_chip = lambda n: grads[n] if BIG[n] == 2 else _split_shards(grads[n], BIG[n])
    gbufs = [jnp.concatenate([by_chip(n).reshape(N_CHIPS, rows[n], w) for n in names], axis=1).reshape(N_DEV, -1, w)
             for w, names in zip(widths, classes)]
    gsmall = _flat([_split_shards(grads[n], SMALL[n]).reshape(N_DEV, -1) for n in SMALL], F32, (N_DEV,), SMALL_ROWS)
    grepl = _flat([grads[n].reshape(-1) for n in REPLICATED], F32, (), REPL_ROWS)
    pbufs = [_pair_sum(jnp.where(core == 0, g[0::2], g[1::2]), t, f"pair_sum_w{w}")
             for w, g, t in zip(widths, gbufs, _pair_exchange(gbufs))]
    recv_bufs, recv_small, recv_repl = _exchange_grads(pbufs, gsmall, grepl)
    recv_bufs = [lax.dynamic_update_slice(r, lax.dynamic_index_in_dim(pb, my_chip, axis=0), (my_chip, 0, 0))
                 for r, pb in zip(recv_bufs, pbufs)]
    halves = [_sum_senders(r, f"sum_w{w}") for w, r in zip(widths, recv_bufs)] + [_sum_senders(recv_small, "sum_small")]
    others = _sibling_exchange(halves)
    g_repl = _sum_senders(recv_repl, "sum_repl")
    both = [(jnp.where(core == 0, a, b), jnp.where(core == 0, b, a)) for a, b in zip(halves, others)]
    sfull = jnp.stack(both[-1])

    out_g, out_d, out_m, out_v = {}, {}, {}, {}
    for w, names, (lower, upper) in zip(widths, classes, both):
        gfull, off = jnp.concatenate([lower, upper], axis=0), 0
        for n in names:
            as_rows = lambda t, w=w: t.reshape(-1, w)
            piece = gfull[off:off + rows[n]]
            off += rows[n]
            d, m2, v2 = _adamw(as_rows(given[n]), piece, as_rows(given['m_' + n]), as_rows(given['v_' + n]), f"adamw_{n}")
            out_g[n], out_d[n], out_m[n], out_v[n] = (t.reshape(shard_shape[n]) for t in (piece, d, m2, v2))

    pack_small = lambda pre: _flat([given[pre + n].reshape(2, -1) for n in SMALL], F32, (2,), SMALL_ROWS).reshape(-1, FLAT_W)
    res = _adamw(pack_small(''), sfull.reshape(-1, FLAT_W), pack_small('m_'), pack_small('v_'), "adamw_small")
    for dst, flat in zip((out_g, out_d, out_m, out_v), (sfull,) + tuple(res)):
        for n, piece in zip(SMALL, _unflat(flat.reshape(2, -1, FLAT_W), 1, [half[n] for n in SMALL])):
            dst[n] = piece.reshape(shard_shape[n])

    pack_repl = lambda pre: _flat([given[pre + n].reshape(-1) for n in REPLICATED], F32, (), REPL_ROWS)
    res = _adamw(pack_repl(''), g_repl, pack_repl('m_'), pack_repl('v_'), "adamw_repl")
    for dst, flat in zip((out_g, out_d, out_m, out_v), (g_repl,) + tuple(res)):
        for n, piece in zip(REPLICATED, _unflat(flat, 0, [2 * half[n] for n in REPLICATED])):
            dst[n] = piece.reshape(shard_shape[n])

    total = lax.psum(loss[0, 0], ("x", "y", "c"))
    return (total, dx[None], *[out_g[n] for n in WEIGHTS], *[out_d[n] for n in WEIGHTS],
            *[out_m[n] for n in WEIGHTS], *[out_v[n] for n in WEIGHTS])
```

```python
import functools
import math

import numpy as np
import jax
import jax.numpy as jnp
from jax import lax
from jax.experimental import pallas as pl
from jax.experimental.pallas import tpu as pltpu

F32 = jnp.float32
BF16 = jnp.bfloat16

EPS = 1e-6
BLOCK = 128
REL_BUCKETS = 32
REL_MAX_DIST = 128
SWA_HEADS, SWA_KV_HEADS, HEAD_DIM = 32, 4, 64
SWA_GROUP = SWA_HEADS // SWA_KV_HEADS
FOX_HEADS = 32
SSM_GROUP, SSM_STATE = 16, 64
GROUPS_PER_STEP = 8
STATE_W = GROUPS_PER_STEP * SSM_STATE
LANES = 128
NEG = -1e30

ADAM_LR, ADAM_B1, ADAM_B2, ADAM_EPS, ADAM_WD, ADAM_STEP = 0.001, 0.9, 0.999, 1e-08, 0.01, 10

VMEM_LIMIT_V7X = 56 * 1024 * 1024
FLAT_W = 1024
MESH = pl.DeviceIdType.MESH
ANY = pl.BlockSpec(memory_space=pl.ANY)


def _cp(sem):
    return pltpu.CompilerParams(dimension_semantics=sem, vmem_limit_bytes=VMEM_LIMIT_V7X)


def _tile(n, target, mult=LANES):
    if n <= target:
        return n
    t = (target // mult) * mult
    while t >= mult:
        if n % t == 0:
            return t
        t -= mult
    return n


MAX_WHOLE_TILE = 1152


def _div(a, b):
    return lax.div(a, jnp.int32(b))


def _rem(a, b):
    return lax.rem(a, jnp.int32(b))


def _mm(a, b, *, ta=False, tb=False, out_dtype=BF16, name, tm=1024, tn=1024, tk=1024, b_split=False, out_split=0):
    if ta:
        kdim, m = a.shape
    else:
        m, kdim = a.shape
    parts = b.shape[0] if b_split else 1
    b_rows, b_cols = (b.shape[1], b.shape[2] * parts) if b_split else b.shape
    n = b_rows if tb else b_cols
    assert (b_cols if tb else b_rows) == kdim
    n_range = n // (out_split or (1 if tb else parts))
    k_range = kdim // (parts if tb else 1)
    tm, tk = _tile(m, tm), _tile(k_range, tk)
    tn = n_range if n_range <= MAX_WHOLE_TILE else _tile(n_range, tn)
    nk = kdim // tk
    dn = (((0 if ta else 1,), (1 if tb else 0,)), ((), ()))

    def body(a_ref, b_ref, o_ref, acc_ref):
        k = pl.program_id(2)

        @pl.when(k == 0)
        def _():
            acc_ref[...] = jnp.zeros_like(acc_ref)

        acc_ref[...] += lax.dot_general(a_ref[...].astype(BF16), b_ref[...].astype(BF16), dn,
                                        preferred_element_type=F32)

        @pl.when(k == nk - 1)
        def _():
            o_ref[...] = acc_ref[...].astype(o_ref.dtype)

    a_spec = pl.BlockSpec((tk, tm), lambda i, j, k: (k, i)) if ta else pl.BlockSpec((tm, tk), lambda i, j, k: (i, k))
    nj, nkr = n_range // tn, k_range // tk
    if not b_split:
        b_spec = pl.BlockSpec((tn, tk), lambda i, j, k: (j, k)) if tb else pl.BlockSpec((tk, tn), lambda i, j, k: (k, j))
    elif tb:
        b_spec = pl.BlockSpec((None, tn, tk), lambda i, j, k: (_div(k, nkr), j, _rem(k, nkr)))
    else:
        b_spec = pl.BlockSpec((None, tk, tn), lambda i, j, k: (_div(j, nj), k, _rem(j, nj)))
    if out_split:
        out_spec = pl.BlockSpec((None, tm, tn), lambda i, j, k: (_div(j, nj), i, _rem(j, nj)))
        out_shape = jax.ShapeDtypeStruct((out_split, m, n_range), out_dtype)
    else:
        out_spec = pl.BlockSpec((tm, tn), lambda i, j, k: (i, j))
        out_shape = jax.ShapeDtypeStruct((m, n), out_dtype)
    return pl.pallas_call(
        body, name=name, grid=(m // tm, n // tn, nk),
        in_specs=[a_spec, b_spec], out_specs=out_spec, out_shape=out_shape,
        scratch_shapes=[pltpu.VMEM((tm, tn), F32)],
        compiler_params=_cp(("parallel", "parallel", "arbitrary")),
    )(a, b)


def _rows(arr, t):
    return (arr, (t, arr.shape[1]), lambda i: (i, 0))


def _cols(arr, cb, off=0):
    return (arr, (arr.shape[0], cb), lambda i: (0, i + off))


def _full(arr):
    nd = arr.ndim
    return (arr, arr.shape, lambda i: (0,) * nd)


def _lead(arr):
    return (arr, (1,) + arr.shape[1:], lambda i: (i, 0, 0))


def _out(shape, dtype, block, imap, acc=False):
    return (jax.ShapeDtypeStruct(shape, dtype), block, imap, acc)


def _orows(s, w, dtype, t):
    return _out((s, w), dtype, (t, w), lambda i: (i, 0))


def _ocols(s, w, dtype, cb):
    return _out((s, w), dtype, (s, cb), lambda i: (0, i))


def _oacc(shape):
    nd = len(shape)
    return _out(shape, F32, shape, lambda i: (0,) * nd, True)


def _olead(n, a, b, dtype=F32):
    return _out((n, a, b), dtype, (1, a, b), lambda i: (i, 0, 0))


def _tiled(fn, ins, outs, n, name):
    has_acc = any(o[3] for o in outs)
    ni = len(ins)

    def body(*refs):
        vals = fn(*[r[...] for r in refs[:ni]])
        i = pl.program_id(0)
        for r, v, o in zip(refs[ni:], vals, outs):
            if o[3]:
                @pl.when(i == 0)
                def _(r=r, v=v):
                    r[...] = v.astype(r.dtype)

                @pl.when(i > 0)
                def _(r=r, v=v):
                    r[...] += v.astype(r.dtype)
            else:
                r[...] = v.astype(r.dtype)

    res = pl.pallas_call(
        body, name=name, grid=(n,),
        in_specs=[pl.BlockSpec(b, m) for _, b, m in ins],
        out_specs=[pl.BlockSpec(b, m) for _, b, m, _ in outs],
        out_shape=[s for s, _, _, _ in outs],
        compiler_params=_cp(("arbitrary",) if has_acc else ("parallel",)),
    )(*[a for a, _, _ in ins])
    return res


def _silu(x):
    return x * jax.nn.sigmoid(x)


def _gelu(x):
    return 0.5 * x * (1.0 + jnp.tanh(math.sqrt(2.0 / math.pi) * (x + 0.044715 * (x * x * x))))


def _rms(x, g):
    r = lax.rsqrt(jnp.mean(x * x, axis=-1, keepdims=True) + EPS)
    return x * r * g


def _shift_rows(x, sh, up=False):
    s = x.shape[0]
    rows = lax.broadcasted_iota(jnp.int32, x.shape, 0)
    if up:
        return jnp.where(rows < s - sh, pltpu.roll(x, s - sh, 0), 0.0)
    return jnp.where(rows >= sh, pltpu.roll(x, sh, 0), 0.0)


SCAN_CHUNK = 128


def _scan_complex(xr, xi, pr, pi, tr, ti, reverse):
    s, lanes = xr.shape
    c = min(SCAN_CHUNK, s)
    nchunk = s // c
    in_chunk = lax.broadcasted_iota(jnp.int32, xr.shape, 0) & (c - 1)

    def shifted(x, sh):
        if reverse:
            return jnp.where(in_chunk < c - sh, pltpu.roll(x, s - sh, 0), 0.0)
        return jnp.where(in_chunk >= sh, pltpu.roll(x, sh, 0), 0.0)

    k = 0
    while (1 << k) < c:
        sr, si = shifted(xr, 1 << k), shifted(xi, 1 << k)
        ar, ai = pr[k:k + 1, :], pi[k:k + 1, :]
        xr, xi = xr + ar * sr - ai * si, xi + ar * si + ai * sr
        k += 1
    if nchunk == 1:
        return xr, xi
    xr, xi = xr.reshape(nchunk, c, lanes), xi.reshape(nchunk, c, lanes)
    edge = 0 if reverse else c - 1
    er, ei = xr[:, edge, :], xi[:, edge, :]
    m = 0
    while (1 << m) < nchunk:
        sr, si = _shift_rows(er, 1 << m, reverse), _shift_rows(ei, 1 << m, reverse)
        ar, ai = pr[k + m:k + m + 1, :], pi[k + m:k + m + 1, :]
        er, ei = er + ar * sr - ai * si, ei + ar * si + ai * sr
        m += 1
    cr, ci = _shift_rows(er, 1, reverse)[:, None, :], _shift_rows(ei, 1, reverse)[:, None, :]
    xr, xi = xr + tr[None] * cr - ti[None] * ci, xi + tr[None] * ci + ti[None] * cr
    return xr.reshape(s, lanes), xi.reshape(s, lanes)


def _cumsum_rows(x, reverse):
    s = x.shape[0]
    k = 0
    while (1 << k) < s:
        x = x + _shift_rows(x, 1 << k, reverse)
        k += 1
    return x


def _dot(a, b, ca=1, cb=0):
    return lax.dot_general(a.astype(BF16), b.astype(BF16), (((ca,), (cb,)), ((), ())), preferred_element_type=F32)


def _t5_bucket(dist):
    max_exact = REL_BUCKETS // 2
    d = np.maximum(dist, 1).astype(np.float32)
    large = max_exact + (np.log(d / max_exact) / np.log(REL_MAX_DIST / max_exact) * (REL_BUCKETS - max_exact)).astype(np.int32)
    large = np.minimum(large, REL_BUCKETS - 1)
    return np.where(dist < max_exact, dist, large).astype(np.int32)


def _swa_bucket_table():
    qi = np.arange(BLOCK)[:, None]
    kj = np.arange(2 * BLOCK)[None, :]
    return _t5_bucket(np.clip(qi + BLOCK - kj, 0, None))


def _stack_heads(x):
    return jnp.concatenate([x[:, g * HEAD_DIM:(g + 1) * HEAD_DIM] for g in range(SWA_GROUP)], axis=0)


def _unstack_heads(x):
    return jnp.concatenate([x[g * BLOCK:(g + 1) * BLOCK] for g in range(SWA_GROUP)], axis=1)


def _sink_rows(sink_ref):
    return jnp.concatenate([jnp.broadcast_to(sink_ref[0, g:g + 1, :1], (BLOCK, 1)) for g in range(SWA_GROUP)], axis=0)


def _swa_scores(qg, kb, bias_g, sk, n):
    s = _dot(qg, kb, 1, 1) * (HEAD_DIM ** -0.5) + bias_g
    row = lax.broadcasted_iota(jnp.int32, s.shape, 0) & (BLOCK - 1)
    col = lax.broadcasted_iota(jnp.int32, s.shape, 1)
    dist = row + BLOCK - col
    mask = (dist >= 0) & (dist < BLOCK) & ((col >= BLOCK) | (n > 0))
    s = jnp.where(mask, s, NEG)
    m = jnp.maximum(jnp.max(s, axis=1, keepdims=True), sk)
    e = jnp.exp(s - m)
    es = jnp.exp(sk - m)
    den = jnp.sum(e, axis=1, keepdims=True) + es
    return e / den, es / den


def _swa_specs(proj, k4, v4, nb, clamp):
    gw = SWA_GROUP * HEAD_DIM
    gate_off = (SWA_HEADS * HEAD_DIM + 2 * SWA_KV_HEADS * HEAD_DIM) // gw
    cur = (lambda n: jnp.minimum(n, nb - 1)) if clamp else (lambda n: n)
    prev = lambda n: jnp.maximum(cur(n) - 1, 0)
    return [
        pl.BlockSpec((BLOCK, gw), lambda h, n: (cur(n), h)),
        pl.BlockSpec((1, BLOCK, HEAD_DIM), lambda h, n: (h, cur(n), 0)),
        pl.BlockSpec((1, BLOCK, HEAD_DIM), lambda h, n: (h, prev(n), 0)),
        pl.BlockSpec((1, BLOCK, HEAD_DIM), lambda h, n: (h, cur(n), 0)),
        pl.BlockSpec((1, BLOCK, HEAD_DIM), lambda h, n: (h, prev(n), 0)),
        pl.BlockSpec((BLOCK, gw), lambda h, n: (cur(n), gate_off + h)),
    ], [proj, k4, k4, v4, v4, proj]


def _swa_fwd(proj, k4, v4, bias, sinks):
    s = proj.shape[0]
    nb = s // BLOCK
    gw = SWA_GROUP * HEAD_DIM

    def body(q_ref, kc_ref, kp_ref, vc_ref, vp_ref, gate_ref, bias_ref, sink_ref, a_ref):
        n = pl.program_id(1)
        kb = jnp.concatenate([kp_ref[0], kc_ref[0]], axis=0)
        vb = jnp.concatenate([vp_ref[0], vc_ref[0]], axis=0)
        p, _ = _swa_scores(_stack_heads(q_ref[...]), kb, bias_ref[...].reshape(-1, 2 * BLOCK), _sink_rows(sink_ref), n)
        a_ref[...] = (_unstack_heads(_dot(p, vb)) * _silu(gate_ref[...].astype(F32))).astype(a_ref.dtype)

    specs, args = _swa_specs(proj, k4, v4, nb, False)
    return pl.pallas_call(
        body, name="swa_fwd", grid=(SWA_KV_HEADS, nb),
        in_specs=specs + [pl.BlockSpec((SWA_GROUP, BLOCK, 2 * BLOCK), lambda h, n: (h, 0, 0)),
                          pl.BlockSpec((1, SWA_GROUP, LANES), lambda h, n: (h, 0, 0))],
        out_specs=pl.BlockSpec((BLOCK, gw), lambda h, n: (n, h)),
        out_shape=jax.ShapeDtypeStruct((s, SWA_HEADS * HEAD_DIM), BF16),
        compiler_params=_cp(("parallel", "parallel")),
    )(*args, bias, sinks)


def _swa_bwd(proj, k4, v4, bias, sinks, da):
    s = proj.shape[0]
    nb = s // BLOCK
    gw = SWA_GROUP * HEAD_DIM

    def body(q_ref, kc_ref, kp_ref, vc_ref, vp_ref, gate_ref, bias_ref, sink_ref, da_ref,
             dq_ref, dgate_ref, dk_ref, dv_ref, dbias_ref, dsink_ref, dk_own, dv_own):
        n = pl.program_id(1)

        @pl.when(n == 0)
        def _():
            dk_own[...] = jnp.zeros_like(dk_own)
            dv_own[...] = jnp.zeros_like(dv_own)
            dbias_ref[...] = jnp.zeros_like(dbias_ref)
            dsink_ref[...] = jnp.zeros_like(dsink_ref)

        @pl.when(n < nb)
        def _():
            kb = jnp.concatenate([kp_ref[0], kc_ref[0]], axis=0)
            vb = jnp.concatenate([vp_ref[0], vc_ref[0]], axis=0)
            dkb = jnp.zeros((2 * BLOCK, HEAD_DIM), F32)
            dvb = jnp.zeros((2 * BLOCK, HEAD_DIM), F32)
            for g in range(SWA_GROUP):
                sl = slice(g * HEAD_DIM, (g + 1) * HEAD_DIM)
                qg = q_ref[:, sl]
                p, p0 = _swa_scores(qg, kb, bias_ref[g], sink_ref[0, g:g + 1, :1], n)
                og = _dot(p, vb)
                gate = gate_ref[:, sl].astype(F32)
                dag = da_ref[:, sl].astype(F32)
                sg = jax.nn.sigmoid(gate)
                do = dag * gate * sg
                dgate_ref[:, sl] = (dag * og * sg * (1.0 + gate * (1.0 - sg))).astype(dgate_ref.dtype)
                dp = _dot(do, vb, 1, 1)
                delta = jnp.sum(do * og, axis=1, keepdims=True)
                ds = p * (dp - delta)
                dbias_ref[g] += ds
                dsink_ref[0, g:g + 1, :] += jnp.zeros((1, LANES), F32) - jnp.sum(p0 * delta, axis=0, keepdims=True)
                dq_ref[:, sl] = (_dot(ds, kb) * (HEAD_DIM ** -0.5)).astype(dq_ref.dtype)
                dkb += _dot(ds, qg, 0, 0) * (HEAD_DIM ** -0.5)
                dvb += _dot(p, do, 0, 0)
            dk_ref[0] = dk_own[...] + dkb[:BLOCK]
            dv_ref[0] = dv_own[...] + dvb[:BLOCK]
            dk_own[...] = dkb[BLOCK:]
            dv_own[...] = dvb[BLOCK:]

        @pl.when(n == nb)
        def _():
            dk_ref[0] = dk_own[...]
            dv_ref[0] = dv_own[...]

    specs, args = _swa_specs(proj, k4, v4, nb, True)
    cur = lambda n: jnp.minimum(n, nb - 1)
    trail = lambda n: jnp.maximum(n - 1, 0)
    return pl.pallas_call(
        body, name="swa_bwd", grid=(SWA_KV_HEADS, nb + 1),
        in_specs=specs + [pl.BlockSpec((SWA_GROUP, BLOCK, 2 * BLOCK), lambda h, n: (h, 0, 0)),
                          pl.BlockSpec((1, SWA_GROUP, LANES), lambda h, n: (h, 0, 0)),
                          pl.BlockSpec((BLOCK, gw), lambda h, n: (cur(n), h))],
        out_specs=[pl.BlockSpec((BLOCK, gw), lambda h, n: (cur(n), h)),
                   pl.BlockSpec((BLOCK, gw), lambda h, n: (cur(n), h)),
                   pl.BlockSpec((1, BLOCK, HEAD_DIM), lambda h, n: (h, trail(n), 0)),
                   pl.BlockSpec((1, BLOCK, HEAD_DIM), lambda h, n: (h, trail(n), 0)),
                   pl.BlockSpec((SWA_GROUP, BLOCK, 2 * BLOCK), lambda h, n: (h, 0, 0)),
                   pl.BlockSpec((1, SWA_GROUP, LANES), lambda h, n: (h, 0, 0))],
        out_shape=[jax.ShapeDtypeStruct((s, SWA_HEADS * HEAD_DIM), BF16),
                   jax.ShapeDtypeStruct((s, SWA_HEADS * HEAD_DIM), BF16),
                   jax.ShapeDtypeStruct((SWA_KV_HEADS, s, HEAD_DIM), F32),
                   jax.ShapeDtypeStruct((SWA_KV_HEADS, s, HEAD_DIM), F32),
                   jax.ShapeDtypeStruct((SWA_HEADS, BLOCK, 2 * BLOCK), F32),
                   jax.ShapeDtypeStruct((SWA_KV_HEADS, SWA_GROUP, LANES), F32)],
        scratch_shapes=[pltpu.VMEM((BLOCK, HEAD_DIM), F32), pltpu.VMEM((BLOCK, HEAD_DIM), F32)],
        compiler_params=_cp(("arbitrary", "arbitrary")),
    )(*args, bias, sinks, da)


FOX_TILE = 1024


def _below_and_on_diagonal(i, j, step):
    @pl.when(j < i)
    def _():
        step(False)

    @pl.when(j == i)
    def _():
        step(True)


FOX_AUG = 128
FOX_CQ, FOX_CK = HEAD_DIM, HEAD_DIM + 1


def _fox_operands(proj, csum):
    s = proj.shape[0]
    hw = FOX_HEADS * HEAD_DIM
    heads = lambda a: a.reshape(s, FOX_HEADS, HEAD_DIM).transpose(1, 0, 2)
    q = heads(proj[:, :hw]) * jnp.asarray(HEAD_DIM ** -0.5, BF16)
    k, v = heads(proj[:, hw:2 * hw]), heads(proj[:, 2 * hw:3 * hw])
    one = jnp.ones((FOX_HEADS, s, 1), BF16)
    zero = jnp.zeros((FOX_HEADS, s, 1), BF16)
    pad = jnp.zeros((FOX_HEADS, s, FOX_AUG - HEAD_DIM - 2), BF16)
    qa = jnp.concatenate([q, zero, -one, pad], axis=-1)
    ka = jnp.concatenate([k, one, zero, pad], axis=-1)
    tr = lambda a: a.transpose(0, 2, 1)
    c = csum[:, :FOX_HEADS].T
    return (qa, ka, tr(qa), tr(ka), v, tr(v), c[:, None, :], jnp.broadcast_to(c[:, :, None], (FOX_HEADS, s, LANES)))


def _fox_scores_t(ka, qa, cq, ck, diag):
    st = _dot(ka, qa, 1, 1) + cq - jnp.concatenate([ck] * (qa.shape[0] // LANES), axis=1)
    if diag:
        st = jnp.where(lax.broadcasted_iota(jnp.int32, st.shape, 0) <= lax.broadcasted_iota(jnp.int32, st.shape, 1), st, NEG)
    return st


def _fox_attention(qa, ka, vt, c_row, c_lanes):
    nh, s, aw = qa.shape
    t = min(FOX_TILE, s)
    nt = s // t
    hd = HEAD_DIM

    def body(qa_ref, ka_ref, vt_ref, cq_ref, ck_ref, o_ref, lse_ref, m_s, l_s, acc_s):
        i, j = pl.program_id(1), pl.program_id(2)

        @pl.when(j == 0)
        def _():
            m_s[...] = jnp.full_like(m_s, NEG)
            l_s[...] = jnp.zeros_like(l_s)
            acc_s[...] = jnp.zeros_like(acc_s)

        def step(diag):
            st = _fox_scores_t(ka_ref[0], qa_ref[0], cq_ref[0], ck_ref[0], diag)
            m_old = m_s[...]
            m_new = jnp.maximum(m_old, jnp.max(st, axis=0, keepdims=True))
            alpha = jnp.exp(m_old - m_new)
            p = jnp.exp(st - m_new)
            l_s[...] = alpha * l_s[...] + jnp.sum(p, axis=0, keepdims=True)
            acc_s[...] = alpha * acc_s[...] + _dot(vt_ref[0], p)
            m_s[...] = m_new

        _below_and_on_diagonal(i, j, step)

        @pl.when(j == nt - 1)
        def _():
            o_ref[0] = (acc_s[...] / l_s[...]).astype(o_ref.dtype)
            lse_ref[0] = m_s[...] + jnp.log(l_s[...])

    kj = lambda i, j: jnp.minimum(j, i)
    return pl.pallas_call(
        body, name="fox_fwd", grid=(nh, nt, nt),
        in_specs=[pl.BlockSpec((1, t, aw), lambda h, i, j: (h, i, 0)),
                  pl.BlockSpec((1, t, aw), lambda h, i, j: (h, kj(i, j), 0)),
                  pl.BlockSpec((1, hd, t), lambda h, i, j: (h, 0, kj(i, j))),
                  pl.BlockSpec((1, 1, t), lambda h, i, j: (h, 0, i)),
                  pl.BlockSpec((1, t, LANES), lambda h, i, j: (h, kj(i, j), 0))],
        out_specs=[pl.BlockSpec((1, hd, t), lambda h, i, j: (h, 0, i)),
                   pl.BlockSpec((1, 1, t), lambda h, i, j: (h, 0, i))],
        out_shape=[jax.ShapeDtypeStruct((nh, hd, s), BF16), jax.ShapeDtypeStruct((nh, 1, s), F32)],
        scratch_shapes=[pltpu.VMEM((1, t), F32), pltpu.VMEM((1, t), F32), pltpu.VMEM((hd, t), F32)],
        compiler_params=_cp(("parallel", "parallel", "arbitrary")),
    )(qa, ka, vt, c_row, c_lanes)


def _fox_attention_bwd(qa, ka, qat, kat, v, c_row, c_lanes, ot, dot, lse):
    nh, s, aw = qa.shape
    t = min(FOX_TILE, s)
    nt = s // t
    hd = HEAD_DIM

    def body(qa_ref, ka_ref, qat_ref, kat_ref, v_ref, cq_ref, ck_ref, ot_ref, dot_ref, lse_ref,
             dq_ref, dcq_ref, dk_ref, dck_ref, dv_ref, dqa_s, dk_s, dv_s):
        j, i = pl.program_id(1), pl.program_id(2)

        @pl.when((j == 0) & (i == 0))
        def _():
            dqa_s[...] = jnp.zeros_like(dqa_s)

        @pl.when(i == 0)
        def _():
            dk_s[...] = jnp.zeros_like(dk_s)
            dv_s[...] = jnp.zeros_like(dv_s)

        def step(diag):
            p = jnp.exp(_fox_scores_t(ka_ref[0], qa_ref[0], cq_ref[0], ck_ref[0], diag) - lse_ref[0])
            do_t = dot_ref[0]
            delta = jnp.sum(do_t.astype(F32) * ot_ref[0].astype(F32), axis=0, keepdims=True)
            ds = (p * (_dot(v_ref[0], do_t) - delta)).astype(BF16)
            dv_s[...] += _dot(do_t, p, 1, 1)
            dk_s[...] += _dot(qat_ref[0], ds, 1, 1)
            cols = pl.ds(pl.multiple_of(i * t, t), t)
            dqa_s[:, cols] += _dot(kat_ref[0], ds)

        _below_and_on_diagonal(i, j, step)

        @pl.when(i == nt - 1)
        def _():
            dk_ref[0] = dk_s[:hd].astype(dk_ref.dtype)
            dck_ref[0] = dk_s[FOX_CK:FOX_CK + 1]
            dv_ref[0] = dv_s[...].astype(dv_ref.dtype)

        @pl.when((j == nt - 1) & (i == nt - 1))
        def _():
            dq_ref[0] = (dqa_s[:hd] * (hd ** -0.5)).astype(dq_ref.dtype)
            dcq_ref[0] = dqa_s[FOX_CQ:FOX_CQ + 1]

    qi = lambda i, j: jnp.maximum(i, j)
    return pl.pallas_call(
        body, name="fox_bwd", grid=(nh, nt, nt),
        in_specs=[pl.BlockSpec((1, t, aw), lambda h, j, i: (h, qi(i, j), 0)),
                  pl.BlockSpec((1, t, aw), lambda h, j, i: (h, j, 0)),
                  pl.BlockSpec((1, aw, t), lambda h, j, i: (h, 0, qi(i, j))),
                  pl.BlockSpec((1, aw, t), lambda h, j, i: (h, 0, j)),
                  pl.BlockSpec((1, t, hd), lambda h, j, i: (h, j, 0)),
                  pl.BlockSpec((1, 1, t), lambda h, j, i: (h, 0, qi(i, j))),
                  pl.BlockSpec((1, t, LANES), lambda h, j, i: (h, j, 0)),
                  pl.BlockSpec((1, hd, t), lambda h, j, i: (h, 0, qi(i, j))),
                  pl.BlockSpec((1, hd, t), lambda h, j, i: (h, 0, qi(i, j))),
                  pl.BlockSpec((1, 1, t), lambda h, j, i: (h, 0, qi(i, j)))],
        out_specs=[pl.BlockSpec((1, hd, s), lambda h, j, i: (h, 0, 0)),
                   pl.BlockSpec((1, 1, s), lambda h, j, i: (h, 0, 0)),
                   pl.BlockSpec((1, hd, t), lambda h, j, i: (h, 0, j)),
                   pl.BlockSpec((1, 1, t), lambda h, j, i: (h, 0, j)),
                   pl.BlockSpec((1, hd, t), lambda h, j, i: (h, 0, j))],
        out_shape=[jax.ShapeDtypeStruct((nh, hd, s), BF16), jax.ShapeDtypeStruct((nh, 1, s), F32),
                   jax.ShapeDtypeStruct((nh, hd, s), BF16), jax.ShapeDtypeStruct((nh, 1, s), F32),
                   jax.ShapeDtypeStruct((nh, hd, s), BF16)],
        scratch_shapes=[pltpu.VMEM((aw, s), F32), pltpu.VMEM((aw, t), F32), pltpu.VMEM((hd, t), F32)],
        compiler_params=_cp(("parallel", "arbitrary", "arbitrary")),
    )(qa, ka, qat, kat, v, c_row, c_lanes, ot, dot, lse)


def _ssm_prep(lam_re, lam_im, log_dt, b_re, b_im, c_re, c_im):
    g, n = lam_re.shape
    dt = jnp.exp(log_dt)[:, None]
    mag = jnp.exp(lam_re * dt)
    ab_re = mag * jnp.cos(lam_im * dt)
    ab_im = mag * jnp.sin(lam_im * dt)
    den = lam_re * lam_re + lam_im * lam_im
    nr = ab_re - 1.0
    coef_re = ((nr * lam_re + ab_im * lam_im) / den)[..., None]
    coef_im = ((ab_im * lam_re - nr * lam_im) / den)[..., None]
    bb_re = coef_re * b_re - coef_im * b_im
    bb_im = coef_re * b_im + coef_im * b_re
    nblk = g // GROUPS_PER_STEP
    eye = jnp.eye(GROUPS_PER_STEP, dtype=F32)

    def bdiag(bb):
        return jnp.einsum('bgnc,gh->bgchn', bb.reshape(nblk, GROUPS_PER_STEP, n, SSM_GROUP), eye).reshape(
            nblk, GROUPS_PER_STEP * SSM_GROUP, STATE_W)

    def cdiag(cc):
        return jnp.einsum('bgcn,gh->bgnhc', cc.reshape(nblk, GROUPS_PER_STEP, SSM_GROUP, n), eye).reshape(
            nblk, STATE_W, GROUPS_PER_STEP * SSM_GROUP)

    return (ab_re.reshape(nblk, 1, STATE_W), ab_im.reshape(nblk, 1, STATE_W),
            bdiag(bb_re), bdiag(bb_im), cdiag(c_re), cdiag(c_im))


def _powers(a_re, a_im, levels):
    rs, ims = [a_re], [a_im]
    for _ in range(levels - 1):
        r, i = rs[-1], ims[-1]
        rs.append(r * r - i * i)
        ims.append(2.0 * r * i)
    return jnp.concatenate(rs, axis=1), jnp.concatenate(ims, axis=1)


def _power_table(a_re, a_im, n):
    tr, ti = a_re, a_im
    while tr.shape[1] < n:
        lr, li = tr[:, -1:], ti[:, -1:]
        tr, ti = (jnp.concatenate([tr, tr * lr - ti * li], axis=1), jnp.concatenate([ti, tr * li + ti * lr], axis=1))
    return tr, ti


def _ssm_fwd(proj, pw_re, pw_im, tab_re, tab_im, bd_re, bd_im, cd_re, cd_im, dskip):
    s = proj.shape[0]
    w = dskip.shape[1]
    nblk = w // LANES
    nch = STATE_W // LANES
    levels = pw_re.shape[1]

    def body(u_ref, pr, pi, tr, ti, br, bi, cr, ci, d_ref, pre_ref, yg_ref, xr_ref, xi_ref):
        ch = pl.program_id(1)
        u = u_ref[...]
        xr, xi = _scan_complex(_dot(u, br[0]), _dot(u, bi[0]), pr[0], pi[0], tr[0], ti[0], False)
        xr_ref[...] = xr.astype(BF16)
        xi_ref[...] = xi.astype(BF16)
        yc = _dot(xr, cr[0]) - _dot(xi, ci[0])

        @pl.when(ch == 0)
        def _():
            pre_ref[...] = yc + d_ref[...] * u

        @pl.when(ch > 0)
        def _():
            pre_ref[...] += yc

        @pl.when(ch == nch - 1)
        def _():
            yg_ref[...] = _gelu(pre_ref[...]).astype(BF16)

    blk = lambda b, c: (0, b)
    col = lambda b, c: (0, b * nch + c)
    in_chunk = pl.BlockSpec((1, LANES, LANES), lambda b, c: (b, 0, c))
    out_chunk = pl.BlockSpec((1, LANES, LANES), lambda b, c: (b, c, 0))
    pw_spec = pl.BlockSpec((1, levels, LANES), lambda b, c: (b, 0, c))
    tab_spec = pl.BlockSpec((1, tab_re.shape[1], LANES), lambda b, c: (b, 0, c))
    return pl.pallas_call(
        body, name="ssm_fwd", grid=(nblk, nch),
        in_specs=[pl.BlockSpec((s, LANES), blk), pw_spec, pw_spec, tab_spec, tab_spec, in_chunk, in_chunk,
                  out_chunk, out_chunk, pl.BlockSpec((1, LANES), blk)],
        out_specs=[pl.BlockSpec((s, LANES), blk), pl.BlockSpec((s, LANES), blk),
                   pl.BlockSpec((s, LANES), col), pl.BlockSpec((s, LANES), col)],
        out_shape=[jax.ShapeDtypeStruct((s, w), F32), jax.ShapeDtypeStruct((s, w), BF16),
                   jax.ShapeDtypeStruct((s, nblk * STATE_W), BF16), jax.ShapeDtypeStruct((s, nblk * STATE_W), BF16)],
        compiler_params=_cp(("parallel", "arbitrary")),
    )(proj, pw_re, pw_im, tab_re, tab_im, bd_re, bd_im, cd_re, cd_im, dskip)


def _ssm_bwd(proj, dyg, pre, xr_all, xi_all, pw_re, pw_im, tab_re, tab_im, bd_re, bd_im, cd_re, cd_im, dskip):
    s = proj.shape[0]
    w = dskip.shape[1]
    nblk = w // LANES
    nch = STATE_W // LANES
    levels = pw_re.shape[1]

    def body(u_ref, dy_ref, pre_ref, xr_ref, xi_ref, pr, pi, tr, ti, br, bi, cr, ci, d_ref,
             du_ref, dd_ref, dar_ref, dai_ref, dbr_ref, dbi_ref, dcr_ref, dci_ref):
        ch = pl.program_id(1)
        u = u_ref[...]
        _, vjp = jax.vjp(_gelu, pre_ref[...])
        dpre = vjp(dy_ref[...].astype(F32))[0]
        zr, zi = _scan_complex(_dot(dpre, cr[0], 1, 1), -_dot(dpre, ci[0], 1, 1), pr[0], -pi[0], tr[0], -ti[0], True)
        xpr = _shift_rows(xr_ref[...].astype(F32), 1)
        xpi = _shift_rows(xi_ref[...].astype(F32), 1)
        dar_ref[0] = jnp.sum(zr * xpr + zi * xpi, axis=0, keepdims=True)
        dai_ref[0] = jnp.sum(zi * xpr - zr * xpi, axis=0, keepdims=True)
        dcr_ref[0] = _dot(xr_ref[...], dpre, 0, 0)
        dci_ref[0] = -_dot(xi_ref[...], dpre, 0, 0)
        dbr_ref[0] = _dot(u, zr, 0, 0)
        dbi_ref[0] = _dot(u, zi, 0, 0)
        duc = _dot(zr, br[0], 1, 1) + _dot(zi, bi[0], 1, 1)

        @pl.when(ch == 0)
        def _():
            du_ref[...] = duc + dpre * d_ref[...]
            dd_ref[...] = jnp.sum(dpre * u, axis=0, keepdims=True)

        @pl.when(ch > 0)
        def _():
            du_ref[...] += duc

    blk = lambda b, c: (0, b)
    col = lambda b, c: (0, b * nch + c)
    in_chunk = pl.BlockSpec((1, LANES, LANES), lambda b, c: (b, 0, c))
    out_chunk = pl.BlockSpec((1, LANES, LANES), lambda b, c: (b, c, 0))
    pw_spec = pl.BlockSpec((1, levels, LANES), lambda b, c: (b, 0, c))
    a_spec = pl.BlockSpec((1, 1, LANES), lambda b, c: (b, 0, c))
    tab_spec = pl.BlockSpec((1, tab_re.shape[1], LANES), lambda b, c: (b, 0, c))
    return pl.pallas_call(
        body, name="ssm_bwd", grid=(nblk, nch),
        in_specs=[pl.BlockSpec((s, LANES), blk), pl.BlockSpec((s, LANES), blk), pl.BlockSpec((s, LANES), blk),
                  pl.BlockSpec((s, LANES), col), pl.BlockSpec((s, LANES), col),
                  pw_spec, pw_spec, tab_spec, tab_spec, in_chunk, in_chunk, out_chunk, out_chunk,
                  pl.BlockSpec((1, LANES), blk)],
        out_specs=[pl.BlockSpec((s, LANES), blk), pl.BlockSpec((1, LANES), blk), a_spec, a_spec,
                   in_chunk, in_chunk, out_chunk, out_chunk],
        out_shape=[jax.ShapeDtypeStruct((s, w), F32), jax.ShapeDtypeStruct((1, w), F32),
                   jax.ShapeDtypeStruct((nblk, 1, STATE_W), F32), jax.ShapeDtypeStruct((nblk, 1, STATE_W), F32),
                   jax.ShapeDtypeStruct((nblk, LANES, STATE_W), F32), jax.ShapeDtypeStruct((nblk, LANES, STATE_W), F32),
                   jax.ShapeDtypeStruct((nblk, STATE_W, LANES), F32), jax.ShapeDtypeStruct((nblk, STATE_W, LANES), F32)],
        compiler_params=_cp(("parallel", "arbitrary")),
    )(proj, dyg, pre, xr_all, xi_all, pw_re, pw_im, tab_re, tab_im, bd_re, bd_im, cd_re, cd_im, dskip)


ROW_TILE = 256


def _norm_fwd(x, g, name):
    s, d = x.shape
    t = min(ROW_TILE, s)
    return _tiled(lambda xv, gv: (_rms(xv, gv),), [_rows(x, t), _full(g)], [_orows(s, d, BF16, t)], s // t, name)[0]


def _norm_bwd(x, g, dh_list, dx_in, name):
    s, d = x.shape
    t = min(ROW_TILE, s)
    nh = len(dh_list)

    def fn(xv, gv, dxv, *dhs):
        dh = dhs[0].astype(F32)
        for other in dhs[1:]:
            dh = dh + other.astype(F32)
        _, vjp = jax.vjp(_rms, xv, gv)
        dx, dg = vjp(dh)
        dx = dx + dxv
        return dx, dx, dg

    return _tiled(fn, [_rows(x, t), _full(g), _rows(dx_in, t)] + [_rows(a, t) for a in dh_list],
                  [_orows(s, d, F32, t), _orows(s, d, BF16, t), _oacc((1, d))], s // t, name)


def _ple_fwd(x, y, pn, name):
    s, d = x.shape
    t = min(ROW_TILE, s)

    def fn(xv, yv, gv):
        x1 = xv + yv
        return x1, _rms(x1, gv)

    return _tiled(fn, [_rows(x, t), _rows(y, t), _full(pn)], [_orows(s, d, F32, t), _orows(s, d, BF16, t)], s // t, name)


def _ple_mix(x1, emb, gl, name):
    s, d = x1.shape
    t = min(ROW_TILE, s)
    return _tiled(lambda a, e, g: (a + e * jax.nn.sigmoid(g),), [_rows(x1, t), _rows(emb, t), _rows(gl, t)],
                  [_orows(s, d, F32, t)], s // t, name)[0]


def _ple_mix_bwd(dx2, emb, gl, name):
    s, d = dx2.shape
    t = min(ROW_TILE, s)

    def fn(dx, e, g):
        sg = jax.nn.sigmoid(g)
        return dx * sg, dx * e * sg * (1.0 - sg)

    return _tiled(fn, [_rows(dx2, t), _rows(emb, t), _rows(gl, t)],
                  [_orows(s, d, BF16, t), _orows(s, d, BF16, t)], s // t, name)


def _loss_grad(x, target, g):
    s, d = x.shape
    t = min(ROW_TILE, s)

    def fn(xv, tv, gv):
        def f(xx, gg):
            err = _rms(xx, gg) - tv
            return 0.5 * jnp.sum(jnp.mean(err * err, axis=-1, keepdims=True), axis=0, keepdims=True)

        loss, vjp = jax.vjp(f, xv, gv)
        dx, dg = vjp(jnp.ones((1, 1), F32))
        return loss, dx, dg

    return _tiled(fn, [_rows(x, t), _rows(target, t), _full(g)],
                  [_oacc((1, 1)), _orows(s, d, F32, t), _oacc((1, d))], s // t, "loss_grad")


def _conv_fwd(proj, kern):
    s = proj.shape[0]
    w = kern.shape[1]
    nb = w // LANES

    def fn(bg, cg, u, gate, k):
        z = cg.astype(F32) * u.astype(F32)
        conv = k[2:3] * z + k[1:2] * _shift_rows(z, 1) + k[0:1] * _shift_rows(z, 2)
        return (bg.astype(F32) * conv * _silu(gate.astype(F32)),)

    return _tiled(fn, [_cols(proj, LANES, q * nb) for q in range(4)] + [_cols(kern, LANES)],
                  [_ocols(s, w, BF16, LANES)], nb, "conv_fwd")[0]


def _conv_bwd(proj, kern, da):
    s = proj.shape[0]
    w = kern.shape[1]
    nb = w // LANES

    def fn(bg, cg, u, gate, k, dav):
        bg, cg, u, gate, dav = (a.astype(F32) for a in (bg, cg, u, gate, dav))
        z = cg * u
        z1, z2 = _shift_rows(z, 1), _shift_rows(z, 2)
        conv = k[2:3] * z + k[1:2] * z1 + k[0:1] * z2
        sg = jax.nn.sigmoid(gate)
        dy = dav * gate * sg
        dgate = dav * bg * conv * sg * (1.0 + gate * (1.0 - sg))
        dconv = dy * bg
        dk = jnp.concatenate([jnp.sum(dconv * zz, axis=0, keepdims=True) for zz in (z2, z1, z)], axis=0)
        dz = k[2:3] * dconv + k[1:2] * _shift_rows(dconv, 1, True) + k[0:1] * _shift_rows(dconv, 2, True)
        return dy * conv, dz * u, dz * cg, dgate, dk

    return _tiled(fn, [_cols(proj, LANES, q * nb) for q in range(4)] + [_cols(kern, LANES), _cols(da, LANES)],
                  [_ocols(s, w, BF16, LANES)] * 4 + [_ocols(3, w, F32, LANES)], nb, "conv_bwd")


def _glu_fwd(gl, proj, bglu):
    s, w2 = gl.shape
    w = w2 // 2
    t = min(ROW_TILE, s)

    def fn(glv, gate, b):
        v = glv + b
        return (v[:, :w] * jax.nn.sigmoid(v[:, w:]) * _silu(gate),)

    return _tiled(fn, [_rows(gl, t), (proj, (t, w), lambda i: (i, 1)), _full(bglu)],
                  [_orows(s, w, BF16, t)], s // t, "glu_fwd")[0]


def _glu_bwd(gl, proj, bglu, da):
    s, w2 = gl.shape
    w = w2 // 2
    t = min(ROW_TILE, s)

    def fn(glv, gate, b, dav):
        def f(gg, gt, bb):
            v = gg + bb
            return v[:, :w] * jax.nn.sigmoid(v[:, w:]) * _silu(gt)

        _, vjp = jax.vjp(f, glv, gate, b)
        return vjp(dav.astype(F32))

    return _tiled(fn, [_rows(gl, t), (proj, (t, w), lambda i: (i, 1)), _full(bglu), _rows(da, t)],
                  [_orows(s, w2, BF16, t), _orows(s, w, BF16, t), _oacc((1, w2))], s // t, "glu_bwd")


def _fg_fwd(z, b):
    def fn(zv, bv):
        v = zv + bv
        logf = jnp.minimum(v, 0.0) - jnp.log(1.0 + jnp.exp(-jnp.abs(v)))
        return (_cumsum_rows(logf, False),)

    return _tiled(fn, [_full(z), _full(b)], [_out(z.shape, F32, z.shape, lambda i: (0, 0))], 1, "fg_fwd")[0]


def _fg_bwd(z, b, dcs):
    def fn(zv, bv, dc):
        dz = _cumsum_rows(dc, True) * jax.nn.sigmoid(-(zv + bv))
        return dz, jnp.sum(dz, axis=0, keepdims=True)

    return _tiled(fn, [_full(z), _full(b), _full(dcs)],
                  [_out(z.shape, BF16, z.shape, lambda i: (0, 0)), _out((1, z.shape[1]), F32, (1, z.shape[1]), lambda i: (0, 0))],
                  1, "fg_bwd")


def _fox_gate_bwd(da, o, proj):
    s, w = o.shape
    t = min(ROW_TILE, s)

    def fn(dav, ov, gate):
        dav, ov, gate = dav.astype(F32), ov.astype(F32), gate.astype(F32)
        sg = jax.nn.sigmoid(gate)
        return dav * gate * sg, dav * ov * sg * (1.0 + gate * (1.0 - sg))

    return _tiled(fn, [_rows(da, t), _rows(o, t), (proj, (t, w), lambda i: (i, 3))],
                  [_orows(s, w, BF16, t), _orows(s, w, BF16, t)], s // t, "fox_gate_bwd")


def _fox_gate_fwd(o, proj):
    s, w = o.shape
    t = min(ROW_TILE, s)
    return _tiled(lambda ov, gate: (ov.astype(F32) * _silu(gate.astype(F32)),),
                  [_rows(o, t), (proj, (t, w), lambda i: (i, 3))], [_orows(s, w, BF16, t)], s // t, "fox_gate_fwd")[0]


def _local_step(x, p, target, w):
    s, d = x.shape
    depth = p.shape[0]
    pb = p.astype(BF16)
    grads = {}
    saved = []

    bucket = _swa_bucket_table()
    onehot = np.eye(REL_BUCKETS, dtype=np.float32)[bucket.reshape(-1)]
    bias = _bias_table(w['rel_bias'], jnp.asarray(onehot.T, BF16))
    sinks = jnp.broadcast_to(w['swa_sinks'].reshape(SWA_KV_HEADS, SWA_GROUP, 1), (SWA_KV_HEADS, SWA_GROUP, LANES))
    ssm_params = tuple(w[k][0] for k in ('ssm_lam_re', 'ssm_lam_im', 'ssm_log_dt', 'ssm_b_re', 'ssm_b_im', 'ssm_c_re', 'ssm_c_im'))
    (a_re, a_im, bd_re, bd_im, cd_re, cd_im), ssm_vjp = jax.vjp(_ssm_prep, *ssm_params)
    levels = max(1, (s - 1).bit_length())
    pw_re, pw_im = _powers(a_re, a_im, levels)
    tab_re, tab_im = _power_table(a_re, a_im, min(SCAN_CHUNK, s))
    rtab_re, rtab_im = jnp.flip(tab_re, axis=1), jnp.flip(tab_im, axis=1)
    wfg = jnp.pad(w['fox_w_fg'][0], ((0, 0), (0, LANES - FOX_HEADS)))
    bfg = jnp.pad(w['fox_b_fg'], ((0, 0), (0, LANES - FOX_HEADS)))

    def qkv4(a):
        return a.reshape(s, SWA_KV_HEADS, HEAD_DIM).transpose(1, 0, 2)

    def unheads(a):
        return a.transpose(2, 0, 1).reshape(s, FOX_HEADS * HEAD_DIM)

    for i in range(depth):
        mixer = i % 4
        hn = _norm_fwd(x, w['norm_g'][i:i + 1], f"norm_fwd{i}")
        sv = {'x': x, 'hn': hn}
        if mixer == 0:
            proj = _mm(hn, w['swa_w_in'][:, 0], b_split=True, name="swa_in")
            qw = SWA_HEADS * HEAD_DIM
            kvw = SWA_KV_HEADS * HEAD_DIM
            k4, v4 = qkv4(proj[:, qw:qw + kvw]), qkv4(proj[:, qw + kvw:qw + 2 * kvw])
            a = _swa_fwd(proj, k4, v4, bias, sinks)
            sv.update(proj=proj, k4=k4, v4=v4)
            w_out = w['swa_w_out'][0]
        elif mixer == 1:
            proj = _mm(hn, w['conv_w_in'][:, 0], b_split=True, name="conv_in")
            a = _conv_fwd(proj, w['conv_kernel'][0])
            sv.update(proj=proj)
            w_out = w['conv_w_out'][0]
        elif mixer == 2:
            proj = _mm(hn, w['ssm_w_in'][:, 0], b_split=True, out_dtype=F32, name="ssm_in")
            pre, yg, xr_all, xi_all = _ssm_fwd(proj, pw_re, pw_im, tab_re, tab_im, bd_re, bd_im, cd_re, cd_im, w['ssm_d'])
            gl = _mm(yg, w['ssm_w_glu'][:, 0], b_split=True, out_dtype=F32, name="ssm_glu")
            a = _glu_fwd(gl, proj, w['ssm_b_glu'])
            sv.update(proj=proj, pre=pre, yg=yg, xr=xr_all, xi=xi_all, gl=gl)
            w_out = w['ssm_w_out'][0]
        else:
            proj = _mm(hn, w['fox_w_in'][:, 0], b_split=True, name="fox_in")
            z = _mm(hn, wfg, out_dtype=F32, name="fox_fg")
            fox_ops = _fox_operands(proj, _fg_fwd(z, bfg))
            ot, lse = _fox_attention(fox_ops[0], fox_ops[1], fox_ops[5], fox_ops[6], fox_ops[7])
            o = unheads(ot)
            a = _fox_gate_fwd(o, proj)
            sv.update(proj=proj, z=z, fox_ops=fox_ops, ot=ot, o=o, lse=lse)
            w_out = w['fox_w_out'][0]
        y = _mm(a, w_out, out_dtype=F32, name=f"mixer_out{i}")
        x1, gn = _ple_fwd(x, y, w['ple_norm'][i:i + 1], f"ple_fwd{i}")
        emb = _mm(pb[i], w['ple_proj'][:, i], b_split=True, out_dtype=F32, name=f"ple_emb{i}")
        gl2 = _mm(gn, w['ple_gate'][i], out_dtype=F32, name=f"ple_gate{i}")
        x = _ple_mix(x1, emb, gl2, f"ple_mix{i}")
        sv.update(a=a, x1=x1, gn=gn, emb=emb, gl2=gl2)
        saved.append(sv)

    loss, dx, grads['final_g'] = _loss_grad(x, target, w['final_g'].reshape(1, d))
    grads['final_g'] = grads['final_g'].reshape(d)

    g_norm, g_ple_norm, g_ple_proj, g_ple_gate = [None] * depth, [None] * depth, [None] * depth, [None] * depth
    for i in reversed(range(depth)):
        sv = saved[i]
        mixer = i % 4
        demb, dgl2 = _ple_mix_bwd(dx, sv['emb'], sv['gl2'], f"ple_mix_bwd{i}")
        g_ple_proj[i] = _mm(pb[i], demb, ta=True, out_split=N_CHIPS, name=f"ple_emb_dw{i}")
        g_ple_gate[i] = _mm(sv['gn'], dgl2, ta=True, name=f"ple_gate_dw{i}")
        dgn = _mm(dgl2, w['ple_gate'][i], tb=True, out_dtype=F32, name=f"ple_gate_dx{i}")
        dx1, dy, g_ple_norm[i] = _norm_bwd(sv['x1'], w['ple_norm'][i:i + 1], [dgn], dx, f"ple_norm_bwd{i}")
        w_out_name = ('swa_w_out', 'conv_w_out', 'ssm_w_out', 'fox_w_out')[mixer]
        grads[w_out_name] = _mm(sv['a'], dy, ta=True, name=f"mixer_out_dw{i}")[None]
        da = _mm(dy, w[w_out_name][0], tb=True, name=f"mixer_out_dx{i}")
        proj = sv['proj']
        dhs = []
        if mixer == 0:
            dq, dgate, dk4, dv4, dbias, dsink = _swa_bwd(proj, sv['k4'], sv['v4'], bias, sinks, da)
            back = lambda t4: t4.transpose(1, 0, 2).reshape(s, SWA_KV_HEADS * HEAD_DIM).astype(BF16)
            dproj = jnp.concatenate([dq, back(dk4), back(dv4), dgate], axis=1)
            grads['rel_bias'] = _bias_grad(dbias.reshape(SWA_HEADS, -1), jnp.asarray(onehot, BF16))
            grads['swa_sinks'] = dsink[:, :, 0].reshape(1, SWA_HEADS)
            w_in_name = 'swa_w_in'
        elif mixer == 1:
            dbg, dcg, du, dgate, dkern = _conv_bwd(proj, w['conv_kernel'][0], da)
            dproj = jnp.concatenate([dbg, dcg, du, dgate], axis=1)
            grads['conv_kernel'] = dkern[None]
            w_in_name = 'conv_w_in'
        elif mixer == 2:
            dgl, dgate, dbglu = _glu_bwd(sv['gl'], proj, w['ssm_b_glu'], da)
            grads['ssm_b_glu'] = dbglu
            grads['ssm_w_glu'] = _mm(sv['yg'], dgl, ta=True, out_split=N_CHIPS, name="ssm_glu_dw")[:, None]
            dyg = _mm(dgl, w['ssm_w_glu'][:, 0], tb=True, b_split=True, out_dtype=F32, name="ssm_glu_dx")
            du, dd, da_re, da_im, dbd_re, dbd_im, dcd_re, dcd_im = _ssm_bwd(
                proj, dyg, sv['pre'], sv['xr'], sv['xi'], pw_re, pw_im, rtab_re, rtab_im, bd_re, bd_im, cd_re, cd_im, w['ssm_d'])
            grads['ssm_d'] = dd
            dparams = ssm_vjp((da_re, da_im, dbd_re, dbd_im, dcd_re, dcd_im))
            for k, v in zip(('ssm_lam_re', 'ssm_lam_im', 'ssm_log_dt', 'ssm_b_re', 'ssm_b_im', 'ssm_c_re', 'ssm_c_im'), dparams):
                grads[k] = v[None]
            dproj = jnp.concatenate([du.astype(BF16), dgate], axis=1)
            w_in_name = 'ssm_w_in'
        else:
            do, dgate = _fox_gate_bwd(da, sv['o'], proj)
            qa, ka, qat, kat, v, _, c_row, c_lanes = sv['fox_ops']
            dot = do.reshape(s, FOX_HEADS, HEAD_DIM).transpose(1, 2, 0)
            dqt, dcq, dkt, dck, dvt = _fox_attention_bwd(qa, ka, qat, kat, v, c_row, c_lanes, sv['ot'], dot, sv['lse'])
            dq, dk, dv = unheads(dqt), unheads(dkt), unheads(dvt)
            dcs = jnp.pad((dcq[:, 0] + dck[:, 0]).T, ((0, 0), (0, LANES - FOX_HEADS)))
            dz, dbfg = _fg_bwd(sv['z'], bfg, dcs)
            grads['fox_b_fg'] = dbfg[:, :FOX_HEADS]
            grads['fox_w_fg'] = _mm(sv['hn'], dz, ta=True, out_dtype=BF16, name="fox_fg_dw")[:, :FOX_HEADS][None]
            dhs.append(_mm(dz, wfg, tb=True, out_dtype=F32, name="fox_fg_dx"))
            dproj = jnp.concatenate([dq, dk, dv, dgate], axis=1)
            w_in_name = 'fox_w_in'
        grads[w_in_name] = _mm(sv['hn'], dproj, ta=True, out_split=N_CHIPS, name=f"mixer_in_dw{i}")[:, None]
        dhs.append(_mm(dproj, w[w_in_name][:, 0], tb=True, b_split=True, out_dtype=F32, name=f"mixer_in_dx{i}"))
        dx, _, g_norm[i] = _norm_bwd(sv['x'], w['norm_g'][i:i + 1], dhs, dx1, f"norm_bwd{i}")

    grads['norm_g'] = jnp.concatenate(g_norm, axis=0)
    grads['ple_norm'] = jnp.concatenate(g_ple_norm, axis=0)
    grads['ple_proj'] = jnp.stack(g_ple_proj, axis=1)
    grads['ple_gate'] = jnp.stack(g_ple_gate)
    return loss, dx, grads


def _bf16_terms(a):
    hi = lax.reduce_precision(a, 8, 7)
    r1 = a - hi
    mid = lax.reduce_precision(r1, 8, 7)
    lo = lax.reduce_precision(r1 - mid, 8, 7)
    return hi.astype(BF16), mid.astype(BF16), lo.astype(BF16)


def _split3(a):
    return jnp.concatenate(_bf16_terms(a), axis=0)


def _bias_grad(dbias, onehot):
    out = _mm(_split3(dbias), onehot, out_dtype=F32, name="rel_bias_grad", tk=2048)
    nh = dbias.shape[0]
    return (out[:nh] + out[nh:2 * nh] + out[2 * nh:]).T


def _bias_table(rel_bias, onehot_t):
    nh = rel_bias.shape[1]
    out = _mm(_split3(rel_bias.T), onehot_t, out_dtype=F32, name="rel_bias_table")
    return (out[:nh] + out[nh:2 * nh] + out[2 * nh:]).reshape(nh, BLOCK, 2 * BLOCK)


WEIGHTS = ['norm_g', 'final_g', 'rel_bias', 'swa_w_in', 'swa_w_out', 'swa_sinks', 'conv_w_in', 'conv_kernel', 'conv_w_out',
           'ssm_w_in', 'ssm_lam_re', 'ssm_lam_im', 'ssm_log_dt', 'ssm_b_re', 'ssm_b_im', 'ssm_c_re', 'ssm_c_im', 'ssm_d',
           'ssm_w_glu', 'ssm_b_glu', 'ssm_w_out', 'fox_w_in', 'fox_w_fg', 'fox_b_fg', 'fox_w_out', 'ple_proj', 'ple_norm',
           'ple_gate']
BIG = {'swa_w_in': 2, 'swa_w_out': 1, 'conv_w_in': 2, 'conv_w_out': 1, 'ssm_w_in': 2, 'ssm_w_glu': 2, 'ssm_w_out': 1,
       'fox_w_in': 2, 'fox_w_fg': 1, 'fox_w_out': 1, 'ple_proj': 2, 'ple_gate': 1}
SMALL = {'conv_kernel': 2, 'ssm_d': 1, 'ssm_b_glu': 1}
REPLICATED = [n for n in WEIGHTS if n not in BIG and n not in SMALL]
N_CHIPS = 4
N_DEV = 8
BIG_ROWS = 256
SMALL_ROWS = 8
REPL_ROWS = 64


def _flat(pieces, dtype, lead, row_mult):
    flat = jnp.concatenate([q.astype(dtype) for q in pieces], axis=-1)
    pad = (-flat.shape[-1]) % (row_mult * FLAT_W)
    flat = jnp.pad(flat, [(0, 0)] * len(lead) + [(0, pad)])
    return flat.reshape(*lead, -1, FLAT_W)


def _unflat(flat, lead_ndim, sizes):
    lead = flat.shape[:lead_ndim]
    flat = flat.reshape(*lead, -1)
    out, off = [], 0
    for n in sizes:
        out.append(flat[..., off:off + n])
        off += n
    return out


def _split_shards(full, axis):
    shp = full.shape
    parts = full.reshape(shp[:axis] + (N_CHIPS, shp[axis] // N_CHIPS) + shp[axis + 1:])
    return jnp.moveaxis(parts, axis, 0)


def _join_shards(parts, axis):
    moved = jnp.moveaxis(parts, 0, axis)
    shp = moved.shape
    return moved.reshape(shp[:axis] + (shp[axis] * shp[axis + 1],) + shp[axis + 2:])


def _coords():
    return lax.axis_index("x"), lax.axis_index("y"), lax.axis_index("c")


def _remote(k, src, dst, to, send_sems, recv_sems):
    return pltpu.make_async_remote_copy(src_ref=src, dst_ref=dst, send_sem=send_sems.at[k], recv_sem=recv_sems.at[k],
                                        device_id=to, device_id_type=MESH)


def _gather_weights(bufs, ssh):
    nb = len(bufs)

    def body(*refs):
        w_refs, s_ref = refs[:nb], refs[nb]
        wouts, sout = refs[nb + 1:2 * nb + 1], refs[2 * nb + 1]
        send_sems, recv_sems = refs[2 * nb + 2:]
        x, y, c = _coords()
        me = 2 * x + y
        chips = [(1 - x, y), (x, 1 - y), (1 - x, 1 - y)]
        rc = functools.partial(_remote, send_sems=send_sems, recv_sems=recv_sems)
        sends = []
        for j, (cx, cy) in enumerate(chips):
            for b in range(nb):
                sends.append(rc(6 * b + j, w_refs[b].at[c], wouts[b].at[me, c], (cx, cy, c)))
            sends.append(rc(6 * nb + j, s_ref, sout.at[me], (cx, cy, c)))
        for cp in sends:
            cp.start()
        for j, (cx, cy) in enumerate(chips):
            k = 2 * cx + cy
            for b in range(nb):
                rc(6 * b + j, w_refs[b].at[c], wouts[b].at[k, c], (x, y, c)).wait_recv()
                fwd = rc(6 * b + 3 + j, wouts[b].at[k, c], wouts[b].at[k, c], (x, y, 1 - c))
                fwd.start()
                sends.append(fwd)
        for j, (cx, cy) in enumerate(chips):
            k = 2 * cx + cy
            for b in range(nb):
                rc(6 * b + 3 + j, w_refs[b].at[c], wouts[b].at[k, 1 - c], (x, y, c)).wait_recv()
            rc(6 * nb + j, s_ref, sout.at[k], (x, y, c)).wait_recv()
        for cp in sends:
            cp.wait_send()

    nsem = 6 * nb + 3
    res = pl.pallas_call(
        body, name="gather_weights", in_specs=[ANY] * (nb + 1), out_specs=[ANY] * (nb + 1),
        out_shape=[jax.ShapeDtypeStruct((N_CHIPS,) + a.shape, a.dtype) for a in (*bufs, ssh)],
        scratch_shapes=[pltpu.SemaphoreType.DMA((nsem,)), pltpu.SemaphoreType.DMA((nsem,))],
    )(*bufs, ssh)
    return res[:nb], res[nb]


def _pair_exchange(gbufs):
    nb = len(gbufs)

    def body(*refs):
        g_refs, outs = refs[:nb], refs[nb:2 * nb]
        send_sems, recv_sems = refs[2 * nb:]
        x, y, c = _coords()
        rc = functools.partial(_remote, send_sems=send_sems, recv_sems=recv_sems)
        sends = [rc(N_CHIPS * b + j, g_refs[b].at[2 * j + 1 - c], outs[b].at[j], (x, y, 1 - c))
                 for b in range(nb) for j in range(N_CHIPS)]
        for cp in sends:
            cp.start()
        for b in range(nb):
            for j in range(N_CHIPS):
                rc(N_CHIPS * b + j, g_refs[b].at[2 * j + c], outs[b].at[j], (x, y, c)).wait_recv()
        for cp in sends:
            cp.wait_send()

    nsem = N_CHIPS * nb
    return pl.pallas_call(
        body, name="pair_exchange", in_specs=[ANY] * nb, out_specs=[ANY] * nb,
        out_shape=[jax.ShapeDtypeStruct((N_CHIPS,) + g.shape[1:], g.dtype) for g in gbufs],
        scratch_shapes=[pltpu.SemaphoreType.DMA((nsem,)), pltpu.SemaphoreType.DMA((nsem,))],
    )(*gbufs)


def _pair_sum(mine, theirs, name):
    n, r, w = mine.shape
    t = _tile(r, 256, 16)
    spec = lambda a: (a, (n, t, w), lambda i: (0, i, 0))
    return _tiled(lambda a, b: (a.astype(F32) + b.astype(F32),), [spec(mine), spec(theirs)],
                  [_out((n, r, w), BF16, (n, t, w), lambda i: (0, i, 0))], r // t, name)[0]


def _exchange_grads(pbufs, gsmall, grepl):
    nb = len(pbufs)

    def body(*refs):
        g_refs, s_ref, r_ref = refs[:nb], refs[nb], refs[nb + 1]
        ogs, osm, orp = refs[nb + 2:2 * nb + 2], refs[2 * nb + 2], refs[2 * nb + 3]
        send_sems, recv_sems, local_sems = refs[2 * nb + 4:]
        x, y, c = _coords()
        me = 4 * x + 2 * y + c
        my_chip = 2 * x + y
        rc = functools.partial(_remote, send_sems=send_sems, recv_sems=recv_sems)
        local = [pltpu.make_async_copy(s_ref.at[me], osm.at[me], local_sems.at[0]),
                 pltpu.make_async_copy(r_ref, orp.at[me], local_sems.at[1])]
        for cp in local:
            cp.start()
        peers = []
        for d in range(1, N_DEV):
            px = 1 - x if d & 4 else x
            py = 1 - y if d & 2 else y
            pc = 1 - c if d & 1 else c
            peers.append((d, 4 * px + 2 * py + pc, 2 * px + py, (px, py, pc)))
        per_peer = nb + 2
        sends = []
        for i, (d, peer, chip, to) in enumerate(peers):
            sends.append(rc(per_peer * i + nb, s_ref.at[peer], osm.at[me], to))
            sends.append(rc(per_peer * i + nb + 1, r_ref, orp.at[me], to))
            if d & 1 == 0:
                for b in range(nb):
                    sends.append(rc(per_peer * i + b, g_refs[b].at[chip], ogs[b].at[my_chip], to))
        for cp in sends:
            cp.start()
        for i, (d, peer, chip, to) in enumerate(peers):
            rc(per_peer * i + nb, s_ref.at[peer], osm.at[peer], to).wait_recv()
            rc(per_peer * i + nb + 1, r_ref, orp.at[peer], to).wait_recv()
            if d & 1 == 0:
                for b in range(nb):
                    rc(per_peer * i + b, g_refs[b].at[chip], ogs[b].at[chip], to).wait_recv()
        for cp in sends:
            cp.wait_send()
        for cp in local:
            cp.wait()

    nsem = (nb + 2) * (N_DEV - 1)
    res = pl.pallas_call(
        body, name="exchange_grads", in_specs=[ANY] * (nb + 2), out_specs=[ANY] * (nb + 2),
        out_shape=[jax.ShapeDtypeStruct(a.shape, a.dtype) for a in (*pbufs, gsmall)]
        + [jax.ShapeDtypeStruct((N_DEV,) + grepl.shape, grepl.dtype)],
        scratch_shapes=[pltpu.SemaphoreType.DMA((nsem,)), pltpu.SemaphoreType.DMA((nsem,)), pltpu.SemaphoreType.DMA((2,))],
    )(*pbufs, gsmall, grepl)
    return res[:nb], res[nb], res[nb + 1]


def _sibling_exchange(halves):
    nb = len(halves)

    def body(*refs):
        ins, outs = refs[:nb], refs[nb:2 * nb]
        send_sems, recv_sems = refs[2 * nb:]
        x, y, c = _coords()
        rc = functools.partial(_remote, send_sems=send_sems, recv_sems=recv_sems)
        sends = [rc(b, ins[b], outs[b], (x, y, 1 - c)) for b in range(nb)]
        for cp in sends:
            cp.start()
        for b in range(nb):
            rc(b, ins[b], outs[b], (x, y, c)).wait_recv()
        for cp in sends:
            cp.wait_send()

    return pl.pallas_call(
        body, name="sibling_exchange", in_specs=[ANY] * nb, out_specs=[ANY] * nb,
        out_shape=[jax.ShapeDtypeStruct(a.shape, a.dtype) for a in halves],
        scratch_shapes=[pltpu.SemaphoreType.DMA((nb,)), pltpu.SemaphoreType.DMA((nb,))],
    )(*halves)


def _sum_senders(recv, name):
    n, r, w = recv.shape
    t = _tile(r, 256, 8)

    def fn(v):
        acc = v[0].astype(F32)
        for i in range(1, n):
            acc = acc + v[i].astype(F32)
        return (acc,)

    return _tiled(fn, [(recv, (n, t, w), lambda i: (0, i, 0))], [_orows(r, w, F32, t)], r // t, name)[0]


def _adamw(w, g, m, v, name):
    r, wd = w.shape
    t = _tile(r, 256, 8)

    def fn(wv, gv, mv, vv):
        m2 = ADAM_B1 * mv + (1.0 - ADAM_B1) * gv
        v2 = ADAM_B2 * vv + (1.0 - ADAM_B2) * (gv * gv)
        m_hat = m2 / (1.0 - ADAM_B1 ** ADAM_STEP)
        v_hat = v2 / (1.0 - ADAM_B2 ** ADAM_STEP)
        delta = -ADAM_LR * (m_hat / (jnp.sqrt(v_hat) + ADAM_EPS) + ADAM_WD * wv)
        return delta, m2, v2

    return _tiled(fn, [_rows(a, t) for a in (w, g, m, v)], [_orows(r, wd, F32, t)] * 3, r // t, name)


def kernel(x, p, norm_g, final_g, rel_bias, swa_w_in, swa_w_out, swa_sinks, conv_w_in, conv_kernel, conv_w_out, ssm_w_in, ssm_lam_re, ssm_lam_im, ssm_log_dt, ssm_b_re, ssm_b_im, ssm_c_re, ssm_c_im, ssm_d, ssm_w_glu, ssm_b_glu, ssm_w_out, fox_w_in, fox_w_fg, fox_b_fg, fox_w_out, ple_proj, ple_norm, ple_gate, loss_target, m_norm_g, m_final_g, m_rel_bias, m_swa_w_in, m_swa_w_out, m_swa_sinks, m_conv_w_in, m_conv_kernel, m_conv_w_out, m_ssm_w_in, m_ssm_lam_re, m_ssm_lam_im, m_ssm_log_dt, m_ssm_b_re, m_ssm_b_im, m_ssm_c_re, m_ssm_c_im, m_ssm_d, m_ssm_w_glu, m_ssm_b_glu, m_ssm_w_out, m_fox_w_in, m_fox_w_fg, m_fox_b_fg, m_fox_w_out, m_ple_proj, m_ple_norm, m_ple_gate, v_norm_g, v_final_g, v_rel_bias, v_swa_w_in, v_swa_w_out, v_swa_sinks, v_conv_w_in, v_conv_kernel, v_conv_w_out, v_ssm_w_in, v_ssm_lam_re, v_ssm_lam_im, v_ssm_log_dt, v_ssm_b_re, v_ssm_b_im, v_ssm_c_re, v_ssm_c_im, v_ssm_d, v_ssm_w_glu, v_ssm_b_glu, v_ssm_w_out, v_fox_w_in, v_fox_w_fg, v_fox_b_fg, v_fox_w_out, v_ple_proj, v_ple_norm, v_ple_gate):
    given = dict(locals())
    shard_shape = {n: given[n].shape for n in WEIGHTS}
    half = {n: math.prod(shard_shape[n]) // 2 for n in WEIGHTS}

    widths = sorted({shard_shape[n][-1] for n in BIG}, reverse=True)
    classes = [[n for n in BIG if shard_shape[n][-1] == w] for w in widths]
    rows = {n: math.prod(shard_shape[n][:-1]) for n in BIG}
    core = lax.axis_index("c")
    my_chip = 2 * lax.axis_index("x") + lax.axis_index("y")

    bufs = [jnp.concatenate([given[n].astype(BF16).reshape(rows[n], w) for n in names], axis=0).reshape(2, -1, w)
            for w, names in zip(widths, classes)]
    ssh = _flat([given[n].reshape(-1) for n in SMALL], F32, (), SMALL_ROWS)
    walls, sall = _gather_weights(bufs, ssh)
    walls = [lax.dynamic_update_slice(a, b[None], (my_chip, 0, 0, 0)) for a, b in zip(walls, bufs)]
    sall = lax.dynamic_update_slice(sall, ssh[None], (my_chip, 0, 0))
    full = {n: given[n] for n in REPLICATED}
    for w, names, wall in zip(widths, classes, walls):
        wall, off = wall.reshape(N_CHIPS, -1, w), 0
        for n in names:
            stacked = wall[:, off:off + rows[n]].reshape((N_CHIPS,) + shard_shape[n])
            full[n] = stacked if BIG[n] == 2 else _join_shards(stacked, BIG[n])
            off += rows[n]
    for n, piece in zip(SMALL, _unflat(sall, 1, [2 * half[n] for n in SMALL])):
        full[n] = _join_shards(piece.reshape((N_CHIPS,) + shard_shape[n]), SMALL[n])

    loss, dx, grads = _local_step(x[0], p[:, 0], loss_target[0], full)

    by_chip = lambda n: grads[n] if BIG[n] == 2 else _split_shards(grads[n], BIG[n])
    gbufs = [jnp.concatenate([by_chip(n).reshape(N_CHIPS, rows[n], w) for n in names], axis=1).reshape(N_DEV, -1, w)
             for w, names in zip(widths, classes)]
    gsmall = _flat([_split_shards(grads[n], SMALL[n]).reshape(N_DEV, -1) for n in SMALL], F32, (N_DEV,), SMALL_ROWS)
    grepl = _flat([grads[n].reshape(-1) for n in REPLICATED], F32, (), REPL_ROWS)
    pbufs = [_pair_sum(jnp.where(core == 0, g[0::2], g[1::2]), t, f"pair_sum_w{w}")
             for w, g, t in zip(widths, gbufs, _pair_exchange(gbufs))]
    recv_bufs, recv_small, recv_repl = _exchange_grads(pbufs, gsmall, grepl)
    recv_bufs = [lax.dynamic_update_slice(r, lax.dynamic_index_in_dim(pb, my_chip, axis=0), (my_chip, 0, 0))
                 for r, pb in zip(recv_bufs, pbufs)]
    halves = [_sum_senders(r, f"sum_w{w}") for w, r in zip(widths, recv_bufs)] + [_sum_senders(recv_small, "sum_small")]
    others = _sibling_exchange(halves)
    g_repl = _sum_senders(recv_repl, "sum_repl")
    both = [(jnp.where(core == 0, a, b), jnp.where(core == 0, b, a)) for a, b in zip(halves, others)]
    sfull = jnp.stack(both[-1])

    out_g, out_d, out_m, out_v = {}, {}, {}, {}
    for w, names, (lower, upper) in zip(widths, classes, both):
        gfull, off = jnp.concatenate([lower, upper], axis=0), 0
        for n in names:
            as_rows = lambda t, w=w: t.reshape(-1, w)
            piece = gfull[off:off + rows[n]]
            off += rows[n]
            d, m2, v2 = _adamw(as_rows(given[n]), piece, as_rows(given['m_' + n]), as_rows(given['v_' + n]), f"adamw_{n}")
            out_g[n], out_d[n], out_m[n], out_v[n] = (t.reshape(shard_shape[n]) for t in (piece, d, m2, v2))

    pack_small = lambda pre: _flat([given[pre + n].reshape(2, -1) for n in SMALL], F32, (2,), SMALL_ROWS).reshape(-1, FLAT_W)
    res = _adamw(pack_small(''), sfull.reshape(-1, FLAT_W), pack_small('m_'), pack_small('v_'), "adamw_small")
    for dst, flat in zip((out_g, out_d, out_m, out_v), (sfull,) + tuple(res)):
        for n, piece in zip(SMALL, _unflat(flat.reshape(2, -1, FLAT_W), 1, [half[n] for n in SMALL])):
            dst[n] = piece.reshape(shard_shape[n])

    pack_repl = lambda pre: _flat([given[pre + n].reshape(-1) for n in REPLICATED], F32, (), REPL_ROWS)
    res = _adamw(pack_repl(''), g_repl, pack_repl('m_'), pack_repl('v_'), "adamw_repl")
    for dst, flat in zip((out_g, out_d, out_m, out_v), (g_repl,) + tuple(res)):
        for n, piece in zip(REPLICATED, _unflat(flat, 0, [2 * half[n] for n in REPLICATED])):
            dst[n] = piece.reshape(shard_shape[n])

    total = lax.psum(loss[0, 0], ("x", "y", "c"))
    return (total, dx[None], *[out_g[n] for n in WEIGHTS], *[out_d[n] for n in WEIGHTS],
            *[out_m[n] for n in WEIGHTS], *[out_v[n] for n in WEIGHTS])
```

```python
import functools
import math

import numpy as np
import jax
import jax.numpy as jnp
from jax import lax
from jax.experimental import pallas as pl
from jax.experimental.pallas import tpu as pltpu

F32 = jnp.float32
BF16 = jnp.bfloat16

EPS = 1e-6
BLOCK = 128
REL_BUCKETS = 32
REL_MAX_DIST = 128
SWA_HEADS, SWA_KV_HEADS, HEAD_DIM = 32, 4, 64
SWA_GROUP = SWA_HEADS // SWA_KV_HEADS
FOX_HEADS = 32
SSM_GROUP, SSM_STATE = 16, 64
GROUPS_PER_STEP = 8
STATE_W = GROUPS_PER_STEP * SSM_STATE
LANES = 128
NEG = -1e30

ADAM_LR, ADAM_B1, ADAM_B2, ADAM_EPS, ADAM_WD, ADAM_STEP = 0.001, 0.9, 0.999, 1e-08, 0.01, 10

VMEM_LIMIT_V7X = 56 * 1024 * 1024
FLAT_W = 1024
MESH = pl.DeviceIdType.MESH
ANY = pl.BlockSpec(memory_space=pl.ANY)


def _cp(sem):
    return pltpu.CompilerParams(dimension_semantics=sem, vmem_limit_bytes=VMEM_LIMIT_V7X)


def _tile(n, target, mult=LANES):
    if n <= target:
        return n
    t = (target // mult) * mult
    while t >= mult:
        if n % t == 0:
            return t
        t -= mult
    return n


MAX_WHOLE_TILE = 1152


def _div(a, b):
    return lax.div(a, jnp.int32(b))


def _rem(a, b):
    return lax.rem(a, jnp.int32(b))


def _mm(a, b, *, ta=False, tb=False, out_dtype=BF16, name, tm=1024, tn=1024, tk=2048, b_split=False, out_split=0):
    if ta:
        kdim, m = a.shape
    else:
        m, kdim = a.shape
    parts = b.shape[0] if b_split else 1
    b_rows, b_cols = (b.shape[1], b.shape[2] * parts) if b_split else b.shape
    n = b_rows if tb else b_cols
    assert (b_cols if tb else b_rows) == kdim
    n_range = n // (out_split or (1 if tb else parts))
    k_range = kdim // (parts if tb else 1)
    tm, tk = _tile(m, tm), _tile(k_range, tk)
    tn = n_range if n_range <= MAX_WHOLE_TILE else _tile(n_range, tn)
    nk = kdim // tk
    dn = (((0 if ta else 1,), (1 if tb else 0,)), ((), ()))

    def body(a_ref, b_ref, o_ref, *acc):
        part = lax.dot_general(a_ref[...].astype(BF16), b_ref[...].astype(BF16), dn, preferred_element_type=F32)
        if nk == 1:
            o_ref[...] = part.astype(o_ref.dtype)
            return
        acc_ref, = acc
        k = pl.program_id(2)

        @pl.when(k == 0)
        def _():
            acc_ref[...] = part

        @pl.when(k > 0)
        def _():
            acc_ref[...] += part

        @pl.when(k == nk - 1)
        def _():
            o_ref[...] = acc_ref[...].astype(o_ref.dtype)

    a_spec = pl.BlockSpec((tk, tm), lambda i, j, k: (k, i)) if ta else pl.BlockSpec((tm, tk), lambda i, j, k: (i, k))
    nj, nkr = n_range // tn, k_range // tk
    if not b_split:
        b_spec = pl.BlockSpec((tn, tk), lambda i, j, k: (j, k)) if tb else pl.BlockSpec((tk, tn), lambda i, j, k: (k, j))
    elif tb:
        b_spec = pl.BlockSpec((None, tn, tk), lambda i, j, k: (_div(k, nkr), j, _rem(k, nkr)))
    else:
        b_spec = pl.BlockSpec((None, tk, tn), lambda i, j, k: (_div(j, nj), k, _rem(j, nj)))
    if out_split:
        out_spec = pl.BlockSpec((None, tm, tn), lambda i, j, k: (_div(j, nj), i, _rem(j, nj)))
        out_shape = jax.ShapeDtypeStruct((out_split, m, n_range), out_dtype)
    else:
        out_spec = pl.BlockSpec((tm, tn), lambda i, j, k: (i, j))
        out_shape = jax.ShapeDtypeStruct((m, n), out_dtype)
    return pl.pallas_call(
        body, name=name, grid=(m // tm, n // tn, nk),
        in_specs=[a_spec, b_spec], out_specs=out_spec, out_shape=out_shape,
        scratch_shapes=[pltpu.VMEM((tm, tn), F32)] if nk > 1 else [],
        compiler_params=_cp(("parallel", "parallel", "arbitrary")),
    )(a, b)


def _rows(arr, t):
    return (arr, (t, arr.shape[1]), lambda i: (i, 0))


def _cols(arr, cb, off=0):
    return (arr, (arr.shape[0], cb), lambda i: (0, i + off))


def _full(arr):
    nd = arr.ndim
    return (arr, arr.shape, lambda i: (0,) * nd)


def _lead(arr):
    return (arr, (1,) + arr.shape[1:], lambda i: (i, 0, 0))


def _out(shape, dtype, block, imap, acc=False):
    return (jax.ShapeDtypeStruct(shape, dtype), block, imap, acc)


def _orows(s, w, dtype, t):
    return _out((s, w), dtype, (t, w), lambda i: (i, 0))


def _ocols(s, w, dtype, cb):
    return _out((s, w), dtype, (s, cb), lambda i: (0, i))


def _oacc(shape):
    nd = len(shape)
    return _out(shape, F32, shape, lambda i: (0,) * nd, True)


def _olead(n, a, b, dtype=F32):
    return _out((n, a, b), dtype, (1, a, b), lambda i: (i, 0, 0))


def _tiled(fn, ins, outs, n, name):
    has_acc = any(o[3] for o in outs)
    ni = len(ins)

    def body(*refs):
        vals = fn(*[r[...] for r in refs[:ni]])
        i = pl.program_id(0)
        for r, v, o in zip(refs[ni:], vals, outs):
            if o[3]:
                @pl.when(i == 0)
                def _(r=r, v=v):
                    r[...] = v.astype(r.dtype)

                @pl.when(i > 0)
                def _(r=r, v=v):
                    r[...] += v.astype(r.dtype)
            else:
                r[...] = v.astype(r.dtype)

    res = pl.pallas_call(
        body, name=name, grid=(n,),
        in_specs=[pl.BlockSpec(b, m) for _, b, m in ins],
        out_specs=[pl.BlockSpec(b, m) for _, b, m, _ in outs],
        out_shape=[s for s, _, _, _ in outs],
        compiler_params=_cp(("arbitrary",) if has_acc else ("parallel",)),
    )(*[a for a, _, _ in ins])
    return res


def _silu(x):
    return x * jax.nn.sigmoid(x)


def _gelu(x):
    return 0.5 * x * (1.0 + jnp.tanh(math.sqrt(2.0 / math.pi) * (x + 0.044715 * (x * x * x))))


def _rms(x, g):
    r = lax.rsqrt(jnp.mean(x * x, axis=-1, keepdims=True) + EPS)
    return x * r * g


def _shift_rows(x, sh, up=False):
    s = x.shape[0]
    rows = lax.broadcasted_iota(jnp.int32, x.shape, 0)
    if up:
        return jnp.where(rows < s - sh, pltpu.roll(x, s - sh, 0), 0.0)
    return jnp.where(rows >= sh, pltpu.roll(x, sh, 0), 0.0)


SCAN_CHUNK = 128


def _scan_complex(xr, xi, pr, pi, tr, ti, reverse):
    s, lanes = xr.shape
    c = min(SCAN_CHUNK, s)
    nchunk = s // c
    in_chunk = lax.broadcasted_iota(jnp.int32, xr.shape, 0) & (c - 1)

    def shifted(x, sh):
        if reverse:
            return jnp.where(in_chunk < c - sh, pltpu.roll(x, s - sh, 0), 0.0)
        return jnp.where(in_chunk >= sh, pltpu.roll(x, sh, 0), 0.0)

    k = 0
    while (1 << k) < c:
        sr, si = shifted(xr, 1 << k), shifted(xi, 1 << k)
        ar, ai = pr[k:k + 1, :], pi[k:k + 1, :]
        xr, xi = xr + ar * sr - ai * si, xi + ar * si + ai * sr
        k += 1
    if nchunk == 1:
        return xr, xi
    xr, xi = xr.reshape(nchunk, c, lanes), xi.reshape(nchunk, c, lanes)
    edge = 0 if reverse else c - 1
    er, ei = xr[:, edge, :], xi[:, edge, :]
    m = 0
    while (1 << m) < nchunk:
        sr, si = _shift_rows(er, 1 << m, reverse), _shift_rows(ei, 1 << m, reverse)
        ar, ai = pr[k + m:k + m + 1, :], pi[k + m:k + m + 1, :]
        er, ei = er + ar * sr - ai * si, ei + ar * si + ai * sr
        m += 1
    cr, ci = _shift_rows(er, 1, reverse)[:, None, :], _shift_rows(ei, 1, reverse)[:, None, :]
    xr, xi = xr + tr[None] * cr - ti[None] * ci, xi + tr[None] * ci + ti[None] * cr
    return xr.reshape(s, lanes), xi.reshape(s, lanes)


def _cumsum_rows(x, reverse):
    s = x.shape[0]
    k = 0
    while (1 << k) < s:
        x = x + _shift_rows(x, 1 << k, reverse)
        k += 1
    return x


def _dot(a, b, ca=1, cb=0):
    return lax.dot_general(a.astype(BF16), b.astype(BF16), (((ca,), (cb,)), ((), ())), preferred_element_type=F32)


def _t5_bucket(dist):
    max_exact = REL_BUCKETS // 2
    d = np.maximum(dist, 1).astype(np.float32)
    large = max_exact + (np.log(d / max_exact) / np.log(REL_MAX_DIST / max_exact) * (REL_BUCKETS - max_exact)).astype(np.int32)
    large = np.minimum(large, REL_BUCKETS - 1)
    return np.where(dist < max_exact, dist, large).astype(np.int32)


def _swa_bucket_table():
    qi = np.arange(BLOCK)[:, None]
    kj = np.arange(2 * BLOCK)[None, :]
    return _t5_bucket(np.clip(qi + BLOCK - kj, 0, None))


def _stack_heads(x):
    return jnp.concatenate([x[:, g * HEAD_DIM:(g + 1) * HEAD_DIM] for g in range(SWA_GROUP)], axis=0)


def _unstack_heads(x):
    return jnp.concatenate([x[g * BLOCK:(g + 1) * BLOCK] for g in range(SWA_GROUP)], axis=1)


def _sink_rows(sink_ref):
    return jnp.concatenate([jnp.broadcast_to(sink_ref[0, g:g + 1, :1], (BLOCK, 1)) for g in range(SWA_GROUP)], axis=0)


def _swa_scores(qg, kb, bias_g, sk, n):
    s = _dot(qg, kb, 1, 1) * (HEAD_DIM ** -0.5) + bias_g
    row = lax.broadcasted_iota(jnp.int32, s.shape, 0) & (BLOCK - 1)
    col = lax.broadcasted_iota(jnp.int32, s.shape, 1)
    dist = row + BLOCK - col
    mask = (dist >= 0) & (dist < BLOCK) & ((col >= BLOCK) | (n > 0))
    s = jnp.where(mask, s, NEG)
    m = jnp.maximum(jnp.max(s, axis=1, keepdims=True), sk)
    e = jnp.exp(s - m)
    es = jnp.exp(sk - m)
    den = jnp.sum(e, axis=1, keepdims=True) + es
    return e / den, es / den


def _swa_specs(proj, k4, v4, nb, clamp):
    gw = SWA_GROUP * HEAD_DIM
    gate_off = (SWA_HEADS * HEAD_DIM + 2 * SWA_KV_HEADS * HEAD_DIM) // gw
    cur = (lambda n: jnp.minimum(n, nb - 1)) if clamp else (lambda n: n)
    prev = lambda n: jnp.maximum(cur(n) - 1, 0)
    return [
        pl.BlockSpec((BLOCK, gw), lambda h, n: (cur(n), h)),
        pl.BlockSpec((1, BLOCK, HEAD_DIM), lambda h, n: (h, cur(n), 0)),
        pl.BlockSpec((1, BLOCK, HEAD_DIM), lambda h, n: (h, prev(n), 0)),
        pl.BlockSpec((1, BLOCK, HEAD_DIM), lambda h, n: (h, cur(n), 0)),
        pl.BlockSpec((1, BLOCK, HEAD_DIM), lambda h, n: (h, prev(n), 0)),
        pl.BlockSpec((BLOCK, gw), lambda h, n: (cur(n), gate_off + h)),
    ], [proj, k4, k4, v4, v4, proj]


def _swa_fwd(proj, k4, v4, bias, sinks):
    s = proj.shape[0]
    nb = s // BLOCK
    gw = SWA_GROUP * HEAD_DIM

    def body(q_ref, kc_ref, kp_ref, vc_ref, vp_ref, gate_ref, bias_ref, sink_ref, a_ref):
        n = pl.program_id(1)
        kb = jnp.concatenate([kp_ref[0], kc_ref[0]], axis=0)
        vb = jnp.concatenate([vp_ref[0], vc_ref[0]], axis=0)
        p, _ = _swa_scores(_stack_heads(q_ref[...]), kb, bias_ref[...].reshape(-1, 2 * BLOCK), _sink_rows(sink_ref), n)
        a_ref[...] = (_unstack_heads(_dot(p, vb)) * _silu(gate_ref[...].astype(F32))).astype(a_ref.dtype)

    specs, args = _swa_specs(proj, k4, v4, nb, False)
    return pl.pallas_call(
        body, name="swa_fwd", grid=(SWA_KV_HEADS, nb),
        in_specs=specs + [pl.BlockSpec((SWA_GROUP, BLOCK, 2 * BLOCK), lambda h, n: (h, 0, 0)),
                          pl.BlockSpec((1, SWA_GROUP, LANES), lambda h, n: (h, 0, 0))],
        out_specs=pl.BlockSpec((BLOCK, gw), lambda h, n: (n, h)),
        out_shape=jax.ShapeDtypeStruct((s, SWA_HEADS * HEAD_DIM), BF16),
        compiler_params=_cp(("parallel", "parallel")),
    )(*args, bias, sinks)


def _swa_bwd(proj, k4, v4, bias, sinks, da):
    s = proj.shape[0]
    nb = s // BLOCK
    gw = SWA_GROUP * HEAD_DIM

    def body(q_ref, kc_ref, kp_ref, vc_ref, vp_ref, gate_ref, bias_ref, sink_ref, da_ref,
             dq_ref, dgate_ref, dk_ref, dv_ref, dbias_ref, dsink_ref, dk_own, dv_own):
        n = pl.program_id(1)

        @pl.when(n == 0)
        def _():
            dk_own[...] = jnp.zeros_like(dk_own)
            dv_own[...] = jnp.zeros_like(dv_own)
            dbias_ref[...] = jnp.zeros_like(dbias_ref)
            dsink_ref[...] = jnp.zeros_like(dsink_ref)

        @pl.when(n < nb)
        def _():
            kb = jnp.concatenate([kp_ref[0], kc_ref[0]], axis=0)
            vb = jnp.concatenate([vp_ref[0], vc_ref[0]], axis=0)
            dkb = jnp.zeros((2 * BLOCK, HEAD_DIM), F32)
            dvb = jnp.zeros((2 * BLOCK, HEAD_DIM), F32)
            for g in range(SWA_GROUP):
                sl = slice(g * HEAD_DIM, (g + 1) * HEAD_DIM)
                qg = q_ref[:, sl]
                p, p0 = _swa_scores(qg, kb, bias_ref[g], sink_ref[0, g:g + 1, :1], n)
                og = _dot(p, vb)
                gate = gate_ref[:, sl].astype(F32)
                dag = da_ref[:, sl].astype(F32)
                sg = jax.nn.sigmoid(gate)
                do = dag * gate * sg
                dgate_ref[:, sl] = (dag * og * sg * (1.0 + gate * (1.0 - sg))).astype(dgate_ref.dtype)
                dp = _dot(do, vb, 1, 1)
                delta = jnp.sum(do * og, axis=1, keepdims=True)
                ds = p * (dp - delta)
                dbias_ref[g] += ds
                dsink_ref[0, g:g + 1, :] += jnp.zeros((1, LANES), F32) - jnp.sum(p0 * delta, axis=0, keepdims=True)
                dq_ref[:, sl] = (_dot(ds, kb) * (HEAD_DIM ** -0.5)).astype(dq_ref.dtype)
                dkb += _dot(ds, qg, 0, 0) * (HEAD_DIM ** -0.5)
                dvb += _dot(p, do, 0, 0)
            dk_ref[0] = dk_own[...] + dkb[:BLOCK]
            dv_ref[0] = dv_own[...] + dvb[:BLOCK]
            dk_own[...] = dkb[BLOCK:]
            dv_own[...] = dvb[BLOCK:]

        @pl.when(n == nb)
        def _():
            dk_ref[0] = dk_own[...]
            dv_ref[0] = dv_own[...]

    specs, args = _swa_specs(proj, k4, v4, nb, True)
    cur = lambda n: jnp.minimum(n, nb - 1)
    trail = lambda n: jnp.maximum(n - 1, 0)
    return pl.pallas_call(
        body, name="swa_bwd", grid=(SWA_KV_HEADS, nb + 1),
        in_specs=specs + [pl.BlockSpec((SWA_GROUP, BLOCK, 2 * BLOCK), lambda h, n: (h, 0, 0)),
                          pl.BlockSpec((1, SWA_GROUP, LANES), lambda h, n: (h, 0, 0)),
                          pl.BlockSpec((BLOCK, gw), lambda h, n: (cur(n), h))],
        out_specs=[pl.BlockSpec((BLOCK, gw), lambda h, n: (cur(n), h)),
                   pl.BlockSpec((BLOCK, gw), lambda h, n: (cur(n), h)),
                   pl.BlockSpec((1, BLOCK, HEAD_DIM), lambda h, n: (h, trail(n), 0)),
                   pl.BlockSpec((1, BLOCK, HEAD_DIM), lambda h, n: (h, trail(n), 0)),
                   pl.BlockSpec((SWA_GROUP, BLOCK, 2 * BLOCK), lambda h, n: (h, 0, 0)),
                   pl.BlockSpec((1, SWA_GROUP, LANES), lambda h, n: (h, 0, 0))],
        out_shape=[jax.ShapeDtypeStruct((s, SWA_HEADS * HEAD_DIM), BF16),
                   jax.ShapeDtypeStruct((s, SWA_HEADS * HEAD_DIM), BF16),
                   jax.ShapeDtypeStruct((SWA_KV_HEADS, s, HEAD_DIM), F32),
                   jax.ShapeDtypeStruct((SWA_KV_HEADS, s, HEAD_DIM), F32),
                   jax.ShapeDtypeStruct((SWA_HEADS, BLOCK, 2 * BLOCK), F32),
                   jax.ShapeDtypeStruct((SWA_KV_HEADS, SWA_GROUP, LANES), F32)],
        scratch_shapes=[pltpu.VMEM((BLOCK, HEAD_DIM), F32), pltpu.VMEM((BLOCK, HEAD_DIM), F32)],
        compiler_params=_cp(("arbitrary", "arbitrary")),
    )(*args, bias, sinks, da)


FOX_TILE = 1024


def _below_and_on_diagonal(i, j, step):
    @pl.when(j < i)
    def _():
        step(False)

    @pl.when(j == i)
    def _():
        step(True)


FOX_AUG = 128
FOX_CQ, FOX_CK = HEAD_DIM, HEAD_DIM + 1


def _fox_operands(proj, csum):
    s = proj.shape[0]
    hw = FOX_HEADS * HEAD_DIM
    heads = lambda a: a.reshape(s, FOX_HEADS, HEAD_DIM).transpose(1, 0, 2)
    q = heads(proj[:, :hw]) * jnp.asarray(HEAD_DIM ** -0.5, BF16)
    k, v = heads(proj[:, hw:2 * hw]), heads(proj[:, 2 * hw:3 * hw])
    one = jnp.ones((FOX_HEADS, s, 1), BF16)
    zero = jnp.zeros((FOX_HEADS, s, 1), BF16)
    pad = jnp.zeros((FOX_HEADS, s, FOX_AUG - HEAD_DIM - 2), BF16)
    qa = jnp.concatenate([q, zero, -one, pad], axis=-1)
    ka = jnp.concatenate([k, one, zero, pad], axis=-1)
    tr = lambda a: a.transpose(0, 2, 1)
    c = csum[:, :FOX_HEADS].T
    return (qa, ka, tr(qa), tr(ka), v, tr(v), c[:, None, :], jnp.broadcast_to(c[:, :, None], (FOX_HEADS, s, LANES)))


def _fox_scores_t(ka, qa, cq, ck, diag):
    st = _dot(ka, qa, 1, 1) + cq - jnp.concatenate([ck] * (qa.shape[0] // LANES), axis=1)
    if diag:
        st = jnp.where(lax.broadcasted_iota(jnp.int32, st.shape, 0) <= lax.broadcasted_iota(jnp.int32, st.shape, 1), st, NEG)
    return st


def _fox_attention(qa, ka, vt, c_row, c_lanes):
    nh, s, aw = qa.shape
    t = min(FOX_TILE, s)
    nt = s // t
    hd = HEAD_DIM

    def body(qa_ref, ka_ref, vt_ref, cq_ref, ck_ref, o_ref, lse_ref, m_s, l_s, acc_s):
        i, j = pl.program_id(1), pl.program_id(2)

        @pl.when(j == 0)
        def _():
            m_s[...] = jnp.full_like(m_s, NEG)
            l_s[...] = jnp.zeros_like(l_s)
            acc_s[...] = jnp.zeros_like(acc_s)

        def step(diag):
            st = _fox_scores_t(ka_ref[0], qa_ref[0], cq_ref[0], ck_ref[0], diag)
            m_old = m_s[...]
            m_new = jnp.maximum(m_old, jnp.max(st, axis=0, keepdims=True))
            alpha = jnp.exp(m_old - m_new)
            p = jnp.exp(st - m_new)
            l_s[...] = alpha * l_s[...] + jnp.sum(p, axis=0, keepdims=True)
            acc_s[...] = alpha * acc_s[...] + _dot(vt_ref[0], p)
            m_s[...] = m_new

        _below_and_on_diagonal(i, j, step)

        @pl.when(j == nt - 1)
        def _():
            o_ref[0] = (acc_s[...] / l_s[...]).astype(o_ref.dtype)
            lse_ref[0] = m_s[...] + jnp.log(l_s[...])

    kj = lambda i, j: jnp.minimum(j, i)
    return pl.pallas_call(
        body, name="fox_fwd", grid=(nh, nt, nt),
        in_specs=[pl.BlockSpec((1, t, aw), lambda h, i, j: (h, i, 0)),
                  pl.BlockSpec((1, t, aw), lambda h, i, j: (h, kj(i, j), 0)),
                  pl.BlockSpec((1, hd, t), lambda h, i, j: (h, 0, kj(i, j))),
                  pl.BlockSpec((1, 1, t), lambda h, i, j: (h, 0, i)),
                  pl.BlockSpec((1, t, LANES), lambda h, i, j: (h, kj(i, j), 0))],
        out_specs=[pl.BlockSpec((1, hd, t), lambda h, i, j: (h, 0, i)),
                   pl.BlockSpec((1, 1, t), lambda h, i, j: (h, 0, i))],
        out_shape=[jax.ShapeDtypeStruct((nh, hd, s), BF16), jax.ShapeDtypeStruct((nh, 1, s), F32)],
        scratch_shapes=[pltpu.VMEM((1, t), F32), pltpu.VMEM((1, t), F32), pltpu.VMEM((hd, t), F32)],
        compiler_params=_cp(("parallel", "parallel", "arbitrary")),
    )(qa, ka, vt, c_row, c_lanes)


def _fox_attention_bwd(qa, ka, qat, kat, v, c_row, c_lanes, ot, dot, lse):
    nh, s, aw = qa.shape
    t = min(FOX_TILE, s)
    nt = s // t
    hd = HEAD_DIM

    def body(qa_ref, ka_ref, qat_ref, kat_ref, v_ref, cq_ref, ck_ref, ot_ref, dot_ref, lse_ref,
             dq_ref, dcq_ref, dk_ref, dck_ref, dv_ref, dqa_s, dk_s, dv_s):
        j, i = pl.program_id(1), pl.program_id(2)

        @pl.when((j == 0) & (i == 0))
        def _():
            dqa_s[...] = jnp.zeros_like(dqa_s)

        @pl.when(i == 0)
        def _():
            dk_s[...] = jnp.zeros_like(dk_s)
            dv_s[...] = jnp.zeros_like(dv_s)

        def step(diag):
            p = jnp.exp(_fox_scores_t(ka_ref[0], qa_ref[0], cq_ref[0], ck_ref[0], diag) - lse_ref[0])
            do_t = dot_ref[0]
            delta = jnp.sum(do_t.astype(F32) * ot_ref[0].astype(F32), axis=0, keepdims=True)
            ds = (p * (_dot(v_ref[0], do_t) - delta)).astype(BF16)
            dv_s[...] += _dot(do_t, p, 1, 1)
            dk_s[...] += _dot(qat_ref[0], ds, 1, 1)
            cols = pl.ds(pl.multiple_of(i * t, t), t)
            dqa_s[:, cols] += _dot(kat_ref[0], ds)

        _below_and_on_diagonal(i, j, step)

        @pl.when(i == nt - 1)
        def _():
            dk_ref[0] = dk_s[:hd].astype(dk_ref.dtype)
            dck_ref[0] = dk_s[FOX_CK:FOX_CK + 1]
            dv_ref[0] = dv_s[...].astype(dv_ref.dtype)

        @pl.when((j == nt - 1) & (i == nt - 1))
        def _():
            dq_ref[0] = (dqa_s[:hd] * (hd ** -0.5)).astype(dq_ref.dtype)
            dcq_ref[0] = dqa_s[FOX_CQ:FOX_CQ + 1]

    qi = lambda i, j: jnp.maximum(i, j)
    return pl.pallas_call(
        body, name="fox_bwd", grid=(nh, nt, nt),
        in_specs=[pl.BlockSpec((1, t, aw), lambda h, j, i: (h, qi(i, j), 0)),
                  pl.BlockSpec((1, t, aw), lambda h, j, i: (h, j, 0)),
                  pl.BlockSpec((1, aw, t), lambda h, j, i: (h, 0, qi(i, j))),
                  pl.BlockSpec((1, aw, t), lambda h, j, i: (h, 0, j)),
                  pl.BlockSpec((1, t, hd), lambda h, j, i: (h, j, 0)),
                  pl.BlockSpec((1, 1, t), lambda h, j, i: (h, 0, qi(i, j))),
                  pl.BlockSpec((1, t, LANES), lambda h, j, i: (h, j, 0)),
                  pl.BlockSpec((1, hd, t), lambda h, j, i: (h, 0, qi(i, j))),
                  pl.BlockSpec((1, hd, t), lambda h, j, i: (h, 0, qi(i, j))),
                  pl.BlockSpec((1, 1, t), lambda h, j, i: (h, 0, qi(i, j)))],
        out_specs=[pl.BlockSpec((1, hd, s), lambda h, j, i: (h, 0, 0)),
                   pl.BlockSpec((1, 1, s), lambda h, j, i: (h, 0, 0)),
                   pl.BlockSpec((1, hd, t), lambda h, j, i: (h, 0, j)),
                   pl.BlockSpec((1, 1, t), lambda h, j, i: (h, 0, j)),
                   pl.BlockSpec((1, hd, t), lambda h, j, i: (h, 0, j))],
        out_shape=[jax.ShapeDtypeStruct((nh, hd, s), BF16), jax.ShapeDtypeStruct((nh, 1, s), F32),
                   jax.ShapeDtypeStruct((nh, hd, s), BF16), jax.ShapeDtypeStruct((nh, 1, s), F32),
                   jax.ShapeDtypeStruct((nh, hd, s), BF16)],
        scratch_shapes=[pltpu.VMEM((aw, s), F32), pltpu.VMEM((aw, t), F32), pltpu.VMEM((hd, t), F32)],
        compiler_params=_cp(("parallel", "arbitrary", "arbitrary")),
    )(qa, ka, qat, kat, v, c_row, c_lanes, ot, dot, lse)


def _ssm_prep(lam_re, lam_im, log_dt, b_re, b_im, c_re, c_im):
    g, n = lam_re.shape
    dt = jnp.exp(log_dt)[:, None]
    mag = jnp.exp(lam_re * dt)
    ab_re = mag * jnp.cos(lam_im * dt)
    ab_im = mag * jnp.sin(lam_im * dt)
    den = lam_re * lam_re + lam_im * lam_im
    nr = ab_re - 1.0
    coef_re = ((nr * lam_re + ab_im * lam_im) / den)[..., None]
    coef_im = ((ab_im * lam_re - nr * lam_im) / den)[..., None]
    bb_re = coef_re * b_re - coef_im * b_im
    bb_im = coef_re * b_im + coef_im * b_re
    nblk = g // GROUPS_PER_STEP
    eye = jnp.eye(GROUPS_PER_STEP, dtype=F32)

    def bdiag(bb):
        return jnp.einsum('bgnc,gh->bgchn', bb.reshape(nblk, GROUPS_PER_STEP, n, SSM_GROUP), eye).reshape(
            nblk, GROUPS_PER_STEP * SSM_GROUP, STATE_W)

    def cdiag(cc):
        return jnp.einsum('bgcn,gh->bgnhc', cc.reshape(nblk, GROUPS_PER_STEP, SSM_GROUP, n), eye).reshape(
            nblk, STATE_W, GROUPS_PER_STEP * SSM_GROUP)

    return (ab_re.reshape(nblk, 1, STATE_W), ab_im.reshape(nblk, 1, STATE_W),
            bdiag(bb_re), bdiag(bb_im), cdiag(c_re), cdiag(c_im))


def _powers(a_re, a_im, levels):
    rs, ims = [a_re], [a_im]
    for _ in range(levels - 1):
        r, i = rs[-1], ims[-1]
        rs.append(r * r - i * i)
        ims.append(2.0 * r * i)
    return jnp.concatenate(rs, axis=1), jnp.concatenate(ims, axis=1)


def _power_table(a_re, a_im, n, descending=False):
    tr, ti = a_re, a_im
    while tr.shape[1] < n:
        top = 0 if descending else -1
        lr, li = tr[:, top:][:, :1], ti[:, top:][:, :1]
        hr, hi = tr * lr - ti * li, tr * li + ti * lr
        tr, ti = ((jnp.concatenate([hr, tr], axis=1), jnp.concatenate([hi, ti], axis=1)) if descending
                  else (jnp.concatenate([tr, hr], axis=1), jnp.concatenate([ti, hi], axis=1)))
    return tr, ti


def _ssm_fwd(proj, pw_re, pw_im, tab_re, tab_im, bd_re, bd_im, cd_re, cd_im, dskip):
    s = proj.shape[0]
    w = dskip.shape[1]
    nblk = w // LANES
    nch = STATE_W // LANES
    levels = pw_re.shape[1]

    def body(u_ref, pr, pi, tr, ti, br, bi, cr, ci, d_ref, pre_ref, yg_ref, xr_ref, xi_ref):
        ch = pl.program_id(1)
        u = u_ref[...]
        xr, xi = _scan_complex(_dot(u, br[0]), _dot(u, bi[0]), pr[0], pi[0], tr[0], ti[0], False)
        xr_ref[...] = xr.astype(BF16)
        xi_ref[...] = xi.astype(BF16)
        yc = _dot(xr, cr[0]) - _dot(xi, ci[0])

        @pl.when(ch == 0)
        def _():
            pre_ref[...] = yc + d_ref[...] * u

        @pl.when(ch > 0)
        def _():
            pre_ref[...] += yc

        @pl.when(ch == nch - 1)
        def _():
            yg_ref[...] = _gelu(pre_ref[...]).astype(BF16)

    blk = lambda b, c: (0, b)
    col = lambda b, c: (0, b * nch + c)
    in_chunk = pl.BlockSpec((1, LANES, LANES), lambda b, c: (b, 0, c))
    out_chunk = pl.BlockSpec((1, LANES, LANES), lambda b, c: (b, c, 0))
    pw_spec = pl.BlockSpec((1, levels, LANES), lambda b, c: (b, 0, c))
    tab_spec = pl.BlockSpec((1, tab_re.shape[1], LANES), lambda b, c: (b, 0, c))
    return pl.pallas_call(
        body, name="ssm_fwd", grid=(nblk, nch),
        in_specs=[pl.BlockSpec((s, LANES), blk), pw_spec, pw_spec, tab_spec, tab_spec, in_chunk, in_chunk,
                  out_chunk, out_chunk, pl.BlockSpec((1, LANES), blk)],
        out_specs=[pl.BlockSpec((s, LANES), blk), pl.BlockSpec((s, LANES), blk),
                   pl.BlockSpec((s, LANES), col), pl.BlockSpec((s, LANES), col)],
        out_shape=[jax.ShapeDtypeStruct((s, w), F32), jax.ShapeDtypeStruct((s, w), BF16),
                   jax.ShapeDtypeStruct((s, nblk * STATE_W), BF16), jax.ShapeDtypeStruct((s, nblk * STATE_W), BF16)],
        compiler_params=_cp(("parallel", "arbitrary")),
    )(proj, pw_re, pw_im, tab_re, tab_im, bd_re, bd_im, cd_re, cd_im, dskip)


def _ssm_bwd(proj, dyg, pre, xr_all, xi_all, pw_re, pw_im, tab_re, tab_im, bd_re, bd_im, cd_re, cd_im, dskip):
    s = proj.shape[0]
    w = dskip.shape[1]
    nblk = w // LANES
    nch = STATE_W // LANES
    levels = pw_re.shape[1]

    def body(u_ref, dy_ref, pre_ref, xr_ref, xi_ref, pr, pi, tr, ti, br, bi, cr, ci, d_ref,
             du_ref, dd_ref, dar_ref, dai_ref, dbr_ref, dbi_ref, dcr_ref, dci_ref):
        ch = pl.program_id(1)
        u = u_ref[...]
        _, vjp = jax.vjp(_gelu, pre_ref[...])
        dpre = vjp(dy_ref[...].astype(F32))[0]
        zr, zi = _scan_complex(_dot(dpre, cr[0], 1, 1), -_dot(dpre, ci[0], 1, 1), pr[0], -pi[0], tr[0], -ti[0], True)
        xpr = _shift_rows(xr_ref[...].astype(F32), 1)
        xpi = _shift_rows(xi_ref[...].astype(F32), 1)
        dar_ref[0] = jnp.sum(zr * xpr + zi * xpi, axis=0, keepdims=True)
        dai_ref[0] = jnp.sum(zi * xpr - zr * xpi, axis=0, keepdims=True)
        dcr_ref[0] = _dot(xr_ref[...], dpre, 0, 0)
        dci_ref[0] = -_dot(xi_ref[...], dpre, 0, 0)
        dbr_ref[0] = _dot(u, zr, 0, 0)
        dbi_ref[0] = _dot(u, zi, 0, 0)
        duc = _dot(zr, br[0], 1, 1) + _dot(zi, bi[0], 1, 1)

        @pl.when(ch == 0)
        def _():
            du_ref[...] = duc + dpre * d_ref[...]
            dd_ref[...] = jnp.sum(dpre * u, axis=0, keepdims=True)

        @pl.when(ch > 0)
        def _():
            du_ref[...] += duc

    blk = lambda b, c: (0, b)
    col = lambda b, c: (0, b * nch + c)
    in_chunk = pl.BlockSpec((1, LANES, LANES), lambda b, c: (b, 0, c))
    out_chunk = pl.BlockSpec((1, LANES, LANES), lambda b, c: (b, c, 0))
    pw_spec = pl.BlockSpec((1, levels, LANES), lambda b, c: (b, 0, c))
    a_spec = pl.BlockSpec((1, 1, LANES), lambda b, c: (b, 0, c))
    tab_spec = pl.BlockSpec((1, tab_re.shape[1], LANES), lambda b, c: (b, 0, c))
    return pl.pallas_call(
        body, name="ssm_bwd", grid=(nblk, nch),
        in_specs=[pl.BlockSpec((s, LANES), blk), pl.BlockSpec((s, LANES), blk), pl.BlockSpec((s, LANES), blk),
                  pl.BlockSpec((s, LANES), col), pl.BlockSpec((s, LANES), col),
                  pw_spec, pw_spec, tab_spec, tab_spec, in_chunk, in_chunk, out_chunk, out_chunk,
                  pl.BlockSpec((1, LANES), blk)],
        out_specs=[pl.BlockSpec((s, LANES), blk), pl.BlockSpec((1, LANES), blk), a_spec, a_spec,
                   in_chunk, in_chunk, out_chunk, out_chunk],
        out_shape=[jax.ShapeDtypeStruct((s, w), F32), jax.ShapeDtypeStruct((1, w), F32),
                   jax.ShapeDtypeStruct((nblk, 1, STATE_W), F32), jax.ShapeDtypeStruct((nblk, 1, STATE_W), F32),
                   jax.ShapeDtypeStruct((nblk, LANES, STATE_W), F32), jax.ShapeDtypeStruct((nblk, LANES, STATE_W), F32),
                   jax.ShapeDtypeStruct((nblk, STATE_W, LANES), F32), jax.ShapeDtypeStruct((nblk, STATE_W, LANES), F32)],
        compiler_params=_cp(("parallel", "arbitrary")),
    )(proj, dyg, pre, xr_all, xi_all, pw_re, pw_im, tab_re, tab_im, bd_re, bd_im, cd_re, cd_im, dskip)


ROW_TILE = 256


def _norm_fwd(x, g, name):
    s, d = x.shape
    t = min(ROW_TILE, s)
    return _tiled(lambda xv, gv: (_rms(xv, gv),), [_rows(x, t), _full(g)], [_orows(s, d, BF16, t)], s // t, name)[0]


def _norm_bwd(x, g, dh_list, dx_in, name):
    s, d = x.shape
    t = min(ROW_TILE, s)
    nh = len(dh_list)

    def fn(xv, gv, dxv, *dhs):
        dh = dhs[0].astype(F32)
        for other in dhs[1:]:
            dh = dh + other.astype(F32)
        _, vjp = jax.vjp(_rms, xv, gv)
        dx, dg = vjp(dh)
        dx = dx + dxv
        return dx, dx, dg

    return _tiled(fn, [_rows(x, t), _full(g), _rows(dx_in, t)] + [_rows(a, t) for a in dh_list],
                  [_orows(s, d, F32, t), _orows(s, d, BF16, t), _oacc((1, d))], s // t, name)


def _ple_fwd(x, y, pn, name):
    s, d = x.shape
    t = min(ROW_TILE, s)

    def fn(xv, yv, gv):
        x1 = xv + yv
        return x1, _rms(x1, gv)

    return _tiled(fn, [_rows(x, t), _rows(y, t), _full(pn)], [_orows(s, d, F32, t), _orows(s, d, BF16, t)], s // t, name)


def _ple_mix(x1, emb, gl, name):
    s, d = x1.shape
    t = min(ROW_TILE, s)
    return _tiled(lambda a, e, g: (a + e * jax.nn.sigmoid(g),), [_rows(x1, t), _rows(emb, t), _rows(gl, t)],
                  [_orows(s, d, F32, t)], s // t, name)[0]


def _ple_mix_bwd(dx2, emb, gl, name):
    s, d = dx2.shape
    t = min(ROW_TILE, s)

    def fn(dx, e, g):
        sg = jax.nn.sigmoid(g)
        return dx * sg, dx * e * sg * (1.0 - sg)

    return _tiled(fn, [_rows(dx2, t), _rows(emb, t), _rows(gl, t)],
                  [_orows(s, d, BF16, t), _orows(s, d, BF16, t)], s // t, name)


def _loss_grad(x, target, g):
    s, d = x.shape
    t = min(ROW_TILE, s)

    def fn(xv, tv, gv):
        def f(xx, gg):
            err = _rms(xx, gg) - tv
            return 0.5 * jnp.sum(jnp.mean(err * err, axis=-1, keepdims=True), axis=0, keepdims=True)

        loss, vjp = jax.vjp(f, xv, gv)
        dx, dg = vjp(jnp.ones((1, 1), F32))
        return loss, dx, dg

    return _tiled(fn, [_rows(x, t), _rows(target, t), _full(g)],
                  [_oacc((1, 1)), _orows(s, d, F32, t), _oacc((1, d))], s // t, "loss_grad")


def _conv_fwd(proj, kern):
    s = proj.shape[0]
    w = kern.shape[1]
    nb = w // LANES

    def fn(bg, cg, u, gate, k):
        z = cg.astype(F32) * u.astype(F32)
        conv = k[2:3] * z + k[1:2] * _shift_rows(z, 1) + k[0:1] * _shift_rows(z, 2)
        return (bg.astype(F32) * conv * _silu(gate.astype(F32)),)

    return _tiled(fn, [_cols(proj, LANES, q * nb) for q in range(4)] + [_cols(kern, LANES)],
                  [_ocols(s, w, BF16, LANES)], nb, "conv_fwd")[0]


def _conv_bwd(proj, kern, da):
    s = proj.shape[0]
    w = kern.shape[1]
    nb = w // LANES

    def fn(bg, cg, u, gate, k, dav):
        bg, cg, u, gate, dav = (a.astype(F32) for a in (bg, cg, u, gate, dav))
        z = cg * u
        z1, z2 = _shift_rows(z, 1), _shift_rows(z, 2)
        conv = k[2:3] * z + k[1:2] * z1 + k[0:1] * z2
        sg = jax.nn.sigmoid(gate)
        dy = dav * gate * sg
        dgate = dav * bg * conv * sg * (1.0 + gate * (1.0 - sg))
        dconv = dy * bg
        dk = jnp.concatenate([jnp.sum(dconv * zz, axis=0, keepdims=True) for zz in (z2, z1, z)], axis=0)
        dz = k[2:3] * dconv + k[1:2] * _shift_rows(dconv, 1, True) + k[0:1] * _shift_rows(dconv, 2, True)
        return dy * conv, dz * u, dz * cg, dgate, dk

    return _tiled(fn, [_cols(proj, LANES, q * nb) for q in range(4)] + [_cols(kern, LANES), _cols(da, LANES)],
                  [_ocols(s, w, BF16, LANES)] * 4 + [_ocols(3, w, F32, LANES)], nb, "conv_bwd")


def _glu_fwd(gl, proj, bglu):
    s, w2 = gl.shape
    w = w2 // 2
    t = min(ROW_TILE, s)

    def fn(glv, gate, b):
        v = glv + b
        return (v[:, :w] * jax.nn.sigmoid(v[:, w:]) * _silu(gate),)

    return _tiled(fn, [_rows(gl, t), (proj, (t, w), lambda i: (i, 1)), _full(bglu)],
                  [_orows(s, w, BF16, t)], s // t, "glu_fwd")[0]


def _glu_bwd(gl, proj, bglu, da):
    s, w2 = gl.shape
    w = w2 // 2
    t = min(ROW_TILE, s)

    def fn(glv, gate, b, dav):
        def f(gg, gt, bb):
            v = gg + bb
            return v[:, :w] * jax.nn.sigmoid(v[:, w:]) * _silu(gt)

        _, vjp = jax.vjp(f, glv, gate, b)
        return vjp(dav.astype(F32))

    return _tiled(fn, [_rows(gl, t), (proj, (t, w), lambda i: (i, 1)), _full(bglu), _rows(da, t)],
                  [_orows(s, w2, BF16, t), _orows(s, w, BF16, t), _oacc((1, w2))], s // t, "glu_bwd")


def _fg_fwd(z, b):
    def fn(zv, bv):
        v = zv + bv
        logf = jnp.minimum(v, 0.0) - jnp.log(1.0 + jnp.exp(-jnp.abs(v)))
        return (_cumsum_rows(logf, False),)

    return _tiled(fn, [_full(z), _full(b)], [_out(z.shape, F32, z.shape, lambda i: (0, 0))], 1, "fg_fwd")[0]


def _fg_bwd(z, b, dcs):
    def fn(zv, bv, dc):
        dz = _cumsum_rows(dc, True) * jax.nn.sigmoid(-(zv + bv))
        return dz, jnp.sum(dz, axis=0, keepdims=True)

    return _tiled(fn, [_full(z), _full(b), _full(dcs)],
                  [_out(z.shape, BF16, z.shape, lambda i: (0, 0)), _out((1, z.shape[1]), F32, (1, z.shape[1]), lambda i: (0, 0))],
                  1, "fg_bwd")


def _fox_gate_bwd(da, o, proj):
    s, w = o.shape
    t = min(ROW_TILE, s)

    def fn(dav, ov, gate):
        dav, ov, gate = dav.astype(F32), ov.astype(F32), gate.astype(F32)
        sg = jax.nn.sigmoid(gate)
        return dav * gate * sg, dav * ov * sg * (1.0 + gate * (1.0 - sg))

    return _tiled(fn, [_rows(da, t), _rows(o, t), (proj, (t, w), lambda i: (i, 3))],
                  [_orows(s, w, BF16, t), _orows(s, w, BF16, t)], s // t, "fox_gate_bwd")


def _fox_gate_fwd(o, proj):
    s, w = o.shape
    t = min(ROW_TILE, s)
    return _tiled(lambda ov, gate: (ov.astype(F32) * _silu(gate.astype(F32)),),
                  [_rows(o, t), (proj, (t, w), lambda i: (i, 3))], [_orows(s, w, BF16, t)], s // t, "fox_gate_fwd")[0]


def _local_step(x, p, target, w):
    s, d = x.shape
    depth = p.shape[0]
    pb = p.astype(BF16)
    grads = {}
    saved = []

    bucket = _swa_bucket_table()
    onehot = np.eye(REL_BUCKETS, dtype=np.float32)[bucket.reshape(-1)]
    bias = _bias_table(w['rel_bias'], jnp.asarray(onehot.T, BF16))
    sinks = jnp.broadcast_to(w['swa_sinks'].reshape(SWA_KV_HEADS, SWA_GROUP, 1), (SWA_KV_HEADS, SWA_GROUP, LANES))
    ssm_params = tuple(w[k][0] for k in ('ssm_lam_re', 'ssm_lam_im', 'ssm_log_dt', 'ssm_b_re', 'ssm_b_im', 'ssm_c_re', 'ssm_c_im'))
    (a_re, a_im, bd_re, bd_im, cd_re, cd_im), ssm_vjp = jax.vjp(_ssm_prep, *ssm_params)
    levels = max(1, (s - 1).bit_length())
    pw_re, pw_im = _powers(a_re, a_im, levels)
    tab_re, tab_im = _power_table(a_re, a_im, min(SCAN_CHUNK, s))
    rtab_re, rtab_im = _power_table(a_re, a_im, min(SCAN_CHUNK, s), descending=True)
    wfg = jnp.pad(w['fox_w_fg'][0], ((0, 0), (0, LANES - FOX_HEADS)))
    bfg = jnp.pad(w['fox_b_fg'], ((0, 0), (0, LANES - FOX_HEADS)))

    def qkv4(a):
        return a.reshape(s, SWA_KV_HEADS, HEAD_DIM).transpose(1, 0, 2)

    def unheads(a):
        return a.transpose(2, 0, 1).reshape(s, FOX_HEADS * HEAD_DIM)

    for i in range(depth):
        mixer = i % 4
        hn = _norm_fwd(x, w['norm_g'][i:i + 1], f"norm_fwd{i}")
        sv = {'x': x, 'hn': hn}
        if mixer == 0:
            proj = _mm(hn, w['swa_w_in'][:, 0], b_split=True, name="swa_in")
            qw = SWA_HEADS * HEAD_DIM
            kvw = SWA_KV_HEADS * HEAD_DIM
            k4, v4 = qkv4(proj[:, qw:qw + kvw]), qkv4(proj[:, qw + kvw:qw + 2 * kvw])
            a = _swa_fwd(proj, k4, v4, bias, sinks)
            sv.update(proj=proj, k4=k4, v4=v4)
            w_out = w['swa_w_out'][0]
        elif mixer == 1:
            proj = _mm(hn, w['conv_w_in'][:, 0], b_split=True, name="conv_in")
            a = _conv_fwd(proj, w['conv_kernel'][0])
            sv.update(proj=proj)
            w_out = w['conv_w_out'][0]
        elif mixer == 2:
            proj = _mm(hn, w['ssm_w_in'][:, 0], b_split=True, out_dtype=F32, name="ssm_in")
            pre, yg, xr_all, xi_all = _ssm_fwd(proj, pw_re, pw_im, tab_re, tab_im, bd_re, bd_im, cd_re, cd_im, w['ssm_d'])
            gl = _mm(yg, w['ssm_w_glu'][:, 0], b_split=True, out_dtype=F32, name="ssm_glu")
            a = _glu_fwd(gl, proj, w['ssm_b_glu'])
            sv.update(proj=proj, pre=pre, yg=yg, xr=xr_all, xi=xi_all, gl=gl)
            w_out = w['ssm_w_out'][0]
        else:
            proj = _mm(hn, w['fox_w_in'][:, 0], b_split=True, name="fox_in")
            z = _mm(hn, wfg, out_dtype=F32, name="fox_fg")
            fox_ops = _fox_operands(proj, _fg_fwd(z, bfg))
            ot, lse = _fox_attention(fox_ops[0], fox_ops[1], fox_ops[5], fox_ops[6], fox_ops[7])
            o = unheads(ot)
            a = _fox_gate_fwd(o, proj)
            sv.update(proj=proj, z=z, fox_ops=fox_ops, ot=ot, o=o, lse=lse)
            w_out = w['fox_w_out'][0]
        y = _mm(a, w_out, out_dtype=F32, name=f"mixer_out{i}")
        x1, gn = _ple_fwd(x, y, w['ple_norm'][i:i + 1], f"ple_fwd{i}")
        emb = _mm(pb[i], w['ple_proj'][:, i], b_split=True, out_dtype=F32, name=f"ple_emb{i}")
        gl2 = _mm(gn, w['ple_gate'][i], out_dtype=F32, name=f"ple_gate{i}")
        x = _ple_mix(x1, emb, gl2, f"ple_mix{i}")
        sv.update(a=a, x1=x1, gn=gn, emb=emb, gl2=gl2)
        saved.append(sv)

    loss, dx, grads['final_g'] = _loss_grad(x, target, w['final_g'].reshape(1, d))
    grads['final_g'] = grads['final_g'].reshape(d)

    g_norm, g_ple_norm, g_ple_proj, g_ple_gate = [None] * depth, [None] * depth, [None] * depth, [None] * depth
    for i in reversed(range(depth)):
        sv = saved[i]
        mixer = i % 4
        demb, dgl2 = _ple_mix_bwd(dx, sv['emb'], sv['gl2'], f"ple_mix_bwd{i}")
        g_ple_proj[i] = _mm(pb[i], demb, ta=True, out_split=N_CHIPS, name=f"ple_emb_dw{i}")
        g_ple_gate[i] = _mm(sv['gn'], dgl2, ta=True, name=f"ple_gate_dw{i}")
        dgn = _mm(dgl2, w['ple_gate'][i], tb=True, out_dtype=F32, name=f"ple_gate_dx{i}")
        dx1, dy, g_ple_norm[i] = _norm_bwd(sv['x1'], w['ple_norm'][i:i + 1], [dgn], dx, f"ple_norm_bwd{i}")
        w_out_name = ('swa_w_out', 'conv_w_out', 'ssm_w_out', 'fox_w_out')[mixer]
        grads[w_out_name] = _mm(sv['a'], dy, ta=True, name=f"mixer_out_dw{i}")[None]
        da = _mm(dy, w[w_out_name][0], tb=True, name=f"mixer_out_dx{i}")
        proj = sv['proj']
        dhs = []
        if mixer == 0:
            dq, dgate, dk4, dv4, dbias, dsink = _swa_bwd(proj, sv['k4'], sv['v4'], bias, sinks, da)
            back = lambda t4: t4.transpose(1, 0, 2).reshape(s, SWA_KV_HEADS * HEAD_DIM).astype(BF16)
            dproj = jnp.concatenate([dq, back(dk4), back(dv4), dgate], axis=1)
            grads['rel_bias'] = _bias_grad(dbias.reshape(SWA_HEADS, -1), jnp.asarray(onehot, BF16))
            grads['swa_sinks'] = dsink[:, :, 0].reshape(1, SWA_HEADS)
            w_in_name = 'swa_w_in'
        elif mixer == 1:
            dbg, dcg, du, dgate, dkern = _conv_bwd(proj, w['conv_kernel'][0], da)
            dproj = jnp.concatenate([dbg, dcg, du, dgate], axis=1)
            grads['conv_kernel'] = dkern[None]
            w_in_name = 'conv_w_in'
        elif mixer == 2:
            dgl, dgate, dbglu = _glu_bwd(sv['gl'], proj, w['ssm_b_glu'], da)
            grads['ssm_b_glu'] = dbglu
            grads['ssm_w_glu'] = _mm(sv['yg'], dgl, ta=True, out_split=N_CHIPS, name="ssm_glu_dw")[:, None]
            dyg = _mm(dgl, w['ssm_w_glu'][:, 0], tb=True, b_split=True, out_dtype=F32, name="ssm_glu_dx")
            du, dd, da_re, da_im, dbd_re, dbd_im, dcd_re, dcd_im = _ssm_bwd(
                proj, dyg, sv['pre'], sv['xr'], sv['xi'], pw_re, pw_im, rtab_re, rtab_im, bd_re, bd_im, cd_re, cd_im, w['ssm_d'])
            grads['ssm_d'] = dd
            dparams = ssm_vjp((da_re, da_im, dbd_re, dbd_im, dcd_re, dcd_im))
            for k, v in zip(('ssm_lam_re', 'ssm_lam_im', 'ssm_log_dt', 'ssm_b_re', 'ssm_b_im', 'ssm_c_re', 'ssm_c_im'), dparams):
                grads[k] = v[None]
            dproj = jnp.concatenate([du.astype(BF16), dgate], axis=1)
            w_in_name = 'ssm_w_in'
        else:
            do, dgate = _fox_gate_bwd(da, sv['o'], proj)
            qa, ka, qat, kat, v, _, c_row, c_lanes = sv['fox_ops']
            dot = do.reshape(s, FOX_HEADS, HEAD_DIM).transpose(1, 2, 0)
            dqt, dcq, dkt, dck, dvt = _fox_attention_bwd(qa, ka, qat, kat, v, c_row, c_lanes, sv['ot'], dot, sv['lse'])
            dq, dk, dv = unheads(dqt), unheads(dkt), unheads(dvt)
            dcs = jnp.pad((dcq[:, 0] + dck[:, 0]).T, ((0, 0), (0, LANES - FOX_HEADS)))
            dz, dbfg = _fg_bwd(sv['z'], bfg, dcs)
            grads['fox_b_fg'] = dbfg[:, :FOX_HEADS]
            grads['fox_w_fg'] = _mm(sv['hn'], dz, ta=True, out_dtype=BF16, name="fox_fg_dw")[:, :FOX_HEADS][None]
            dhs.append(_mm(dz, wfg, tb=True, out_dtype=F32, name="fox_fg_dx"))
            dproj = jnp.concatenate([dq, dk, dv, dgate], axis=1)
            w_in_name = 'fox_w_in'
        grads[w_in_name] = _mm(sv['hn'], dproj, ta=True, out_split=N_CHIPS, name=f"mixer_in_dw{i}")[:, None]
        dhs.append(_mm(dproj, w[w_in_name][:, 0], tb=True, b_split=True, out_dtype=F32, name=f"mixer_in_dx{i}"))
        dx, _, g_norm[i] = _norm_bwd(sv['x'], w['norm_g'][i:i + 1], dhs, dx1, f"norm_bwd{i}")

    grads['norm_g'] = jnp.concatenate(g_norm, axis=0)
    grads['ple_norm'] = jnp.concatenate(g_ple_norm, axis=0)
    grads['ple_proj'] = jnp.stack(g_ple_proj, axis=1)
    grads['ple_gate'] = jnp.stack(g_ple_gate)
    return loss, dx, grads


def _bf16_terms(a):
    hi = lax.reduce_precision(a, 8, 7)
    r1 = a - hi
    mid = lax.reduce_precision(r1, 8, 7)
    lo = lax.reduce_precision(r1 - mid, 8, 7)
    return hi.astype(BF16), mid.astype(BF16), lo.astype(BF16)


def _split3(a):
    return jnp.concatenate(_bf16_terms(a), axis=0)


def _bias_grad(dbias, onehot):
    out = _mm(_split3(dbias), onehot, out_dtype=F32, name="rel_bias_grad", tk=2048)
    nh = dbias.shape[0]
    return (out[:nh] + out[nh:2 * nh] + out[2 * nh:]).T


def _bias_table(rel_bias, onehot_t):
    nh = rel_bias.shape[1]
    out = _mm(_split3(rel_bias.T), onehot_t, out_dtype=F32, name="rel_bias_table")
    return (out[:nh] + out[nh:2 * nh] + out[2 * nh:]).reshape(nh, BLOCK, 2 * BLOCK)


WEIGHTS = ['norm_g', 'final_g', 'rel_bias', 'swa_w_in', 'swa_w_out', 'swa_sinks', 'conv_w_in', 'conv_kernel', 'conv_w_out',
           'ssm_w_in', 'ssm_lam_re', 'ssm_lam_im', 'ssm_log_dt', 'ssm_b_re', 'ssm_b_im', 'ssm_c_re', 'ssm_c_im', 'ssm_d',
           'ssm_w_glu', 'ssm_b_glu', 'ssm_w_out', 'fox_w_in', 'fox_w_fg', 'fox_b_fg', 'fox_w_out', 'ple_proj', 'ple_norm',
           'ple_gate']
BIG = {'swa_w_in': 2, 'swa_w_out': 1, 'conv_w_in': 2, 'conv_w_out': 1, 'ssm_w_in': 2, 'ssm_w_glu': 2, 'ssm_w_out': 1,
       'fox_w_in': 2, 'fox_w_fg': 1, 'fox_w_out': 1, 'ple_proj': 2, 'ple_gate': 1}
SMALL = {'conv_kernel': 2, 'ssm_d': 1, 'ssm_b_glu': 1}
REPLICATED = [n for n in WEIGHTS if n not in BIG and n not in SMALL]
N_CHIPS = 4
N_DEV = 8
BIG_ROWS = 256
SMALL_ROWS = 8
REPL_ROWS = 64


def _flat(pieces, dtype, lead, row_mult):
    flat = jnp.concatenate([q.astype(dtype) for q in pieces], axis=-1)
    pad = (-flat.shape[-1]) % (row_mult * FLAT_W)
    flat = jnp.pad(flat, [(0, 0)] * len(lead) + [(0, pad)])
    return flat.reshape(*lead, -1, FLAT_W)


def _unflat(flat, lead_ndim, sizes):
    lead = flat.shape[:lead_ndim]
    flat = flat.reshape(*lead, -1)
    out, off = [], 0
    for n in sizes:
        out.append(flat[..., off:off + n])
        off += n
    return out


def _split_shards(full, axis):
    shp = full.shape
    parts = full.reshape(shp[:axis] + (N_CHIPS, shp[axis] // N_CHIPS) + shp[axis + 1:])
    return jnp.moveaxis(parts, axis, 0)


def _join_shards(parts, axis):
    moved = jnp.moveaxis(parts, 0, axis)
    shp = moved.shape
    return moved.reshape(shp[:axis] + (shp[axis] * shp[axis + 1],) + shp[axis + 2:])


def _coords():
    return lax.axis_index("x"), lax.axis_index("y"), lax.axis_index("c")


def _remote(k, src, dst, to, send_sems, recv_sems):
    return pltpu.make_async_remote_copy(src_ref=src, dst_ref=dst, send_sem=send_sems.at[k], recv_sem=recv_sems.at[k],
                                        device_id=to, device_id_type=MESH)


def _gather_weights(bufs, ssh):
    nb = len(bufs)

    def body(*refs):
        w_refs, s_ref = refs[:nb], refs[nb]
        wouts, sout = refs[nb + 1:2 * nb + 1], refs[2 * nb + 1]
        send_sems, recv_sems = refs[2 * nb + 2:]
        x, y, c = _coords()
        me = 2 * x + y
        chips = [(1 - x, y), (x, 1 - y), (1 - x, 1 - y)]
        rc = functools.partial(_remote, send_sems=send_sems, recv_sems=recv_sems)
        sends = []
        for j, (cx, cy) in enumerate(chips):
            for b in range(nb):
                sends.append(rc(6 * b + j, w_refs[b].at[c], wouts[b].at[me, c], (cx, cy, c)))
            sends.append(rc(6 * nb + j, s_ref, sout.at[me], (cx, cy, c)))
        for cp in sends:
            cp.start()
        for j, (cx, cy) in enumerate(chips):
            k = 2 * cx + cy
            for b in range(nb):
                rc(6 * b + j, w_refs[b].at[c], wouts[b].at[k, c], (x, y, c)).wait_recv()
                fwd = rc(6 * b + 3 + j, wouts[b].at[k, c], wouts[b].at[k, c], (x, y, 1 - c))
                fwd.start()
                sends.append(fwd)
        for j, (cx, cy) in enumerate(chips):
            k = 2 * cx + cy
            for b in range(nb):
                rc(6 * b + 3 + j, w_refs[b].at[c], wouts[b].at[k, 1 - c], (x, y, c)).wait_recv()
            rc(6 * nb + j, s_ref, sout.at[k], (x, y, c)).wait_recv()
        for cp in sends:
            cp.wait_send()

    nsem = 6 * nb + 3
    res = pl.pallas_call(
        body, name="gather_weights", in_specs=[ANY] * (nb + 1), out_specs=[ANY] * (nb + 1),
        out_shape=[jax.ShapeDtypeStruct((N_CHIPS,) + a.shape, a.dtype) for a in (*bufs, ssh)],
        scratch_shapes=[pltpu.SemaphoreType.DMA((nsem,)), pltpu.SemaphoreType.DMA((nsem,))],
    )(*bufs, ssh)
    return res[:nb], res[nb]


def _pair_exchange(gbufs):
    nb = len(gbufs)

    def body(*refs):
        g_refs, outs = refs[:nb], refs[nb:2 * nb]
        send_sems, recv_sems = refs[2 * nb:]
        x, y, c = _coords()
        rc = functools.partial(_remote, send_sems=send_sems, recv_sems=recv_sems)
        sends = [rc(N_CHIPS * b + j, g_refs[b].at[2 * j + 1 - c], outs[b].at[j], (x, y, 1 - c))
                 for b in range(nb) for j in range(N_CHIPS)]
        for cp in sends:
            cp.start()
        for b in range(nb):
            for j in range(N_CHIPS):
                rc(N_CHIPS * b + j, g_refs[b].at[2 * j + c], outs[b].at[j], (x, y, c)).wait_recv()
        for cp in sends:
            cp.wait_send()

    nsem = N_CHIPS * nb
    return pl.pallas_call(
        body, name="pair_exchange", in_specs=[ANY] * nb, out_specs=[ANY] * nb,
        out_shape=[jax.ShapeDtypeStruct((N_CHIPS,) + g.shape[1:], g.dtype) for g in gbufs],
        scratch_shapes=[pltpu.SemaphoreType.DMA((nsem,)), pltpu.SemaphoreType.DMA((nsem,))],
    )(*gbufs)


def _pair_sum(mine, theirs, name):
    n, r, w = mine.shape
    t = _tile(r, 256, 16)
    spec = lambda a: (a, (n, t, w), lambda i: (0, i, 0))
    return _tiled(lambda a, b: (a.astype(F32) + b.astype(F32),), [spec(mine), spec(theirs)],
                  [_out((n, r, w), BF16, (n, t, w), lambda i: (0, i, 0))], r // t, name)[0]


def _exchange_grads(pbufs, gsmall, grepl):
    nb = len(pbufs)

    def body(*refs):
        g_refs, s_ref, r_ref = refs[:nb], refs[nb], refs[nb + 1]
        ogs, osm, orp = refs[nb + 2:2 * nb + 2], refs[2 * nb + 2], refs[2 * nb + 3]
        send_sems, recv_sems, local_sems = refs[2 * nb + 4:]
        x, y, c = _coords()
        me = 4 * x + 2 * y + c
        my_chip = 2 * x + y
        rc = functools.partial(_remote, send_sems=send_sems, recv_sems=recv_sems)
        local = [pltpu.make_async_copy(s_ref.at[me], osm.at[me], local_sems.at[0]),
                 pltpu.make_async_copy(r_ref, orp.at[me], local_sems.at[1])]
        for cp in local:
            cp.start()
        peers = []
        for d in range(1, N_DEV):
            px = 1 - x if d & 4 else x
            py = 1 - y if d & 2 else y
            pc = 1 - c if d & 1 else c
            peers.append((d, 4 * px + 2 * py + pc, 2 * px + py, (px, py, pc)))
        per_peer = nb + 2
        sends = []
        for i, (d, peer, chip, to) in enumerate(peers):
            sends.append(rc(per_peer * i + nb, s_ref.at[peer], osm.at[me], to))
            sends.append(rc(per_peer * i + nb + 1, r_ref, orp.at[me], to))
            if d & 1 == 0:
                for b in range(nb):
                    sends.append(rc(per_peer * i + b, g_refs[b].at[chip], ogs[b].at[my_chip], to))
        for cp in sends:
            cp.start()
        for i, (d, peer, chip, to) in enumerate(peers):
            rc(per_peer * i + nb, s_ref.at[peer], osm.at[peer], to).wait_recv()
            rc(per_peer * i + nb + 1, r_ref, orp.at[peer], to).wait_recv()
            if d & 1 == 0:
                for b in range(nb):
                    rc(per_peer * i + b, g_refs[b].at[chip], ogs[b].at[chip], to).wait_recv()
        for cp in sends:
            cp.wait_send()
        for cp in local:
            cp.wait()

    nsem = (nb + 2) * (N_DEV - 1)
    res = pl.pallas_call(
        body, name="exchange_grads", in_specs=[ANY] * (nb + 2), out_specs=[ANY] * (nb + 2),
        out_shape=[jax.ShapeDtypeStruct(a.shape, a.dtype) for a in (*pbufs, gsmall)]
        + [jax.ShapeDtypeStruct((N_DEV,) + grepl.shape, grepl.dtype)],
        scratch_shapes=[pltpu.SemaphoreType.DMA((nsem,)), pltpu.SemaphoreType.DMA((nsem,)), pltpu.SemaphoreType.DMA((2,))],
    )(*pbufs, gsmall, grepl)
    return res[:nb], res[nb], res[nb + 1]


def _sibling_exchange(halves):
    nb = len(halves)

    def body(*refs):
        ins, outs = refs[:nb], refs[nb:2 * nb]
        send_sems, recv_sems = refs[2 * nb:]
        x, y, c = _coords()
        rc = functools.partial(_remote, send_sems=send_sems, recv_sems=recv_sems)
        sends = [rc(b, ins[b], outs[b], (x, y, 1 - c)) for b in range(nb)]
        for cp in sends:
            cp.start()
        for b in range(nb):
            rc(b, ins[b], outs[b], (x, y, c)).wait_recv()
        for cp in sends:
            cp.wait_send()

    return pl.pallas_call(
        body, name="sibling_exchange", in_specs=[ANY] * nb, out_specs=[ANY] * nb,
        out_shape=[jax.ShapeDtypeStruct(a.shape, a.dtype) for a in halves],
        scratch_shapes=[pltpu.SemaphoreType.DMA((nb,)), pltpu.SemaphoreType.DMA((nb,))],
    )(*halves)


def _sum_senders(recv, name):
    n, r, w = recv.shape
    t = _tile(r, 256, 8)

    def fn(v):
        acc = v[0].astype(F32)
        for i in range(1, n):
            acc = acc + v[i].astype(F32)
        return (acc,)

    return _tiled(fn, [(recv, (n, t, w), lambda i: (0, i, 0))], [_orows(r, w, F32, t)], r // t, name)[0]


def _adamw(w, g, m, v, name):
    r, wd = w.shape
    t = _tile(r, 256, 8)

    def fn(wv, gv, mv, vv):
        m2 = ADAM_B1 * mv + (1.0 - ADAM_B1) * gv
        v2 = ADAM_B2 * vv + (1.0 - ADAM_B2) * (gv * gv)
        m_hat = m2 / (1.0 - ADAM_B1 ** ADAM_STEP)
        v_hat = v2 / (1.0 - ADAM_B2 ** ADAM_STEP)
        delta = -ADAM_LR * (m_hat / (jnp.sqrt(v_hat) + ADAM_EPS) + ADAM_WD * wv)
        return delta, m2, v2

    return _tiled(fn, [_rows(a, t) for a in (w, g, m, v)], [_orows(r, wd, F32, t)] * 3, r // t, name)


def kernel(x, p, norm_g, final_g, rel_bias, swa_w_in, swa_w_out, swa_sinks, conv_w_in, conv_kernel, conv_w_out, ssm_w_in, ssm_lam_re, ssm_lam_im, ssm_log_dt, ssm_b_re, ssm_b_im, ssm_c_re, ssm_c_im, ssm_d, ssm_w_glu, ssm_b_glu, ssm_w_out, fox_w_in, fox_w_fg, fox_b_fg, fox_w_out, ple_proj, ple_norm, ple_gate, loss_target, m_norm_g, m_final_g, m_rel_bias, m_swa_w_in, m_swa_w_out, m_swa_sinks, m_conv_w_in, m_conv_kernel, m_conv_w_out, m_ssm_w_in, m_ssm_lam_re, m_ssm_lam_im, m_ssm_log_dt, m_ssm_b_re, m_ssm_b_im, m_ssm_c_re, m_ssm_c_im, m_ssm_d, m_ssm_w_glu, m_ssm_b_glu, m_ssm_w_out, m_fox_w_in, m_fox_w_fg, m_fox_b_fg, m_fox_w_out, m_ple_proj, m_ple_norm, m_ple_gate, v_norm_g, v_final_g, v_rel_bias, v_swa_w_in, v_swa_w_out, v_swa_sinks, v_conv_w_in, v_conv_kernel, v_conv_w_out, v_ssm_w_in, v_ssm_lam_re, v_ssm_lam_im, v_ssm_log_dt, v_ssm_b_re, v_ssm_b_im, v_ssm_c_re, v_ssm_c_im, v_ssm_d, v_ssm_w_glu, v_ssm_b_glu, v_ssm_w_out, v_fox_w_in, v_fox_w_fg, v_fox_b_fg, v_fox_w_out, v_ple_proj, v_ple_norm, v_ple_gate):
    given = dict(locals())
    shard_shape = {n: given[n].shape for n in WEIGHTS}
    half = {n: math.prod(shard_shape[n]) // 2 for n in WEIGHTS}

    widths = sorted({shard_shape[n][-1] for n in BIG}, reverse=True)
    classes = [[n for n in BIG if shard_shape[n][-1] == w] for w in widths]
    rows = {n: math.prod(shard_shape[n][:-1]) for n in BIG}
    core = lax.axis_index("c")
    my_chip = 2 * lax.axis_index("x") + lax.axis_index("y")

    bufs = [jnp.concatenate([given[n].astype(BF16).reshape(rows[n], w) for n in names], axis=0).reshape(2, -1, w)
            for w, names in zip(widths, classes)]
    ssh = _flat([given[n].reshape(-1) for n in SMALL], F32, (), SMALL_ROWS)
    walls, sall = _gather_weights(bufs, ssh)
    walls = [lax.dynamic_update_slice(a, b[None], (my_chip, 0, 0, 0)) for a, b in zip(walls, bufs)]
    sall = lax.dynamic_update_slice(sall, ssh[None], (my_chip, 0, 0))
    full = {n: given[n] for n in REPLICATED}
    for w, names, wall in zip(widths, classes, walls):
        wall, off = wall.reshape(N_CHIPS, -1, w), 0
        for n in names:
            stacked = wall[:, off:off + rows[n]].reshape((N_CHIPS,) + shard_shape[n])
            full[n] = stacked if BIG[n] == 2 else _join_shards(stacked, BIG[n])
            off += rows[n]
    for n, piece in zip(SMALL, _unflat(sall, 1, [2 * half[n] for n in SMALL])):
        full[n] = _join_shards(piece.reshape((N_CHIPS,) + shard_shape[n]), SMALL[n])

    loss, dx, grads = _local_step(x[0], p[:, 0], loss_target[0], full)

    by_chip = lambda n: grads[n] if BIG[n] == 2 else _split_shards(grads[n], BIG[n])
    gbufs = [jnp.concatenate([by_chip(n).reshape(N_CHIPS, rows[n], w) for n in names], axis=1).reshape(N_DEV, -1, w)
             for w, names in zip(widths, classes)]
    gsmall = _flat([_split_shards(grads[n], SMALL[n]).reshape(N_DEV, -1) for n in SMALL], F32, (N_DEV,), SMALL_ROWS)
    grepl = _flat([grads[n].reshape(-1) for n in REPLICATED], F32, (), REPL_ROWS)
    pbufs = [_pair_sum(jnp.where(core == 0, g[0::2], g[1::2]), t, f"pair_sum_w{w}")
             for w, g, t in zip(widths, gbufs, _pair_exchange(gbufs))]
    recv_bufs, recv_small, recv_repl = _exchange_grads(pbufs, gsmall, grepl)
    recv_bufs = [lax.dynamic_update_slice(r, lax.dynamic_index_in_dim(pb, my_chip, axis=0), (my_chip, 0, 0))
                 for r, pb in zip(recv_bufs, pbufs)]
    halves = [_sum_senders(r, f"sum_w{w}") for w, r in zip(widths, recv_bufs)] + [_sum_senders(recv_small, "sum_small")]
    others = _sibling_exchange(halves)
    g_repl = _sum_senders(recv_repl, "sum_repl")
    both = [(jnp.where(core == 0, a, b), jnp.where(core == 0, b, a)) for a, b in zip(halves, others)]
    sfull = jnp.stack(both[-1])

    out_g, out_d, out_m, out_v = {}, {}, {}, {}
    for w, names, (lower, upper) in zip(widths, classes, both):
        gfull, off = jnp.concatenate([lower, upper], axis=0), 0
        for n in names:
            as_rows = lambda t, w=w: t.reshape(-1, w)
            piece = gfull[off:off + rows[n]]
            off += rows[n]
            d, m2, v2 = _adamw(as_rows(given[n]), piece, as_rows(given['m_' + n]), as_rows(given['v_' + n]), f"adamw_{n}")
            out_g[n], out_d[n], out_m[n], out_v[n] = (t.reshape(shard_shape[n]) for t in (piece, d, m2, v2))

    pack_small = lambda pre: _flat([given[pre + n].reshape(2, -1) for n in SMALL], F32, (2,), SMALL_ROWS).reshape(-1, FLAT_W)
    res = _adamw(pack_small(''), sfull.reshape(-1, FLAT_W), pack_small('m_'), pack_small('v_'), "adamw_small")
    for dst, flat in zip((out_g, out_d, out_m, out_v), (sfull,) + tuple(res)):
        for n, piece in zip(SMALL, _unflat(flat.reshape(2, -1, FLAT_W), 1, [half[n] for n in SMALL])):
            dst[n] = piece.reshape(shard_shape[n])

    pack_repl = lambda pre: _flat([given[pre + n].reshape(-1) for n in REPLICATED], F32, (), REPL_ROWS)
    res = _adamw(pack_repl(''), g_repl, pack_repl('m_'), pack_repl('v_'), "adamw_repl")
    for dst, flat in zip((out_g, out_d, out_m, out_v), (g_repl,) + tuple(res)):
        for n, piece in zip(REPLICATED, _unflat(flat, 0, [2 * half[n] for n in REPLICATED])):
            dst[n] = piece.reshape(shard_shape[n])

    total = lax.psum(loss[0, 0], ("x", "y", "c"))
    return (total, dx[None], *[out_g[n] for n in WEIGHTS], *[out_d[n] for n in WEIGHTS],
            *[out_m[n] for n in WEIGHTS], *[out_v[n] for n in WEIGHTS])
```

```python
import functools
import math

import numpy as np
import jax
import jax.numpy as jnp
from jax import lax
from jax.experimental import pallas as pl
from jax.experimental.pallas import tpu as pltpu

F32 = jnp.float32
BF16 = jnp.bfloat16

EPS = 1e-6
BLOCK = 128
REL_BUCKETS = 32
REL_MAX_DIST = 128
SWA_HEADS, SWA_KV_HEADS, HEAD_DIM = 32, 4, 64
SWA_GROUP = SWA_HEADS // SWA_KV_HEADS
FOX_HEADS = 32
SSM_GROUP, SSM_STATE = 16, 64
GROUPS_PER_STEP = 8
STATE_W = GROUPS_PER_STEP * SSM_STATE
LANES = 128
NEG = -1e30

ADAM_LR, ADAM_B1, ADAM_B2, ADAM_EPS, ADAM_WD, ADAM_STEP = 0.001, 0.9, 0.999, 1e-08, 0.01, 10

VMEM_LIMIT_V7X = 56 * 1024 * 1024
FLAT_W = 1024
MESH = pl.DeviceIdType.MESH
ANY = pl.BlockSpec(memory_space=pl.ANY)


def _cp(sem):
    return pltpu.CompilerParams(dimension_semantics=sem, vmem_limit_bytes=VMEM_LIMIT_V7X)


def _tile(n, target, mult=LANES):
    if n <= target:
        return n
    t = (target // mult) * mult
    while t >= mult:
        if n % t == 0:
            return t
        t -= mult
    return n


MAX_WHOLE_TILE = 1152


def _div(a, b):
    return lax.div(a, jnp.int32(b))


def _rem(a, b):
    return lax.rem(a, jnp.int32(b))


def _mm(a, b, *, ta=False, tb=False, out_dtype=BF16, name, tm=1024, tn=1024, tk=4096, b_split=False, out_split=0):
    if ta:
        kdim, m = a.shape
    else:
        m, kdim = a.shape
    parts = b.shape[0] if b_split else 1
    b_rows, b_cols = (b.shape[1], b.shape[2] * parts) if b_split else b.shape
    n = b_rows if tb else b_cols
    assert (b_cols if tb else b_rows) == kdim
    n_range = n // (out_split or (1 if tb else parts))
    k_range = kdim // (parts if tb else 1)
    tm, tk = _tile(m, tm), _tile(k_range, tk)
    tn = n_range if n_range <= MAX_WHOLE_TILE else _tile(n_range, tn)
    nk = kdim // tk
    dn = (((0 if ta else 1,), (1 if tb else 0,)), ((), ()))

    def body(a_ref, b_ref, o_ref, *acc):
        part = lax.dot_general(a_ref[...].astype(BF16), b_ref[...].astype(BF16), dn, preferred_element_type=F32)
        if nk == 1:
            o_ref[...] = part.astype(o_ref.dtype)
            return
        acc_ref, = acc
        k = pl.program_id(2)

        @pl.when(k == 0)
        def _():
            acc_ref[...] = part

        @pl.when(k > 0)
        def _():
            acc_ref[...] += part

        @pl.when(k == nk - 1)
        def _():
            o_ref[...] = acc_ref[...].astype(o_ref.dtype)

    a_spec = pl.BlockSpec((tk, tm), lambda i, j, k: (k, i)) if ta else pl.BlockSpec((tm, tk), lambda i, j, k: (i, k))
    nj, nkr = n_range // tn, k_range // tk
    if not b_split:
        b_spec = pl.BlockSpec((tn, tk), lambda i, j, k: (j, k)) if tb else pl.BlockSpec((tk, tn), lambda i, j, k: (k, j))
    elif tb:
        b_spec = pl.BlockSpec((None, tn, tk), lambda i, j, k: (_div(k, nkr), j, _rem(k, nkr)))
    else:
        b_spec = pl.BlockSpec((None, tk, tn), lambda i, j, k: (_div(j, nj), k, _rem(j, nj)))
    if out_split:
        out_spec = pl.BlockSpec((None, tm, tn), lambda i, j, k: (_div(j, nj), i, _rem(j, nj)))
        out_shape = jax.ShapeDtypeStruct((out_split, m, n_range), out_dtype)
    else:
        out_spec = pl.BlockSpec((tm, tn), lambda i, j, k: (i, j))
        out_shape = jax.ShapeDtypeStruct((m, n), out_dtype)
    return pl.pallas_call(
        body, name=name, grid=(m // tm, n // tn, nk),
        in_specs=[a_spec, b_spec], out_specs=out_spec, out_shape=out_shape,
        scratch_shapes=[pltpu.VMEM((tm, tn), F32)] if nk > 1 else [],
        compiler_params=_cp(("parallel", "parallel", "arbitrary")),
    )(a, b)


def _rows(arr, t):
    return (arr, (t, arr.shape[1]), lambda i: (i, 0))


def _cols(arr, cb, off=0):
    return (arr, (arr.shape[0], cb), lambda i: (0, i + off))


def _full(arr):
    nd = arr.ndim
    return (arr, arr.shape, lambda i: (0,) * nd)


def _lead(arr):
    return (arr, (1,) + arr.shape[1:], lambda i: (i, 0, 0))


def _out(shape, dtype, block, imap, acc=False):
    return (jax.ShapeDtypeStruct(shape, dtype), block, imap, acc)


def _orows(s, w, dtype, t):
    return _out((s, w), dtype, (t, w), lambda i: (i, 0))


def _ocols(s, w, dtype, cb):
    return _out((s, w), dtype, (s, cb), lambda i: (0, i))


def _oacc(shape):
    nd = len(shape)
    return _out(shape, F32, shape, lambda i: (0,) * nd, True)


def _olead(n, a, b, dtype=F32):
    return _out((n, a, b), dtype, (1, a, b), lambda i: (i, 0, 0))


def _tiled(fn, ins, outs, n, name):
    has_acc = any(o[3] for o in outs)
    ni = len(ins)

    def body(*refs):
        vals = fn(*[r[...] for r in refs[:ni]])
        i = pl.program_id(0)
        for r, v, o in zip(refs[ni:], vals, outs):
            if o[3]:
                @pl.when(i == 0)
                def _(r=r, v=v):
                    r[...] = v.astype(r.dtype)

                @pl.when(i > 0)
                def _(r=r, v=v):
                    r[...] += v.astype(r.dtype)
            else:
                r[...] = v.astype(r.dtype)

    res = pl.pallas_call(
        body, name=name, grid=(n,),
        in_specs=[pl.BlockSpec(b, m) for _, b, m in ins],
        out_specs=[pl.BlockSpec(b, m) for _, b, m, _ in outs],
        out_shape=[s for s, _, _, _ in outs],
        compiler_params=_cp(("arbitrary",) if has_acc else ("parallel",)),
    )(*[a for a, _, _ in ins])
    return res


def _silu(x):
    return x * jax.nn.sigmoid(x)


def _gelu(x):
    return 0.5 * x * (1.0 + jnp.tanh(math.sqrt(2.0 / math.pi) * (x + 0.044715 * (x * x * x))))


def _rms(x, g):
    r = lax.rsqrt(jnp.mean(x * x, axis=-1, keepdims=True) + EPS)
    return x * r * g


def _shift_rows(x, sh, up=False):
    s = x.shape[0]
    rows = lax.broadcasted_iota(jnp.int32, x.shape, 0)
    if up:
        return jnp.where(rows < s - sh, pltpu.roll(x, s - sh, 0), 0.0)
    return jnp.where(rows >= sh, pltpu.roll(x, sh, 0), 0.0)


SCAN_CHUNK = 128


def _scan_complex(xr, xi, pr, pi, tr, ti, reverse):
    s, lanes = xr.shape
    c = min(SCAN_CHUNK, s)
    nchunk = s // c
    in_chunk = lax.broadcasted_iota(jnp.int32, xr.shape, 0) & (c - 1)

    def shifted(x, sh):
        if reverse:
            return jnp.where(in_chunk < c - sh, pltpu.roll(x, s - sh, 0), 0.0)
        return jnp.where(in_chunk >= sh, pltpu.roll(x, sh, 0), 0.0)

    k = 0
    while (1 << k) < c:
        sr, si = shifted(xr, 1 << k), shifted(xi, 1 << k)
        ar, ai = pr[k:k + 1, :], pi[k:k + 1, :]
        xr, xi = xr + ar * sr - ai * si, xi + ar * si + ai * sr
        k += 1
    if nchunk == 1:
        return xr, xi
    xr, xi = xr.reshape(nchunk, c, lanes), xi.reshape(nchunk, c, lanes)
    edge = 0 if reverse else c - 1
    er, ei = xr[:, edge, :], xi[:, edge, :]
    m = 0
    while (1 << m) < nchunk:
        sr, si = _shift_rows(er, 1 << m, reverse), _shift_rows(ei, 1 << m, reverse)
        ar, ai = pr[k + m:k + m + 1, :], pi[k + m:k + m + 1, :]
        er, ei = er + ar * sr - ai * si, ei + ar * si + ai * sr
        m += 1
    cr, ci = _shift_rows(er, 1, reverse)[:, None, :], _shift_rows(ei, 1, reverse)[:, None, :]
    xr, xi = xr + tr[None] * cr - ti[None] * ci, xi + tr[None] * ci + ti[None] * cr
    return xr.reshape(s, lanes), xi.reshape(s, lanes)


def _cumsum_rows(x, reverse):
    s = x.shape[0]
    k = 0
    while (1 << k) < s:
        x = x + _shift_rows(x, 1 << k, reverse)
        k += 1
    return x


def _dot(a, b, ca=1, cb=0):
    return lax.dot_general(a.astype(BF16), b.astype(BF16), (((ca,), (cb,)), ((), ())), preferred_element_type=F32)


def _t5_bucket(dist):
    max_exact = REL_BUCKETS // 2
    d = np.maximum(dist, 1).astype(np.float32)
    large = max_exact + (np.log(d / max_exact) / np.log(REL_MAX_DIST / max_exact) * (REL_BUCKETS - max_exact)).astype(np.int32)
    large = np.minimum(large, REL_BUCKETS - 1)
    return np.where(dist < max_exact, dist, large).astype(np.int32)


def _swa_bucket_table():
    qi = np.arange(BLOCK)[:, None]
    kj = np.arange(2 * BLOCK)[None, :]
    return _t5_bucket(np.clip(qi + BLOCK - kj, 0, None))


def _stack_heads(x):
    return jnp.concatenate([x[:, g * HEAD_DIM:(g + 1) * HEAD_DIM] for g in range(SWA_GROUP)], axis=0)


def _unstack_heads(x):
    return jnp.concatenate([x[g * BLOCK:(g + 1) * BLOCK] for g in range(SWA_GROUP)], axis=1)


def _sink_rows(sink_ref):
    return jnp.concatenate([jnp.broadcast_to(sink_ref[0, g:g + 1, :1], (BLOCK, 1)) for g in range(SWA_GROUP)], axis=0)


def _swa_scores(qg, kb, bias_g, sk, n):
    s = _dot(qg, kb, 1, 1) * (HEAD_DIM ** -0.5) + bias_g
    row = lax.broadcasted_iota(jnp.int32, s.shape, 0) & (BLOCK - 1)
    col = lax.broadcasted_iota(jnp.int32, s.shape, 1)
    dist = row + BLOCK - col
    mask = (dist >= 0) & (dist < BLOCK) & ((col >= BLOCK) | (n > 0))
    s = jnp.where(mask, s, NEG)
    m = jnp.maximum(jnp.max(s, axis=1, keepdims=True), sk)
    e = jnp.exp(s - m)
    es = jnp.exp(sk - m)
    den = jnp.sum(e, axis=1, keepdims=True) + es
    return e / den, es / den


def _swa_specs(proj, k4, v4, nb, clamp):
    gw = SWA_GROUP * HEAD_DIM
    gate_off = (SWA_HEADS * HEAD_DIM + 2 * SWA_KV_HEADS * HEAD_DIM) // gw
    cur = (lambda n: jnp.minimum(n, nb - 1)) if clamp else (lambda n: n)
    prev = lambda n: jnp.maximum(cur(n) - 1, 0)
    return [
        pl.BlockSpec((BLOCK, gw), lambda h, n: (cur(n), h)),
        pl.BlockSpec((1, BLOCK, HEAD_DIM), lambda h, n: (h, cur(n), 0)),
        pl.BlockSpec((1, BLOCK, HEAD_DIM), lambda h, n: (h, prev(n), 0)),
        pl.BlockSpec((1, BLOCK, HEAD_DIM), lambda h, n: (h, cur(n), 0)),
        pl.BlockSpec((1, BLOCK, HEAD_DIM), lambda h, n: (h, prev(n), 0)),
        pl.BlockSpec((BLOCK, gw), lambda h, n: (cur(n), gate_off + h)),
    ], [proj, k4, k4, v4, v4, proj]


def _swa_fwd(proj, k4, v4, bias, sinks):
    s = proj.shape[0]
    nb = s // BLOCK
    gw = SWA_GROUP * HEAD_DIM

    def body(q_ref, kc_ref, kp_ref, vc_ref, vp_ref, gate_ref, bias_ref, sink_ref, a_ref):
        n = pl.program_id(1)
        kb = jnp.concatenate([kp_ref[0], kc_ref[0]], axis=0)
        vb = jnp.concatenate([vp_ref[0], vc_ref[0]], axis=0)
        p, _ = _swa_scores(_stack_heads(q_ref[...]), kb, bias_ref[...].reshape(-1, 2 * BLOCK), _sink_rows(sink_ref), n)
        a_ref[...] = (_unstack_heads(_dot(p, vb)) * _silu(gate_ref[...].astype(F32))).astype(a_ref.dtype)

    specs, args = _swa_specs(proj, k4, v4, nb, False)
    return pl.pallas_call(
        body, name="swa_fwd", grid=(SWA_KV_HEADS, nb),
        in_specs=specs + [pl.BlockSpec((SWA_GROUP, BLOCK, 2 * BLOCK), lambda h, n: (h, 0, 0)),
                          pl.BlockSpec((1, SWA_GROUP, LANES), lambda h, n: (h, 0, 0))],
        out_specs=pl.BlockSpec((BLOCK, gw), lambda h, n: (n, h)),
        out_shape=jax.ShapeDtypeStruct((s, SWA_HEADS * HEAD_DIM), BF16),
        compiler_params=_cp(("parallel", "parallel")),
    )(*args, bias, sinks)


def _swa_bwd(proj, k4, v4, bias, sinks, da):
    s = proj.shape[0]
    nb = s // BLOCK
    gw = SWA_GROUP * HEAD_DIM

    def body(q_ref, kc_ref, kp_ref, vc_ref, vp_ref, gate_ref, bias_ref, sink_ref, da_ref,
             dq_ref, dgate_ref, dk_ref, dv_ref, dbias_ref, dsink_ref, dk_own, dv_own):
        n = pl.program_id(1)

        @pl.when(n == 0)
        def _():
            dk_own[...] = jnp.zeros_like(dk_own)
            dv_own[...] = jnp.zeros_like(dv_own)
            dbias_ref[...] = jnp.zeros_like(dbias_ref)
            dsink_ref[...] = jnp.zeros_like(dsink_ref)

        @pl.when(n < nb)
        def _():
            kb = jnp.concatenate([kp_ref[0], kc_ref[0]], axis=0)
            vb = jnp.concatenate([vp_ref[0], vc_ref[0]], axis=0)
            dkb = jnp.zeros((2 * BLOCK, HEAD_DIM), F32)
            dvb = jnp.zeros((2 * BLOCK, HEAD_DIM), F32)
            for g in range(SWA_GROUP):
                sl = slice(g * HEAD_DIM, (g + 1) * HEAD_DIM)
                qg = q_ref[:, sl]
                p, p0 = _swa_scores(qg, kb, bias_ref[g], sink_ref[0, g:g + 1, :1], n)
                og = _dot(p, vb)
                gate = gate_ref[:, sl].astype(F32)
                dag = da_ref[:, sl].astype(F32)
                sg = jax.nn.sigmoid(gate)
                do = dag * gate * sg
                dgate_ref[:, sl] = (dag * og * sg * (1.0 + gate * (1.0 - sg))).astype(dgate_ref.dtype)
                dp = _dot(do, vb, 1, 1)
                delta = jnp.sum(do * og, axis=1, keepdims=True)
                ds = p * (dp - delta)
                dbias_ref[g] += ds
                dsink_ref[0, g:g + 1, :] += jnp.zeros((1, LANES), F32) - jnp.sum(p0 * delta, axis=0, keepdims=True)
                dq_ref[:, sl] = (_dot(ds, kb) * (HEAD_DIM ** -0.5)).astype(dq_ref.dtype)
                dkb += _dot(ds, qg, 0, 0) * (HEAD_DIM ** -0.5)
                dvb += _dot(p, do, 0, 0)
            dk_ref[0] = dk_own[...] + dkb[:BLOCK]
            dv_ref[0] = dv_own[...] + dvb[:BLOCK]
            dk_own[...] = dkb[BLOCK:]
            dv_own[...] = dvb[BLOCK:]

        @pl.when(n == nb)
        def _():
            dk_ref[0] = dk_own[...]
            dv_ref[0] = dv_own[...]

    specs, args = _swa_specs(proj, k4, v4, nb, True)
    cur = lambda n: jnp.minimum(n, nb - 1)
    trail = lambda n: jnp.maximum(n - 1, 0)
    return pl.pallas_call(
        body, name="swa_bwd", grid=(SWA_KV_HEADS, nb + 1),
        in_specs=specs + [pl.BlockSpec((SWA_GROUP, BLOCK, 2 * BLOCK), lambda h, n: (h, 0, 0)),
                          pl.BlockSpec((1, SWA_GROUP, LANES), lambda h, n: (h, 0, 0)),
                          pl.BlockSpec((BLOCK, gw), lambda h, n: (cur(n), h))],
        out_specs=[pl.BlockSpec((BLOCK, gw), lambda h, n: (cur(n), h)),
                   pl.BlockSpec((BLOCK, gw), lambda h, n: (cur(n), h)),
                   pl.BlockSpec((1, BLOCK, HEAD_DIM), lambda h, n: (h, trail(n), 0)),
                   pl.BlockSpec((1, BLOCK, HEAD_DIM), lambda h, n: (h, trail(n), 0)),
                   pl.BlockSpec((SWA_GROUP, BLOCK, 2 * BLOCK), lambda h, n: (h, 0, 0)),
                   pl.BlockSpec((1, SWA_GROUP, LANES), lambda h, n: (h, 0, 0))],
        out_shape=[jax.ShapeDtypeStruct((s, SWA_HEADS * HEAD_DIM), BF16),
                   jax.ShapeDtypeStruct((s, SWA_HEADS * HEAD_DIM), BF16),
                   jax.ShapeDtypeStruct((SWA_KV_HEADS, s, HEAD_DIM), F32),
                   jax.ShapeDtypeStruct((SWA_KV_HEADS, s, HEAD_DIM), F32),
                   jax.ShapeDtypeStruct((SWA_HEADS, BLOCK, 2 * BLOCK), F32),
                   jax.ShapeDtypeStruct((SWA_KV_HEADS, SWA_GROUP, LANES), F32)],
        scratch_shapes=[pltpu.VMEM((BLOCK, HEAD_DIM), F32), pltpu.VMEM((BLOCK, HEAD_DIM), F32)],
        compiler_params=_cp(("arbitrary", "arbitrary")),
    )(*args, bias, sinks, da)


FOX_TILE = 1024


def _below_and_on_diagonal(i, j, step):
    @pl.when(j < i)
    def _():
        step(False)

    @pl.when(j == i)
    def _():
        step(True)


FOX_AUG = 128
FOX_CQ, FOX_CK = HEAD_DIM, HEAD_DIM + 1


def _fox_operands(proj, csum):
    s = proj.shape[0]
    hw = FOX_HEADS * HEAD_DIM
    heads = lambda a: a.reshape(s, FOX_HEADS, HEAD_DIM).transpose(1, 0, 2)
    q = heads(proj[:, :hw]) * jnp.asarray(HEAD_DIM ** -0.5, BF16)
    k, v = heads(proj[:, hw:2 * hw]), heads(proj[:, 2 * hw:3 * hw])
    one = jnp.ones((FOX_HEADS, s, 1), BF16)
    zero = jnp.zeros((FOX_HEADS, s, 1), BF16)
    pad = jnp.zeros((FOX_HEADS, s, FOX_AUG - HEAD_DIM - 2), BF16)
    qa = jnp.concatenate([q, zero, -one, pad], axis=-1)
    ka = jnp.concatenate([k, one, zero, pad], axis=-1)
    tr = lambda a: a.transpose(0, 2, 1)
    c = csum[:, :FOX_HEADS].T
    return (qa, ka, tr(qa), tr(ka), v, tr(v), c[:, None, :], jnp.broadcast_to(c[:, :, None], (FOX_HEADS, s, LANES)))


def _fox_scores_t(ka, qa, cq, ck, diag):
    st = _dot(ka, qa, 1, 1) + cq - jnp.concatenate([ck] * (qa.shape[0] // LANES), axis=1)
    if diag:
        st = jnp.where(lax.broadcasted_iota(jnp.int32, st.shape, 0) <= lax.broadcasted_iota(jnp.int32, st.shape, 1), st, NEG)
    return st


def _fox_attention(qa, ka, vt, c_row, c_lanes):
    nh, s, aw = qa.shape
    t = min(FOX_TILE, s)
    nt = s // t
    hd = HEAD_DIM

    def body(qa_ref, ka_ref, vt_ref, cq_ref, ck_ref, o_ref, lse_ref, m_s, l_s, acc_s):
        i, j = pl.program_id(1), pl.program_id(2)

        @pl.when(j == 0)
        def _():
            m_s[...] = jnp.full_like(m_s, NEG)
            l_s[...] = jnp.zeros_like(l_s)
            acc_s[...] = jnp.zeros_like(acc_s)

        def step(diag):
            st = _fox_scores_t(ka_ref[0], qa_ref[0], cq_ref[0], ck_ref[0], diag)
            m_old = m_s[...]
            m_new = jnp.maximum(m_old, jnp.max(st, axis=0, keepdims=True))
            alpha = jnp.exp(m_old - m_new)
            p = jnp.exp(st - m_new)
            l_s[...] = alpha * l_s[...] + jnp.sum(p, axis=0, keepdims=True)
            acc_s[...] = alpha * acc_s[...] + _dot(vt_ref[0], p)
            m_s[...] = m_new

        _below_and_on_diagonal(i, j, step)

        @pl.when(j == nt - 1)
        def _():
            o_ref[0] = (acc_s[...] / l_s[...]).astype(o_ref.dtype)
            lse_ref[0] = m_s[...] + jnp.log(l_s[...])

    kj = lambda i, j: jnp.minimum(j, i)
    return pl.pallas_call(
        body, name="fox_fwd", grid=(nh, nt, nt),
        in_specs=[pl.BlockSpec((1, t, aw), lambda h, i, j: (h, i, 0)),
                  pl.BlockSpec((1, t, aw), lambda h, i, j: (h, kj(i, j), 0)),
                  pl.BlockSpec((1, hd, t), lambda h, i, j: (h, 0, kj(i, j))),
                  pl.BlockSpec((1, 1, t), lambda h, i, j: (h, 0, i)),
                  pl.BlockSpec((1, t, LANES), lambda h, i, j: (h, kj(i, j), 0))],
        out_specs=[pl.BlockSpec((1, hd, t), lambda h, i, j: (h, 0, i)),
                   pl.BlockSpec((1, 1, t), lambda h, i, j: (h, 0, i))],
        out_shape=[jax.ShapeDtypeStruct((nh, hd, s), BF16), jax.ShapeDtypeStruct((nh, 1, s), F32)],
        scratch_shapes=[pltpu.VMEM((1, t), F32), pltpu.VMEM((1, t), F32), pltpu.VMEM((hd, t), F32)],
        compiler_params=_cp(("parallel", "parallel", "arbitrary")),
    )(qa, ka, vt, c_row, c_lanes)


def _fox_attention_bwd(qa, ka, qat, kat, v, c_row, c_lanes, ot, dot, lse):
    nh, s, aw = qa.shape
    t = min(FOX_TILE, s)
    nt = s // t
    hd = HEAD_DIM

    def body(qa_ref, ka_ref, qat_ref, kat_ref, v_ref, cq_ref, ck_ref, ot_ref, dot_ref, lse_ref,
             dq_ref, dcq_ref, dk_ref, dck_ref, dv_ref, dqa_s, dk_s, dv_s):
        j, i = pl.program_id(1), pl.program_id(2)

        @pl.when((j == 0) & (i == 0))
        def _():
            dqa_s[...] = jnp.zeros_like(dqa_s)

        @pl.when(i == 0)
        def _():
            dk_s[...] = jnp.zeros_like(dk_s)
            dv_s[...] = jnp.zeros_like(dv_s)

        def step(diag):
            p = jnp.exp(_fox_scores_t(ka_ref[0], qa_ref[0], cq_ref[0], ck_ref[0], diag) - lse_ref[0])
            do_t = dot_ref[0]
            delta = jnp.sum(do_t.astype(F32) * ot_ref[0].astype(F32), axis=0, keepdims=True)
            ds = (p * (_dot(v_ref[0], do_t) - delta)).astype(BF16)
            dv_s[...] += _dot(do_t, p, 1, 1)
            dk_s[...] += _dot(qat_ref[0], ds, 1, 1)
            cols = pl.ds(pl.multiple_of(i * t, t), t)
            dqa_s[:, cols] += _dot(kat_ref[0], ds)

        _below_and_on_diagonal(i, j, step)

        @pl.when(i == nt - 1)
        def _():
            dk_ref[0] = dk_s[:hd].astype(dk_ref.dtype)
            dck_ref[0] = dk_s[FOX_CK:FOX_CK + 1]
            dv_ref[0] = dv_s[...].astype(dv_ref.dtype)

        @pl.when((j == nt - 1) & (i == nt - 1))
        def _():
            dq_ref[0] = (dqa_s[:hd] * (hd ** -0.5)).astype(dq_ref.dtype)
            dcq_ref[0] = dqa_s[FOX_CQ:FOX_CQ + 1]

    qi = lambda i, j: jnp.maximum(i, j)
    return pl.pallas_call(
        body, name="fox_bwd", grid=(nh, nt, nt),
        in_specs=[pl.BlockSpec((1, t, aw), lambda h, j, i: (h, qi(i, j), 0)),
                  pl.BlockSpec((1, t, aw), lambda h, j, i: (h, j, 0)),
                  pl.BlockSpec((1, aw, t), lambda h, j, i: (h, 0, qi(i, j))),
                  pl.BlockSpec((1, aw, t), lambda h, j, i: (h, 0, j)),
                  pl.BlockSpec((1, t, hd), lambda h, j, i: (h, j, 0)),
                  pl.BlockSpec((1, 1, t), lambda h, j, i: (h, 0, qi(i, j))),
                  pl.BlockSpec((1, t, LANES), lambda h, j, i: (h, j, 0)),
                  pl.BlockSpec((1, hd, t), lambda h, j, i: (h, 0, qi(i, j))),
                  pl.BlockSpec((1, hd, t), lambda h, j, i: (h, 0, qi(i, j))),
                  pl.BlockSpec((1, 1, t), lambda h, j, i: (h, 0, qi(i, j)))],
        out_specs=[pl.BlockSpec((1, hd, s), lambda h, j, i: (h, 0, 0)),
                   pl.BlockSpec((1, 1, s), lambda h, j, i: (h, 0, 0)),
                   pl.BlockSpec((1, hd, t), lambda h, j, i: (h, 0, j)),
                   pl.BlockSpec((1, 1, t), lambda h, j, i: (h, 0, j)),
                   pl.BlockSpec((1, hd, t), lambda h, j, i: (h, 0, j))],
        out_shape=[jax.ShapeDtypeStruct((nh, hd, s), BF16), jax.ShapeDtypeStruct((nh, 1, s), F32),
                   jax.ShapeDtypeStruct((nh, hd, s), BF16), jax.ShapeDtypeStruct((nh, 1, s), F32),
                   jax.ShapeDtypeStruct((nh, hd, s), BF16)],
        scratch_shapes=[pltpu.VMEM((aw, s), F32), pltpu.VMEM((aw, t), F32), pltpu.VMEM((hd, t), F32)],
        compiler_params=_cp(("parallel", "arbitrary", "arbitrary")),
    )(qa, ka, qat, kat, v, c_row, c_lanes, ot, dot, lse)


def _ssm_prep(lam_re, lam_im, log_dt, b_re, b_im, c_re, c_im):
    g, n = lam_re.shape
    dt = jnp.exp(log_dt)[:, None]
    mag = jnp.exp(lam_re * dt)
    ab_re = mag * jnp.cos(lam_im * dt)
    ab_im = mag * jnp.sin(lam_im * dt)
    den = lam_re * lam_re + lam_im * lam_im
    nr = ab_re - 1.0
    coef_re = ((nr * lam_re + ab_im * lam_im) / den)[..., None]
    coef_im = ((ab_im * lam_re - nr * lam_im) / den)[..., None]
    bb_re = coef_re * b_re - coef_im * b_im
    bb_im = coef_re * b_im + coef_im * b_re
    nblk = g // GROUPS_PER_STEP
    eye = jnp.eye(GROUPS_PER_STEP, dtype=F32)

    def bdiag(bb):
        return jnp.einsum('bgnc,gh->bgchn', bb.reshape(nblk, GROUPS_PER_STEP, n, SSM_GROUP), eye).reshape(
            nblk, GROUPS_PER_STEP * SSM_GROUP, STATE_W)

    def cdiag(cc):
        return jnp.einsum('bgcn,gh->bgnhc', cc.reshape(nblk, GROUPS_PER_STEP, SSM_GROUP, n), eye).reshape(
            nblk, STATE_W, GROUPS_PER_STEP * SSM_GROUP)

    return (ab_re.reshape(nblk, 1, STATE_W), ab_im.reshape(nblk, 1, STATE_W),
            bdiag(bb_re), bdiag(bb_im), cdiag(c_re), cdiag(c_im))


def _powers(a_re, a_im, levels):
    rs, ims = [a_re], [a_im]
    for _ in range(levels - 1):
        r, i = rs[-1], ims[-1]
        rs.append(r * r - i * i)
        ims.append(2.0 * r * i)
    return jnp.concatenate(rs, axis=1), jnp.concatenate(ims, axis=1)


def _power_table(a_re, a_im, n, descending=False):
    tr, ti = a_re, a_im
    while tr.shape[1] < n:
        top = 0 if descending else -1
        lr, li = tr[:, top:][:, :1], ti[:, top:][:, :1]
        hr, hi = tr * lr - ti * li, tr * li + ti * lr
        tr, ti = ((jnp.concatenate([hr, tr], axis=1), jnp.concatenate([hi, ti], axis=1)) if descending
                  else (jnp.concatenate([tr, hr], axis=1), jnp.concatenate([ti, hi], axis=1)))
    return tr, ti


def _ssm_fwd(proj, pw_re, pw_im, tab_re, tab_im, bd_re, bd_im, cd_re, cd_im, dskip):
    s = proj.shape[0]
    w = dskip.shape[1]
    nblk = w // LANES
    nch = STATE_W // LANES
    levels = pw_re.shape[1]

    def body(u_ref, pr, pi, tr, ti, br, bi, cr, ci, d_ref, pre_ref, yg_ref, xr_ref, xi_ref):
        ch = pl.program_id(1)
        u = u_ref[...]
        xr, xi = _scan_complex(_dot(u, br[0]), _dot(u, bi[0]), pr[0], pi[0], tr[0], ti[0], False)
        xr_ref[...] = xr.astype(BF16)
        xi_ref[...] = xi.astype(BF16)
        yc = _dot(xr, cr[0]) - _dot(xi, ci[0])

        @pl.when(ch == 0)
        def _():
            pre_ref[...] = yc + d_ref[...] * u

        @pl.when(ch > 0)
        def _():
            pre_ref[...] += yc

        @pl.when(ch == nch - 1)
        def _():
            yg_ref[...] = _gelu(pre_ref[...]).astype(BF16)

    blk = lambda b, c: (0, b)
    col = lambda b, c: (0, b * nch + c)
    in_chunk = pl.BlockSpec((1, LANES, LANES), lambda b, c: (b, 0, c))
    out_chunk = pl.BlockSpec((1, LANES, LANES), lambda b, c: (b, c, 0))
    pw_spec = pl.BlockSpec((1, levels, LANES), lambda b, c: (b, 0, c))
    tab_spec = pl.BlockSpec((1, tab_re.shape[1], LANES), lambda b, c: (b, 0, c))
    return pl.pallas_call(
        body, name="ssm_fwd", grid=(nblk, nch),
        in_specs=[pl.BlockSpec((s, LANES), blk), pw_spec, pw_spec, tab_spec, tab_spec, in_chunk, in_chunk,
                  out_chunk, out_chunk, pl.BlockSpec((1, LANES), blk)],
        out_specs=[pl.BlockSpec((s, LANES), blk), pl.BlockSpec((s, LANES), blk),
                   pl.BlockSpec((s, LANES), col), pl.BlockSpec((s, LANES), col)],
        out_shape=[jax.ShapeDtypeStruct((s, w), F32), jax.ShapeDtypeStruct((s, w), BF16),
                   jax.ShapeDtypeStruct((s, nblk * STATE_W), BF16), jax.ShapeDtypeStruct((s, nblk * STATE_W), BF16)],
        compiler_params=_cp(("parallel", "arbitrary")),
    )(proj, pw_re, pw_im, tab_re, tab_im, bd_re, bd_im, cd_re, cd_im, dskip)


def _ssm_bwd(proj, dyg, pre, xr_all, xi_all, pw_re, pw_im, tab_re, tab_im, bd_re, bd_im, cd_re, cd_im, dskip):
    s = proj.shape[0]
    w = dskip.shape[1]
    nblk = w // LANES
    nch = STATE_W // LANES
    levels = pw_re.shape[1]

    def body(u_ref, dy_ref, pre_ref, xr_ref, xi_ref, pr, pi, tr, ti, br, bi, cr, ci, d_ref,
             du_ref, dd_ref, dar_ref, dai_ref, dbr_ref, dbi_ref, dcr_ref, dci_ref):
        ch = pl.program_id(1)
        u = u_ref[...]
        _, vjp = jax.vjp(_gelu, pre_ref[...])
        dpre = vjp(dy_ref[...].astype(F32))[0]
        zr, zi = _scan_complex(_dot(dpre, cr[0], 1, 1), -_dot(dpre, ci[0], 1, 1), pr[0], -pi[0], tr[0], -ti[0], True)
        xpr = _shift_rows(xr_ref[...].astype(F32), 1)
        xpi = _shift_rows(xi_ref[...].astype(F32), 1)
        dar_ref[0] = jnp.sum(zr * xpr + zi * xpi, axis=0, keepdims=True)
        dai_ref[0] = jnp.sum(zi * xpr - zr * xpi, axis=0, keepdims=True)
        dcr_ref[0] = _dot(xr_ref[...], dpre, 0, 0)
        dci_ref[0] = -_dot(xi_ref[...], dpre, 0, 0)
        dbr_ref[0] = _dot(u, zr, 0, 0)
        dbi_ref[0] = _dot(u, zi, 0, 0)
        duc = _dot(zr, br[0], 1, 1) + _dot(zi, bi[0], 1, 1)

        @pl.when(ch == 0)
        def _():
            du_ref[...] = duc + dpre * d_ref[...]
            dd_ref[...] = jnp.sum(dpre * u, axis=0, keepdims=True)

        @pl.when(ch > 0)
        def _():
            du_ref[...] += duc

    blk = lambda b, c: (0, b)
    col = lambda b, c: (0, b * nch + c)
    in_chunk = pl.BlockSpec((1, LANES, LANES), lambda b, c: (b, 0, c))
    out_chunk = pl.BlockSpec((1, LANES, LANES), lambda b, c: (b, c, 0))
    pw_spec = pl.BlockSpec((1, levels, LANES), lambda b, c: (b, 0, c))
    a_spec = pl.BlockSpec((1, 1, LANES), lambda b, c: (b, 0, c))
    tab_spec = pl.BlockSpec((1, tab_re.shape[1], LANES), lambda b, c: (b, 0, c))
    return pl.pallas_call(
        body, name="ssm_bwd", grid=(nblk, nch),
        in_specs=[pl.BlockSpec((s, LANES), blk), pl.BlockSpec((s, LANES), blk), pl.BlockSpec((s, LANES), blk),
                  pl.BlockSpec((s, LANES), col), pl.BlockSpec((s, LANES), col),
                  pw_spec, pw_spec, tab_spec, tab_spec, in_chunk, in_chunk, out_chunk, out_chunk,
                  pl.BlockSpec((1, LANES), blk)],
        out_specs=[pl.BlockSpec((s, LANES), blk), pl.BlockSpec((1, LANES), blk), a_spec, a_spec,
                   in_chunk, in_chunk, out_chunk, out_chunk],
        out_shape=[jax.ShapeDtypeStruct((s, w), F32), jax.ShapeDtypeStruct((1, w), F32),
                   jax.ShapeDtypeStruct((nblk, 1, STATE_W), F32), jax.ShapeDtypeStruct((nblk, 1, STATE_W), F32),
                   jax.ShapeDtypeStruct((nblk, LANES, STATE_W), F32), jax.ShapeDtypeStruct((nblk, LANES, STATE_W), F32),
                   jax.ShapeDtypeStruct((nblk, STATE_W, LANES), F32), jax.ShapeDtypeStruct((nblk, STATE_W, LANES), F32)],
        compiler_params=_cp(("parallel", "arbitrary")),
    )(proj, dyg, pre, xr_all, xi_all, pw_re, pw_im, tab_re, tab_im, bd_re, bd_im, cd_re, cd_im, dskip)


ROW_TILE = 256


def _norm_fwd(x, g, name):
    s, d = x.shape
    t = min(ROW_TILE, s)
    return _tiled(lambda xv, gv: (_rms(xv, gv),), [_rows(x, t), _full(g)], [_orows(s, d, BF16, t)], s // t, name)[0]


def _norm_bwd(x, g, dh_list, dx_in, name):
    s, d = x.shape
    t = min(ROW_TILE, s)
    nh = len(dh_list)

    def fn(xv, gv, dxv, *dhs):
        dh = dhs[0].astype(F32)
        for other in dhs[1:]:
            dh = dh + other.astype(F32)
        _, vjp = jax.vjp(_rms, xv, gv)
        dx, dg = vjp(dh)
        dx = dx + dxv
        return dx, dx, dg

    return _tiled(fn, [_rows(x, t), _full(g), _rows(dx_in, t)] + [_rows(a, t) for a in dh_list],
                  [_orows(s, d, F32, t), _orows(s, d, BF16, t), _oacc((1, d))], s // t, name)


def _ple_fwd(x, y, pn, name):
    s, d = x.shape
    t = min(ROW_TILE, s)

    def fn(xv, yv, gv):
        x1 = xv + yv
        return x1, _rms(x1, gv)

    return _tiled(fn, [_rows(x, t), _rows(y, t), _full(pn)], [_orows(s, d, F32, t), _orows(s, d, BF16, t)], s // t, name)


def _ple_mix(x1, emb, gl, name):
    s, d = x1.shape
    t = min(ROW_TILE, s)
    return _tiled(lambda a, e, g: (a + e * jax.nn.sigmoid(g),), [_rows(x1, t), _rows(emb, t), _rows(gl, t)],
                  [_orows(s, d, F32, t)], s // t, name)[0]


def _ple_mix_bwd(dx2, emb, gl, name):
    s, d = dx2.shape
    t = min(ROW_TILE, s)

    def fn(dx, e, g):
        sg = jax.nn.sigmoid(g)
        return dx * sg, dx * e * sg * (1.0 - sg)

    return _tiled(fn, [_rows(dx2, t), _rows(emb, t), _rows(gl, t)],
                  [_orows(s, d, BF16, t), _orows(s, d, BF16, t)], s // t, name)


def _loss_grad(x, target, g):
    s, d = x.shape
    t = min(ROW_TILE, s)

    def fn(xv, tv, gv):
        def f(xx, gg):
            err = _rms(xx, gg) - tv
            return 0.5 * jnp.sum(jnp.mean(err * err, axis=-1, keepdims=True), axis=0, keepdims=True)

        loss, vjp = jax.vjp(f, xv, gv)
        dx, dg = vjp(jnp.ones((1, 1), F32))
        return loss, dx, dg

    return _tiled(fn, [_rows(x, t), _rows(target, t), _full(g)],
                  [_oacc((1, 1)), _orows(s, d, F32, t), _oacc((1, d))], s // t, "loss_grad")


def _conv_fwd(proj, kern):
    s = proj.shape[0]
    w = kern.shape[1]
    nb = w // LANES

    def fn(bg, cg, u, gate, k):
        z = cg.astype(F32) * u.astype(F32)
        conv = k[2:3] * z + k[1:2] * _shift_rows(z, 1) + k[0:1] * _shift_rows(z, 2)
        return (bg.astype(F32) * conv * _silu(gate.astype(F32)),)

    return _tiled(fn, [_cols(proj, LANES, q * nb) for q in range(4)] + [_cols(kern, LANES)],
                  [_ocols(s, w, BF16, LANES)], nb, "conv_fwd")[0]


def _conv_bwd(proj, kern, da):
    s = proj.shape[0]
    w = kern.shape[1]
    nb = w // LANES

    def fn(bg, cg, u, gate, k, dav):
        bg, cg, u, gate, dav = (a.astype(F32) for a in (bg, cg, u, gate, dav))
        z = cg * u
        z1, z2 = _shift_rows(z, 1), _shift_rows(z, 2)
        conv = k[2:3] * z + k[1:2] * z1 + k[0:1] * z2
        sg = jax.nn.sigmoid(gate)
        dy = dav * gate * sg
        dgate = dav * bg * conv * sg * (1.0 + gate * (1.0 - sg))
        dconv = dy * bg
        dk = jnp.concatenate([jnp.sum(dconv * zz, axis=0, keepdims=True) for zz in (z2, z1, z)], axis=0)
        dz = k[2:3] * dconv + k[1:2] * _shift_rows(dconv, 1, True) + k[0:1] * _shift_rows(dconv, 2, True)
        return dy * conv, dz * u, dz * cg, dgate, dk

    return _tiled(fn, [_cols(proj, LANES, q * nb) for q in range(4)] + [_cols(kern, LANES), _cols(da, LANES)],
                  [_ocols(s, w, BF16, LANES)] * 4 + [_ocols(3, w, F32, LANES)], nb, "conv_bwd")


def _glu_fwd(gl, proj, bglu):
    s, w2 = gl.shape
    w = w2 // 2
    t = min(ROW_TILE, s)

    def fn(glv, gate, b):
        v = glv + b
        return (v[:, :w] * jax.nn.sigmoid(v[:, w:]) * _silu(gate),)

    return _tiled(fn, [_rows(gl, t), (proj, (t, w), lambda i: (i, 1)), _full(bglu)],
                  [_orows(s, w, BF16, t)], s // t, "glu_fwd")[0]


def _glu_bwd(gl, proj, bglu, da):
    s, w2 = gl.shape
    w = w2 // 2
    t = min(ROW_TILE, s)

    def fn(glv, gate, b, dav):
        def f(gg, gt, bb):
            v = gg + bb
            return v[:, :w] * jax.nn.sigmoid(v[:, w:]) * _silu(gt)

        _, vjp = jax.vjp(f, glv, gate, b)
        return vjp(dav.astype(F32))

    return _tiled(fn, [_rows(gl, t), (proj, (t, w), lambda i: (i, 1)), _full(bglu), _rows(da, t)],
                  [_orows(s, w2, BF16, t), _orows(s, w, BF16, t), _oacc((1, w2))], s // t, "glu_bwd")


def _fg_fwd(z, b):
    def fn(zv, bv):
        v = zv + bv
        logf = jnp.minimum(v, 0.0) - jnp.log(1.0 + jnp.exp(-jnp.abs(v)))
        return (_cumsum_rows(logf, False),)

    return _tiled(fn, [_full(z), _full(b)], [_out(z.shape, F32, z.shape, lambda i: (0, 0))], 1, "fg_fwd")[0]


def _fg_bwd(z, b, dcs):
    def fn(zv, bv, dc):
        dz = _cumsum_rows(dc, True) * jax.nn.sigmoid(-(zv + bv))
        return dz, jnp.sum(dz, axis=0, keepdims=True)

    return _tiled(fn, [_full(z), _full(b), _full(dcs)],
                  [_out(z.shape, BF16, z.shape, lambda i: (0, 0)), _out((1, z.shape[1]), F32, (1, z.shape[1]), lambda i: (0, 0))],
                  1, "fg_bwd")


def _fox_gate_bwd(da, o, proj):
    s, w = o.shape
    t = min(ROW_TILE, s)

    def fn(dav, ov, gate):
        dav, ov, gate = dav.astype(F32), ov.astype(F32), gate.astype(F32)
        sg = jax.nn.sigmoid(gate)
        return dav * gate * sg, dav * ov * sg * (1.0 + gate * (1.0 - sg))

    return _tiled(fn, [_rows(da, t), _rows(o, t), (proj, (t, w), lambda i: (i, 3))],
                  [_orows(s, w, BF16, t), _orows(s, w, BF16, t)], s // t, "fox_gate_bwd")


def _fox_gate_fwd(o, proj):
    s, w = o.shape
    t = min(ROW_TILE, s)
    return _tiled(lambda ov, gate: (ov.astype(F32) * _silu(gate.astype(F32)),),
                  [_rows(o, t), (proj, (t, w), lambda i: (i, 3))], [_orows(s, w, BF16, t)], s // t, "fox_gate_fwd")[0]


def _local_step(x, p, target, w):
    s, d = x.shape
    depth = p.shape[0]
    pb = p.astype(BF16)
    grads = {}
    saved = []

    bucket = _swa_bucket_table()
    onehot = np.eye(REL_BUCKETS, dtype=np.float32)[bucket.reshape(-1)]
    bias = _bias_table(w['rel_bias'], jnp.asarray(onehot.T, BF16))
    sinks = jnp.broadcast_to(w['swa_sinks'].reshape(SWA_KV_HEADS, SWA_GROUP, 1), (SWA_KV_HEADS, SWA_GROUP, LANES))
    ssm_params = tuple(w[k][0] for k in ('ssm_lam_re', 'ssm_lam_im', 'ssm_log_dt', 'ssm_b_re', 'ssm_b_im', 'ssm_c_re', 'ssm_c_im'))
    (a_re, a_im, bd_re, bd_im, cd_re, cd_im), ssm_vjp = jax.vjp(_ssm_prep, *ssm_params)
    levels = max(1, (s - 1).bit_length())
    pw_re, pw_im = _powers(a_re, a_im, levels)
    tab_re, tab_im = _power_table(a_re, a_im, min(SCAN_CHUNK, s))
    rtab_re, rtab_im = _power_table(a_re, a_im, min(SCAN_CHUNK, s), descending=True)
    wfg = jnp.pad(w['fox_w_fg'][0], ((0, 0), (0, LANES - FOX_HEADS)))
    bfg = jnp.pad(w['fox_b_fg'], ((0, 0), (0, LANES - FOX_HEADS)))

    def qkv4(a):
        return a.reshape(s, SWA_KV_HEADS, HEAD_DIM).transpose(1, 0, 2)

    def unheads(a):
        return a.transpose(2, 0, 1).reshape(s, FOX_HEADS * HEAD_DIM)

    for i in range(depth):
        mixer = i % 4
        hn = _norm_fwd(x, w['norm_g'][i:i + 1], f"norm_fwd{i}")
        sv = {'x': x, 'hn': hn}
        if mixer == 0:
            proj = _mm(hn, w['swa_w_in'][:, 0], b_split=True, name="swa_in")
            qw = SWA_HEADS * HEAD_DIM
            kvw = SWA_KV_HEADS * HEAD_DIM
            k4, v4 = qkv4(proj[:, qw:qw + kvw]), qkv4(proj[:, qw + kvw:qw + 2 * kvw])
            a = _swa_fwd(proj, k4, v4, bias, sinks)
            sv.update(proj=proj, k4=k4, v4=v4)
            w_out = w['swa_w_out'][0]
        elif mixer == 1:
            proj = _mm(hn, w['conv_w_in'][:, 0], b_split=True, name="conv_in")
            a = _conv_fwd(proj, w['conv_kernel'][0])
            sv.update(proj=proj)
            w_out = w['conv_w_out'][0]
        elif mixer == 2:
            proj = _mm(hn, w['ssm_w_in'][:, 0], b_split=True, out_dtype=F32, name="ssm_in")
            pre, yg, xr_all, xi_all = _ssm_fwd(proj, pw_re, pw_im, tab_re, tab_im, bd_re, bd_im, cd_re, cd_im, w['ssm_d'])
            gl = _mm(yg, w['ssm_w_glu'][:, 0], b_split=True, out_dtype=F32, name="ssm_glu")
            a = _glu_fwd(gl, proj, w['ssm_b_glu'])
            sv.update(proj=proj, pre=pre, yg=yg, xr=xr_all, xi=xi_all, gl=gl)
            w_out = w['ssm_w_out'][0]
        else:
            proj = _mm(hn, w['fox_w_in'][:, 0], b_split=True, name="fox_in")
            z = _mm(hn, wfg, out_dtype=F32, name="fox_fg")
            fox_ops = _fox_operands(proj, _fg_fwd(z, bfg))
            ot, lse = _fox_attention(fox_ops[0], fox_ops[1], fox_ops[5], fox_ops[6], fox_ops[7])
            o = unheads(ot)
            a = _fox_gate_fwd(o, proj)
            sv.update(proj=proj, z=z, fox_ops=fox_ops, ot=ot, o=o, lse=lse)
            w_out = w['fox_w_out'][0]
        y = _mm(a, w_out, out_dtype=F32, name=f"mixer_out{i}")
        x1, gn = _ple_fwd(x, y, w['ple_norm'][i:i + 1], f"ple_fwd{i}")
        emb = _mm(pb[i], w['ple_proj'][:, i], b_split=True, out_dtype=F32, name=f"ple_emb{i}")
        gl2 = _mm(gn, w['ple_gate'][i], out_dtype=F32, name=f"ple_gate{i}")
        x = _ple_mix(x1, emb, gl2, f"ple_mix{i}")
        sv.update(a=a, x1=x1, gn=gn, emb=emb, gl2=gl2)
        saved.append(sv)

    loss, dx, grads['final_g'] = _loss_grad(x, target, w['final_g'].reshape(1, d))
    grads['final_g'] = grads['final_g'].reshape(d)

    g_norm, g_ple_norm, g_ple_proj, g_ple_gate = [None] * depth, [None] * depth, [None] * depth, [None] * depth
    for i in reversed(range(depth)):
        sv = saved[i]
        mixer = i % 4
        demb, dgl2 = _ple_mix_bwd(dx, sv['emb'], sv['gl2'], f"ple_mix_bwd{i}")
        g_ple_proj[i] = _mm(pb[i], demb, ta=True, out_split=N_CHIPS, name=f"ple_emb_dw{i}")
        g_ple_gate[i] = _mm(sv['gn'], dgl2, ta=True, name=f"ple_gate_dw{i}")
        dgn = _mm(dgl2, w['ple_gate'][i], tb=True, out_dtype=F32, name=f"ple_gate_dx{i}")
        dx1, dy, g_ple_norm[i] = _norm_bwd(sv['x1'], w['ple_norm'][i:i + 1], [dgn], dx, f"ple_norm_bwd{i}")
        w_out_name = ('swa_w_out', 'conv_w_out', 'ssm_w_out', 'fox_w_out')[mixer]
        grads[w_out_name] = _mm(sv['a'], dy, ta=True, name=f"mixer_out_dw{i}")[None]
        da = _mm(dy, w[w_out_name][0], tb=True, name=f"mixer_out_dx{i}")
        proj = sv['proj']
        dhs = []
        if mixer == 0:
            dq, dgate, dk4, dv4, dbias, dsink = _swa_bwd(proj, sv['k4'], sv['v4'], bias, sinks, da)
            back = lambda t4: t4.transpose(1, 0, 2).reshape(s, SWA_KV_HEADS * HEAD_DIM).astype(BF16)
            dproj = jnp.concatenate([dq, back(dk4), back(dv4), dgate], axis=1)
            grads['rel_bias'] = _bias_grad(dbias.reshape(SWA_HEADS, -1), jnp.asarray(onehot, BF16))
            grads['swa_sinks'] = dsink[:, :, 0].reshape(1, SWA_HEADS)
            w_in_name = 'swa_w_in'
        elif mixer == 1:
            dbg, dcg, du, dgate, dkern = _conv_bwd(proj, w['conv_kernel'][0], da)
            dproj = jnp.concatenate([dbg, dcg, du, dgate], axis=1)
            grads['conv_kernel'] = dkern[None]
            w_in_name = 'conv_w_in'
        elif mixer == 2:
            dgl, dgate, dbglu = _glu_bwd(sv['gl'], proj, w['ssm_b_glu'], da)
            grads['ssm_b_glu'] = dbglu
            grads['ssm_w_glu'] = _mm(sv['yg'], dgl, ta=True, out_split=N_CHIPS, name="ssm_glu_dw")[:, None]
            dyg = _mm(dgl, w['ssm_w_glu'][:, 0], tb=True, b_split=True, out_dtype=F32, name="ssm_glu_dx")
            du, dd, da_re, da_im, dbd_re, dbd_im, dcd_re, dcd_im = _ssm_bwd(
                proj, dyg, sv['pre'], sv['xr'], sv['xi'], pw_re, pw_im, rtab_re, rtab_im, bd_re, bd_im, cd_re, cd_im, w['ssm_d'])
            grads['ssm_d'] = dd
            dparams = ssm_vjp((da_re, da_im, dbd_re, dbd_im, dcd_re, dcd_im))
            for k, v in zip(('ssm_lam_re', 'ssm_lam_im', 'ssm_log_dt', 'ssm_b_re', 'ssm_b_im', 'ssm_c_re', 'ssm_c_im'), dparams):
                grads[k] = v[None]
            dproj = jnp.concatenate([du.astype(BF16), dgate], axis=1)
            w_in_name = 'ssm_w_in'
        else:
            do, dgate = _fox_gate_bwd(da, sv['o'], proj)
            qa, ka, qat, kat, v, _, c_row, c_lanes = sv['fox_ops']
            dot = do.reshape(s, FOX_HEADS, HEAD_DIM).transpose(1, 2, 0)
            dqt, dcq, dkt, dck, dvt = _fox_attention_bwd(qa, ka, qat, kat, v, c_row, c_lanes, sv['ot'], dot, sv['lse'])
            dq, dk, dv = unheads(dqt), unheads(dkt), unheads(dvt)
            dcs = jnp.pad((dcq[:, 0] + dck[:, 0]).T, ((0, 0), (0, LANES - FOX_HEADS)))
            dz, dbfg = _fg_bwd(sv['z'], bfg, dcs)
            grads['fox_b_fg'] = dbfg[:, :FOX_HEADS]
            grads['fox_w_fg'] = _mm(sv['hn'], dz, ta=True, out_dtype=BF16, name="fox_fg_dw")[:, :FOX_HEADS][None]
            dhs.append(_mm(dz, wfg, tb=True, out_dtype=F32, name="fox_fg_dx"))
            dproj = jnp.concatenate([dq, dk, dv, dgate], axis=1)
            w_in_name = 'fox_w_in'
        grads[w_in_name] = _mm(sv['hn'], dproj, ta=True, out_split=N_CHIPS, name=f"mixer_in_dw{i}")[:, None]
        dhs.append(_mm(dproj, w[w_in_name][:, 0], tb=True, b_split=True, out_dtype=F32, name=f"mixer_in_dx{i}"))
        dx, _, g_norm[i] = _norm_bwd(sv['x'], w['norm_g'][i:i + 1], dhs, dx1, f"norm_bwd{i}")

    grads['norm_g'] = jnp.concatenate(g_norm, axis=0)
    grads['ple_norm'] = jnp.concatenate(g_ple_norm, axis=0)
    grads['ple_proj'] = jnp.stack(g_ple_proj, axis=1)
    grads['ple_gate'] = jnp.stack(g_ple_gate)
    return loss, dx, grads


def _bf16_terms(a):
    hi = lax.reduce_precision(a, 8, 7)
    r1 = a - hi
    mid = lax.reduce_precision(r1, 8, 7)
    lo = lax.reduce_precision(r1 - mid, 8, 7)
    return hi.astype(BF16), mid.astype(BF16), lo.astype(BF16)


def _split3(a):
    return jnp.concatenate(_bf16_terms(a), axis=0)


def _bias_grad(dbias, onehot):
    out = _mm(_split3(dbias), onehot, out_dtype=F32, name="rel_bias_grad", tk=2048)
    nh = dbias.shape[0]
    return (out[:nh] + out[nh:2 * nh] + out[2 * nh:]).T


def _bias_table(rel_bias, onehot_t):
    nh = rel_bias.shape[1]
    out = _mm(_split3(rel_bias.T), onehot_t, out_dtype=F32, name="rel_bias_table")
    return (out[:nh] + out[nh:2 * nh] + out[2 * nh:]).reshape(nh, BLOCK, 2 * BLOCK)


WEIGHTS = ['norm_g', 'final_g', 'rel_bias', 'swa_w_in', 'swa_w_out', 'swa_sinks', 'conv_w_in', 'conv_kernel', 'conv_w_out',
           'ssm_w_in', 'ssm_lam_re', 'ssm_lam_im', 'ssm_log_dt', 'ssm_b_re', 'ssm_b_im', 'ssm_c_re', 'ssm_c_im', 'ssm_d',
           'ssm_w_glu', 'ssm_b_glu', 'ssm_w_out', 'fox_w_in', 'fox_w_fg', 'fox_b_fg', 'fox_w_out', 'ple_proj', 'ple_norm',
           'ple_gate']
BIG = {'swa_w_in': 2, 'swa_w_out': 1, 'conv_w_in': 2, 'conv_w_out': 1, 'ssm_w_in': 2, 'ssm_w_glu': 2, 'ssm_w_out': 1,
       'fox_w_in': 2, 'fox_w_fg': 1, 'fox_w_out': 1, 'ple_proj': 2, 'ple_gate': 1}
SMALL = {'conv_kernel': 2, 'ssm_d': 1, 'ssm_b_glu': 1}
REPLICATED = [n for n in WEIGHTS if n not in BIG and n not in SMALL]
N_CHIPS = 4
N_DEV = 8
BIG_ROWS = 256
SMALL_ROWS = 8
REPL_ROWS = 64


def _flat(pieces, dtype, lead, row_mult):
    flat = jnp.concatenate([q.astype(dtype) for q in pieces], axis=-1)
    pad = (-flat.shape[-1]) % (row_mult * FLAT_W)
    flat = jnp.pad(flat, [(0, 0)] * len(lead) + [(0, pad)])
    return flat.reshape(*lead, -1, FLAT_W)


def _unflat(flat, lead_ndim, sizes):
    lead = flat.shape[:lead_ndim]
    flat = flat.reshape(*lead, -1)
    out, off = [], 0
    for n in sizes:
        out.append(flat[..., off:off + n])
        off += n
    return out


def _split_shards(full, axis):
    shp = full.shape
    parts = full.reshape(shp[:axis] + (N_CHIPS, shp[axis] // N_CHIPS) + shp[axis + 1:])
    return jnp.moveaxis(parts, axis, 0)


def _join_shards(parts, axis):
    moved = jnp.moveaxis(parts, 0, axis)
    shp = moved.shape
    return moved.reshape(shp[:axis] + (shp[axis] * shp[axis + 1],) + shp[axis + 2:])


def _coords():
    return lax.axis_index("x"), lax.axis_index("y"), lax.axis_index("c")


def _remote(k, src, dst, to, send_sems, recv_sems):
    return pltpu.make_async_remote_copy(src_ref=src, dst_ref=dst, send_sem=send_sems.at[k], recv_sem=recv_sems.at[k],
                                        device_id=to, device_id_type=MESH)


def _gather_weights(bufs, ssh):
    nb = len(bufs)

    def body(*refs):
        w_refs, s_ref = refs[:nb], refs[nb]
        wouts, sout = refs[nb + 1:2 * nb + 1], refs[2 * nb + 1]
        send_sems, recv_sems = refs[2 * nb + 2:]
        x, y, c = _coords()
        me = 2 * x + y
        chips = [(1 - x, y), (x, 1 - y), (1 - x, 1 - y)]
        rc = functools.partial(_remote, send_sems=send_sems, recv_sems=recv_sems)
        sends = []
        for j, (cx, cy) in enumerate(chips):
            for b in range(nb):
                sends.append(rc(6 * b + j, w_refs[b].at[c], wouts[b].at[me, c], (cx, cy, c)))
            sends.append(rc(6 * nb + j, s_ref, sout.at[me], (cx, cy, c)))
        for cp in sends:
            cp.start()
        for j, (cx, cy) in enumerate(chips):
            k = 2 * cx + cy
            for b in range(nb):
                rc(6 * b + j, w_refs[b].at[c], wouts[b].at[k, c], (x, y, c)).wait_recv()
                fwd = rc(6 * b + 3 + j, wouts[b].at[k, c], wouts[b].at[k, c], (x, y, 1 - c))
                fwd.start()
                sends.append(fwd)
        for j, (cx, cy) in enumerate(chips):
            k = 2 * cx + cy
            for b in range(nb):
                rc(6 * b + 3 + j, w_refs[b].at[c], wouts[b].at[k, 1 - c], (x, y, c)).wait_recv()
            rc(6 * nb + j, s_ref, sout.at[k], (x, y, c)).wait_recv()
        for cp in sends:
            cp.wait_send()

    nsem = 6 * nb + 3
    res = pl.pallas_call(
        body, name="gather_weights", in_specs=[ANY] * (nb + 1), out_specs=[ANY] * (nb + 1),
        out_shape=[jax.ShapeDtypeStruct((N_CHIPS,) + a.shape, a.dtype) for a in (*bufs, ssh)],
        scratch_shapes=[pltpu.SemaphoreType.DMA((nsem,)), pltpu.SemaphoreType.DMA((nsem,))],
    )(*bufs, ssh)
    return res[:nb], res[nb]


def _pair_exchange(gbufs):
    nb = len(gbufs)

    def body(*refs):
        g_refs, outs = refs[:nb], refs[nb:2 * nb]
        send_sems, recv_sems = refs[2 * nb:]
        x, y, c = _coords()
        rc = functools.partial(_remote, send_sems=send_sems, recv_sems=recv_sems)
        sends = [rc(N_CHIPS * b + j, g_refs[b].at[2 * j + 1 - c], outs[b].at[j], (x, y, 1 - c))
                 for b in range(nb) for j in range(N_CHIPS)]
        for cp in sends:
            cp.start()
        for b in range(nb):
            for j in range(N_CHIPS):
                rc(N_CHIPS * b + j, g_refs[b].at[2 * j + c], outs[b].at[j], (x, y, c)).wait_recv()
        for cp in sends:
            cp.wait_send()

    nsem = N_CHIPS * nb
    return pl.pallas_call(
        body, name="pair_exchange", in_specs=[ANY] * nb, out_specs=[ANY] * nb,
        out_shape=[jax.ShapeDtypeStruct((N_CHIPS,) + g.shape[1:], g.dtype) for g in gbufs],
        scratch_shapes=[pltpu.SemaphoreType.DMA((nsem,)), pltpu.SemaphoreType.DMA((nsem,))],
    )(*gbufs)


def _pair_sum(mine, theirs, name):
    n, r, w = mine.shape
    t = _tile(r, 256, 16)
    spec = lambda a: (a, (n, t, w), lambda i: (0, i, 0))
    return _tiled(lambda a, b: (a.astype(F32) + b.astype(F32),), [spec(mine), spec(theirs)],
                  [_out((n, r, w), BF16, (n, t, w), lambda i: (0, i, 0))], r // t, name)[0]


def _exchange_grads(pbufs, gsmall, grepl):
    nb = len(pbufs)

    def body(*refs):
        g_refs, s_ref, r_ref = refs[:nb], refs[nb], refs[nb + 1]
        ogs, osm, orp = refs[nb + 2:2 * nb + 2], refs[2 * nb + 2], refs[2 * nb + 3]
        send_sems, recv_sems, local_sems = refs[2 * nb + 4:]
        x, y, c = _coords()
        me = 4 * x + 2 * y + c
        my_chip = 2 * x + y
        rc = functools.partial(_remote, send_sems=send_sems, recv_sems=recv_sems)
        local = [pltpu.make_async_copy(s_ref.at[me], osm.at[me], local_sems.at[0]),
                 pltpu.make_async_copy(r_ref, orp.at[me], local_sems.at[1])]
        for cp in local:
            cp.start()
        peers = []
        for d in range(1, N_DEV):
            px = 1 - x if d & 4 else x
            py = 1 - y if d & 2 else y
            pc = 1 - c if d & 1 else c
            peers.append((d, 4 * px + 2 * py + pc, 2 * px + py, (px, py, pc)))
        per_peer = nb + 2
        sends = []
        for i, (d, peer, chip, to) in enumerate(peers):
            sends.append(rc(per_peer * i + nb, s_ref.at[peer], osm.at[me], to))
            sends.append(rc(per_peer * i + nb + 1, r_ref, orp.at[me], to))
            if d & 1 == 0:
                for b in range(nb):
                    sends.append(rc(per_peer * i + b, g_refs[b].at[chip], ogs[b].at[my_chip], to))
        for cp in sends:
            cp.start()
        for i, (d, peer, chip, to) in enumerate(peers):
            rc(per_peer * i + nb, s_ref.at[peer], osm.at[peer], to).wait_recv()
            rc(per_peer * i + nb + 1, r_ref, orp.at[peer], to).wait_recv()
            if d & 1 == 0:
                for b in range(nb):
                    rc(per_peer * i + b, g_refs[b].at[chip], ogs[b].at[chip], to).wait_recv()
        for cp in sends:
            cp.wait_send()
        for cp in local:
            cp.wait()

    nsem = (nb + 2) * (N_DEV - 1)
    res = pl.pallas_call(
        body, name="exchange_grads", in_specs=[ANY] * (nb + 2), out_specs=[ANY] * (nb + 2),
        out_shape=[jax.ShapeDtypeStruct(a.shape, a.dtype) for a in (*pbufs, gsmall)]
        + [jax.ShapeDtypeStruct((N_DEV,) + grepl.shape, grepl.dtype)],
        scratch_shapes=[pltpu.SemaphoreType.DMA((nsem,)), pltpu.SemaphoreType.DMA((nsem,)), pltpu.SemaphoreType.DMA((2,))],
    )(*pbufs, gsmall, grepl)
    return res[:nb], res[nb], res[nb + 1]


def _sibling_exchange(halves):
    nb = len(halves)

    def body(*refs):
        ins, outs = refs[:nb], refs[nb:2 * nb]
        send_sems, recv_sems = refs[2 * nb:]
        x, y, c = _coords()
        rc = functools.partial(_remote, send_sems=send_sems, recv_sems=recv_sems)
        sends = [rc(b, ins[b], outs[b], (x, y, 1 - c)) for b in range(nb)]
        for cp in sends:
            cp.start()
        for b in range(nb):
            rc(b, ins[b], outs[b], (x, y, c)).wait_recv()
        for cp in sends:
            cp.wait_send()

    return pl.pallas_call(
        body, name="sibling_exchange", in_specs=[ANY] * nb, out_specs=[ANY] * nb,
        out_shape=[jax.ShapeDtypeStruct(a.shape, a.dtype) for a in halves],
        scratch_shapes=[pltpu.SemaphoreType.DMA((nb,)), pltpu.SemaphoreType.DMA((nb,))],
    )(*halves)


def _sum_senders(recv, name):
    n, r, w = recv.shape
    t = _tile(r, 256, 8)

    def fn(v):
        acc = v[0].astype(F32)
        for i in range(1, n):
            acc = acc + v[i].astype(F32)
        return (acc,)

    return _tiled(fn, [(recv, (n, t, w), lambda i: (0, i, 0))], [_orows(r, w, F32, t)], r // t, name)[0]


def _adamw(w, g, m, v, name):
    r, wd = w.shape
    t = _tile(r, 256, 8)

    def fn(wv, gv, mv, vv):
        m2 = ADAM_B1 * mv + (1.0 - ADAM_B1) * gv
        v2 = ADAM_B2 * vv + (1.0 - ADAM_B2) * (gv * gv)
        m_hat = m2 / (1.0 - ADAM_B1 ** ADAM_STEP)
        v_hat = v2 / (1.0 - ADAM_B2 ** ADAM_STEP)
        delta = -ADAM_LR * (m_hat / (jnp.sqrt(v_hat) + ADAM_EPS) + ADAM_WD * wv)
        return delta, m2, v2

    return _tiled(fn, [_rows(a, t) for a in (w, g, m, v)], [_orows(r, wd, F32, t)] * 3, r // t, name)


def kernel(x, p, norm_g, final_g, rel_bias, swa_w_in, swa_w_out, swa_sinks, conv_w_in, conv_kernel, conv_w_out, ssm_w_in, ssm_lam_re, ssm_lam_im, ssm_log_dt, ssm_b_re, ssm_b_im, ssm_c_re, ssm_c_im, ssm_d, ssm_w_glu, ssm_b_glu, ssm_w_out, fox_w_in, fox_w_fg, fox_b_fg, fox_w_out, ple_proj, ple_norm, ple_gate, loss_target, m_norm_g, m_final_g, m_rel_bias, m_swa_w_in, m_swa_w_out, m_swa_sinks, m_conv_w_in, m_conv_kernel, m_conv_w_out, m_ssm_w_in, m_ssm_lam_re, m_ssm_lam_im, m_ssm_log_dt, m_ssm_b_re, m_ssm_b_im, m_ssm_c_re, m_ssm_c_im, m_ssm_d, m_ssm_w_glu, m_ssm_b_glu, m_ssm_w_out, m_fox_w_in, m_fox_w_fg, m_fox_b_fg, m_fox_w_out, m_ple_proj, m_ple_norm, m_ple_gate, v_norm_g, v_final_g, v_rel_bias, v_swa_w_in, v_swa_w_out, v_swa_sinks, v_conv_w_in, v_conv_kernel, v_conv_w_out, v_ssm_w_in, v_ssm_lam_re, v_ssm_lam_im, v_ssm_log_dt, v_ssm_b_re, v_ssm_b_im, v_ssm_c_re, v_ssm_c_im, v_ssm_d, v_ssm_w_glu, v_ssm_b_glu, v_ssm_w_out, v_fox_w_in, v_fox_w_fg, v_fox_b_fg, v_fox_w_out, v_ple_proj, v_ple_norm, v_ple_gate):
    given = dict(locals())
    shard_shape = {n: given[n].shape for n in WEIGHTS}
    half = {n: math.prod(shard_shape[n]) // 2 for n in WEIGHTS}

    widths = sorted({shard_shape[n][-1] for n in BIG}, reverse=True)
    classes = [[n for n in BIG if shard_shape[n][-1] == w] for w in widths]
    rows = {n: math.prod(shard_shape[n][:-1]) for n in BIG}
    core = lax.axis_index("c")
    my_chip = 2 * lax.axis_index("x") + lax.axis_index("y")

    bufs = [jnp.concatenate([given[n].astype(BF16).reshape(rows[n], w) for n in names], axis=0).reshape(2, -1, w)
            for w, names in zip(widths, classes)]
    ssh = _flat([given[n].reshape(-1) for n in SMALL], F32, (), SMALL_ROWS)
    walls, sall = _gather_weights(bufs, ssh)
    walls = [lax.dynamic_update_slice(a, b[None], (my_chip, 0, 0, 0)) for a, b in zip(walls, bufs)]
    sall = lax.dynamic_update_slice(sall, ssh[None], (my_chip, 0, 0))
    full = {n: given[n] for n in REPLICATED}
    for w, names, wall in zip(widths, classes, walls):
        wall, off = wall.reshape(N_CHIPS, -1, w), 0
        for n in names:
            stacked = wall[:, off:off + rows[n]].reshape((N_CHIPS,) + shard_shape[n])
            full[n] = stacked if BIG[n] == 2 else _join_shards(stacked, BIG[n])
            off += rows[n]
    for n, piece in zip(SMALL, _unflat(sall, 1, [2 * half[n] for n in SMALL])):
        full[n] = _join_shards(piece.reshape((N_CHIPS,) + shard_shape[n]), SMALL[n])

    loss, dx, grads = _local_step(x[0], p[:, 0], loss_target[0], full)

    by_chip = lambda n: grads[n] if BIG[n] == 2 else _split_shards(grads[n], BIG[n])
    gbufs = [jnp.concatenate([by_chip(n).reshape(N_CHIPS, rows[n], w) for n in names], axis=1).reshape(N_DEV, -1, w)
             for w, names in zip(widths, classes)]
    gsmall = _flat([_split_shards(grads[n], SMALL[n]).reshape(N_DEV, -1) for n in SMALL], F32, (N_DEV,), SMALL_ROWS)
    grepl = _flat([grads[n].reshape(-1) for n in REPLICATED], F32, (), REPL_ROWS)
    pbufs = [_pair_sum(jnp.where(core == 0, g[0::2], g[1::2]), t, f"pair_sum_w{w}")
             for w, g, t in zip(widths, gbufs, _pair_exchange(gbufs))]
    recv_bufs, recv_small, recv_repl = _exchange_grads(pbufs, gsmall, grepl)
    recv_bufs = [lax.dynamic_update_slice(r, lax.dynamic_index_in_dim(pb, my_chip, axis=0), (my_chip, 0, 0))
                 for r, pb in zip(recv_bufs, pbufs)]
    halves = [_sum_senders(r, f"sum_w{w}") for w, r in zip(widths, recv_bufs)] + [_sum_senders(recv_small, "sum_small")]
    others = _sibling_exchange(halves)
    g_repl = _sum_senders(recv_repl, "sum_repl")
    both = [(jnp.where(core == 0, a, b), jnp.where(core == 0, b, a)) for a, b in zip(halves, others)]
    sfull = jnp.stack(both[-1])

    out_g, out_d, out_m, out_v = {}, {}, {}, {}
    for w, names, (lower, upper) in zip(widths, classes, both):
        gfull, off = jnp.concatenate([lower, upper], axis=0), 0
        for n in names:
            as_rows = lambda t, w=w: t.reshape(-1, w)
            piece = gfull[off:off + rows[n]]
            off += rows[n]
            d, m2, v2 = _adamw(as_rows(given[n]), piece, as_rows(given['m_' + n]), as_rows(given['v_' + n]), f"adamw_{n}")
            out_g[n], out_d[n], out_m[n], out_v[n] = (t.reshape(shard_shape[n]) for t in (piece, d, m2, v2))

    pack_small = lambda pre: _flat([given[pre + n].reshape(2, -1) for n in SMALL], F32, (2,), SMALL_ROWS).reshape(-1, FLAT_W)
    res = _adamw(pack_small(''), sfull.reshape(-1, FLAT_W), pack_small('m_'), pack_small('v_'), "adamw_small")
    for dst, flat in zip((out_g, out_d, out_m, out_v), (sfull,) + tuple(res)):
        for n, piece in zip(SMALL, _unflat(flat.reshape(2, -1, FLAT_W), 1, [half[n] for n in SMALL])):
            dst[n] = piece.reshape(shard_shape[n])

    pack_repl = lambda pre: _flat([given[pre + n].reshape(-1) for n in REPLICATED], F32, (), REPL_ROWS)
    res = _adamw(pack_repl(''), g_repl, pack_repl('m_'), pack_repl('v_'), "adamw_repl")
    for dst, flat in zip((out_g, out_d, out_m, out_v), (g_repl,) + tuple(res)):
        for n, piece in zip(REPLICATED, _unflat(flat, 0, [2 * half[n] for n in REPLICATED])):
            dst[n] = piece.reshape(shard_shape[n])

    total = lax.psum(loss[0, 0], ("x", "y", "c"))
    return (total, dx[None], *[out_g[n] for n in WEIGHTS], *[out_d[n] for n in WEIGHTS],
            *[out_m[n] for n in WEIGHTS], *[out_v[n] for n in WEIGHTS])
```

```python
import functools
import math

import numpy as np
import jax
import jax.numpy as jnp
from jax import lax
from jax.experimental import pallas as pl
from jax.experimental.pallas import tpu as pltpu

F32 = jnp.float32
BF16 = jnp.bfloat16

EPS = 1e-6
BLOCK = 128
REL_BUCKETS = 32
REL_MAX_DIST = 128
SWA_HEADS, SWA_KV_HEADS, HEAD_DIM = 32, 4, 64
SWA_GROUP = SWA_HEADS // SWA_KV_HEADS
FOX_HEADS = 32
SSM_GROUP, SSM_STATE = 16, 64
GROUPS_PER_STEP = 8
STATE_W = GROUPS_PER_STEP * SSM_STATE
LANES = 128
NEG = -1e30

ADAM_LR, ADAM_B1, ADAM_B2, ADAM_EPS, ADAM_WD, ADAM_STEP = 0.001, 0.9, 0.999, 1e-08, 0.01, 10

VMEM_LIMIT_V7X = 56 * 1024 * 1024
FLAT_W = 1024
MESH = pl.DeviceIdType.MESH
ANY = pl.BlockSpec(memory_space=pl.ANY)


def _cp(sem):
    return pltpu.CompilerParams(dimension_semantics=sem, vmem_limit_bytes=VMEM_LIMIT_V7X)


def _tile(n, target, mult=LANES):
    if n <= target:
        return n
    t = (target // mult) * mult
    while t >= mult:
        if n % t == 0:
            return t
        t -= mult
    return n


MAX_WHOLE_TILE = 1152


def _div(a, b):
    return lax.div(a, jnp.int32(b))


def _rem(a, b):
    return lax.rem(a, jnp.int32(b))


def _mm(a, b, *, ta=False, tb=False, out_dtype=BF16, name, tm=1024, tn=1024, tk=4096, b_split=False, out_split=0):
    if ta:
        kdim, m = a.shape
    else:
        m, kdim = a.shape
    parts = b.shape[0] if b_split else 1
    b_rows, b_cols = (b.shape[1], b.shape[2] * parts) if b_split else b.shape
    n = b_rows if tb else b_cols
    assert (b_cols if tb else b_rows) == kdim
    n_range = n // (out_split or (1 if tb else parts))
    k_range = kdim // (parts if tb else 1)
    tm, tk = _tile(m, tm), _tile(k_range, tk)
    tn = n_range if n_range <= MAX_WHOLE_TILE else _tile(n_range, tn)
    nk = kdim // tk
    dn = (((0 if ta else 1,), (1 if tb else 0,)), ((), ()))

    def body(a_ref, b_ref, o_ref, *acc):
        part = lax.dot_general(a_ref[...].astype(BF16), b_ref[...].astype(BF16), dn, preferred_element_type=F32)
        if nk == 1:
            o_ref[...] = part.astype(o_ref.dtype)
            return
        acc_ref, = acc
        k = pl.program_id(2)

        @pl.when(k == 0)
        def _():
            acc_ref[...] = part

        @pl.when(k > 0)
        def _():
            acc_ref[...] += part

        @pl.when(k == nk - 1)
        def _():
            o_ref[...] = acc_ref[...].astype(o_ref.dtype)

    a_spec = pl.BlockSpec((tk, tm), lambda i, j, k: (k, i)) if ta else pl.BlockSpec((tm, tk), lambda i, j, k: (i, k))
    nj, nkr = n_range // tn, k_range // tk
    if not b_split:
        b_spec = pl.BlockSpec((tn, tk), lambda i, j, k: (j, k)) if tb else pl.BlockSpec((tk, tn), lambda i, j, k: (k, j))
    elif tb:
        b_spec = pl.BlockSpec((None, tn, tk), lambda i, j, k: (_div(k, nkr), j, _rem(k, nkr)))
    else:
        b_spec = pl.BlockSpec((None, tk, tn), lambda i, j, k: (_div(j, nj), k, _rem(j, nj)))
    if out_split:
        out_spec = pl.BlockSpec((None, tm, tn), lambda i, j, k: (_div(j, nj), i, _rem(j, nj)))
        out_shape = jax.ShapeDtypeStruct((out_split, m, n_range), out_dtype)
    else:
        out_spec = pl.BlockSpec((tm, tn), lambda i, j, k: (i, j))
        out_shape = jax.ShapeDtypeStruct((m, n), out_dtype)
    return pl.pallas_call(
        body, name=name, grid=(m // tm, n // tn, nk),
        in_specs=[a_spec, b_spec], out_specs=out_spec, out_shape=out_shape,
        scratch_shapes=[pltpu.VMEM((tm, tn), F32)] if nk > 1 else [],
        compiler_params=_cp(("parallel", "parallel", "arbitrary")),
    )(a, b)


def _rows(arr, t):
    return (arr, (t, arr.shape[1]), lambda i: (i, 0))


def _cols(arr, cb, off=0):
    return (arr, (arr.shape[0], cb), lambda i: (0, i + off))


def _full(arr):
    nd = arr.ndim
    return (arr, arr.shape, lambda i: (0,) * nd)


def _out(shape, dtype, block, imap, acc=False):
    return (jax.ShapeDtypeStruct(shape, dtype), block, imap, acc)


def _orows(s, w, dtype, t):
    return _out((s, w), dtype, (t, w), lambda i: (i, 0))


def _ocols(s, w, dtype, cb):
    return _out((s, w), dtype, (s, cb), lambda i: (0, i))


def _oacc(shape):
    nd = len(shape)
    return _out(shape, F32, shape, lambda i: (0,) * nd, True)


def _tiled(fn, ins, outs, n, name):
    has_acc = any(o[3] for o in outs)
    ni = len(ins)

    def body(*refs):
        vals = fn(*[r[...] for r in refs[:ni]])
        i = pl.program_id(0)
        for r, v, o in zip(refs[ni:], vals, outs):
            if o[3]:
                @pl.when(i == 0)
                def _(r=r, v=v):
                    r[...] = v.astype(r.dtype)

                @pl.when(i > 0)
                def _(r=r, v=v):
                    r[...] += v.astype(r.dtype)
            else:
                r[...] = v.astype(r.dtype)

    res = pl.pallas_call(
        body, name=name, grid=(n,),
        in_specs=[pl.BlockSpec(b, m) for _, b, m in ins],
        out_specs=[pl.BlockSpec(b, m) for _, b, m, _ in outs],
        out_shape=[s for s, _, _, _ in outs],
        compiler_params=_cp(("arbitrary",) if has_acc else ("parallel",)),
    )(*[a for a, _, _ in ins])
    return res


def _silu(x):
    return x * jax.nn.sigmoid(x)


def _gelu(x):
    return 0.5 * x * (1.0 + jnp.tanh(math.sqrt(2.0 / math.pi) * (x + 0.044715 * (x * x * x))))


def _rms(x, g):
    r = lax.rsqrt(jnp.mean(x * x, axis=-1, keepdims=True) + EPS)
    return x * r * g


def _shift_rows(x, sh, up=False):
    s = x.shape[0]
    rows = lax.broadcasted_iota(jnp.int32, x.shape, 0)
    if up:
        return jnp.where(rows < s - sh, pltpu.roll(x, s - sh, 0), 0.0)
    return jnp.where(rows >= sh, pltpu.roll(x, sh, 0), 0.0)


SCAN_CHUNK = 128


def _scan_complex(xr, xi, pr, pi, tr, ti, reverse):
    s, lanes = xr.shape
    c = min(SCAN_CHUNK, s)
    nchunk = s // c
    in_chunk = lax.broadcasted_iota(jnp.int32, xr.shape, 0) & (c - 1)

    def shifted(x, sh):
        if reverse:
            return jnp.where(in_chunk < c - sh, pltpu.roll(x, s - sh, 0), 0.0)
        return jnp.where(in_chunk >= sh, pltpu.roll(x, sh, 0), 0.0)

    k = 0
    while (1 << k) < c:
        sr, si = shifted(xr, 1 << k), shifted(xi, 1 << k)
        ar, ai = pr[k:k + 1, :], pi[k:k + 1, :]
        xr, xi = xr + ar * sr - ai * si, xi + ar * si + ai * sr
        k += 1
    if nchunk == 1:
        return xr, xi
    xr, xi = xr.reshape(nchunk, c, lanes), xi.reshape(nchunk, c, lanes)
    edge = 0 if reverse else c - 1
    er, ei = xr[:, edge, :], xi[:, edge, :]
    m = 0
    while (1 << m) < nchunk:
        sr, si = _shift_rows(er, 1 << m, reverse), _shift_rows(ei, 1 << m, reverse)
        ar, ai = pr[k + m:k + m + 1, :], pi[k + m:k + m + 1, :]
        er, ei = er + ar * sr - ai * si, ei + ar * si + ai * sr
        m += 1
    cr, ci = _shift_rows(er, 1, reverse)[:, None, :], _shift_rows(ei, 1, reverse)[:, None, :]
    xr, xi = xr + tr[None] * cr - ti[None] * ci, xi + tr[None] * ci + ti[None] * cr
    return xr.reshape(s, lanes), xi.reshape(s, lanes)


def _cumsum_rows(x, reverse):
    s = x.shape[0]
    k = 0
    while (1 << k) < s:
        x = x + _shift_rows(x, 1 << k, reverse)
        k += 1
    return x


def _dot(a, b, ca=1, cb=0):
    return lax.dot_general(a.astype(BF16), b.astype(BF16), (((ca,), (cb,)), ((), ())), preferred_element_type=F32)


def _t5_bucket(dist):
    max_exact = REL_BUCKETS // 2
    d = np.maximum(dist, 1).astype(np.float32)
    large = max_exact + (np.log(d / max_exact) / np.log(REL_MAX_DIST / max_exact) * (REL_BUCKETS - max_exact)).astype(np.int32)
    large = np.minimum(large, REL_BUCKETS - 1)
    return np.where(dist < max_exact, dist, large).astype(np.int32)


def _swa_bucket_table():
    qi = np.arange(BLOCK)[:, None]
    kj = np.arange(2 * BLOCK)[None, :]
    return _t5_bucket(np.clip(qi + BLOCK - kj, 0, None))


SWA_BWD_HEADS = 2


def _stack_heads(x):
    return jnp.concatenate([x[:, g * HEAD_DIM:(g + 1) * HEAD_DIM] for g in range(SWA_GROUP)], axis=0)


def _unstack_heads(x):
    return jnp.concatenate([x[g * BLOCK:(g + 1) * BLOCK] for g in range(SWA_GROUP)], axis=1)


def _sink_rows(sink_ref):
    return jnp.concatenate([jnp.broadcast_to(sink_ref[0, g:g + 1, :1], (BLOCK, 1)) for g in range(SWA_GROUP)], axis=0)


def _swa_scores(qg, kb, bias_g, sk, n):
    s = _dot(qg, kb, 1, 1) * (HEAD_DIM ** -0.5) + bias_g
    row = lax.broadcasted_iota(jnp.int32, s.shape, 0) & (BLOCK - 1)
    col = lax.broadcasted_iota(jnp.int32, s.shape, 1)
    dist = row + BLOCK - col
    mask = (dist >= 0) & (dist < BLOCK) & ((col >= BLOCK) | (n > 0))
    s = jnp.where(mask, s, NEG)
    m = jnp.maximum(jnp.max(s, axis=1, keepdims=True), sk)
    e = jnp.exp(s - m)
    es = jnp.exp(sk - m)
    den = jnp.sum(e, axis=1, keepdims=True) + es
    return e / den, es / den


def _swa_specs(proj, k4, v4, nb, clamp):
    gw = SWA_GROUP * HEAD_DIM
    gate_off = (SWA_HEADS * HEAD_DIM + 2 * SWA_KV_HEADS * HEAD_DIM) // gw
    cur = (lambda n: jnp.minimum(n, nb - 1)) if clamp else (lambda n: n)
    prev = lambda n: jnp.maximum(cur(n) - 1, 0)
    return [
        pl.BlockSpec((BLOCK, gw), lambda h, n: (cur(n), h)),
        pl.BlockSpec((1, BLOCK, HEAD_DIM), lambda h, n: (h, cur(n), 0)),
        pl.BlockSpec((1, BLOCK, HEAD_DIM), lambda h, n: (h, prev(n), 0)),
        pl.BlockSpec((1, BLOCK, HEAD_DIM), lambda h, n: (h, cur(n), 0)),
        pl.BlockSpec((1, BLOCK, HEAD_DIM), lambda h, n: (h, prev(n), 0)),
        pl.BlockSpec((BLOCK, gw), lambda h, n: (cur(n), gate_off + h)),
    ], [proj, k4, k4, v4, v4, proj]


def _swa_fwd(proj, k4, v4, bias, sinks):
    s = proj.shape[0]
    nb = s // BLOCK
    gw = SWA_GROUP * HEAD_DIM

    def body(q_ref, kc_ref, kp_ref, vc_ref, vp_ref, gate_ref, bias_ref, sink_ref, a_ref):
        n = pl.program_id(1)
        kb = jnp.concatenate([kp_ref[0], kc_ref[0]], axis=0)
        vb = jnp.concatenate([vp_ref[0], vc_ref[0]], axis=0)
        p, _ = _swa_scores(_stack_heads(q_ref[...]), kb, bias_ref[...].reshape(-1, 2 * BLOCK), _sink_rows(sink_ref), n)
        a_ref[...] = (_unstack_heads(_dot(p, vb)) * _silu(gate_ref[...].astype(F32))).astype(a_ref.dtype)

    specs, args = _swa_specs(proj, k4, v4, nb, False)
    return pl.pallas_call(
        body, name="swa_fwd", grid=(SWA_KV_HEADS, nb),
        in_specs=specs + [pl.BlockSpec((SWA_GROUP, BLOCK, 2 * BLOCK), lambda h, n: (h, 0, 0)),
                          pl.BlockSpec((1, SWA_GROUP, LANES), lambda h, n: (h, 0, 0))],
        out_specs=pl.BlockSpec((BLOCK, gw), lambda h, n: (n, h)),
        out_shape=jax.ShapeDtypeStruct((s, SWA_HEADS * HEAD_DIM), BF16),
        compiler_params=_cp(("parallel", "parallel")),
    )(*args, bias, sinks)


def _swa_bwd(proj, k4, v4, bias, sinks, da):
    s = proj.shape[0]
    nb = s // BLOCK
    gw = SWA_GROUP * HEAD_DIM

    def body(q_ref, kc_ref, kp_ref, vc_ref, vp_ref, gate_ref, bias_ref, sink_ref, da_ref,
             dq_ref, dgate_ref, dk_ref, dv_ref, dbias_ref, dsink_ref, dk_own, dv_own):
        n = pl.program_id(1)

        @pl.when(n == 0)
        def _():
            dk_own[...] = jnp.zeros_like(dk_own)
            dv_own[...] = jnp.zeros_like(dv_own)
            dbias_ref[...] = jnp.zeros_like(dbias_ref)
            dsink_ref[...] = jnp.zeros_like(dsink_ref)

        @pl.when(n < nb)
        def _():
            kb = jnp.concatenate([kp_ref[0], kc_ref[0]], axis=0)
            vb = jnp.concatenate([vp_ref[0], vc_ref[0]], axis=0)
            dkb = jnp.zeros((2 * BLOCK, HEAD_DIM), F32)
            dvb = jnp.zeros((2 * BLOCK, HEAD_DIM), F32)
            for g0 in range(0, SWA_GROUP, SWA_BWD_HEADS):
                sl = slice(g0 * HEAD_DIM, (g0 + SWA_BWD_HEADS) * HEAD_DIM)
                stack = lambda x: jnp.concatenate([x[:, i * HEAD_DIM:(i + 1) * HEAD_DIM] for i in range(SWA_BWD_HEADS)], axis=0)
                unstack = lambda x: jnp.concatenate([x[i * BLOCK:(i + 1) * BLOCK] for i in range(SWA_BWD_HEADS)], axis=1)
                q2 = stack(q_ref[:, sl])
                sk = jnp.concatenate([jnp.broadcast_to(sink_ref[0, g0 + i:g0 + i + 1, :1], (BLOCK, 1))
                                      for i in range(SWA_BWD_HEADS)], axis=0)
                p, p0 = _swa_scores(q2, kb, bias_ref[g0:g0 + SWA_BWD_HEADS].reshape(-1, 2 * BLOCK), sk, n)
                o2 = _dot(p, vb)
                gate = gate_ref[:, sl].astype(F32)
                dag = da_ref[:, sl].astype(F32)
                sg = jax.nn.sigmoid(gate)
                do2 = stack(dag * gate * sg)
                dgate_ref[:, sl] = (dag * unstack(o2) * sg * (1.0 + gate * (1.0 - sg))).astype(dgate_ref.dtype)
                delta = jnp.sum(do2 * o2, axis=1, keepdims=True)
                ds = p * (_dot(do2, vb, 1, 1) - delta)
                dbias_ref[g0:g0 + SWA_BWD_HEADS] += ds.reshape(SWA_BWD_HEADS, BLOCK, 2 * BLOCK)
                sink_part = p0 * delta
                for i in range(SWA_BWD_HEADS):
                    dsink_ref[0, g0 + i:g0 + i + 1, :] += (jnp.zeros((1, LANES), F32)
                                                           - jnp.sum(sink_part[i * BLOCK:(i + 1) * BLOCK], axis=0, keepdims=True))
                dq_ref[:, sl] = (unstack(_dot(ds, kb)) * (HEAD_DIM ** -0.5)).astype(dq_ref.dtype)
                dkb += _dot(ds, q2, 0, 0) * (HEAD_DIM ** -0.5)
                dvb += _dot(p, do2, 0, 0)
            dk_ref[0] = dk_own[...] + dkb[:BLOCK]
            dv_ref[0] = dv_own[...] + dvb[:BLOCK]
            dk_own[...] = dkb[BLOCK:]
            dv_own[...] = dvb[BLOCK:]

        @pl.when(n == nb)
        def _():
            dk_ref[0] = dk_own[...]
            dv_ref[0] = dv_own[...]

    specs, args = _swa_specs(proj, k4, v4, nb, True)
    cur = lambda n: jnp.minimum(n, nb - 1)
    trail = lambda n: jnp.maximum(n - 1, 0)
    return pl.pallas_call(
        body, name="swa_bwd", grid=(SWA_KV_HEADS, nb + 1),
        in_specs=specs + [pl.BlockSpec((SWA_GROUP, BLOCK, 2 * BLOCK), lambda h, n: (h, 0, 0)),
                          pl.BlockSpec((1, SWA_GROUP, LANES), lambda h, n: (h, 0, 0)),
                          pl.BlockSpec((BLOCK, gw), lambda h, n: (cur(n), h))],
        out_specs=[pl.BlockSpec((BLOCK, gw), lambda h, n: (cur(n), h)),
                   pl.BlockSpec((BLOCK, gw), lambda h, n: (cur(n), h)),
                   pl.BlockSpec((1, BLOCK, HEAD_DIM), lambda h, n: (h, trail(n), 0)),
                   pl.BlockSpec((1, BLOCK, HEAD_DIM), lambda h, n: (h, trail(n), 0)),
                   pl.BlockSpec((SWA_GROUP, BLOCK, 2 * BLOCK), lambda h, n: (h, 0, 0)),
                   pl.BlockSpec((1, SWA_GROUP, LANES), lambda h, n: (h, 0, 0))],
        out_shape=[jax.ShapeDtypeStruct((s, SWA_HEADS * HEAD_DIM), BF16),
                   jax.ShapeDtypeStruct((s, SWA_HEADS * HEAD_DIM), BF16),
                   jax.ShapeDtypeStruct((SWA_KV_HEADS, s, HEAD_DIM), F32),
                   jax.ShapeDtypeStruct((SWA_KV_HEADS, s, HEAD_DIM), F32),
                   jax.ShapeDtypeStruct((SWA_HEADS, BLOCK, 2 * BLOCK), F32),
                   jax.ShapeDtypeStruct((SWA_KV_HEADS, SWA_GROUP, LANES), F32)],
        scratch_shapes=[pltpu.VMEM((BLOCK, HEAD_DIM), F32), pltpu.VMEM((BLOCK, HEAD_DIM), F32)],
        compiler_params=_cp(("arbitrary", "arbitrary")),
    )(*args, bias, sinks, da)


FOX_TILE = 1024


def _below_and_on_diagonal(i, j, step):
    @pl.when(j < i)
    def _():
        step(False)

    @pl.when(j == i)
    def _():
        step(True)


FOX_AUG = 128
FOX_CQ, FOX_CK = HEAD_DIM, HEAD_DIM + 1


def _fox_operands(proj, csum):
    s = proj.shape[0]
    hw = FOX_HEADS * HEAD_DIM
    heads = lambda a: a.reshape(s, FOX_HEADS, HEAD_DIM).transpose(1, 0, 2)
    q = heads(proj[:, :hw]) * jnp.asarray(HEAD_DIM ** -0.5, BF16)
    k, v = heads(proj[:, hw:2 * hw]), heads(proj[:, 2 * hw:3 * hw])
    one = jnp.ones((FOX_HEADS, s, 1), BF16)
    zero = jnp.zeros((FOX_HEADS, s, 1), BF16)
    pad = jnp.zeros((FOX_HEADS, s, FOX_AUG - HEAD_DIM - 2), BF16)
    qa = jnp.concatenate([q, zero, -one, pad], axis=-1)
    ka = jnp.concatenate([k, one, zero, pad], axis=-1)
    tr = lambda a: a.transpose(0, 2, 1)
    c = csum[:, :FOX_HEADS].T
    return (qa, ka, tr(qa), tr(ka), v, tr(v), c[:, None, :], jnp.broadcast_to(c[:, :, None], (FOX_HEADS, s, LANES)))


def _fox_scores_t(ka, qa, cq, ck, diag):
    st = _dot(ka, qa, 1, 1) + cq - jnp.concatenate([ck] * (qa.shape[0] // LANES), axis=1)
    if diag:
        st = jnp.where(lax.broadcasted_iota(jnp.int32, st.shape, 0) <= lax.broadcasted_iota(jnp.int32, st.shape, 1), st, NEG)
    return st


def _fox_attention(qa, ka, vt, c_row, c_lanes):
    nh, s, aw = qa.shape
    t = min(FOX_TILE, s)
    nt = s // t
    hd = HEAD_DIM

    def body(qa_ref, ka_ref, vt_ref, cq_ref, ck_ref, o_ref, lse_ref, m_s, l_s, acc_s):
        i, j = pl.program_id(1), pl.program_id(2)

        @pl.when(j == 0)
        def _():
            m_s[...] = jnp.full_like(m_s, NEG)
            l_s[...] = jnp.zeros_like(l_s)
            acc_s[...] = jnp.zeros_like(acc_s)

        def step(diag):
            st = _fox_scores_t(ka_ref[0], qa_ref[0], cq_ref[0], ck_ref[0], diag)
            m_old = m_s[...]
            m_new = jnp.maximum(m_old, jnp.max(st, axis=0, keepdims=True))
            alpha = jnp.exp(m_old - m_new)
            p = jnp.exp(st - m_new)
            l_s[...] = alpha * l_s[...] + jnp.sum(p, axis=0, keepdims=True)
            acc_s[...] = alpha * acc_s[...] + _dot(vt_ref[0], p)
            m_s[...] = m_new

        _below_and_on_diagonal(i, j, step)

        @pl.when(j == nt - 1)
        def _():
            o_ref[0] = (acc_s[...] / l_s[...]).astype(o_ref.dtype)
            lse_ref[0] = m_s[...] + jnp.log(l_s[...])

    kj = lambda i, j: jnp.minimum(j, i)
    return pl.pallas_call(
        body, name="fox_fwd", grid=(nh, nt, nt),
        in_specs=[pl.BlockSpec((1, t, aw), lambda h, i, j: (h, i, 0)),
                  pl.BlockSpec((1, t, aw), lambda h, i, j: (h, kj(i, j), 0)),
                  pl.BlockSpec((1, hd, t), lambda h, i, j: (h, 0, kj(i, j))),
                  pl.BlockSpec((1, 1, t), lambda h, i, j: (h, 0, i)),
                  pl.BlockSpec((1, t, LANES), lambda h, i, j: (h, kj(i, j), 0))],
        out_specs=[pl.BlockSpec((1, hd, t), lambda h, i, j: (h, 0, i)),
                   pl.BlockSpec((1, 1, t), lambda h, i, j: (h, 0, i))],
        out_shape=[jax.ShapeDtypeStruct((nh, hd, s), BF16), jax.ShapeDtypeStruct((nh, 1, s), F32)],
        scratch_shapes=[pltpu.VMEM((1, t), F32), pltpu.VMEM((1, t), F32), pltpu.VMEM((hd, t), F32)],
        compiler_params=_cp(("parallel", "parallel", "arbitrary")),
    )(qa, ka, vt, c_row, c_lanes)


def _fox_attention_bwd(qa, ka, qat, kat, v, c_row, c_lanes, ot, dot, lse):
    nh, s, aw = qa.shape
    t = min(FOX_TILE, s)
    nt = s // t
    hd = HEAD_DIM

    def body(qa_ref, ka_ref, qat_ref, kat_ref, v_ref, cq_ref, ck_ref, ot_ref, dot_ref, lse_ref,
             dq_ref, dcq_ref, dk_ref, dck_ref, dv_ref, dqa_s, dk_s, dv_s):
        j, i = pl.program_id(1), pl.program_id(2)

        @pl.when((j == 0) & (i == 0))
        def _():
            dqa_s[...] = jnp.zeros_like(dqa_s)

        @pl.when(i == 0)
        def _():
            dk_s[...] = jnp.zeros_like(dk_s)
            dv_s[...] = jnp.zeros_like(dv_s)

        def step(diag):
            p = jnp.exp(_fox_scores_t(ka_ref[0], qa_ref[0], cq_ref[0], ck_ref[0], diag) - lse_ref[0])
            do_t = dot_ref[0]
            delta = jnp.sum(do_t.astype(F32) * ot_ref[0].astype(F32), axis=0, keepdims=True)
            ds = (p * (_dot(v_ref[0], do_t) - delta)).astype(BF16)
            dv_s[...] += _dot(do_t, p, 1, 1)
            dk_s[...] += _dot(qat_ref[0], ds, 1, 1)
            cols = pl.ds(pl.multiple_of(i * t, t), t)
            dqa_s[:, cols] += _dot(kat_ref[0], ds)

        _below_and_on_diagonal(i, j, step)

        @pl.when(i == nt - 1)
        def _():
            dk_ref[0] = dk_s[:hd].astype(dk_ref.dtype)
            dck_ref[0] = dk_s[FOX_CK:FOX_CK + 1]
            dv_ref[0] = dv_s[...].astype(dv_ref.dtype)

        @pl.when((j == nt - 1) & (i == nt - 1))
        def _():
            dq_ref[0] = (dqa_s[:hd] * (hd ** -0.5)).astype(dq_ref.dtype)
            dcq_ref[0] = dqa_s[FOX_CQ:FOX_CQ + 1]

    qi = lambda i, j: jnp.maximum(i, j)
    return pl.pallas_call(
        body, name="fox_bwd", grid=(nh, nt, nt),
        in_specs=[pl.BlockSpec((1, t, aw), lambda h, j, i: (h, qi(i, j), 0)),
                  pl.BlockSpec((1, t, aw), lambda h, j, i: (h, j, 0)),
                  pl.BlockSpec((1, aw, t), lambda h, j, i: (h, 0, qi(i, j))),
                  pl.BlockSpec((1, aw, t), lambda h, j, i: (h, 0, j)),
                  pl.BlockSpec((1, t, hd), lambda h, j, i: (h, j, 0)),
                  pl.BlockSpec((1, 1, t), lambda h, j, i: (h, 0, qi(i, j))),
                  pl.BlockSpec((1, t, LANES), lambda h, j, i: (h, j, 0)),
                  pl.BlockSpec((1, hd, t), lambda h, j, i: (h, 0, qi(i, j))),
                  pl.BlockSpec((1, hd, t), lambda h, j, i: (h, 0, qi(i, j))),
                  pl.BlockSpec((1, 1, t), lambda h, j, i: (h, 0, qi(i, j)))],
        out_specs=[pl.BlockSpec((1, hd, s), lambda h, j, i: (h, 0, 0)),
                   pl.BlockSpec((1, 1, s), lambda h, j, i: (h, 0, 0)),
                   pl.BlockSpec((1, hd, t), lambda h, j, i: (h, 0, j)),
                   pl.BlockSpec((1, 1, t), lambda h, j, i: (h, 0, j)),
                   pl.BlockSpec((1, hd, t), lambda h, j, i: (h, 0, j))],
        out_shape=[jax.ShapeDtypeStruct((nh, hd, s), BF16), jax.ShapeDtypeStruct((nh, 1, s), F32),
                   jax.ShapeDtypeStruct((nh, hd, s), BF16), jax.ShapeDtypeStruct((nh, 1, s), F32),
                   jax.ShapeDtypeStruct((nh, hd, s), BF16)],
        scratch_shapes=[pltpu.VMEM((aw, s), F32), pltpu.VMEM((aw, t), F32), pltpu.VMEM((hd, t), F32)],
        compiler_params=_cp(("parallel", "arbitrary", "arbitrary")),
    )(qa, ka, qat, kat, v, c_row, c_lanes, ot, dot, lse)


def _ssm_prep(lam_re, lam_im, log_dt, b_re, b_im, c_re, c_im):
    g, n = lam_re.shape
    dt = jnp.exp(log_dt)[:, None]
    mag = jnp.exp(lam_re * dt)
    ab_re = mag * jnp.cos(lam_im * dt)
    ab_im = mag * jnp.sin(lam_im * dt)
    den = lam_re * lam_re + lam_im * lam_im
    nr = ab_re - 1.0
    coef_re = ((nr * lam_re + ab_im * lam_im) / den)[..., None]
    coef_im = ((ab_im * lam_re - nr * lam_im) / den)[..., None]
    bb_re = coef_re * b_re - coef_im * b_im
    bb_im = coef_re * b_im + coef_im * b_re
    nblk = g // GROUPS_PER_STEP
    eye = jnp.eye(GROUPS_PER_STEP, dtype=F32)

    def bdiag(bb):
        return jnp.einsum('bgnc,gh->bgchn', bb.reshape(nblk, GROUPS_PER_STEP, n, SSM_GROUP), eye).reshape(
            nblk, GROUPS_PER_STEP * SSM_GROUP, STATE_W)

    def cdiag(cc):
        return jnp.einsum('bgcn,gh->bgnhc', cc.reshape(nblk, GROUPS_PER_STEP, SSM_GROUP, n), eye).reshape(
            nblk, STATE_W, GROUPS_PER_STEP * SSM_GROUP)

    return (ab_re.reshape(nblk, 1, STATE_W), ab_im.reshape(nblk, 1, STATE_W),
            bdiag(bb_re), bdiag(bb_im), cdiag(c_re), cdiag(c_im))


def _powers(a_re, a_im, levels):
    rs, ims = [a_re], [a_im]
    for _ in range(levels - 1):
        r, i = rs[-1], ims[-1]
        rs.append(r * r - i * i)
        ims.append(2.0 * r * i)
    return jnp.concatenate(rs, axis=1), jnp.concatenate(ims, axis=1)


def _power_table(a_re, a_im, n, descending=False):
    tr, ti = a_re, a_im
    while tr.shape[1] < n:
        top = 0 if descending else -1
        lr, li = tr[:, top:][:, :1], ti[:, top:][:, :1]
        hr, hi = tr * lr - ti * li, tr * li + ti * lr
        tr, ti = ((jnp.concatenate([hr, tr], axis=1), jnp.concatenate([hi, ti], axis=1)) if descending
                  else (jnp.concatenate([tr, hr], axis=1), jnp.concatenate([ti, hi], axis=1)))
    return tr, ti


def _ssm_fwd(proj, pw_re, pw_im, tab_re, tab_im, bd_re, bd_im, cd_re, cd_im, dskip):
    s = proj.shape[0]
    w = dskip.shape[1]
    nblk = w // LANES
    nch = STATE_W // LANES
    levels = pw_re.shape[1]

    def body(u_ref, pr, pi, tr, ti, br, bi, cr, ci, d_ref, pre_ref, yg_ref, xr_ref, xi_ref):
        ch = pl.program_id(1)
        u = u_ref[...]
        xr, xi = _scan_complex(_dot(u, br[0]), _dot(u, bi[0]), pr[0], pi[0], tr[0], ti[0], False)
        xr_ref[...] = xr.astype(BF16)
        xi_ref[...] = xi.astype(BF16)
        yc = _dot(xr, cr[0]) - _dot(xi, ci[0])

        @pl.when(ch == 0)
        def _():
            pre_ref[...] = yc + d_ref[...] * u

        @pl.when(ch > 0)
        def _():
            pre_ref[...] += yc

        @pl.when(ch == nch - 1)
        def _():
            yg_ref[...] = _gelu(pre_ref[...]).astype(BF16)

    blk = lambda b, c: (0, b)
    col = lambda b, c: (0, b * nch + c)
    in_chunk = pl.BlockSpec((1, LANES, LANES), lambda b, c: (b, 0, c))
    out_chunk = pl.BlockSpec((1, LANES, LANES), lambda b, c: (b, c, 0))
    pw_spec = pl.BlockSpec((1, levels, LANES), lambda b, c: (b, 0, c))
    tab_spec = pl.BlockSpec((1, tab_re.shape[1], LANES), lambda b, c: (b, 0, c))
    return pl.pallas_call(
        body, name="ssm_fwd", grid=(nblk, nch),
        in_specs=[pl.BlockSpec((s, LANES), blk), pw_spec, pw_spec, tab_spec, tab_spec, in_chunk, in_chunk,
                  out_chunk, out_chunk, pl.BlockSpec((1, LANES), blk)],
        out_specs=[pl.BlockSpec((s, LANES), blk), pl.BlockSpec((s, LANES), blk),
                   pl.BlockSpec((s, LANES), col), pl.BlockSpec((s, LANES), col)],
        out_shape=[jax.ShapeDtypeStruct((s, w), F32), jax.ShapeDtypeStruct((s, w), BF16),
                   jax.ShapeDtypeStruct((s, nblk * STATE_W), BF16), jax.ShapeDtypeStruct((s, nblk * STATE_W), BF16)],
        compiler_params=_cp(("parallel", "arbitrary")),
    )(proj, pw_re, pw_im, tab_re, tab_im, bd_re, bd_im, cd_re, cd_im, dskip)


def _ssm_bwd(proj, dyg, pre, xr_all, xi_all, pw_re, pw_im, tab_re, tab_im, bd_re, bd_im, cd_re, cd_im, dskip):
    s = proj.shape[0]
    w = dskip.shape[1]
    nblk = w // LANES
    nch = STATE_W // LANES
    levels = pw_re.shape[1]

    def body(u_ref, dy_ref, pre_ref, xr_ref, xi_ref, pr, pi, tr, ti, br, bi, cr, ci, d_ref,
             du_ref, dd_ref, dar_ref, dai_ref, dbr_ref, dbi_ref, dcr_ref, dci_ref):
        ch = pl.program_id(1)
        u = u_ref[...]
        _, vjp = jax.vjp(_gelu, pre_ref[...])
        dpre = vjp(dy_ref[...].astype(F32))[0]
        zr, zi = _scan_complex(_dot(dpre, cr[0], 1, 1), -_dot(dpre, ci[0], 1, 1), pr[0], -pi[0], tr[0], -ti[0], True)
        xpr = _shift_rows(xr_ref[...].astype(F32), 1)
        xpi = _shift_rows(xi_ref[...].astype(F32), 1)
        dar_ref[0] = jnp.sum(zr * xpr + zi * xpi, axis=0, keepdims=True)
        dai_ref[0] = jnp.sum(zi * xpr - zr * xpi, axis=0, keepdims=True)
        dcr_ref[0] = _dot(xr_ref[...], dpre, 0, 0)
        dci_ref[0] = -_dot(xi_ref[...], dpre, 0, 0)
        dbr_ref[0] = _dot(u, zr, 0, 0)
        dbi_ref[0] = _dot(u, zi, 0, 0)
        duc = _dot(zr, br[0], 1, 1) + _dot(zi, bi[0], 1, 1)

        @pl.when(ch == 0)
        def _():
            du_ref[...] = duc + dpre * d_ref[...]
            dd_ref[...] = jnp.sum(dpre * u, axis=0, keepdims=True)

        @pl.when(ch > 0)
        def _():
            du_ref[...] += duc

    blk = lambda b, c: (0, b)
    col = lambda b, c: (0, b * nch + c)
    in_chunk = pl.BlockSpec((1, LANES, LANES), lambda b, c: (b, 0, c))
    out_chunk = pl.BlockSpec((1, LANES, LANES), lambda b, c: (b, c, 0))
    pw_spec = pl.BlockSpec((1, levels, LANES), lambda b, c: (b, 0, c))
    a_spec = pl.BlockSpec((1, 1, LANES), lambda b, c: (b, 0, c))
    tab_spec = pl.BlockSpec((1, tab_re.shape[1], LANES), lambda b, c: (b, 0, c))
    return pl.pallas_call(
        body, name="ssm_bwd", grid=(nblk, nch),
        in_specs=[pl.BlockSpec((s, LANES), blk), pl.BlockSpec((s, LANES), blk), pl.BlockSpec((s, LANES), blk),
                  pl.BlockSpec((s, LANES), col), pl.BlockSpec((s, LANES), col),
                  pw_spec, pw_spec, tab_spec, tab_spec, in_chunk, in_chunk, out_chunk, out_chunk,
                  pl.BlockSpec((1, LANES), blk)],
        out_specs=[pl.BlockSpec((s, LANES), blk), pl.BlockSpec((1, LANES), blk), a_spec, a_spec,
                   in_chunk, in_chunk, out_chunk, out_chunk],
        out_shape=[jax.ShapeDtypeStruct((s, w), F32), jax.ShapeDtypeStruct((1, w), F32),
                   jax.ShapeDtypeStruct((nblk, 1, STATE_W), F32), jax.ShapeDtypeStruct((nblk, 1, STATE_W), F32),
                   jax.ShapeDtypeStruct((nblk, LANES, STATE_W), F32), jax.ShapeDtypeStruct((nblk, LANES, STATE_W), F32),
                   jax.ShapeDtypeStruct((nblk, STATE_W, LANES), F32), jax.ShapeDtypeStruct((nblk, STATE_W, LANES), F32)],
        compiler_params=_cp(("parallel", "arbitrary")),
    )(proj, dyg, pre, xr_all, xi_all, pw_re, pw_im, tab_re, tab_im, bd_re, bd_im, cd_re, cd_im, dskip)


ROW_TILE = 256


def _norm_fwd(x, g, name):
    s, d = x.shape
    t = min(ROW_TILE, s)
    return _tiled(lambda xv, gv: (_rms(xv, gv),), [_rows(x, t), _full(g)], [_orows(s, d, BF16, t)], s // t, name)[0]


def _norm_bwd(x, g, dh_list, dx_in, name):
    s, d = x.shape
    t = min(ROW_TILE, s)

    def fn(xv, gv, dxv, *dhs):
        dh = dhs[0].astype(F32)
        for other in dhs[1:]:
            dh = dh + other.astype(F32)
        _, vjp = jax.vjp(_rms, xv, gv)
        dx, dg = vjp(dh)
        dx = dx + dxv
        return dx, dx, dg

    return _tiled(fn, [_rows(x, t), _full(g), _rows(dx_in, t)] + [_rows(a, t) for a in dh_list],
                  [_orows(s, d, F32, t), _orows(s, d, BF16, t), _oacc((1, d))], s // t, name)


def _ple_fwd(x, y, pn, name):
    s, d = x.shape
    t = min(ROW_TILE, s)

    def fn(xv, yv, gv):
        x1 = xv + yv
        return x1, _rms(x1, gv)

    return _tiled(fn, [_rows(x, t), _rows(y, t), _full(pn)], [_orows(s, d, F32, t), _orows(s, d, BF16, t)], s // t, name)


def _ple_mix(x1, emb, gl, name):
    s, d = x1.shape
    t = min(ROW_TILE, s)
    return _tiled(lambda a, e, g: (a + e * jax.nn.sigmoid(g),), [_rows(x1, t), _rows(emb, t), _rows(gl, t)],
                  [_orows(s, d, F32, t)], s // t, name)[0]


def _ple_mix_bwd(dx2, emb, gl, name):
    s, d = dx2.shape
    t = min(ROW_TILE, s)

    def fn(dx, e, g):
        sg = jax.nn.sigmoid(g)
        return dx * sg, dx * e * sg * (1.0 - sg)

    return _tiled(fn, [_rows(dx2, t), _rows(emb, t), _rows(gl, t)],
                  [_orows(s, d, BF16, t), _orows(s, d, BF16, t)], s // t, name)


def _loss_grad(x, target, g):
    s, d = x.shape
    t = min(ROW_TILE, s)

    def fn(xv, tv, gv):
        def f(xx, gg):
            err = _rms(xx, gg) - tv
            return 0.5 * jnp.sum(jnp.mean(err * err, axis=-1, keepdims=True), axis=0, keepdims=True)

        loss, vjp = jax.vjp(f, xv, gv)
        dx, dg = vjp(jnp.ones((1, 1), F32))
        return loss, dx, dg

    return _tiled(fn, [_rows(x, t), _rows(target, t), _full(g)],
                  [_oacc((1, 1)), _orows(s, d, F32, t), _oacc((1, d))], s // t, "loss_grad")


def _conv_fwd(proj, kern):
    s = proj.shape[0]
    w = kern.shape[1]
    nb = w // LANES

    def fn(bg, cg, u, gate, k):
        z = cg.astype(F32) * u.astype(F32)
        conv = k[2:3] * z + k[1:2] * _shift_rows(z, 1) + k[0:1] * _shift_rows(z, 2)
        return (bg.astype(F32) * conv * _silu(gate.astype(F32)),)

    return _tiled(fn, [_cols(proj, LANES, q * nb) for q in range(4)] + [_cols(kern, LANES)],
                  [_ocols(s, w, BF16, LANES)], nb, "conv_fwd")[0]


def _conv_bwd(proj, kern, da):
    s = proj.shape[0]
    w = kern.shape[1]
    nb = w // LANES

    def fn(bg, cg, u, gate, k, dav):
        bg, cg, u, gate, dav = (a.astype(F32) for a in (bg, cg, u, gate, dav))
        z = cg * u
        z1, z2 = _shift_rows(z, 1), _shift_rows(z, 2)
        conv = k[2:3] * z + k[1:2] * z1 + k[0:1] * z2
        sg = jax.nn.sigmoid(gate)
        dy = dav * gate * sg
        dgate = dav * bg * conv * sg * (1.0 + gate * (1.0 - sg))
        dconv = dy * bg
        dk = jnp.concatenate([jnp.sum(dconv * zz, axis=0, keepdims=True) for zz in (z2, z1, z)], axis=0)
        dz = k[2:3] * dconv + k[1:2] * _shift_rows(dconv, 1, True) + k[0:1] * _shift_rows(dconv, 2, True)
        return dy * conv, dz * u, dz * cg, dgate, dk

    return _tiled(fn, [_cols(proj, LANES, q * nb) for q in range(4)] + [_cols(kern, LANES), _cols(da, LANES)],
                  [_ocols(s, w, BF16, LANES)] * 4 + [_ocols(3, w, F32, LANES)], nb, "conv_bwd")


def _glu_fwd(gl, proj, bglu):
    s, w2 = gl.shape
    w = w2 // 2
    t = min(ROW_TILE, s)

    def fn(glv, gate, b):
        v = glv + b
        return (v[:, :w] * jax.nn.sigmoid(v[:, w:]) * _silu(gate),)

    return _tiled(fn, [_rows(gl, t), (proj, (t, w), lambda i: (i, 1)), _full(bglu)],
                  [_orows(s, w, BF16, t)], s // t, "glu_fwd")[0]


def _glu_bwd(gl, proj, bglu, da):
    s, w2 = gl.shape
    w = w2 // 2
    t = min(ROW_TILE, s)

    def fn(glv, gate, b, dav):
        def f(gg, gt, bb):
            v = gg + bb
            return v[:, :w] * jax.nn.sigmoid(v[:, w:]) * _silu(gt)

        _, vjp = jax.vjp(f, glv, gate, b)
        return vjp(dav.astype(F32))

    return _tiled(fn, [_rows(gl, t), (proj, (t, w), lambda i: (i, 1)), _full(bglu), _rows(da, t)],
                  [_orows(s, w2, BF16, t), _orows(s, w, BF16, t), _oacc((1, w2))], s // t, "glu_bwd")


def _fg_fwd(z, b):
    def fn(zv, bv):
        v = zv + bv
        logf = jnp.minimum(v, 0.0) - jnp.log(1.0 + jnp.exp(-jnp.abs(v)))
        return (_cumsum_rows(logf, False),)

    return _tiled(fn, [_full(z), _full(b)], [_out(z.shape, F32, z.shape, lambda i: (0, 0))], 1, "fg_fwd")[0]


def _fg_bwd(z, b, dcs):
    def fn(zv, bv, dc):
        dz = _cumsum_rows(dc, True) * jax.nn.sigmoid(-(zv + bv))
        return dz, jnp.sum(dz, axis=0, keepdims=True)

    return _tiled(fn, [_full(z), _full(b), _full(dcs)],
                  [_out(z.shape, BF16, z.shape, lambda i: (0, 0)), _out((1, z.shape[1]), F32, (1, z.shape[1]), lambda i: (0, 0))],
                  1, "fg_bwd")


def _fox_gate_bwd(da, o, proj):
    s, w = o.shape
    t = min(ROW_TILE, s)

    def fn(dav, ov, gate):
        dav, ov, gate = dav.astype(F32), ov.astype(F32), gate.astype(F32)
        sg = jax.nn.sigmoid(gate)
        return dav * gate * sg, dav * ov * sg * (1.0 + gate * (1.0 - sg))

    return _tiled(fn, [_rows(da, t), _rows(o, t), (proj, (t, w), lambda i: (i, 3))],
                  [_orows(s, w, BF16, t), _orows(s, w, BF16, t)], s // t, "fox_gate_bwd")


def _fox_gate_fwd(o, proj):
    s, w = o.shape
    t = min(ROW_TILE, s)
    return _tiled(lambda ov, gate: (ov.astype(F32) * _silu(gate.astype(F32)),),
                  [_rows(o, t), (proj, (t, w), lambda i: (i, 3))], [_orows(s, w, BF16, t)], s // t, "fox_gate_fwd")[0]


def _local_step(x, p, target, w):
    s, d = x.shape
    depth = p.shape[0]
    pb = p.astype(BF16)
    grads = {}
    saved = []

    bucket = _swa_bucket_table()
    onehot = np.eye(REL_BUCKETS, dtype=np.float32)[bucket.reshape(-1)]
    bias = _bias_table(w['rel_bias'], jnp.asarray(onehot.T, BF16))
    sinks = jnp.broadcast_to(w['swa_sinks'].reshape(SWA_KV_HEADS, SWA_GROUP, 1), (SWA_KV_HEADS, SWA_GROUP, LANES))
    ssm_params = tuple(w[k][0] for k in ('ssm_lam_re', 'ssm_lam_im', 'ssm_log_dt', 'ssm_b_re', 'ssm_b_im', 'ssm_c_re', 'ssm_c_im'))
    (a_re, a_im, bd_re, bd_im, cd_re, cd_im), ssm_vjp = jax.vjp(_ssm_prep, *ssm_params)
    levels = max(1, (s - 1).bit_length())
    pw_re, pw_im = _powers(a_re, a_im, levels)
    tab_re, tab_im = _power_table(a_re, a_im, min(SCAN_CHUNK, s))
    rtab_re, rtab_im = _power_table(a_re, a_im, min(SCAN_CHUNK, s), descending=True)
    wfg = jnp.pad(w['fox_w_fg'][0], ((0, 0), (0, LANES - FOX_HEADS)))
    bfg = jnp.pad(w['fox_b_fg'], ((0, 0), (0, LANES - FOX_HEADS)))

    def qkv4(a):
        return a.reshape(s, SWA_KV_HEADS, HEAD_DIM).transpose(1, 0, 2)

    def unheads(a):
        return a.transpose(2, 0, 1).reshape(s, FOX_HEADS * HEAD_DIM)

    for i in range(depth):
        mixer = i % 4
        hn = _norm_fwd(x, w['norm_g'][i:i + 1], f"norm_fwd{i}")
        sv = {'x': x, 'hn': hn}
        if mixer == 0:
            proj = _mm(hn, w['swa_w_in'][:, 0], b_split=True, name="swa_in")
            qw = SWA_HEADS * HEAD_DIM
            kvw = SWA_KV_HEADS * HEAD_DIM
            k4, v4 = qkv4(proj[:, qw:qw + kvw]), qkv4(proj[:, qw + kvw:qw + 2 * kvw])
            a = _swa_fwd(proj, k4, v4, bias, sinks)
            sv.update(proj=proj, k4=k4, v4=v4)
            w_out = w['swa_w_out'][0]
        elif mixer == 1:
            proj = _mm(hn, w['conv_w_in'][:, 0], b_split=True, name="conv_in")
            a = _conv_fwd(proj, w['conv_kernel'][0])
            sv.update(proj=proj)
            w_out = w['conv_w_out'][0]
        elif mixer == 2:
            proj = _mm(hn, w['ssm_w_in'][:, 0], b_split=True, out_dtype=F32, name="ssm_in")
            pre, yg, xr_all, xi_all = _ssm_fwd(proj, pw_re, pw_im, tab_re, tab_im, bd_re, bd_im, cd_re, cd_im, w['ssm_d'])
            gl = _mm(yg, w['ssm_w_glu'][:, 0], b_split=True, out_dtype=F32, name="ssm_glu")
            a = _glu_fwd(gl, proj, w['ssm_b_glu'])
            sv.update(proj=proj, pre=pre, yg=yg, xr=xr_all, xi=xi_all, gl=gl)
            w_out = w['ssm_w_out'][0]
        else:
            proj = _mm(hn, w['fox_w_in'][:, 0], b_split=True, name="fox_in")
            z = _mm(hn, wfg, out_dtype=F32, name="fox_fg")
            fox_ops = _fox_operands(proj, _fg_fwd(z, bfg))
            ot, lse = _fox_attention(fox_ops[0], fox_ops[1], fox_ops[5], fox_ops[6], fox_ops[7])
            o = unheads(ot)
            a = _fox_gate_fwd(o, proj)
            sv.update(proj=proj, z=z, fox_ops=fox_ops, ot=ot, o=o, lse=lse)
            w_out = w['fox_w_out'][0]
        y = _mm(a, w_out, out_dtype=F32, name=f"mixer_out{i}")
        x1, gn = _ple_fwd(x, y, w['ple_norm'][i:i + 1], f"ple_fwd{i}")
        emb = _mm(pb[i], w['ple_proj'][:, i], b_split=True, out_dtype=F32, name=f"ple_emb{i}")
        gl2 = _mm(gn, w['ple_gate'][i], out_dtype=F32, name=f"ple_gate{i}")
        x = _ple_mix(x1, emb, gl2, f"ple_mix{i}")
        sv.update(a=a, x1=x1, gn=gn, emb=emb, gl2=gl2)
        saved.append(sv)

    loss, dx, grads['final_g'] = _loss_grad(x, target, w['final_g'].reshape(1, d))
    grads['final_g'] = grads['final_g'].reshape(d)

    g_norm, g_ple_norm, g_ple_proj, g_ple_gate = [None] * depth, [None] * depth, [None] * depth, [None] * depth
    for i in reversed(range(depth)):
        sv = saved[i]
        mixer = i % 4
        demb, dgl2 = _ple_mix_bwd(dx, sv['emb'], sv['gl2'], f"ple_mix_bwd{i}")
        g_ple_proj[i] = _mm(pb[i], demb, ta=True, out_split=N_CHIPS, name=f"ple_emb_dw{i}")
        g_ple_gate[i] = _mm(sv['gn'], dgl2, ta=True, name=f"ple_gate_dw{i}")
        dgn = _mm(dgl2, w['ple_gate'][i], tb=True, out_dtype=F32, name=f"ple_gate_dx{i}")
        dx1, dy, g_ple_norm[i] = _norm_bwd(sv['x1'], w['ple_norm'][i:i + 1], [dgn], dx, f"ple_norm_bwd{i}")
        w_out_name = ('swa_w_out', 'conv_w_out', 'ssm_w_out', 'fox_w_out')[mixer]
        grads[w_out_name] = _mm(sv['a'], dy, ta=True, name=f"mixer_out_dw{i}")[None]
        da = _mm(dy, w[w_out_name][0], tb=True, name=f"mixer_out_dx{i}")
        proj = sv['proj']
        dhs = []
        if mixer == 0:
            dq, dgate, dk4, dv4, dbias, dsink = _swa_bwd(proj, sv['k4'], sv['v4'], bias, sinks, da)
            back = lambda t4: t4.transpose(1, 0, 2).reshape(s, SWA_KV_HEADS * HEAD_DIM).astype(BF16)
            dproj = jnp.concatenate([dq, back(dk4), back(dv4), dgate], axis=1)
            grads['rel_bias'] = _bias_grad(dbias.reshape(SWA_HEADS, -1), jnp.asarray(onehot, BF16))
            grads['swa_sinks'] = dsink[:, :, 0].reshape(1, SWA_HEADS)
            w_in_name = 'swa_w_in'
        elif mixer == 1:
            dbg, dcg, du, dgate, dkern = _conv_bwd(proj, w['conv_kernel'][0], da)
            dproj = jnp.concatenate([dbg, dcg, du, dgate], axis=1)
            grads['conv_kernel'] = dkern[None]
            w_in_name = 'conv_w_in'
        elif mixer == 2:
            dgl, dgate, dbglu = _glu_bwd(sv['gl'], proj, w['ssm_b_glu'], da)
            grads['ssm_b_glu'] = dbglu
            grads['ssm_w_glu'] = _mm(sv['yg'], dgl, ta=True, out_split=N_CHIPS, name="ssm_glu_dw")[:, None]
            dyg = _mm(dgl, w['ssm_w_glu'][:, 0], tb=True, b_split=True, out_dtype=F32, name="ssm_glu_dx")
            du, dd, da_re, da_im, dbd_re, dbd_im, dcd_re, dcd_im = _ssm_bwd(
                proj, dyg, sv['pre'], sv['xr'], sv['xi'], pw_re, pw_im, rtab_re, rtab_im, bd_re, bd_im, cd_re, cd_im, w['ssm_d'])
            grads['ssm_d'] = dd
            dparams = ssm_vjp((da_re, da_im, dbd_re, dbd_im, dcd_re, dcd_im))
            for k, v in zip(('ssm_lam_re', 'ssm_lam_im', 'ssm_log_dt', 'ssm_b_re', 'ssm_b_im', 'ssm_c_re', 'ssm_c_im'), dparams):
                grads[k] = v[None]
            dproj = jnp.concatenate([du.astype(BF16), dgate], axis=1)
            w_in_name = 'ssm_w_in'
        else:
            do, dgate = _fox_gate_bwd(da, sv['o'], proj)
            qa, ka, qat, kat, v, _, c_row, c_lanes = sv['fox_ops']
            dot = do.reshape(s, FOX_HEADS, HEAD_DIM).transpose(1, 2, 0)
            dqt, dcq, dkt, dck, dvt = _fox_attention_bwd(qa, ka, qat, kat, v, c_row, c_lanes, sv['ot'], dot, sv['lse'])
            dq, dk, dv = unheads(dqt), unheads(dkt), unheads(dvt)
            dcs = jnp.pad((dcq[:, 0] + dck[:, 0]).T, ((0, 0), (0, LANES - FOX_HEADS)))
            dz, dbfg = _fg_bwd(sv['z'], bfg, dcs)
            grads['fox_b_fg'] = dbfg[:, :FOX_HEADS]
            grads['fox_w_fg'] = _mm(sv['hn'], dz, ta=True, out_dtype=BF16, name="fox_fg_dw")[:, :FOX_HEADS][None]
            dhs.append(_mm(dz, wfg, tb=True, out_dtype=F32, name="fox_fg_dx"))
            dproj = jnp.concatenate([dq, dk, dv, dgate], axis=1)
            w_in_name = 'fox_w_in'
        grads[w_in_name] = _mm(sv['hn'], dproj, ta=True, out_split=N_CHIPS, name=f"mixer_in_dw{i}")[:, None]
        dhs.append(_mm(dproj, w[w_in_name][:, 0], tb=True, b_split=True, out_dtype=F32, name=f"mixer_in_dx{i}"))
        dx, _, g_norm[i] = _norm_bwd(sv['x'], w['norm_g'][i:i + 1], dhs, dx1, f"norm_bwd{i}")

    grads['norm_g'] = jnp.concatenate(g_norm, axis=0)
    grads['ple_norm'] = jnp.concatenate(g_ple_norm, axis=0)
    grads['ple_proj'] = jnp.stack(g_ple_proj, axis=1)
    grads['ple_gate'] = jnp.stack(g_ple_gate)
    return loss, dx, grads


def _bf16_terms(a):
    hi = lax.reduce_precision(a, 8, 7)
    r1 = a - hi
    mid = lax.reduce_precision(r1, 8, 7)
    lo = lax.reduce_precision(r1 - mid, 8, 7)
    return hi.astype(BF16), mid.astype(BF16), lo.astype(BF16)


def _split3(a):
    return jnp.concatenate(_bf16_terms(a), axis=0)


def _bias_grad(dbias, onehot):
    out = _mm(_split3(dbias), onehot, out_dtype=F32, name="rel_bias_grad", tk=2048)
    nh = dbias.shape[0]
    return (out[:nh] + out[nh:2 * nh] + out[2 * nh:]).T


def _bias_table(rel_bias, onehot_t):
    nh = rel_bias.shape[1]
    out = _mm(_split3(rel_bias.T), onehot_t, out_dtype=F32, name="rel_bias_table")
    return (out[:nh] + out[nh:2 * nh] + out[2 * nh:]).reshape(nh, BLOCK, 2 * BLOCK)


WEIGHTS = ['norm_g', 'final_g', 'rel_bias', 'swa_w_in', 'swa_w_out', 'swa_sinks', 'conv_w_in', 'conv_kernel', 'conv_w_out',
           'ssm_w_in', 'ssm_lam_re', 'ssm_lam_im', 'ssm_log_dt', 'ssm_b_re', 'ssm_b_im', 'ssm_c_re', 'ssm_c_im', 'ssm_d',
           'ssm_w_glu', 'ssm_b_glu', 'ssm_w_out', 'fox_w_in', 'fox_w_fg', 'fox_b_fg', 'fox_w_out', 'ple_proj', 'ple_norm',
           'ple_gate']
BIG = {'swa_w_in': 2, 'swa_w_out': 1, 'conv_w_in': 2, 'conv_w_out': 1, 'ssm_w_in': 2, 'ssm_w_glu': 2, 'ssm_w_out': 1,
       'fox_w_in': 2, 'fox_w_fg': 1, 'fox_w_out': 1, 'ple_proj': 2, 'ple_gate': 1}
SMALL = {'conv_kernel': 2, 'ssm_d': 1, 'ssm_b_glu': 1}
REPLICATED = [n for n in WEIGHTS if n not in BIG and n not in SMALL]
N_CHIPS = 4
N_DEV = 8
SMALL_ROWS = 8
REPL_ROWS = 64


def _flat(pieces, dtype, lead, row_mult):
    flat = jnp.concatenate([q.astype(dtype) for q in pieces], axis=-1)
    pad = (-flat.shape[-1]) % (row_mult * FLAT_W)
    flat = jnp.pad(flat, [(0, 0)] * len(lead) + [(0, pad)])
    return flat.reshape(*lead, -1, FLAT_W)


def _unflat(flat, lead_ndim, sizes):
    lead = flat.shape[:lead_ndim]
    flat = flat.reshape(*lead, -1)
    out, off = [], 0
    for n in sizes:
        out.append(flat[..., off:off + n])
        off += n
    return out


def _split_shards(full, axis):
    shp = full.shape
    parts = full.reshape(shp[:axis] + (N_CHIPS, shp[axis] // N_CHIPS) + shp[axis + 1:])
    return jnp.moveaxis(parts, axis, 0)


def _join_shards(parts, axis):
    moved = jnp.moveaxis(parts, 0, axis)
    shp = moved.shape
    return moved.reshape(shp[:axis] + (shp[axis] * shp[axis + 1],) + shp[axis + 2:])


def _coords():
    return lax.axis_index("x"), lax.axis_index("y"), lax.axis_index("c")


def _remote(k, src, dst, to, send_sems, recv_sems):
    return pltpu.make_async_remote_copy(src_ref=src, dst_ref=dst, send_sem=send_sems.at[k], recv_sem=recv_sems.at[k],
                                        device_id=to, device_id_type=MESH)


def _gather_weights(bufs, ssh):
    nb = len(bufs)

    def body(*refs):
        w_refs, s_ref = refs[:nb], refs[nb]
        wouts, sout = refs[nb + 1:2 * nb + 1], refs[2 * nb + 1]
        send_sems, recv_sems = refs[2 * nb + 2:]
        x, y, c = _coords()
        me = 2 * x + y
        chips = [(1 - x, y), (x, 1 - y), (1 - x, 1 - y)]
        rc = functools.partial(_remote, send_sems=send_sems, recv_sems=recv_sems)
        sends = []
        for j, (cx, cy) in enumerate(chips):
            for b in range(nb):
                sends.append(rc(6 * b + j, w_refs[b].at[c], wouts[b].at[me, c], (cx, cy, c)))
            sends.append(rc(6 * nb + j, s_ref, sout.at[me], (cx, cy, c)))
        for cp in sends:
            cp.start()
        for j, (cx, cy) in enumerate(chips):
            k = 2 * cx + cy
            for b in range(nb):
                rc(6 * b + j, w_refs[b].at[c], wouts[b].at[k, c], (x, y, c)).wait_recv()
                fwd = rc(6 * b + 3 + j, wouts[b].at[k, c], wouts[b].at[k, c], (x, y, 1 - c))
                fwd.start()
                sends.append(fwd)
        for j, (cx, cy) in enumerate(chips):
            k = 2 * cx + cy
            for b in range(nb):
                rc(6 * b + 3 + j, w_refs[b].at[c], wouts[b].at[k, 1 - c], (x, y, c)).wait_recv()
            rc(6 * nb + j, s_ref, sout.at[k], (x, y, c)).wait_recv()
        for cp in sends:
            cp.wait_send()

    nsem = 6 * nb + 3
    res = pl.pallas_call(
        body, name="gather_weights", in_specs=[ANY] * (nb + 1), out_specs=[ANY] * (nb + 1),
        out_shape=[jax.ShapeDtypeStruct((N_CHIPS,) + a.shape, a.dtype) for a in (*bufs, ssh)],
        scratch_shapes=[pltpu.SemaphoreType.DMA((nsem,)), pltpu.SemaphoreType.DMA((nsem,))],
    )(*bufs, ssh)
    return res[:nb], res[nb]


def _pair_exchange(gbufs):
    nb = len(gbufs)

    def body(*refs):
        g_refs, outs = refs[:nb], refs[nb:2 * nb]
        send_sems, recv_sems = refs[2 * nb:]
        x, y, c = _coords()
        rc = functools.partial(_remote, send_sems=send_sems, recv_sems=recv_sems)
        sends = [rc(N_CHIPS * b + j, g_refs[b].at[2 * j + 1 - c], outs[b].at[j], (x, y, 1 - c))
                 for b in range(nb) for j in range(N_CHIPS)]
        for cp in sends:
            cp.start()
        for b in range(nb):
            for j in range(N_CHIPS):
                rc(N_CHIPS * b + j, g_refs[b].at[2 * j + c], outs[b].at[j], (x, y, c)).wait_recv()
        for cp in sends:
            cp.wait_send()

    nsem = N_CHIPS * nb
    return pl.pallas_call(
        body, name="pair_exchange", in_specs=[ANY] * nb, out_specs=[ANY] * nb,
        out_shape=[jax.ShapeDtypeStruct((N_CHIPS,) + g.shape[1:], g.dtype) for g in gbufs],
        scratch_shapes=[pltpu.SemaphoreType.DMA((nsem,)), pltpu.SemaphoreType.DMA((nsem,))],
    )(*gbufs)


def _pair_sum(mine, theirs, name):
    n, r, w = mine.shape
    t = _tile(r, 256, 16)
    spec = lambda a: (a, (n, t, w), lambda i: (0, i, 0))
    return _tiled(lambda a, b: (a.astype(F32) + b.astype(F32),), [spec(mine), spec(theirs)],
                  [_out((n, r, w), BF16, (n, t, w), lambda i: (0, i, 0))], r // t, name)[0]


def _exchange_grads(pbufs, gsmall, grepl):
    nb = len(pbufs)

    def body(*refs):
        g_refs, s_ref, r_ref = refs[:nb], refs[nb], refs[nb + 1]
        ogs, osm, orp = refs[nb + 2:2 * nb + 2], refs[2 * nb + 2], refs[2 * nb + 3]
        send_sems, recv_sems, local_sems = refs[2 * nb + 4:]
        x, y, c = _coords()
        me = 4 * x + 2 * y + c
        my_chip = 2 * x + y
        rc = functools.partial(_remote, send_sems=send_sems, recv_sems=recv_sems)
        local = [pltpu.make_async_copy(s_ref.at[me], osm.at[me], local_sems.at[0]),
                 pltpu.make_async_copy(r_ref, orp.at[me], local_sems.at[1])]
        for cp in local:
            cp.start()
        peers = []
        for d in range(1, N_DEV):
            px = 1 - x if d & 4 else x
            py = 1 - y if d & 2 else y
            pc = 1 - c if d & 1 else c
            peers.append((d, 4 * px + 2 * py + pc, 2 * px + py, (px, py, pc)))
        per_peer = nb + 2
        sends = []
        for i, (d, peer, chip, to) in enumerate(peers):
            sends.append(rc(per_peer * i + nb, s_ref.at[peer], osm.at[me], to))
            sends.append(rc(per_peer * i + nb + 1, r_ref, orp.at[me], to))
            if d & 1 == 0:
                for b in range(nb):
                    sends.append(rc(per_peer * i + b, g_refs[b].at[chip], ogs[b].at[my_chip], to))
        for cp in sends:
            cp.start()
        for i, (d, peer, chip, to) in enumerate(peers):
            rc(per_peer * i + nb, s_ref.at[peer], osm.at[peer], to).wait_recv()
            rc(per_peer * i + nb + 1, r_ref, orp.at[peer], to).wait_recv()
            if d & 1 == 0:
                for b in range(nb):
                    rc(per_peer * i + b, g_refs[b].at[chip], ogs[b].at[chip], to).wait_recv()
        for cp in sends:
            cp.wait_send()
        for cp in local:
            cp.wait()

    nsem = (nb + 2) * (N_DEV - 1)
    res = pl.pallas_call(
        body, name="exchange_grads", in_specs=[ANY] * (nb + 2), out_specs=[ANY] * (nb + 2),
        out_shape=[jax.ShapeDtypeStruct(a.shape, a.dtype) for a in (*pbufs, gsmall)]
        + [jax.ShapeDtypeStruct((N_DEV,) + grepl.shape, grepl.dtype)],
        scratch_shapes=[pltpu.SemaphoreType.DMA((nsem,)), pltpu.SemaphoreType.DMA((nsem,)), pltpu.SemaphoreType.DMA((2,))],
    )(*pbufs, gsmall, grepl)
    return res[:nb], res[nb], res[nb + 1]


def _sibling_exchange(halves):
    nb = len(halves)

    def body(*refs):
        ins, outs = refs[:nb], refs[nb:2 * nb]
        send_sems, recv_sems = refs[2 * nb:]
        x, y, c = _coords()
        rc = functools.partial(_remote, send_sems=send_sems, recv_sems=recv_sems)
        sends = [rc(b, ins[b], outs[b], (x, y, 1 - c)) for b in range(nb)]
        for cp in sends:
            cp.start()
        for b in range(nb):
            rc(b, ins[b], outs[b], (x, y, c)).wait_recv()
        for cp in sends:
            cp.wait_send()

    return pl.pallas_call(
        body, name="sibling_exchange", in_specs=[ANY] * nb, out_specs=[ANY] * nb,
        out_shape=[jax.ShapeDtypeStruct(a.shape, a.dtype) for a in halves],
        scratch_shapes=[pltpu.SemaphoreType.DMA((nb,)), pltpu.SemaphoreType.DMA((nb,))],
    )(*halves)


def _sum_senders(recv, name):
    n, r, w = recv.shape
    t = _tile(r, 256, 8)

    def fn(v):
        acc = v[0].astype(F32)
        for i in range(1, n):
            acc = acc + v[i].astype(F32)
        return (acc,)

    return _tiled(fn, [(recv, (n, t, w), lambda i: (0, i, 0))], [_orows(r, w, F32, t)], r // t, name)[0]


def _adamw(w, g, m, v, name):
    r, wd = w.shape
    t = _tile(r, 256, 8)

    def fn(wv, gv, mv, vv):
        m2 = ADAM_B1 * mv + (1.0 - ADAM_B1) * gv
        v2 = ADAM_B2 * vv + (1.0 - ADAM_B2) * (gv * gv)
        m_hat = m2 / (1.0 - ADAM_B1 ** ADAM_STEP)
        v_hat = v2 / (1.0 - ADAM_B2 ** ADAM_STEP)
        delta = -ADAM_LR * (m_hat / (jnp.sqrt(v_hat) + ADAM_EPS) + ADAM_WD * wv)
        return delta, m2, v2

    return _tiled(fn, [_rows(a, t) for a in (w, g, m, v)], [_orows(r, wd, F32, t)] * 3, r // t, name)


def kernel(x, p, norm_g, final_g, rel_bias, swa_w_in, swa_w_out, swa_sinks, conv_w_in, conv_kernel, conv_w_out, ssm_w_in, ssm_lam_re, ssm_lam_im, ssm_log_dt, ssm_b_re, ssm_b_im, ssm_c_re, ssm_c_im, ssm_d, ssm_w_glu, ssm_b_glu, ssm_w_out, fox_w_in, fox_w_fg, fox_b_fg, fox_w_out, ple_proj, ple_norm, ple_gate, loss_target, m_norm_g, m_final_g, m_rel_bias, m_swa_w_in, m_swa_w_out, m_swa_sinks, m_conv_w_in, m_conv_kernel, m_conv_w_out, m_ssm_w_in, m_ssm_lam_re, m_ssm_lam_im, m_ssm_log_dt, m_ssm_b_re, m_ssm_b_im, m_ssm_c_re, m_ssm_c_im, m_ssm_d, m_ssm_w_glu, m_ssm_b_glu, m_ssm_w_out, m_fox_w_in, m_fox_w_fg, m_fox_b_fg, m_fox_w_out, m_ple_proj, m_ple_norm, m_ple_gate, v_norm_g, v_final_g, v_rel_bias, v_swa_w_in, v_swa_w_out, v_swa_sinks, v_conv_w_in, v_conv_kernel, v_conv_w_out, v_ssm_w_in, v_ssm_lam_re, v_ssm_lam_im, v_ssm_log_dt, v_ssm_b_re, v_ssm_b_im, v_ssm_c_re, v_ssm_c_im, v_ssm_d, v_ssm_w_glu, v_ssm_b_glu, v_ssm_w_out, v_fox_w_in, v_fox_w_fg, v_fox_b_fg, v_fox_w_out, v_ple_proj, v_ple_norm, v_ple_gate):
    given = dict(locals())
    shard_shape = {n: given[n].shape for n in WEIGHTS}
    half = {n: math.prod(shard_shape[n]) // 2 for n in WEIGHTS}

    widths = sorted({shard_shape[n][-1] for n in BIG}, reverse=True)
    classes = [[n for n in BIG if shard_shape[n][-1] == w] for w in widths]
    rows = {n: math.prod(shard_shape[n][:-1]) for n in BIG}
    core = lax.axis_index("c")
    my_chip = 2 * lax.axis_index("x") + lax.axis_index("y")

    bufs = [jnp.concatenate([given[n].astype(BF16).reshape(rows[n], w) for n in names], axis=0).reshape(2, -1, w)
            for w, names in zip(widths, classes)]
    ssh = _flat([given[n].reshape(-1) for n in SMALL], F32, (), SMALL_ROWS)
    walls, sall = _gather_weights(bufs, ssh)
    walls = [lax.dynamic_update_slice(a, b[None], (my_chip, 0, 0, 0)) for a, b in zip(walls, bufs)]
    sall = lax.dynamic_update_slice(sall, ssh[None], (my_chip, 0, 0))
    full = {n: given[n] for n in REPLICATED}
    for w, names, wall in zip(widths, classes, walls):
        wall, off = wall.reshape(N_CHIPS, -1, w), 0
        for n in names:
            stacked = wall[:, off:off + rows[n]].reshape((N_CHIPS,) + shard_shape[n])
            full[n] = stacked if BIG[n] == 2 else _join_shards(stacked, BIG[n])
            off += rows[n]
    for n, piece in zip(SMALL, _unflat(sall, 1, [2 * half[n] for n in SMALL])):
        full[n] = _join_shards(piece.reshape((N_CHIPS,) + shard_shape[n]), SMALL[n])

    loss, dx, grads = _local_step(x[0], p[:, 0], loss_target[0], full)

    by_chip = lambda n: grads[n] if BIG[n] == 2 else _split_shards(grads[n], BIG[n])
    gbufs = [jnp.concatenate([by_chip(n).reshape(N_CHIPS, rows[n], w) for n in names], axis=1).reshape(N_DEV, -1, w)
             for w, names in zip(widths, classes)]
    gsmall = _flat([_split_shards(grads[n], SMALL[n]).reshape(N_DEV, -1) for n in SMALL], F32, (N_DEV,), SMALL_ROWS)
    grepl = _flat([grads[n].reshape(-1) for n in REPLICATED], F32, (), REPL_ROWS)
    pbufs = [_pair_sum(jnp.where(core == 0, g[0::2], g[1::2]), t, f"pair_sum_w{w}")
             for w, g, t in zip(widths, gbufs, _pair_exchange(gbufs))]
    recv_bufs, recv_small, recv_repl = _exchange_grads(pbufs, gsmall, grepl)
    recv_bufs = [lax.dynamic_update_slice(r, lax.dynamic_index_in_dim(pb, my_chip, axis=0), (my_chip, 0, 0))
                 for r, pb in zip(recv_bufs, pbufs)]
    halves = [_sum_senders(r, f"sum_w{w}") for w, r in zip(widths, recv_bufs)] + [_sum_senders(recv_small, "sum_small")]
    others = _sibling_exchange(halves)
    g_repl = _sum_senders(recv_repl, "sum_repl")
    both = [(jnp.where(core == 0, a, b), jnp.where(core == 0, b, a)) for a, b in zip(halves, others)]
    sfull = jnp.stack(both[-1])

    out_g, out_d, out_m, out_v = {}, {}, {}, {}
    for w, names, (lower, upper) in zip(widths, classes, both):
        gfull, off = jnp.concatenate([lower, upper], axis=0), 0
        for n in names:
            as_rows = lambda t, w=w: t.reshape(-1, w)
            piece = gfull[off:off + rows[n]]
            off += rows[n]
            d, m2, v2 = _adamw(as_rows(given[n]), piece, as_rows(given['m_' + n]), as_rows(given['v_' + n]), f"adamw_{n}")
            out_g[n], out_d[n], out_m[n], out_v[n] = (t.reshape(shard_shape[n]) for t in (piece, d, m2, v2))

    pack_small = lambda pre: _flat([given[pre + n].reshape(2, -1) for n in SMALL], F32, (2,), SMALL_ROWS).reshape(-1, FLAT_W)
    res = _adamw(pack_small(''), sfull.reshape(-1, FLAT_W), pack_small('m_'), pack_small('v_'), "adamw_small")
    for dst, flat in zip((out_g, out_d, out_m, out_v), (sfull,) + tuple(res)):
        for n, piece in zip(SMALL, _unflat(flat.reshape(2, -1, FLAT_W), 1, [half[n] for n in SMALL])):
            dst[n] = piece.reshape(shard_shape[n])

    pack_repl = lambda pre: _flat([given[pre + n].reshape(-1) for n in REPLICATED], F32, (), REPL_ROWS)
    res = _adamw(pack_repl(''), g_repl, pack_repl('m_'), pack_repl('v_'), "adamw_repl")
    for dst, flat in zip((out_g, out_d, out_m, out_v), (g_repl,) + tuple(res)):
        for n, piece in zip(REPLICATED, _unflat(flat, 0, [2 * half[n] for n in REPLICATED])):
            dst[n] = piece.reshape(shard_shape[n])

    total = lax.psum(loss[0, 0], ("x", "y", "c"))
    return (total, dx[None], *[out_g[n] for n in WEIGHTS], *[out_d[n] for n in WEIGHTS],
            *[out_m[n] for n in WEIGHTS], *[out_v[n] for n in WEIGHTS])
```

```python
import functools
import math

import numpy as np
import jax
import jax.numpy as jnp
from jax import lax
from jax.experimental import pallas as pl
from jax.experimental.pallas import tpu as pltpu

F32 = jnp.float32
BF16 = jnp.bfloat16

EPS = 1e-6
BLOCK = 128
REL_BUCKETS = 32
REL_MAX_DIST = 128
SWA_HEADS, SWA_KV_HEADS, HEAD_DIM = 32, 4, 64
SWA_GROUP = SWA_HEADS // SWA_KV_HEADS
FOX_HEADS = 32
SSM_GROUP, SSM_STATE = 16, 64
GROUPS_PER_STEP = 8
STATE_W = GROUPS_PER_STEP * SSM_STATE
LANES = 128
NEG = -1e30

ADAM_LR, ADAM_B1, ADAM_B2, ADAM_EPS, ADAM_WD, ADAM_STEP = 0.001, 0.9, 0.999, 1e-08, 0.01, 10

VMEM_LIMIT_V7X = 56 * 1024 * 1024
FLAT_W = 1024
MESH = pl.DeviceIdType.MESH
ANY = pl.BlockSpec(memory_space=pl.ANY)


def _cp(sem):
    return pltpu.CompilerParams(dimension_semantics=sem, vmem_limit_bytes=VMEM_LIMIT_V7X)


def _tile(n, target, mult=LANES):
    if n <= target:
        return n
    t = (target // mult) * mult
    while t >= mult:
        if n % t == 0:
            return t
        t -= mult
    return n


MAX_WHOLE_TILE = 1152


def _div(a, b):
    return lax.div(a, jnp.int32(b))


def _rem(a, b):
    return lax.rem(a, jnp.int32(b))


def _mm(a, b, *, ta=False, tb=False, out_dtype=BF16, name, tm=1024, tn=1024, tk=4096, b_split=False, out_split=0):
    if ta:
        kdim, m = a.shape
    else:
        m, kdim = a.shape
    parts = b.shape[0] if b_split else 1
    b_rows, b_cols = (b.shape[1], b.shape[2] * parts) if b_split else b.shape
    n = b_rows if tb else b_cols
    assert (b_cols if tb else b_rows) == kdim
    n_range = n // (out_split or (1 if tb else parts))
    k_range = kdim // (parts if tb else 1)
    tm, tk = _tile(m, tm), _tile(k_range, tk)
    tn = n_range if n_range <= MAX_WHOLE_TILE else _tile(n_range, tn)
    nk = kdim // tk
    dn = (((0 if ta else 1,), (1 if tb else 0,)), ((), ()))

    def body(a_ref, b_ref, o_ref, *acc):
        part = lax.dot_general(a_ref[...].astype(BF16), b_ref[...].astype(BF16), dn, preferred_element_type=F32)
        if nk == 1:
            o_ref[...] = part.astype(o_ref.dtype)
            return
        acc_ref, = acc
        k = pl.program_id(2)

        @pl.when(k == 0)
        def _():
            acc_ref[...] = part

        @pl.when(k > 0)
        def _():
            acc_ref[...] += part

        @pl.when(k == nk - 1)
        def _():
            o_ref[...] = acc_ref[...].astype(o_ref.dtype)

    a_spec = pl.BlockSpec((tk, tm), lambda i, j, k: (k, i)) if ta else pl.BlockSpec((tm, tk), lambda i, j, k: (i, k))
    nj, nkr = n_range // tn, k_range // tk
    if not b_split:
        b_spec = pl.BlockSpec((tn, tk), lambda i, j, k: (j, k)) if tb else pl.BlockSpec((tk, tn), lambda i, j, k: (k, j))
    elif tb:
        b_spec = pl.BlockSpec((None, tn, tk), lambda i, j, k: (_div(k, nkr), j, _rem(k, nkr)))
    else:
        b_spec = pl.BlockSpec((None, tk, tn), lambda i, j, k: (_div(j, nj), k, _rem(j, nj)))
    if out_split:
        out_spec = pl.BlockSpec((None, tm, tn), lambda i, j, k: (_div(j, nj), i, _rem(j, nj)))
        out_shape = jax.ShapeDtypeStruct((out_split, m, n_range), out_dtype)
    else:
        out_spec = pl.BlockSpec((tm, tn), lambda i, j, k: (i, j))
        out_shape = jax.ShapeDtypeStruct((m, n), out_dtype)
    return pl.pallas_call(
        body, name=name, grid=(m // tm, n // tn, nk),
        in_specs=[a_spec, b_spec], out_specs=out_spec, out_shape=out_shape,
        scratch_shapes=[pltpu.VMEM((tm, tn), F32)] if nk > 1 else [],
        compiler_params=_cp(("parallel", "parallel", "arbitrary")),
    )(a, b)


def _rows(arr, t):
    return (arr, (t, arr.shape[1]), lambda i: (i, 0))


def _cols(arr, cb, off=0):
    return (arr, (arr.shape[0], cb), lambda i: (0, i + off))


def _full(arr):
    nd = arr.ndim
    return (arr, arr.shape, lambda i: (0,) * nd)


def _out(shape, dtype, block, imap, acc=False):
    return (jax.ShapeDtypeStruct(shape, dtype), block, imap, acc)


def _orows(s, w, dtype, t):
    return _out((s, w), dtype, (t, w), lambda i: (i, 0))


def _ocols(s, w, dtype, cb):
    return _out((s, w), dtype, (s, cb), lambda i: (0, i))


def _oacc(shape):
    nd = len(shape)
    return _out(shape, F32, shape, lambda i: (0,) * nd, True)


def _tiled(fn, ins, outs, n, name):
    has_acc = any(o[3] for o in outs)
    ni = len(ins)

    def body(*refs):
        vals = fn(*[r[...] for r in refs[:ni]])
        i = pl.program_id(0)
        for r, v, o in zip(refs[ni:], vals, outs):
            if o[3]:
                @pl.when(i == 0)
                def _(r=r, v=v):
                    r[...] = v.astype(r.dtype)

                @pl.when(i > 0)
                def _(r=r, v=v):
                    r[...] += v.astype(r.dtype)
            else:
                r[...] = v.astype(r.dtype)

    res = pl.pallas_call(
        body, name=name, grid=(n,),
        in_specs=[pl.BlockSpec(b, m) for _, b, m in ins],
        out_specs=[pl.BlockSpec(b, m) for _, b, m, _ in outs],
        out_shape=[s for s, _, _, _ in outs],
        compiler_params=_cp(("arbitrary",) if has_acc else ("parallel",)),
    )(*[a for a, _, _ in ins])
    return res


def _silu(x):
    return x * jax.nn.sigmoid(x)


def _gelu(x):
    return 0.5 * x * (1.0 + jnp.tanh(math.sqrt(2.0 / math.pi) * (x + 0.044715 * (x * x * x))))


def _rms(x, g):
    r = lax.rsqrt(jnp.mean(x * x, axis=-1, keepdims=True) + EPS)
    return x * r * g


def _shift_rows(x, sh, up=False):
    s = x.shape[0]
    rows = lax.broadcasted_iota(jnp.int32, x.shape, 0)
    if up:
        return jnp.where(rows < s - sh, pltpu.roll(x, s - sh, 0), 0.0)
    return jnp.where(rows >= sh, pltpu.roll(x, sh, 0), 0.0)


SCAN_CHUNK = 128


def _scan_complex(xr, xi, pr, pi, tr, ti, reverse):
    s, lanes = xr.shape
    c = min(SCAN_CHUNK, s)
    nchunk = s // c
    in_chunk = lax.broadcasted_iota(jnp.int32, xr.shape, 0) & (c - 1)

    def shifted(x, sh):
        if reverse:
            return jnp.where(in_chunk < c - sh, pltpu.roll(x, s - sh, 0), 0.0)
        return jnp.where(in_chunk >= sh, pltpu.roll(x, sh, 0), 0.0)

    k = 0
    while (1 << k) < c:
        sr, si = shifted(xr, 1 << k), shifted(xi, 1 << k)
        ar, ai = pr[k:k + 1, :], pi[k:k + 1, :]
        xr, xi = xr + ar * sr - ai * si, xi + ar * si + ai * sr
        k += 1
    if nchunk == 1:
        return xr, xi
    xr, xi = xr.reshape(nchunk, c, lanes), xi.reshape(nchunk, c, lanes)
    edge = 0 if reverse else c - 1
    er, ei = xr[:, edge, :], xi[:, edge, :]
    m = 0
    while (1 << m) < nchunk:
        sr, si = _shift_rows(er, 1 << m, reverse), _shift_rows(ei, 1 << m, reverse)
        ar, ai = pr[k + m:k + m + 1, :], pi[k + m:k + m + 1, :]
        er, ei = er + ar * sr - ai * si, ei + ar * si + ai * sr
        m += 1
    cr, ci = _shift_rows(er, 1, reverse)[:, None, :], _shift_rows(ei, 1, reverse)[:, None, :]
    xr, xi = xr + tr[None] * cr - ti[None] * ci, xi + tr[None] * ci + ti[None] * cr
    return xr.reshape(s, lanes), xi.reshape(s, lanes)


def _cumsum_rows(x, reverse):
    s = x.shape[0]
    k = 0
    while (1 << k) < s:
        x = x + _shift_rows(x, 1 << k, reverse)
        k += 1
    return x


def _dot(a, b, ca=1, cb=0):
    return lax.dot_general(a.astype(BF16), b.astype(BF16), (((ca,), (cb,)), ((), ())), preferred_element_type=F32)


def _t5_bucket(dist):
    max_exact = REL_BUCKETS // 2
    d = np.maximum(dist, 1).astype(np.float32)
    large = max_exact + (np.log(d / max_exact) / np.log(REL_MAX_DIST / max_exact) * (REL_BUCKETS - max_exact)).astype(np.int32)
    large = np.minimum(large, REL_BUCKETS - 1)
    return np.where(dist < max_exact, dist, large).astype(np.int32)


def _swa_bucket_table():
    qi = np.arange(BLOCK)[:, None]
    kj = np.arange(2 * BLOCK)[None, :]
    return _t5_bucket(np.clip(qi + BLOCK - kj, 0, None))


SWA_BWD_HEADS = 2


def _stack_heads(x):
    return jnp.concatenate([x[:, g * HEAD_DIM:(g + 1) * HEAD_DIM] for g in range(SWA_GROUP)], axis=0)


def _unstack_heads(x):
    return jnp.concatenate([x[g * BLOCK:(g + 1) * BLOCK] for g in range(SWA_GROUP)], axis=1)


def _sink_rows(sink_ref):
    return jnp.concatenate([jnp.broadcast_to(sink_ref[0, g:g + 1, :1], (BLOCK, 1)) for g in range(SWA_GROUP)], axis=0)


def _swa_scores(qg, kb, bias_g, sk, n):
    s = _dot(qg, kb, 1, 1) * (HEAD_DIM ** -0.5) + bias_g
    row = lax.broadcasted_iota(jnp.int32, s.shape, 0) & (BLOCK - 1)
    col = lax.broadcasted_iota(jnp.int32, s.shape, 1)
    dist = row + BLOCK - col
    mask = (dist >= 0) & (dist < BLOCK) & ((col >= BLOCK) | (n > 0))
    s = jnp.where(mask, s, NEG)
    m = jnp.maximum(jnp.max(s, axis=1, keepdims=True), sk)
    e = jnp.exp(s - m)
    es = jnp.exp(sk - m)
    den = jnp.sum(e, axis=1, keepdims=True) + es
    return e / den, es / den


def _swa_specs(proj, k4, v4, nb, clamp):
    gw = SWA_GROUP * HEAD_DIM
    gate_off = (SWA_HEADS * HEAD_DIM + 2 * SWA_KV_HEADS * HEAD_DIM) // gw
    cur = (lambda n: jnp.minimum(n, nb - 1)) if clamp else (lambda n: n)
    prev = lambda n: jnp.maximum(cur(n) - 1, 0)
    return [
        pl.BlockSpec((BLOCK, gw), lambda h, n: (cur(n), h)),
        pl.BlockSpec((1, BLOCK, HEAD_DIM), lambda h, n: (h, cur(n), 0)),
        pl.BlockSpec((1, BLOCK, HEAD_DIM), lambda h, n: (h, prev(n), 0)),
        pl.BlockSpec((1, BLOCK, HEAD_DIM), lambda h, n: (h, cur(n), 0)),
        pl.BlockSpec((1, BLOCK, HEAD_DIM), lambda h, n: (h, prev(n), 0)),
        pl.BlockSpec((BLOCK, gw), lambda h, n: (cur(n), gate_off + h)),
    ], [proj, k4, k4, v4, v4, proj]


def _swa_fwd(proj, k4, v4, bias, sinks):
    s = proj.shape[0]
    nb = s // BLOCK
    gw = SWA_GROUP * HEAD_DIM

    def body(q_ref, kc_ref, kp_ref, vc_ref, vp_ref, gate_ref, bias_ref, sink_ref, a_ref):
        n = pl.program_id(1)
        kb = jnp.concatenate([kp_ref[0], kc_ref[0]], axis=0)
        vb = jnp.concatenate([vp_ref[0], vc_ref[0]], axis=0)
        p, _ = _swa_scores(_stack_heads(q_ref[...]), kb, bias_ref[...].reshape(-1, 2 * BLOCK), _sink_rows(sink_ref), n)
        a_ref[...] = (_unstack_heads(_dot(p, vb)) * _silu(gate_ref[...].astype(F32))).astype(a_ref.dtype)

    specs, args = _swa_specs(proj, k4, v4, nb, False)
    return pl.pallas_call(
        body, name="swa_fwd", grid=(SWA_KV_HEADS, nb),
        in_specs=specs + [pl.BlockSpec((SWA_GROUP, BLOCK, 2 * BLOCK), lambda h, n: (h, 0, 0)),
                          pl.BlockSpec((1, SWA_GROUP, LANES), lambda h, n: (h, 0, 0))],
        out_specs=pl.BlockSpec((BLOCK, gw), lambda h, n: (n, h)),
        out_shape=jax.ShapeDtypeStruct((s, SWA_HEADS * HEAD_DIM), BF16),
        compiler_params=_cp(("parallel", "parallel")),
    )(*args, bias, sinks)


def _swa_bwd(proj, k4, v4, bias, sinks, da):
    s = proj.shape[0]
    nb = s // BLOCK
    gw = SWA_GROUP * HEAD_DIM

    def body(q_ref, kc_ref, kp_ref, vc_ref, vp_ref, gate_ref, bias_ref, sink_ref, da_ref,
             dq_ref, dgate_ref, dk_ref, dv_ref, dbias_ref, dsink_ref, dk_own, dv_own):
        n = pl.program_id(1)

        @pl.when(n == 0)
        def _():
            dk_own[...] = jnp.zeros_like(dk_own)
            dv_own[...] = jnp.zeros_like(dv_own)
            dbias_ref[...] = jnp.zeros_like(dbias_ref)
            dsink_ref[...] = jnp.zeros_like(dsink_ref)

        @pl.when(n < nb)
        def _():
            kb = jnp.concatenate([kp_ref[0], kc_ref[0]], axis=0)
            vb = jnp.concatenate([vp_ref[0], vc_ref[0]], axis=0)
            dkb = jnp.zeros((2 * BLOCK, HEAD_DIM), F32)
            dvb = jnp.zeros((2 * BLOCK, HEAD_DIM), F32)
            for g0 in range(0, SWA_GROUP, SWA_BWD_HEADS):
                sl = slice(g0 * HEAD_DIM, (g0 + SWA_BWD_HEADS) * HEAD_DIM)
                stack = lambda x: jnp.concatenate([x[:, i * HEAD_DIM:(i + 1) * HEAD_DIM] for i in range(SWA_BWD_HEADS)], axis=0)
                unstack = lambda x: jnp.concatenate([x[i * BLOCK:(i + 1) * BLOCK] for i in range(SWA_BWD_HEADS)], axis=1)
                q2 = stack(q_ref[:, sl])
                sk = jnp.concatenate([jnp.broadcast_to(sink_ref[0, g0 + i:g0 + i + 1, :1], (BLOCK, 1))
                                      for i in range(SWA_BWD_HEADS)], axis=0)
                p, p0 = _swa_scores(q2, kb, bias_ref[g0:g0 + SWA_BWD_HEADS].reshape(-1, 2 * BLOCK), sk, n)
                o2 = _dot(p, vb)
                gate = gate_ref[:, sl].astype(F32)
                dag = da_ref[:, sl].astype(F32)
                sg = jax.nn.sigmoid(gate)
                do2 = stack(dag * gate * sg)
                dgate_ref[:, sl] = (dag * unstack(o2) * sg * (1.0 + gate * (1.0 - sg))).astype(dgate_ref.dtype)
                delta = jnp.sum(do2 * o2, axis=1, keepdims=True)
                ds = p * (_dot(do2, vb, 1, 1) - delta)
                dbias_ref[g0:g0 + SWA_BWD_HEADS] += ds.reshape(SWA_BWD_HEADS, BLOCK, 2 * BLOCK)
                sink_part = p0 * delta
                for i in range(SWA_BWD_HEADS):
                    dsink_ref[0, g0 + i:g0 + i + 1, :] += (jnp.zeros((1, LANES), F32)
                                                           - jnp.sum(sink_part[i * BLOCK:(i + 1) * BLOCK], axis=0, keepdims=True))
                dq_ref[:, sl] = (unstack(_dot(ds, kb)) * (HEAD_DIM ** -0.5)).astype(dq_ref.dtype)
                dkb += _dot(ds, q2, 0, 0) * (HEAD_DIM ** -0.5)
                dvb += _dot(p, do2, 0, 0)
            dk_ref[0] = dk_own[...] + dkb[:BLOCK]
            dv_ref[0] = dv_own[...] + dvb[:BLOCK]
            dk_own[...] = dkb[BLOCK:]
            dv_own[...] = dvb[BLOCK:]

        @pl.when(n == nb)
        def _():
            dk_ref[0] = dk_own[...]
            dv_ref[0] = dv_own[...]

    specs, args = _swa_specs(proj, k4, v4, nb, True)
    cur = lambda n: jnp.minimum(n, nb - 1)
    trail = lambda n: jnp.maximum(n - 1, 0)
    return pl.pallas_call(
        body, name="swa_bwd", grid=(SWA_KV_HEADS, nb + 1),
        in_specs=specs + [pl.BlockSpec((SWA_GROUP, BLOCK, 2 * BLOCK), lambda h, n: (h, 0, 0)),
                          pl.BlockSpec((1, SWA_GROUP, LANES), lambda h, n: (h, 0, 0)),
                          pl.BlockSpec((BLOCK, gw), lambda h, n: (cur(n), h))],
        out_specs=[pl.BlockSpec((BLOCK, gw), lambda h, n: (cur(n), h)),
                   pl.BlockSpec((BLOCK, gw), lambda h, n: (cur(n), h)),
                   pl.BlockSpec((1, BLOCK, HEAD_DIM), lambda h, n: (h, trail(n), 0)),
                   pl.BlockSpec((1, BLOCK, HEAD_DIM), lambda h, n: (h, trail(n), 0)),
                   pl.BlockSpec((SWA_GROUP, BLOCK, 2 * BLOCK), lambda h, n: (h, 0, 0)),
                   pl.BlockSpec((1, SWA_GROUP, LANES), lambda h, n: (h, 0, 0))],
        out_shape=[jax.ShapeDtypeStruct((s, SWA_HEADS * HEAD_DIM), BF16),
                   jax.ShapeDtypeStruct((s, SWA_HEADS * HEAD_DIM), BF16),
                   jax.ShapeDtypeStruct((SWA_KV_HEADS, s, HEAD_DIM), F32),
                   jax.ShapeDtypeStruct((SWA_KV_HEADS, s, HEAD_DIM), F32),
                   jax.ShapeDtypeStruct((SWA_HEADS, BLOCK, 2 * BLOCK), F32),
                   jax.ShapeDtypeStruct((SWA_KV_HEADS, SWA_GROUP, LANES), F32)],
        scratch_shapes=[pltpu.VMEM((BLOCK, HEAD_DIM), F32), pltpu.VMEM((BLOCK, HEAD_DIM), F32)],
        compiler_params=_cp(("arbitrary", "arbitrary")),
    )(*args, bias, sinks, da)


FOX_TILE = 1024


def _below_and_on_diagonal(i, j, step):
    @pl.when(j < i)
    def _():
        step(False)

    @pl.when(j == i)
    def _():
        step(True)


FOX_AUG = 128
FOX_CQ, FOX_CK = HEAD_DIM, HEAD_DIM + 1


def _fox_operands(proj, csum):
    s = proj.shape[0]
    hw = FOX_HEADS * HEAD_DIM
    heads = lambda a: a.reshape(s, FOX_HEADS, HEAD_DIM).transpose(1, 0, 2)
    q = heads(proj[:, :hw]) * jnp.asarray(HEAD_DIM ** -0.5, BF16)
    k, v = heads(proj[:, hw:2 * hw]), heads(proj[:, 2 * hw:3 * hw])
    one = jnp.ones((FOX_HEADS, s, 1), BF16)
    zero = jnp.zeros((FOX_HEADS, s, 1), BF16)
    pad = jnp.zeros((FOX_HEADS, s, FOX_AUG - HEAD_DIM - 2), BF16)
    qa = jnp.concatenate([q, zero, -one, pad], axis=-1)
    ka = jnp.concatenate([k, one, zero, pad], axis=-1)
    tr = lambda a: a.transpose(0, 2, 1)
    c = csum[:, :FOX_HEADS].T
    return (qa, ka, tr(qa), tr(ka), v, tr(v), c[:, None, :], jnp.broadcast_to(c[:, :, None], (FOX_HEADS, s, LANES)))


def _fox_scores_t(ka, qa, cq, ck, diag):
    st = _dot(ka, qa, 1, 1) + cq - jnp.concatenate([ck] * (qa.shape[0] // LANES), axis=1)
    if diag:
        st = jnp.where(lax.broadcasted_iota(jnp.int32, st.shape, 0) <= lax.broadcasted_iota(jnp.int32, st.shape, 1), st, NEG)
    return st


def _fox_attention(qa, ka, vt, c_row, c_lanes):
    nh, s, aw = qa.shape
    t = min(FOX_TILE, s)
    nt = s // t
    hd = HEAD_DIM

    def body(qa_ref, ka_ref, vt_ref, cq_ref, ck_ref, o_ref, lse_ref, m_s, l_s, acc_s):
        i, j = pl.program_id(1), pl.program_id(2)

        @pl.when(j == 0)
        def _():
            m_s[...] = jnp.full_like(m_s, NEG)
            l_s[...] = jnp.zeros_like(l_s)
            acc_s[...] = jnp.zeros_like(acc_s)

        def step(diag):
            st = _fox_scores_t(ka_ref[0], qa_ref[0], cq_ref[0], ck_ref[0], diag)
            m_old = m_s[...]
            m_new = jnp.maximum(m_old, jnp.max(st, axis=0, keepdims=True))
            alpha = jnp.exp(m_old - m_new)
            p = jnp.exp(st - m_new)
            l_s[...] = alpha * l_s[...] + jnp.sum(p, axis=0, keepdims=True)
            acc_s[...] = alpha * acc_s[...] + _dot(vt_ref[0], p)
            m_s[...] = m_new

        _below_and_on_diagonal(i, j, step)

        @pl.when(j == nt - 1)
        def _():
            o_ref[0] = (acc_s[...] / l_s[...]).astype(o_ref.dtype)
            lse_ref[0] = m_s[...] + jnp.log(l_s[...])

    kj = lambda i, j: jnp.minimum(j, i)
    return pl.pallas_call(
        body, name="fox_fwd", grid=(nh, nt, nt),
        in_specs=[pl.BlockSpec((1, t, aw), lambda h, i, j: (h, i, 0)),
                  pl.BlockSpec((1, t, aw), lambda h, i, j: (h, kj(i, j), 0)),
                  pl.BlockSpec((1, hd, t), lambda h, i, j: (h, 0, kj(i, j))),
                  pl.BlockSpec((1, 1, t), lambda h, i, j: (h, 0, i)),
                  pl.BlockSpec((1, t, LANES), lambda h, i, j: (h, kj(i, j), 0))],
        out_specs=[pl.BlockSpec((1, hd, t), lambda h, i, j: (h, 0, i)),
                   pl.BlockSpec((1, 1, t), lambda h, i, j: (h, 0, i))],
        out_shape=[jax.ShapeDtypeStruct((nh, hd, s), BF16), jax.ShapeDtypeStruct((nh, 1, s), F32)],
        scratch_shapes=[pltpu.VMEM((1, t), F32), pltpu.VMEM((1, t), F32), pltpu.VMEM((hd, t), F32)],
        compiler_params=_cp(("parallel", "parallel", "arbitrary")),
    )(qa, ka, vt, c_row, c_lanes)


def _fox_attention_bwd(qa, ka, qat, kat, v, c_row, c_lanes, ot, dot, lse):
    nh, s, aw = qa.shape
    t = min(FOX_TILE, s)
    nt = s // t
    hd = HEAD_DIM

    def body(qa_ref, ka_ref, qat_ref, kat_ref, v_ref, cq_ref, ck_ref, ot_ref, dot_ref, lse_ref,
             dq_ref, dcq_ref, dk_ref, dck_ref, dv_ref, dqa_s, dk_s, dv_s):
        j, i = pl.program_id(1), pl.program_id(2)

        @pl.when((j == 0) & (i == 0))
        def _():
            dqa_s[...] = jnp.zeros_like(dqa_s)

        @pl.when(i == 0)
        def _():
            dk_s[...] = jnp.zeros_like(dk_s)
            dv_s[...] = jnp.zeros_like(dv_s)

        def step(diag):
            p = jnp.exp(_fox_scores_t(ka_ref[0], qa_ref[0], cq_ref[0], ck_ref[0], diag) - lse_ref[0])
            do_t = dot_ref[0]
            delta = jnp.sum(do_t.astype(F32) * ot_ref[0].astype(F32), axis=0, keepdims=True)
            ds = (p * (_dot(v_ref[0], do_t) - delta)).astype(BF16)
            dv_s[...] += _dot(do_t, p, 1, 1)
            dk_s[...] += _dot(qat_ref[0], ds, 1, 1)
            cols = pl.ds(pl.multiple_of(i * t, t), t)
            dqa_s[:, cols] += _dot(kat_ref[0], ds)

        _below_and_on_diagonal(i, j, step)

        @pl.when(i == nt - 1)
        def _():
            dk_ref[0] = dk_s[:hd].astype(dk_ref.dtype)
            dck_ref[0] = dk_s[FOX_CK:FOX_CK + 1]
            dv_ref[0] = dv_s[...].astype(dv_ref.dtype)

        @pl.when((j == nt - 1) & (i == nt - 1))
        def _():
            dq_ref[0] = (dqa_s[:hd] * (hd ** -0.5)).astype(dq_ref.dtype)
            dcq_ref[0] = dqa_s[FOX_CQ:FOX_CQ + 1]

    qi = lambda i, j: jnp.maximum(i, j)
    return pl.pallas_call(
        body, name="fox_bwd", grid=(nh, nt, nt),
        in_specs=[pl.BlockSpec((1, t, aw), lambda h, j, i: (h, qi(i, j), 0)),
                  pl.BlockSpec((1, t, aw), lambda h, j, i: (h, j, 0)),
                  pl.BlockSpec((1, aw, t), lambda h, j, i: (h, 0, qi(i, j))),
                  pl.BlockSpec((1, aw, t), lambda h, j, i: (h, 0, j)),
                  pl.BlockSpec((1, t, hd), lambda h, j, i: (h, j, 0)),
                  pl.BlockSpec((1, 1, t), lambda h, j, i: (h, 0, qi(i, j))),
                  pl.BlockSpec((1, t, LANES), lambda h, j, i: (h, j, 0)),
                  pl.BlockSpec((1, hd, t), lambda h, j, i: (h, 0, qi(i, j))),
                  pl.BlockSpec((1, hd, t), lambda h, j, i: (h, 0, qi(i, j))),
                  pl.BlockSpec((1, 1, t), lambda h, j, i: (h, 0, qi(i, j)))],
        out_specs=[pl.BlockSpec((1, hd, s), lambda h, j, i: (h, 0, 0)),
                   pl.BlockSpec((1, 1, s), lambda h, j, i: (h, 0, 0)),
                   pl.BlockSpec((1, hd, t), lambda h, j, i: (h, 0, j)),
                   pl.BlockSpec((1, 1, t), lambda h, j, i: (h, 0, j)),
                   pl.BlockSpec((1, hd, t), lambda h, j, i: (h, 0, j))],
        out_shape=[jax.ShapeDtypeStruct((nh, hd, s), BF16), jax.ShapeDtypeStruct((nh, 1, s), F32),
                   jax.ShapeDtypeStruct((nh, hd, s), BF16), jax.ShapeDtypeStruct((nh, 1, s), F32),
                   jax.ShapeDtypeStruct((nh, hd, s), BF16)],
        scratch_shapes=[pltpu.VMEM((aw, s), F32), pltpu.VMEM((aw, t), F32), pltpu.VMEM((hd, t), F32)],
        compiler_params=_cp(("parallel", "arbitrary", "arbitrary")),
    )(qa, ka, qat, kat, v, c_row, c_lanes, ot, dot, lse)


def _ssm_prep(lam_re, lam_im, log_dt, b_re, b_im, c_re, c_im):
    g, n = lam_re.shape
    dt = jnp.exp(log_dt)[:, None]
    mag = jnp.exp(lam_re * dt)
    ab_re = mag * jnp.cos(lam_im * dt)
    ab_im = mag * jnp.sin(lam_im * dt)
    den = lam_re * lam_re + lam_im * lam_im
    nr = ab_re - 1.0
    coef_re = ((nr * lam_re + ab_im * lam_im) / den)[..., None]
    coef_im = ((ab_im * lam_re - nr * lam_im) / den)[..., None]
    bb_re = coef_re * b_re - coef_im * b_im
    bb_im = coef_re * b_im + coef_im * b_re
    nblk = g // GROUPS_PER_STEP
    eye = jnp.eye(GROUPS_PER_STEP, dtype=F32)

    def bdiag(bb):
        return jnp.einsum('bgnc,gh->bgchn', bb.reshape(nblk, GROUPS_PER_STEP, n, SSM_GROUP), eye).reshape(
            nblk, GROUPS_PER_STEP * SSM_GROUP, STATE_W)

    def cdiag(cc):
        return jnp.einsum('bgcn,gh->bgnhc', cc.reshape(nblk, GROUPS_PER_STEP, SSM_GROUP, n), eye).reshape(
            nblk, STATE_W, GROUPS_PER_STEP * SSM_GROUP)

    return (ab_re.reshape(nblk, 1, STATE_W), ab_im.reshape(nblk, 1, STATE_W),
            bdiag(bb_re), bdiag(bb_im), cdiag(c_re), cdiag(c_im))


def _powers(a_re, a_im, levels):
    rs, ims = [a_re], [a_im]
    for _ in range(levels - 1):
        r, i = rs[-1], ims[-1]
        rs.append(r * r - i * i)
        ims.append(2.0 * r * i)
    return jnp.concatenate(rs, axis=1), jnp.concatenate(ims, axis=1)


def _power_table(a_re, a_im, n, descending=False):
    tr, ti = a_re, a_im
    while tr.shape[1] < n:
        top = 0 if descending else -1
        lr, li = tr[:, top:][:, :1], ti[:, top:][:, :1]
        hr, hi = tr * lr - ti * li, tr * li + ti * lr
        tr, ti = ((jnp.concatenate([hr, tr], axis=1), jnp.concatenate([hi, ti], axis=1)) if descending
                  else (jnp.concatenate([tr, hr], axis=1), jnp.concatenate([ti, hi], axis=1)))
    return tr, ti


def _ssm_fwd(proj, pw_re, pw_im, tab_re, tab_im, bd_re, bd_im, cd_re, cd_im, dskip):
    s = proj.shape[0]
    w = dskip.shape[1]
    nblk = w // LANES
    nch = STATE_W // LANES
    levels = pw_re.shape[1]

    def body(u_ref, pr, pi, tr, ti, br, bi, cr, ci, d_ref, pre_ref, yg_ref, xr_ref, xi_ref):
        ch = pl.program_id(1)
        u = u_ref[...]
        xr, xi = _scan_complex(_dot(u, br[0]), _dot(u, bi[0]), pr[0], pi[0], tr[0], ti[0], False)
        xr_ref[...] = xr.astype(BF16)
        xi_ref[...] = xi.astype(BF16)
        yc = _dot(xr, cr[0]) - _dot(xi, ci[0])

        @pl.when(ch == 0)
        def _():
            pre_ref[...] = yc + d_ref[...] * u

        @pl.when(ch > 0)
        def _():
            pre_ref[...] += yc

        @pl.when(ch == nch - 1)
        def _():
            yg_ref[...] = _gelu(pre_ref[...]).astype(BF16)

    blk = lambda b, c: (0, b)
    col = lambda b, c: (0, b * nch + c)
    in_chunk = pl.BlockSpec((1, LANES, LANES), lambda b, c: (b, 0, c))
    out_chunk = pl.BlockSpec((1, LANES, LANES), lambda b, c: (b, c, 0))
    pw_spec = pl.BlockSpec((1, levels, LANES), lambda b, c: (b, 0, c))
    tab_spec = pl.BlockSpec((1, tab_re.shape[1], LANES), lambda b, c: (b, 0, c))
    return pl.pallas_call(
        body, name="ssm_fwd", grid=(nblk, nch),
        in_specs=[pl.BlockSpec((s, LANES), blk), pw_spec, pw_spec, tab_spec, tab_spec, in_chunk, in_chunk,
                  out_chunk, out_chunk, pl.BlockSpec((1, LANES), blk)],
        out_specs=[pl.BlockSpec((s, LANES), blk), pl.BlockSpec((s, LANES), blk),
                   pl.BlockSpec((s, LANES), col), pl.BlockSpec((s, LANES), col)],
        out_shape=[jax.ShapeDtypeStruct((s, w), F32), jax.ShapeDtypeStruct((s, w), BF16),
                   jax.ShapeDtypeStruct((s, nblk * STATE_W), BF16), jax.ShapeDtypeStruct((s, nblk * STATE_W), BF16)],
        compiler_params=_cp(("parallel", "arbitrary")),
    )(proj, pw_re, pw_im, tab_re, tab_im, bd_re, bd_im, cd_re, cd_im, dskip)


def _ssm_bwd(proj, dyg, pre, xr_all, xi_all, pw_re, pw_im, tab_re, tab_im, bd_re, bd_im, cd_re, cd_im, dskip):
    s = proj.shape[0]
    w = dskip.shape[1]
    nblk = w // LANES
    nch = STATE_W // LANES
    levels = pw_re.shape[1]

    def body(u_ref, dy_ref, pre_ref, xr_ref, xi_ref, pr, pi, tr, ti, br, bi, cr, ci, d_ref,
             du_ref, dd_ref, dar_ref, dai_ref, dbr_ref, dbi_ref, dcr_ref, dci_ref):
        ch = pl.program_id(1)
        u = u_ref[...]
        _, vjp = jax.vjp(_gelu, pre_ref[...])
        dpre = vjp(dy_ref[...].astype(F32))[0]
        zr, zi = _scan_complex(_dot(dpre, cr[0], 1, 1), -_dot(dpre, ci[0], 1, 1), pr[0], -pi[0], tr[0], -ti[0], True)
        xpr = _shift_rows(xr_ref[...].astype(F32), 1)
        xpi = _shift_rows(xi_ref[...].astype(F32), 1)
        dar_ref[0] = jnp.sum(zr * xpr + zi * xpi, axis=0, keepdims=True)
        dai_ref[0] = jnp.sum(zi * xpr - zr * xpi, axis=0, keepdims=True)
        dcr_ref[0] = _dot(xr_ref[...], dpre, 0, 0)
        dci_ref[0] = -_dot(xi_ref[...], dpre, 0, 0)
        dbr_ref[0] = _dot(u, zr, 0, 0)
        dbi_ref[0] = _dot(u, zi, 0, 0)
        duc = _dot(zr, br[0], 1, 1) + _dot(zi, bi[0], 1, 1)

        @pl.when(ch == 0)
        def _():
            du_ref[...] = duc + dpre * d_ref[...]
            dd_ref[...] = jnp.sum(dpre * u, axis=0, keepdims=True)

        @pl.when(ch > 0)
        def _():
            du_ref[...] += duc

    blk = lambda b, c: (0, b)
    col = lambda b, c: (0, b * nch + c)
    in_chunk = pl.BlockSpec((1, LANES, LANES), lambda b, c: (b, 0, c))
    out_chunk = pl.BlockSpec((1, LANES, LANES), lambda b, c: (b, c, 0))
    pw_spec = pl.BlockSpec((1, levels, LANES), lambda b, c: (b, 0, c))
    a_spec = pl.BlockSpec((1, 1, LANES), lambda b, c: (b, 0, c))
    tab_spec = pl.BlockSpec((1, tab_re.shape[1], LANES), lambda b, c: (b, 0, c))
    return pl.pallas_call(
        body, name="ssm_bwd", grid=(nblk, nch),
        in_specs=[pl.BlockSpec((s, LANES), blk), pl.BlockSpec((s, LANES), blk), pl.BlockSpec((s, LANES), blk),
                  pl.BlockSpec((s, LANES), col), pl.BlockSpec((s, LANES), col),
                  pw_spec, pw_spec, tab_spec, tab_spec, in_chunk, in_chunk, out_chunk, out_chunk,
                  pl.BlockSpec((1, LANES), blk)],
        out_specs=[pl.BlockSpec((s, LANES), blk), pl.BlockSpec((1, LANES), blk), a_spec, a_spec,
                   in_chunk, in_chunk, out_chunk, out_chunk],
        out_shape=[jax.ShapeDtypeStruct((s, w), F32), jax.ShapeDtypeStruct((1, w), F32),
                   jax.ShapeDtypeStruct((nblk, 1, STATE_W), F32), jax.ShapeDtypeStruct((nblk, 1, STATE_W), F32),
                   jax.ShapeDtypeStruct((nblk, LANES, STATE_W), F32), jax.ShapeDtypeStruct((nblk, LANES, STATE_W), F32),
                   jax.ShapeDtypeStruct((nblk, STATE_W, LANES), F32), jax.ShapeDtypeStruct((nblk, STATE_W, LANES), F32)],
        compiler_params=_cp(("parallel", "arbitrary")),
    )(proj, dyg, pre, xr_all, xi_all, pw_re, pw_im, tab_re, tab_im, bd_re, bd_im, cd_re, cd_im, dskip)


ROW_TILE = 256


def _norm_fwd(x, g, name):
    s, d = x.shape
    t = min(ROW_TILE, s)
    return _tiled(lambda xv, gv: (_rms(xv, gv),), [_rows(x, t), _full(g)], [_orows(s, d, BF16, t)], s // t, name)[0]


def _norm_bwd(x, g, dh_list, dx_in, name):
    s, d = x.shape
    t = min(ROW_TILE, s)

    def fn(xv, gv, dxv, *dhs):
        dh = dhs[0].astype(F32)
        for other in dhs[1:]:
            dh = dh + other.astype(F32)
        _, vjp = jax.vjp(_rms, xv, gv)
        dx, dg = vjp(dh)
        dx = dx + dxv
        return dx, dx, dg

    return _tiled(fn, [_rows(x, t), _full(g), _rows(dx_in, t)] + [_rows(a, t) for a in dh_list],
                  [_orows(s, d, F32, t), _orows(s, d, BF16, t), _oacc((1, d))], s // t, name)


def _ple_fwd(x, y, pn, name):
    s, d = x.shape
    t = min(ROW_TILE, s)

    def fn(xv, yv, gv):
        x1 = xv + yv
        return x1, _rms(x1, gv)

    return _tiled(fn, [_rows(x, t), _rows(y, t), _full(pn)], [_orows(s, d, F32, t), _orows(s, d, BF16, t)], s // t, name)


def _ple_mix(x1, emb, gl, name, g_next=None):
    s, d = x1.shape
    t = min(ROW_TILE, s)
    if g_next is None:
        return _tiled(lambda a, e, g: (a + e * jax.nn.sigmoid(g),), [_rows(x1, t), _rows(emb, t), _rows(gl, t)],
                      [_orows(s, d, F32, t)], s // t, name)[0], None

    def fn(a, e, g, gn):
        x2 = a + e * jax.nn.sigmoid(g)
        return x2, _rms(x2, gn)

    return _tiled(fn, [_rows(x1, t), _rows(emb, t), _rows(gl, t), _full(g_next)],
                  [_orows(s, d, F32, t), _orows(s, d, BF16, t)], s // t, name)


def _ple_mix_bwd(dx2, emb, gl, name):
    s, d = dx2.shape
    t = min(ROW_TILE, s)

    def fn(dx, e, g):
        sg = jax.nn.sigmoid(g)
        return dx * sg, dx * e * sg * (1.0 - sg)

    return _tiled(fn, [_rows(dx2, t), _rows(emb, t), _rows(gl, t)],
                  [_orows(s, d, BF16, t), _orows(s, d, BF16, t)], s // t, name)


def _loss_grad(x, target, g):
    s, d = x.shape
    t = min(ROW_TILE, s)

    def fn(xv, tv, gv):
        def f(xx, gg):
            err = _rms(xx, gg) - tv
            return 0.5 * jnp.sum(jnp.mean(err * err, axis=-1, keepdims=True), axis=0, keepdims=True)

        loss, vjp = jax.vjp(f, xv, gv)
        dx, dg = vjp(jnp.ones((1, 1), F32))
        return loss, dx, dg

    return _tiled(fn, [_rows(x, t), _rows(target, t), _full(g)],
                  [_oacc((1, 1)), _orows(s, d, F32, t), _oacc((1, d))], s // t, "loss_grad")


def _conv_fwd(proj, kern):
    s = proj.shape[0]
    w = kern.shape[1]
    nb = w // LANES

    def fn(bg, cg, u, gate, k):
        z = cg.astype(F32) * u.astype(F32)
        conv = k[2:3] * z + k[1:2] * _shift_rows(z, 1) + k[0:1] * _shift_rows(z, 2)
        return (bg.astype(F32) * conv * _silu(gate.astype(F32)),)

    return _tiled(fn, [_cols(proj, LANES, q * nb) for q in range(4)] + [_cols(kern, LANES)],
                  [_ocols(s, w, BF16, LANES)], nb, "conv_fwd")[0]


def _conv_bwd(proj, kern, da):
    s = proj.shape[0]
    w = kern.shape[1]
    nb = w // LANES

    def fn(bg, cg, u, gate, k, dav):
        bg, cg, u, gate, dav = (a.astype(F32) for a in (bg, cg, u, gate, dav))
        z = cg * u
        z1, z2 = _shift_rows(z, 1), _shift_rows(z, 2)
        conv = k[2:3] * z + k[1:2] * z1 + k[0:1] * z2
        sg = jax.nn.sigmoid(gate)
        dy = dav * gate * sg
        dgate = dav * bg * conv * sg * (1.0 + gate * (1.0 - sg))
        dconv = dy * bg
        dk = jnp.concatenate([jnp.sum(dconv * zz, axis=0, keepdims=True) for zz in (z2, z1, z)], axis=0)
        dz = k[2:3] * dconv + k[1:2] * _shift_rows(dconv, 1, True) + k[0:1] * _shift_rows(dconv, 2, True)
        return dy * conv, dz * u, dz * cg, dgate, dk

    return _tiled(fn, [_cols(proj, LANES, q * nb) for q in range(4)] + [_cols(kern, LANES), _cols(da, LANES)],
                  [_ocols(s, w, BF16, LANES)] * 4 + [_ocols(3, w, F32, LANES)], nb, "conv_bwd")


def _glu_fwd(gl, proj, bglu):
    s, w2 = gl.shape
    w = w2 // 2
    t = min(ROW_TILE, s)

    def fn(glv, gate, b):
        v = glv + b
        return (v[:, :w] * jax.nn.sigmoid(v[:, w:]) * _silu(gate),)

    return _tiled(fn, [_rows(gl, t), (proj, (t, w), lambda i: (i, 1)), _full(bglu)],
                  [_orows(s, w, BF16, t)], s // t, "glu_fwd")[0]


def _glu_bwd(gl, proj, bglu, da):
    s, w2 = gl.shape
    w = w2 // 2
    t = min(ROW_TILE, s)

    def fn(glv, gate, b, dav):
        def f(gg, gt, bb):
            v = gg + bb
            return v[:, :w] * jax.nn.sigmoid(v[:, w:]) * _silu(gt)

        _, vjp = jax.vjp(f, glv, gate, b)
        return vjp(dav.astype(F32))

    return _tiled(fn, [_rows(gl, t), (proj, (t, w), lambda i: (i, 1)), _full(bglu), _rows(da, t)],
                  [_orows(s, w2, BF16, t), _orows(s, w, BF16, t), _oacc((1, w2))], s // t, "glu_bwd")


def _fg_fwd(z, b):
    def fn(zv, bv):
        v = zv + bv
        logf = jnp.minimum(v, 0.0) - jnp.log(1.0 + jnp.exp(-jnp.abs(v)))
        return (_cumsum_rows(logf, False),)

    return _tiled(fn, [_full(z), _full(b)], [_out(z.shape, F32, z.shape, lambda i: (0, 0))], 1, "fg_fwd")[0]


def _fg_bwd(z, b, dcs):
    def fn(zv, bv, dc):
        dz = _cumsum_rows(dc, True) * jax.nn.sigmoid(-(zv + bv))
        return dz, jnp.sum(dz, axis=0, keepdims=True)

    return _tiled(fn, [_full(z), _full(b), _full(dcs)],
                  [_out(z.shape, BF16, z.shape, lambda i: (0, 0)), _out((1, z.shape[1]), F32, (1, z.shape[1]), lambda i: (0, 0))],
                  1, "fg_bwd")


def _fox_gate_bwd(da, o, proj):
    s, w = o.shape
    t = min(ROW_TILE, s)

    def fn(dav, ov, gate):
        dav, ov, gate = dav.astype(F32), ov.astype(F32), gate.astype(F32)
        sg = jax.nn.sigmoid(gate)
        return dav * gate * sg, dav * ov * sg * (1.0 + gate * (1.0 - sg))

    return _tiled(fn, [_rows(da, t), _rows(o, t), (proj, (t, w), lambda i: (i, 3))],
                  [_orows(s, w, BF16, t), _orows(s, w, BF16, t)], s // t, "fox_gate_bwd")


def _fox_gate_fwd(o, proj):
    s, w = o.shape
    t = min(ROW_TILE, s)
    return _tiled(lambda ov, gate: (ov.astype(F32) * _silu(gate.astype(F32)),),
                  [_rows(o, t), (proj, (t, w), lambda i: (i, 3))], [_orows(s, w, BF16, t)], s // t, "fox_gate_fwd")[0]


def _local_step(x, p, target, w):
    s, d = x.shape
    depth = p.shape[0]
    pb = p.astype(BF16)
    grads = {}
    saved = []

    bucket = _swa_bucket_table()
    onehot = np.eye(REL_BUCKETS, dtype=np.float32)[bucket.reshape(-1)]
    bias = _bias_table(w['rel_bias'], jnp.asarray(onehot.T, BF16))
    sinks = jnp.broadcast_to(w['swa_sinks'].reshape(SWA_KV_HEADS, SWA_GROUP, 1), (SWA_KV_HEADS, SWA_GROUP, LANES))
    ssm_params = tuple(w[k][0] for k in ('ssm_lam_re', 'ssm_lam_im', 'ssm_log_dt', 'ssm_b_re', 'ssm_b_im', 'ssm_c_re', 'ssm_c_im'))
    (a_re, a_im, bd_re, bd_im, cd_re, cd_im), ssm_vjp = jax.vjp(_ssm_prep, *ssm_params)
    levels = max(1, (s - 1).bit_length())
    pw_re, pw_im = _powers(a_re, a_im, levels)
    tab_re, tab_im = _power_table(a_re, a_im, min(SCAN_CHUNK, s))
    rtab_re, rtab_im = _power_table(a_re, a_im, min(SCAN_CHUNK, s), descending=True)
    wfg = jnp.pad(w['fox_w_fg'][0], ((0, 0), (0, LANES - FOX_HEADS)))
    bfg = jnp.pad(w['fox_b_fg'], ((0, 0), (0, LANES - FOX_HEADS)))

    def qkv4(a):
        return a.reshape(s, SWA_KV_HEADS, HEAD_DIM).transpose(1, 0, 2)

    def unheads(a):
        return a.transpose(2, 0, 1).reshape(s, FOX_HEADS * HEAD_DIM)

    for i in range(depth):
        mixer = i % 4
        hn = _norm_fwd(x, w['norm_g'][i:i + 1], f"norm_fwd{i}") if i == 0 else hn_next
        sv = {'x': x, 'hn': hn}
        if mixer == 0:
            proj = _mm(hn, w['swa_w_in'][:, 0], b_split=True, name="swa_in")
            qw = SWA_HEADS * HEAD_DIM
            kvw = SWA_KV_HEADS * HEAD_DIM
            k4, v4 = qkv4(proj[:, qw:qw + kvw]), qkv4(proj[:, qw + kvw:qw + 2 * kvw])
            a = _swa_fwd(proj, k4, v4, bias, sinks)
            sv.update(proj=proj, k4=k4, v4=v4)
            w_out = w['swa_w_out'][0]
        elif mixer == 1:
            proj = _mm(hn, w['conv_w_in'][:, 0], b_split=True, name="conv_in")
            a = _conv_fwd(proj, w['conv_kernel'][0])
            sv.update(proj=proj)
            w_out = w['conv_w_out'][0]
        elif mixer == 2:
            proj = _mm(hn, w['ssm_w_in'][:, 0], b_split=True, out_dtype=F32, name="ssm_in")
            pre, yg, xr_all, xi_all = _ssm_fwd(proj, pw_re, pw_im, tab_re, tab_im, bd_re, bd_im, cd_re, cd_im, w['ssm_d'])
            gl = _mm(yg, w['ssm_w_glu'][:, 0], b_split=True, out_dtype=F32, name="ssm_glu")
            a = _glu_fwd(gl, proj, w['ssm_b_glu'])
            sv.update(proj=proj, pre=pre, yg=yg, xr=xr_all, xi=xi_all, gl=gl)
            w_out = w['ssm_w_out'][0]
        else:
            proj = _mm(hn, w['fox_w_in'][:, 0], b_split=True, name="fox_in")
            z = _mm(hn, wfg, out_dtype=F32, name="fox_fg")
            fox_ops = _fox_operands(proj, _fg_fwd(z, bfg))
            ot, lse = _fox_attention(fox_ops[0], fox_ops[1], fox_ops[5], fox_ops[6], fox_ops[7])
            o = unheads(ot)
            a = _fox_gate_fwd(o, proj)
            sv.update(proj=proj, z=z, fox_ops=fox_ops, ot=ot, o=o, lse=lse)
            w_out = w['fox_w_out'][0]
        y = _mm(a, w_out, out_dtype=F32, name=f"mixer_out{i}")
        x1, gn = _ple_fwd(x, y, w['ple_norm'][i:i + 1], f"ple_fwd{i}")
        emb = _mm(pb[i], w['ple_proj'][:, i], b_split=True, out_dtype=F32, name=f"ple_emb{i}")
        gl2 = _mm(gn, w['ple_gate'][i], out_dtype=F32, name=f"ple_gate{i}")
        x, hn_next = _ple_mix(x1, emb, gl2, f"ple_mix{i}", w['norm_g'][i + 1:i + 2] if i + 1 < depth else None)
        sv.update(a=a, x1=x1, gn=gn, emb=emb, gl2=gl2)
        saved.append(sv)

    loss, dx, grads['final_g'] = _loss_grad(x, target, w['final_g'].reshape(1, d))
    grads['final_g'] = grads['final_g'].reshape(d)

    g_norm, g_ple_norm, g_ple_proj, g_ple_gate = [None] * depth, [None] * depth, [None] * depth, [None] * depth
    for i in reversed(range(depth)):
        sv = saved[i]
        mixer = i % 4
        demb, dgl2 = _ple_mix_bwd(dx, sv['emb'], sv['gl2'], f"ple_mix_bwd{i}")
        g_ple_proj[i] = _mm(pb[i], demb, ta=True, out_split=N_CHIPS, name=f"ple_emb_dw{i}")
        g_ple_gate[i] = _mm(sv['gn'], dgl2, ta=True, name=f"ple_gate_dw{i}")
        dgn = _mm(dgl2, w['ple_gate'][i], tb=True, out_dtype=F32, name=f"ple_gate_dx{i}")
        dx1, dy, g_ple_norm[i] = _norm_bwd(sv['x1'], w['ple_norm'][i:i + 1], [dgn], dx, f"ple_norm_bwd{i}")
        w_out_name = ('swa_w_out', 'conv_w_out', 'ssm_w_out', 'fox_w_out')[mixer]
        grads[w_out_name] = _mm(sv['a'], dy, ta=True, name=f"mixer_out_dw{i}")[None]
        da = _mm(dy, w[w_out_name][0], tb=True, name=f"mixer_out_dx{i}")
        proj = sv['proj']
        dhs = []
        if mixer == 0:
            dq, dgate, dk4, dv4, dbias, dsink = _swa_bwd(proj, sv['k4'], sv['v4'], bias, sinks, da)
            back = lambda t4: t4.transpose(1, 0, 2).reshape(s, SWA_KV_HEADS * HEAD_DIM).astype(BF16)
            dproj = jnp.concatenate([dq, back(dk4), back(dv4), dgate], axis=1)
            grads['rel_bias'] = _bias_grad(dbias.reshape(SWA_HEADS, -1), jnp.asarray(onehot, BF16))
            grads['swa_sinks'] = dsink[:, :, 0].reshape(1, SWA_HEADS)
            w_in_name = 'swa_w_in'
        elif mixer == 1:
            dbg, dcg, du, dgate, dkern = _conv_bwd(proj, w['conv_kernel'][0], da)
            dproj = jnp.concatenate([dbg, dcg, du, dgate], axis=1)
            grads['conv_kernel'] = dkern[None]
            w_in_name = 'conv_w_in'
        elif mixer == 2:
            dgl, dgate, dbglu = _glu_bwd(sv['gl'], proj, w['ssm_b_glu'], da)
            grads['ssm_b_glu'] = dbglu
            grads['ssm_w_glu'] = _mm(sv['yg'], dgl, ta=True, out_split=N_CHIPS, name="ssm_glu_dw")[:, None]
            dyg = _mm(dgl, w['ssm_w_glu'][:, 0], tb=True, b_split=True, out_dtype=F32, name="ssm_glu_dx")
            du, dd, da_re, da_im, dbd_re, dbd_im, dcd_re, dcd_im = _ssm_bwd(
                proj, dyg, sv['pre'], sv['xr'], sv['xi'], pw_re, pw_im, rtab_re, rtab_im, bd_re, bd_im, cd_re, cd_im, w['ssm_d'])
            grads['ssm_d'] = dd
            dparams = ssm_vjp((da_re, da_im, dbd_re, dbd_im, dcd_re, dcd_im))
            for k, v in zip(('ssm_lam_re', 'ssm_lam_im', 'ssm_log_dt', 'ssm_b_re', 'ssm_b_im', 'ssm_c_re', 'ssm_c_im'), dparams):
                grads[k] = v[None]
            dproj = jnp.concatenate([du.astype(BF16), dgate], axis=1)
            w_in_name = 'ssm_w_in'
        else:
            do, dgate = _fox_gate_bwd(da, sv['o'], proj)
            qa, ka, qat, kat, v, _, c_row, c_lanes = sv['fox_ops']
            dot = do.reshape(s, FOX_HEADS, HEAD_DIM).transpose(1, 2, 0)
            dqt, dcq, dkt, dck, dvt = _fox_attention_bwd(qa, ka, qat, kat, v, c_row, c_lanes, sv['ot'], dot, sv['lse'])
            dq, dk, dv = unheads(dqt), unheads(dkt), unheads(dvt)
            dcs = jnp.pad((dcq[:, 0] + dck[:, 0]).T, ((0, 0), (0, LANES - FOX_HEADS)))
            dz, dbfg = _fg_bwd(sv['z'], bfg, dcs)
            grads['fox_b_fg'] = dbfg[:, :FOX_HEADS]
            grads['fox_w_fg'] = _mm(sv['hn'], dz, ta=True, out_dtype=BF16, name="fox_fg_dw")[:, :FOX_HEADS][None]
            dhs.append(_mm(dz, wfg, tb=True, out_dtype=F32, name="fox_fg_dx"))
            dproj = jnp.concatenate([dq, dk, dv, dgate], axis=1)
            w_in_name = 'fox_w_in'
        grads[w_in_name] = _mm(sv['hn'], dproj, ta=True, out_split=N_CHIPS, name=f"mixer_in_dw{i}")[:, None]
        dhs.append(_mm(dproj, w[w_in_name][:, 0], tb=True, b_split=True, out_dtype=F32, name=f"mixer_in_dx{i}"))
        dx, _, g_norm[i] = _norm_bwd(sv['x'], w['norm_g'][i:i + 1], dhs, dx1, f"norm_bwd{i}")

    grads['norm_g'] = jnp.concatenate(g_norm, axis=0)
    grads['ple_norm'] = jnp.concatenate(g_ple_norm, axis=0)
    grads['ple_proj'] = jnp.stack(g_ple_proj, axis=1)
    grads['ple_gate'] = jnp.stack(g_ple_gate)
    return loss, dx, grads


def _bf16_terms(a):
    hi = lax.reduce_precision(a, 8, 7)
    r1 = a - hi
    mid = lax.reduce_precision(r1, 8, 7)
    lo = lax.reduce_precision(r1 - mid, 8, 7)
    return hi.astype(BF16), mid.astype(BF16), lo.astype(BF16)


def _split3(a):
    return jnp.concatenate(_bf16_terms(a), axis=0)


def _bias_grad(dbias, onehot):
    out = _mm(_split3(dbias), onehot, out_dtype=F32, name="rel_bias_grad", tk=2048)
    nh = dbias.shape[0]
    return (out[:nh] + out[nh:2 * nh] + out[2 * nh:]).T


def _bias_table(rel_bias, onehot_t):
    nh = rel_bias.shape[1]
    out = _mm(_split3(rel_bias.T), onehot_t, out_dtype=F32, name="rel_bias_table")
    return (out[:nh] + out[nh:2 * nh] + out[2 * nh:]).reshape(nh, BLOCK, 2 * BLOCK)


WEIGHTS = ['norm_g', 'final_g', 'rel_bias', 'swa_w_in', 'swa_w_out', 'swa_sinks', 'conv_w_in', 'conv_kernel', 'conv_w_out',
           'ssm_w_in', 'ssm_lam_re', 'ssm_lam_im', 'ssm_log_dt', 'ssm_b_re', 'ssm_b_im', 'ssm_c_re', 'ssm_c_im', 'ssm_d',
           'ssm_w_glu', 'ssm_b_glu', 'ssm_w_out', 'fox_w_in', 'fox_w_fg', 'fox_b_fg', 'fox_w_out', 'ple_proj', 'ple_norm',
           'ple_gate']
BIG = {'swa_w_in': 2, 'swa_w_out': 1, 'conv_w_in': 2, 'conv_w_out': 1, 'ssm_w_in': 2, 'ssm_w_glu': 2, 'ssm_w_out': 1,
       'fox_w_in': 2, 'fox_w_fg': 1, 'fox_w_out': 1, 'ple_proj': 2, 'ple_gate': 1}
SMALL = {'conv_kernel': 2, 'ssm_d': 1, 'ssm_b_glu': 1}
REPLICATED = [n for n in WEIGHTS if n not in BIG and n not in SMALL]
N_CHIPS = 4
N_DEV = 8
SMALL_ROWS = 8
REPL_ROWS = 64


def _flat(pieces, dtype, lead, row_mult):
    flat = jnp.concatenate([q.astype(dtype) for q in pieces], axis=-1)
    pad = (-flat.shape[-1]) % (row_mult * FLAT_W)
    flat = jnp.pad(flat, [(0, 0)] * len(lead) + [(0, pad)])
    return flat.reshape(*lead, -1, FLAT_W)


def _unflat(flat, lead_ndim, sizes):
    lead = flat.shape[:lead_ndim]
    flat = flat.reshape(*lead, -1)
    out, off = [], 0
    for n in sizes:
        out.append(flat[..., off:off + n])
        off += n
    return out


def _split_shards(full, axis):
    shp = full.shape
    parts = full.reshape(shp[:axis] + (N_CHIPS, shp[axis] // N_CHIPS) + shp[axis + 1:])
    return jnp.moveaxis(parts, axis, 0)


def _join_shards(parts, axis):
    moved = jnp.moveaxis(parts, 0, axis)
    shp = moved.shape
    return moved.reshape(shp[:axis] + (shp[axis] * shp[axis + 1],) + shp[axis + 2:])


def _coords():
    return lax.axis_index("x"), lax.axis_index("y"), lax.axis_index("c")


def _remote(k, src, dst, to, send_sems, recv_sems):
    return pltpu.make_async_remote_copy(src_ref=src, dst_ref=dst, send_sem=send_sems.at[k], recv_sem=recv_sems.at[k],
                                        device_id=to, device_id_type=MESH)


def _gather_weights(bufs, ssh):
    nb = len(bufs)

    def body(*refs):
        w_refs, s_ref = refs[:nb], refs[nb]
        wouts, sout = refs[nb + 1:2 * nb + 1], refs[2 * nb + 1]
        send_sems, recv_sems = refs[2 * nb + 2:]
        x, y, c = _coords()
        me = 2 * x + y
        chips = [(1 - x, y), (x, 1 - y), (1 - x, 1 - y)]
        rc = functools.partial(_remote, send_sems=send_sems, recv_sems=recv_sems)
        sends = []
        for j, (cx, cy) in enumerate(chips):
            for b in range(nb):
                sends.append(rc(6 * b + j, w_refs[b].at[c], wouts[b].at[me, c], (cx, cy, c)))
            sends.append(rc(6 * nb + j, s_ref, sout.at[me], (cx, cy, c)))
        for cp in sends:
            cp.start()
        for j, (cx, cy) in enumerate(chips):
            k = 2 * cx + cy
            for b in range(nb):
                rc(6 * b + j, w_refs[b].at[c], wouts[b].at[k, c], (x, y, c)).wait_recv()
                fwd = rc(6 * b + 3 + j, wouts[b].at[k, c], wouts[b].at[k, c], (x, y, 1 - c))
                fwd.start()
                sends.append(fwd)
        for j, (cx, cy) in enumerate(chips):
            k = 2 * cx + cy
            for b in range(nb):
                rc(6 * b + 3 + j, w_refs[b].at[c], wouts[b].at[k, 1 - c], (x, y, c)).wait_recv()
            rc(6 * nb + j, s_ref, sout.at[k], (x, y, c)).wait_recv()
        for cp in sends:
            cp.wait_send()

    nsem = 6 * nb + 3
    res = pl.pallas_call(
        body, name="gather_weights", in_specs=[ANY] * (nb + 1), out_specs=[ANY] * (nb + 1),
        out_shape=[jax.ShapeDtypeStruct((N_CHIPS,) + a.shape, a.dtype) for a in (*bufs, ssh)],
        scratch_shapes=[pltpu.SemaphoreType.DMA((nsem,)), pltpu.SemaphoreType.DMA((nsem,))],
    )(*bufs, ssh)
    return res[:nb], res[nb]


def _pair_exchange(gbufs):
    nb = len(gbufs)

    def body(*refs):
        g_refs, outs = refs[:nb], refs[nb:2 * nb]
        send_sems, recv_sems = refs[2 * nb:]
        x, y, c = _coords()
        rc = functools.partial(_remote, send_sems=send_sems, recv_sems=recv_sems)
        sends = [rc(N_CHIPS * b + j, g_refs[b].at[2 * j + 1 - c], outs[b].at[j], (x, y, 1 - c))
                 for b in range(nb) for j in range(N_CHIPS)]
        for cp in sends:
            cp.start()
        for b in range(nb):
            for j in range(N_CHIPS):
                rc(N_CHIPS * b + j, g_refs[b].at[2 * j + c], outs[b].at[j], (x, y, c)).wait_recv()
        for cp in sends:
            cp.wait_send()

    nsem = N_CHIPS * nb
    return pl.pallas_call(
        body, name="pair_exchange", in_specs=[ANY] * nb, out_specs=[ANY] * nb,
        out_shape=[jax.ShapeDtypeStruct((N_CHIPS,) + g.shape[1:], g.dtype) for g in gbufs],
        scratch_shapes=[pltpu.SemaphoreType.DMA((nsem,)), pltpu.SemaphoreType.DMA((nsem,))],
    )(*gbufs)


def _pair_sum(mine, theirs, name):
    n, r, w = mine.shape
    t = _tile(r, 256, 16)
    spec = lambda a: (a, (n, t, w), lambda i: (0, i, 0))
    return _tiled(lambda a, b: (a.astype(F32) + b.astype(F32),), [spec(mine), spec(theirs)],
                  [_out((n, r, w), BF16, (n, t, w), lambda i: (0, i, 0))], r // t, name)[0]


def _exchange_grads(pbufs, gsmall, grepl):
    nb = len(pbufs)

    def body(*refs):
        g_refs, s_ref, r_ref = refs[:nb], refs[nb], refs[nb + 1]
        ogs, osm, orp = refs[nb + 2:2 * nb + 2], refs[2 * nb + 2], refs[2 * nb + 3]
        send_sems, recv_sems, local_sems = refs[2 * nb + 4:]
        x, y, c = _coords()
        me = 4 * x + 2 * y + c
        my_chip = 2 * x + y
        rc = functools.partial(_remote, send_sems=send_sems, recv_sems=recv_sems)
        local = [pltpu.make_async_copy(s_ref.at[me], osm.at[me], local_sems.at[0]),
                 pltpu.make_async_copy(r_ref, orp.at[me], local_sems.at[1])]
        for cp in local:
            cp.start()
        peers = []
        for d in range(1, N_DEV):
            px = 1 - x if d & 4 else x
            py = 1 - y if d & 2 else y
            pc = 1 - c if d & 1 else c
            peers.append((d, 4 * px + 2 * py + pc, 2 * px + py, (px, py, pc)))
        per_peer = nb + 2
        sends = []
        for i, (d, peer, chip, to) in enumerate(peers):
            sends.append(rc(per_peer * i + nb, s_ref.at[peer], osm.at[me], to))
            sends.append(rc(per_peer * i + nb + 1, r_ref, orp.at[me], to))
            if d & 1 == 0:
                for b in range(nb):
                    sends.append(rc(per_peer * i + b, g_refs[b].at[chip], ogs[b].at[my_chip], to))
        for cp in sends:
            cp.start()
        for i, (d, peer, chip, to) in enumerate(peers):
            rc(per_peer * i + nb, s_ref.at[peer], osm.at[peer], to).wait_recv()
            rc(per_peer * i + nb + 1, r_ref, orp.at[peer], to).wait_recv()
            if d & 1 == 0:
                for b in range(nb):
                    rc(per_peer * i + b, g_refs[b].at[chip], ogs[b].at[chip], to).wait_recv()
        for cp in sends:
            cp.wait_send()
        for cp in local:
            cp.wait()

    nsem = (nb + 2) * (N_DEV - 1)
    res = pl.pallas_call(
        body, name="exchange_grads", in_specs=[ANY] * (nb + 2), out_specs=[ANY] * (nb + 2),
        out_shape=[jax.ShapeDtypeStruct(a.shape, a.dtype) for a in (*pbufs, gsmall)]
        + [jax.ShapeDtypeStruct((N_DEV,) + grepl.shape, grepl.dtype)],
        scratch_shapes=[pltpu.SemaphoreType.DMA((nsem,)), pltpu.SemaphoreType.DMA((nsem,)), pltpu.SemaphoreType.DMA((2,))],
    )(*pbufs, gsmall, grepl)
    return res[:nb], res[nb], res[nb + 1]


def _sibling_exchange(halves):
    nb = len(halves)

    def body(*refs):
        ins, outs = refs[:nb], refs[nb:2 * nb]
        send_sems, recv_sems = refs[2 * nb:]
        x, y, c = _coords()
        rc = functools.partial(_remote, send_sems=send_sems, recv_sems=recv_sems)
        sends = [rc(b, ins[b], outs[b], (x, y, 1 - c)) for b in range(nb)]
        for cp in sends:
            cp.start()
        for b in range(nb):
            rc(b, ins[b], outs[b], (x, y, c)).wait_recv()
        for cp in sends:
            cp.wait_send()

    return pl.pallas_call(
        body, name="sibling_exchange", in_specs=[ANY] * nb, out_specs=[ANY] * nb,
        out_shape=[jax.ShapeDtypeStruct(a.shape, a.dtype) for a in halves],
        scratch_shapes=[pltpu.SemaphoreType.DMA((nb,)), pltpu.SemaphoreType.DMA((nb,))],
    )(*halves)


def _sum_senders(recv, name):
    n, r, w = recv.shape
    t = _tile(r, 256, 8)

    def fn(v):
        acc = v[0].astype(F32)
        for i in range(1, n):
            acc = acc + v[i].astype(F32)
        return (acc,)

    return _tiled(fn, [(recv, (n, t, w), lambda i: (0, i, 0))], [_orows(r, w, F32, t)], r // t, name)[0]


def _adamw(w, g, m, v, name):
    r, wd = w.shape
    t = _tile(r, 256, 8)

    def fn(wv, gv, mv, vv):
        m2 = ADAM_B1 * mv + (1.0 - ADAM_B1) * gv
        v2 = ADAM_B2 * vv + (1.0 - ADAM_B2) * (gv * gv)
        m_hat = m2 / (1.0 - ADAM_B1 ** ADAM_STEP)
        v_hat = v2 / (1.0 - ADAM_B2 ** ADAM_STEP)
        delta = -ADAM_LR * (m_hat / (jnp.sqrt(v_hat) + ADAM_EPS) + ADAM_WD * wv)
        return delta, m2, v2

    return _tiled(fn, [_rows(a, t) for a in (w, g, m, v)], [_orows(r, wd, F32, t)] * 3, r // t, name)


def kernel(x, p, norm_g, final_g, rel_bias, swa_w_in, swa_w_out, swa_sinks, conv_w_in, conv_kernel, conv_w_out, ssm_w_in, ssm_lam_re, ssm_lam_im, ssm_log_dt, ssm_b_re, ssm_b_im, ssm_c_re, ssm_c_im, ssm_d, ssm_w_glu, ssm_b_glu, ssm_w_out, fox_w_in, fox_w_fg, fox_b_fg, fox_w_out, ple_proj, ple_norm, ple_gate, loss_target, m_norm_g, m_final_g, m_rel_bias, m_swa_w_in, m_swa_w_out, m_swa_sinks, m_conv_w_in, m_conv_kernel, m_conv_w_out, m_ssm_w_in, m_ssm_lam_re, m_ssm_lam_im, m_ssm_log_dt, m_ssm_b_re, m_ssm_b_im, m_ssm_c_re, m_ssm_c_im, m_ssm_d, m_ssm_w_glu, m_ssm_b_glu, m_ssm_w_out, m_fox_w_in, m_fox_w_fg, m_fox_b_fg, m_fox_w_out, m_ple_proj, m_ple_norm, m_ple_gate, v_norm_g, v_final_g, v_rel_bias, v_swa_w_in, v_swa_w_out, v_swa_sinks, v_conv_w_in, v_conv_kernel, v_conv_w_out, v_ssm_w_in, v_ssm_lam_re, v_ssm_lam_im, v_ssm_log_dt, v_ssm_b_re, v_ssm_b_im, v_ssm_c_re, v_ssm_c_im, v_ssm_d, v_ssm_w_glu, v_ssm_b_glu, v_ssm_w_out, v_fox_w_in, v_fox_w_fg, v_fox_b_fg, v_fox_w_out, v_ple_proj, v_ple_norm, v_ple_gate):
    given = dict(locals())
    shard_shape = {n: given[n].shape for n in WEIGHTS}
    half = {n: math.prod(shard_shape[n]) // 2 for n in WEIGHTS}

    widths = sorted({shard_shape[n][-1] for n in BIG}, reverse=True)
    classes = [[n for n in BIG if shard_shape[n][-1] == w] for w in widths]
    rows = {n: math.prod(shard_shape[n][:-1]) for n in BIG}
    core = lax.axis_index("c")
    my_chip = 2 * lax.axis_index("x") + lax.axis_index("y")

    bufs = [jnp.concatenate([given[n].astype(BF16).reshape(rows[n], w) for n in names], axis=0).reshape(2, -1, w)
            for w, names in zip(widths, classes)]
    ssh = _flat([given[n].reshape(-1) for n in SMALL], F32, (), SMALL_ROWS)
    walls, sall = _gather_weights(bufs, ssh)
    walls = [lax.dynamic_update_slice(a, b[None], (my_chip, 0, 0, 0)) for a, b in zip(walls, bufs)]
    sall = lax.dynamic_update_slice(sall, ssh[None], (my_chip, 0, 0))
    full = {n: given[n] for n in REPLICATED}
    for w, names, wall in zip(widths, classes, walls):
        wall, off = wall.reshape(N_CHIPS, -1, w), 0
        for n in names:
            stacked = wall[:, off:off + rows[n]].reshape((N_CHIPS,) + shard_shape[n])
            full[n] = stacked if BIG[n] == 2 else _join_shards(stacked, BIG[n])
            off += rows[n]
    for n, piece in zip(SMALL, _unflat(sall, 1, [2 * half[n] for n in SMALL])):
        full[n] = _join_shards(piece.reshape((N_CHIPS,) + shard_shape[n]), SMALL[n])

    loss, dx, grads = _local_step(x[0], p[:, 0], loss_target[0], full)

    by_chip = lambda n: grads[n] if BIG[n] == 2 else _split_shards(grads[n], BIG[n])
    gbufs = [jnp.concatenate([by_chip(n).reshape(N_CHIPS, rows[n], w) for n in names], axis=1).reshape(N_DEV, -1, w)
             for w, names in zip(widths, classes)]
    gsmall = _flat([_split_shards(grads[n], SMALL[n]).reshape(N_DEV, -1) for n in SMALL], F32, (N_DEV,), SMALL_ROWS)
    grepl = _flat([grads[n].reshape(-1) for n in REPLICATED], F32, (), REPL_ROWS)
    pbufs = [_pair_sum(jnp.where(core == 0, g[0::2], g[1::2]), t, f"pair_sum_w{w}")
             for w, g, t in zip(widths, gbufs, _pair_exchange(gbufs))]
    recv_bufs, recv_small, recv_repl = _exchange_grads(pbufs, gsmall, grepl)
    recv_bufs = [lax.dynamic_update_slice(r, lax.dynamic_index_in_dim(pb, my_chip, axis=0), (my_chip, 0, 0))
                 for r, pb in zip(recv_bufs, pbufs)]
    halves = [_sum_senders(r, f"sum_w{w}") for w, r in zip(widths, recv_bufs)] + [_sum_senders(recv_small, "sum_small")]
    others = _sibling_exchange(halves)
    g_repl = _sum_senders(recv_repl, "sum_repl")
    both = [(jnp.where(core == 0, a, b), jnp.where(core == 0, b, a)) for a, b in zip(halves, others)]
    sfull = jnp.stack(both[-1])

    out_g, out_d, out_m, out_v = {}, {}, {}, {}
    for w, names, (lower, upper) in zip(widths, classes, both):
        gfull, off = jnp.concatenate([lower, upper], axis=0), 0
        for n in names:
            as_rows = lambda t, w=w: t.reshape(-1, w)
            piece = gfull[off:off + rows[n]]
            off += rows[n]
            d, m2, v2 = _adamw(as_rows(given[n]), piece, as_rows(given['m_' + n]), as_rows(given['v_' + n]), f"adamw_{n}")
            out_g[n], out_d[n], out_m[n], out_v[n] = (t.reshape(shard_shape[n]) for t in (piece, d, m2, v2))

    pack_small = lambda pre: _flat([given[pre + n].reshape(2, -1) for n in SMALL], F32, (2,), SMALL_ROWS).reshape(-1, FLAT_W)
    res = _adamw(pack_small(''), sfull.reshape(-1, FLAT_W), pack_small('m_'), pack_small('v_'), "adamw_small")
    for dst, flat in zip((out_g, out_d, out_m, out_v), (sfull,) + tuple(res)):
        for n, piece in zip(SMALL, _unflat(flat.reshape(2, -1, FLAT_W), 1, [half[n] for n in SMALL])):
            dst[n] = piece.reshape(shard_shape[n])

    pack_repl = lambda pre: _flat([given[pre + n].reshape(-1) for n in REPLICATED], F32, (), REPL_ROWS)
    res = _adamw(pack_repl(''), g_repl, pack_repl('m_'), pack_repl('v_'), "adamw_repl")
    for dst, flat in zip((out_g, out_d, out_m, out_v), (g_repl,) + tuple(res)):
        for n, piece in zip(REPLICATED, _unflat(flat, 0, [2 * half[n] for n in REPLICATED])):
            dst[n] = piece.reshape(shard_shape[n])

    total = lax.psum(loss[0, 0], ("x", "y", "c"))
    return (total, dx[None], *[out_g[n] for n in WEIGHTS], *[out_d[n] for n in WEIGHTS],
            *[out_m[n] for n in WEIGHTS], *[out_v[n] for n in WEIGHTS])
```
